```python
import jax, jax.numpy as jnp
from jax import lax
import numpy as np

D_MODEL = 2048
BATCH = 4
SEQ = 2048
DEPTH = 2

GRID_W = 64
CTX_LEN = 256
HEAD_DIM = 128
A_Q_HEADS = 8
A_KV_HEADS = 2
WINDOW = 128
A_BLOCK = 128
ROPE_THETA = 10000.0
GLA_HEADS = 4
GLA_DK = 128
GLA_DV = 256
GLA_RANK = 16
GLA_TAU = 16.0
GLA_CHUNK = 64
CONV_WIDTH = 3
CONV_INNER = D_MODEL
N_EXPERTS = 32
N_GROUPS = 4
TOP_K = 2
D_EXPERT = 1024
MOE_BLOCK = 128

EPS = 1e-6
NEG_INF = -1e30
A_Q_DIM = A_Q_HEADS * HEAD_DIM
A_KV_DIM = A_KV_HEADS * HEAD_DIM
GLA_K_DIM = GLA_HEADS * GLA_DK
GLA_V_DIM = GLA_HEADS * GLA_DV
ATTN_SPLIT_SIZES = (A_Q_DIM, A_KV_DIM, A_KV_DIM, GLA_K_DIM, GLA_K_DIM, GLA_V_DIM, GLA_V_DIM, 2 * GLA_RANK)
ATTN_IN_DIM = sum(ATTN_SPLIT_SIZES)
MIX_DIM = A_Q_DIM + GLA_V_DIM
N_ATTN_LAYERS = (DEPTH + 1) // 2
N_CONV_LAYERS = DEPTH // 2

kernel_name = "hybrid_prefix_dit_block"


def rms_norm(x, gain=None):
    xf = x.astype(jnp.float32)
    y = xf * lax.rsqrt(jnp.mean(xf * xf, axis=-1, keepdims=True) + EPS)
    if gain is not None:
        y = y * gain.astype(jnp.float32)
    return y.astype(x.dtype)


def ada_modulation(cond, w, b):
    m = jax.nn.silu(cond) @ w + b
    return jnp.split(m[..., None, :], 6, axis=-1)


def modulate(x, shift, scale):
    return rms_norm(x) * (1 + scale) + shift


def axial_rope_tables(n):
    rows = n // GRID_W
    f = HEAD_DIM // 4
    inv_freq = ROPE_THETA ** (-jnp.arange(f, dtype=jnp.float32) / f)
    r = jnp.broadcast_to(jnp.arange(rows, dtype=jnp.float32)[:, None], (rows, GRID_W)).reshape(n)
    col = jnp.broadcast_to(jnp.arange(GRID_W, dtype=jnp.float32)[None, :], (rows, GRID_W)).reshape(n)
    ang = jnp.stack([r, col], axis=-1)[..., None] * inv_freq
    return jnp.cos(ang), jnp.sin(ang)


def apply_axial_rope(x, cos, sin):
    b, n, h, d = x.shape
    xr = x.astype(jnp.float32).reshape(b, n, h, 2, 2, d // 4)
    x1, x2 = xr[..., 0, :], xr[..., 1, :]
    cs, sn = cos[None, :, None], sin[None, :, None]
    out = jnp.stack([x1 * cs - x2 * sn, x2 * cs + x1 * sn], axis=-2)
    return out.reshape(b, n, h, d).astype(x.dtype)


def band_blocks(t, nb):
    tp = jnp.pad(t, ((0, 0), (A_BLOCK, A_BLOCK), (0, 0), (0, 0)))
    tp = tp.reshape(t.shape[0], nb + 2, A_BLOCK, t.shape[2], t.shape[3])
    return jnp.concatenate([tp[:, :-2], tp[:, 1:-1], tp[:, 2:]], axis=2)


def windowed_sink_attention(q, k, v, k_ctx, v_ctx, sink):
    B, S, H, _ = q.shape
    L = k_ctx.shape[1]
    nb = S // A_BLOCK
    G = H // A_KV_HEADS
    qb = q.reshape(B, nb, A_BLOCK, A_KV_HEADS, G, HEAD_DIM)
    kw, vw = band_blocks(k, nb), band_blocks(v, nb)
    scale = HEAD_DIM ** -0.5
    s_win = jnp.einsum('bnqkgd,bnjkd->bnkgqj', qb, kw, preferred_element_type=jnp.float32) * scale
    s_ctx = jnp.einsum('bnqkgd,bjkd->bnkgqj', qb, k_ctx, preferred_element_type=jnp.float32) * scale
    qi = jnp.arange(A_BLOCK)[:, None]
    kj = jnp.arange(3 * A_BLOCK)[None, :]
    kpos = (jnp.arange(nb)[:, None, None] - 1) * A_BLOCK + kj[None]
    band = (jnp.abs(kj - A_BLOCK - qi) <= WINDOW)[None] & (kpos >= 0) & (kpos < S)
    s_win = jnp.where(band[None, :, None, None], s_win, NEG_INF)
    sink_l = jnp.broadcast_to(sink.astype(jnp.float32).reshape(A_KV_HEADS, G)[None, None, :, :, None, None],
                              s_win.shape[:-1] + (1,))
    p = jax.nn.softmax(jnp.concatenate([s_ctx, s_win, sink_l], axis=-1), axis=-1).astype(v.dtype)
    out = (jnp.einsum('bnkgqj,bjkd->bnqkgd', p[..., :L], v_ctx)
           + jnp.einsum('bnkgqj,bnjkd->bnqkgd', p[..., L:L + 3 * A_BLOCK], vw))
    return out.reshape(B, S, H * HEAD_DIM)


def context_sink_attention(q, k, v, sink):
    B, L, H, _ = q.shape
    G = H // A_KV_HEADS
    qg = q.reshape(B, L, A_KV_HEADS, G, HEAD_DIM)
    s = jnp.einsum('blkgd,bjkd->bkglj', qg, k, preferred_element_type=jnp.float32) * HEAD_DIM ** -0.5
    sink_l = jnp.broadcast_to(sink.astype(jnp.float32).reshape(A_KV_HEADS, G)[None, :, :, None, None],
                              s.shape[:-1] + (1,))
    p = jax.nn.softmax(jnp.concatenate([s, sink_l], axis=-1), axis=-1)[..., :L].astype(v.dtype)
    return jnp.einsum('bkglj,bjkd->blkgd', p, v).reshape(B, L, H * HEAD_DIM)


def gla_chunked(q, k, v, log_a, s0):
    b, n, h, _ = q.shape
    nc = n // GLA_CHUNK
    def to_chunks(t):
        return jnp.moveaxis(t.reshape(b, nc, GLA_CHUNK, h, t.shape[-1]), 1, 0)
    incl = jnp.tril(jnp.ones((GLA_CHUNK, GLA_CHUNK), dtype=bool))
    def step(s, inp):
        qc, kc, vc, ac = inp
        bcum = jnp.cumsum(ac, axis=1)
        blast = bcum[:, -1:]
        qf = qc.astype(jnp.float32) * jnp.exp(bcum)
        kf = kc.astype(jnp.float32) * jnp.exp(-bcum)
        kl = kc.astype(jnp.float32) * jnp.exp(blast - bcum)
        vf = vc.astype(jnp.float32)
        att = jnp.where(incl, jnp.einsum('bihd,bjhd->bhij', qf, kf), 0.0)
        o = jnp.einsum('bhij,bjhe->bihe', att, vf) + jnp.einsum('bihd,bhde->bihe', qf, s)
        s_new = jnp.exp(blast[:, 0])[..., None] * s + jnp.einsum('bjhd,bjhe->bhde', kl, vf)
        return s_new, o
    s_fin, o = lax.scan(step, s0, (to_chunks(q), to_chunks(k), to_chunks(v), to_chunks(log_a)))
    o = jnp.moveaxis(o, 0, 1).reshape(b, n, h, -1)
    return o.astype(v.dtype), s_fin


def gla_bidirectional(ctx_qkv, lat_qkv, la_ctx, la_lat):
    qc, kc, vc = ctx_qkv
    ql, kl, vl = lat_qkv
    s0 = jnp.zeros((ql.shape[0], GLA_HEADS, GLA_DK, GLA_DV), jnp.float32)
    rev = lambda t: t[:, ::-1]
    o_cf, s_cf = gla_chunked(qc, kc, vc, la_ctx[0], s0)
    o_lf, _ = gla_chunked(ql, kl, vl, la_lat[0], s_cf)
    o_cb, s_cb = gla_chunked(rev(qc), rev(kc), rev(vc), rev(la_ctx[1]), s0)
    o_lb, _ = gla_chunked(rev(ql), rev(kl), rev(vl), rev(la_lat[1]), s_cb)
    return o_lf + rev(o_lb), o_cf + rev(o_cb)


def hybrid_attn_gla(h_lat, h_ctx, cos, sin, w_in, q_gain, k_gain, sink, wa2, ba, o_gain, w_out, ctx_live):
    B, S, _ = h_lat.shape
    L = h_ctx.shape[1]
    idx = np.cumsum(ATTN_SPLIT_SIZES)[:-1].tolist()
    aq_l, ak_l, av_l, gq_l, gk_l, gv_l, gg_l, lr_l = jnp.split(h_lat @ w_in, idx, axis=-1)
    aq_c, ak_c, av_c, gq_c, gk_c, gv_c, gg_c, lr_c = jnp.split(h_ctx @ w_in, idx, axis=-1)
    heads = lambda t, h: t.reshape(t.shape[0], t.shape[1], h, -1)

    q_l = apply_axial_rope(rms_norm(heads(aq_l, A_Q_HEADS), q_gain), cos, sin)
    k_l = apply_axial_rope(rms_norm(heads(ak_l, A_KV_HEADS), k_gain), cos, sin)
    k_c = rms_norm(heads(ak_c, A_KV_HEADS), k_gain)
    v_l, v_c = heads(av_l, A_KV_HEADS), heads(av_c, A_KV_HEADS)
    attn_l = windowed_sink_attention(q_l, k_l, v_l, k_c, v_c, sink)

    def gla_inputs(gq, gk, gv, lr):
        q = heads(gq, GLA_HEADS) * GLA_DK ** -0.5
        las = [(jax.nn.log_sigmoid((lr_d @ wa2[d] + ba[d]).astype(jnp.float32)) / GLA_TAU)
               .reshape(lr.shape[0], lr.shape[1], GLA_HEADS, GLA_DK)
               for d, lr_d in enumerate(jnp.split(lr, 2, axis=-1))]
        return (q, heads(gk, GLA_HEADS), heads(gv, GLA_HEADS)), las
    qkv_l, la_l = gla_inputs(gq_l, gk_l, gv_l, lr_l)
    qkv_c, la_c = gla_inputs(gq_c, gk_c, gv_c, lr_c)
    o_l, o_c = gla_bidirectional(qkv_c, qkv_l, la_c, la_l)
    gla_out = lambda o, g: (rms_norm(o, o_gain) * jax.nn.silu(heads(g, GLA_HEADS))).reshape(o.shape[0], o.shape[1], GLA_V_DIM)

    mix_lat = jnp.concatenate([attn_l, gla_out(o_l, gg_l)], axis=-1) @ w_out
    mix_ctx = None
    if ctx_live:
        q_c = rms_norm(heads(aq_c, A_Q_HEADS), q_gain)
        attn_c = context_sink_attention(q_c, k_c, v_c, sink)
        mix_ctx = jnp.concatenate([attn_c, gla_out(o_c, gg_c)], axis=-1) @ w_out
    return mix_lat, mix_ctx


def short_gated_conv(h, w_in, w_conv, w_out):
    gb, gc, u = jnp.split(h @ w_in, 3, axis=-1)
    y = lax.conv_general_dilated(gc * u, w_conv[:, None, :], window_strides=(1,), padding=[(1, 1)],
                                 dimension_numbers=('NWC', 'WIO', 'NWC'), feature_group_count=CONV_INNER)
    return (gb * y) @ w_out


def grouped_moe(h, router_w, router_b, w1, w3, w2):
    T, D = h.shape
    gsz = N_EXPERTS // N_GROUPS
    scores = jax.nn.sigmoid((h @ router_w).astype(jnp.float32))
    grp = (scores + router_b.astype(jnp.float32)).reshape(T, N_GROUPS, gsz)
    gsel = jnp.argmax(lax.top_k(grp, TOP_K)[0].sum(-1), axis=-1)
    in_grp = jnp.take_along_axis(grp, gsel[:, None, None], axis=1)[:, 0]
    eidx = gsel[:, None] * gsz + lax.top_k(in_grp, TOP_K)[1]
    wsel = jnp.take_along_axis(scores, eidx, axis=1)
    wsel = wsel / jnp.sum(wsel, axis=-1, keepdims=True)

    A = T * TOP_K
    e_flat = eidx.reshape(A)
    tok_flat = jnp.repeat(jnp.arange(T), TOP_K)
    order = jnp.argsort(e_flat)
    e_s, tok_s, w_s = e_flat[order], tok_flat[order], wsel.reshape(A)[order]
    counts = jnp.bincount(e_flat, length=N_EXPERTS)
    starts = jnp.cumsum(counts) - counts
    pcounts = (counts + MOE_BLOCK - 1) // MOE_BLOCK * MOE_BLOCK
    pends = jnp.cumsum(pcounts)
    dest = (pends - pcounts)[e_s] + (jnp.arange(A) - starts[e_s])
    nblk = -(-A // MOE_BLOCK) + N_EXPERTS
    P = nblk * MOE_BLOCK
    slot_tok = jnp.zeros((P,), jnp.int32).at[dest].set(tok_s)
    slot_w = jnp.zeros((P,), jnp.float32).at[dest].set(w_s)
    blk_exp = jnp.minimum(jnp.sum(jnp.arange(nblk)[:, None] * MOE_BLOCK >= pends[None, :], axis=1), N_EXPERTS - 1)

    def expert_block(args):
        xb, e = args
        return (jax.nn.silu(xb @ w1[e]) * (xb @ w3[e])) @ w2[e]
    yb = lax.map(expert_block, (h[slot_tok].reshape(nblk, MOE_BLOCK, D), blk_exp))
    y = yb.reshape(P, D) * slot_w[:, None].astype(h.dtype)
    return jnp.zeros_like(h).at[slot_tok].add(y)


def setup_inputs(seed: int = 0) -> dict:
    key = jax.random.key(seed)
    ks = jax.random.split(key, 24)
    nrm = lambda k, shape, s: jax.random.normal(k, shape, jnp.float32) * s
    D = D_MODEL
    return {
        'x': nrm(ks[0], (BATCH, SEQ, D), 1.0),
        'c': nrm(ks[1], (BATCH, D), 1.0),
        'ctx': nrm(ks[2], (BATCH, CTX_LEN, D), 1.0),
        'c_ctx': nrm(ks[3], (D,), 1.0),
        'mod_w': nrm(ks[4], (DEPTH, D, 6 * D), 0.5 * D ** -0.5),
        'mod_b': nrm(ks[5], (DEPTH, 6 * D), 0.02),
        'attn_w_in': nrm(ks[6], (N_ATTN_LAYERS, D, ATTN_IN_DIM), D ** -0.5),
        'attn_q_norm': 1.0 + nrm(ks[7], (N_ATTN_LAYERS, HEAD_DIM), 0.02),
        'attn_k_norm': 1.0 + nrm(ks[8], (N_ATTN_LAYERS, HEAD_DIM), 0.02),
        'attn_sink': nrm(ks[9], (N_ATTN_LAYERS, A_Q_HEADS), 0.5),
        'gla_wa2': nrm(ks[10], (N_ATTN_LAYERS, 2, GLA_RANK, GLA_K_DIM), GLA_RANK ** -0.5),
        'gla_ba': nrm(ks[11], (N_ATTN_LAYERS, 2, GLA_K_DIM), 0.5),
        'gla_norm': 1.0 + nrm(ks[12], (N_ATTN_LAYERS, GLA_DV), 0.02),
        'attn_w_out': nrm(ks[13], (N_ATTN_LAYERS, MIX_DIM, D), MIX_DIM ** -0.5),
        'conv_w_in': nrm(ks[14], (N_CONV_LAYERS, D, 3 * CONV_INNER), D ** -0.5),
        'conv_w': nrm(ks[15], (N_CONV_LAYERS, CONV_WIDTH, CONV_INNER), CONV_WIDTH ** -0.5),
        'conv_w_out': nrm(ks[16], (N_CONV_LAYERS, CONV_INNER, D), CONV_INNER ** -0.5),
        'router_w': nrm(ks[17], (D, N_EXPERTS), D ** -0.5),
        'router_b': nrm(ks[18], (N_EXPERTS,), 0.01),
        'exp_w1': nrm(ks[19], (DEPTH, N_EXPERTS, D, D_EXPERT), D ** -0.5),
        'exp_w3': nrm(ks[20], (DEPTH, N_EXPERTS, D, D_EXPERT), D ** -0.5),
        'exp_w2': nrm(ks[21], (DEPTH, N_EXPERTS, D_EXPERT, D), D_EXPERT ** -0.5),
    }


def reference(x, c, ctx, c_ctx, mod_w, mod_b, attn_w_in, attn_q_norm, attn_k_norm, attn_sink,
              gla_wa2, gla_ba, gla_norm, attn_w_out, conv_w_in, conv_w, conv_w_out,
              router_w, router_b, exp_w1, exp_w3, exp_w2):
    B, S, D = x.shape
    L = ctx.shape[1]
    cos, sin = axial_rope_tables(S)
    for layer in range(DEPTH):
        ctx_live = any(j % 2 == 0 for j in range(layer + 1, DEPTH))
        m_lat = ada_modulation(c, mod_w[layer], mod_b[layer])
        m_ctx = ada_modulation(c_ctx, mod_w[layer], mod_b[layer])
        h_lat = modulate(x, m_lat[0], m_lat[1])
        i = layer // 2
        if layer % 2 == 0:
            h_ctx = modulate(ctx, m_ctx[0], m_ctx[1])
            mix_lat, mix_ctx = hybrid_attn_gla(h_lat, h_ctx, cos, sin, attn_w_in[i], attn_q_norm[i], attn_k_norm[i],
                                               attn_sink[i], gla_wa2[i], gla_ba[i], gla_norm[i], attn_w_out[i], ctx_live)
        else:
            mix_lat = short_gated_conv(h_lat, conv_w_in[i], conv_w[i], conv_w_out[i])
            mix_ctx = None
            if ctx_live:
                mix_ctx = short_gated_conv(modulate(ctx, m_ctx[0], m_ctx[1]), conv_w_in[i], conv_w[i], conv_w_out[i])
        x = x + m_lat[2] * mix_lat
        if ctx_live:
            ctx = ctx + m_ctx[2] * mix_ctx
        h2_lat = modulate(x, m_lat[3], m_lat[4]).reshape(B * S, D)
        if ctx_live:
            h2_ctx = modulate(ctx, m_ctx[3], m_ctx[4]).reshape(B * L, D)
            y = grouped_moe(jnp.concatenate([h2_lat, h2_ctx], axis=0), router_w, router_b,
                            exp_w1[layer], exp_w3[layer], exp_w2[layer])
            x = x + m_lat[5] * y[:B * S].reshape(B, S, D)
            ctx = ctx + m_ctx[5] * y[B * S:].reshape(B, L, D)
        else:
            y = grouped_moe(h2_lat, router_w, router_b, exp_w1[layer], exp_w3[layer], exp_w2[layer])
            x = x + m_lat[5] * y.reshape(B, S, D)
    return x
```

```python
import functools

import jax
import jax.numpy as jnp
from jax import lax
from jax.experimental import pallas as pl
from jax.experimental.pallas import tpu as pltpu

F32 = jnp.float32
BF16 = jnp.bfloat16
I32 = jnp.int32

D_MODEL = 2048
N_BATCH = 4
SEQ_LEN = 2048
CTX_LEN = 256
N_TOK = N_BATCH * SEQ_LEN
GRID_W = 64
HEAD_DIM = 128
A_Q_HEADS = 8
A_KV_HEADS = 2
A_GROUP = A_Q_HEADS // A_KV_HEADS
WINDOW = 128
A_BLOCK = 128
ROPE_THETA = 10000.0
GLA_HEADS = 4
GLA_DK = 128
GLA_DV = 256
GLA_RANK = 16
GLA_TAU = 16.0
GLA_CHUNK = 64
N_EXPERTS = 32
N_GROUPS = 4
GROUP_SIZE = N_EXPERTS // N_GROUPS
D_EXPERT = 1024
EPS = 1e-6
NEG_INF = -1e30
A_Q_DIM = A_Q_HEADS * HEAD_DIM
A_KV_DIM = A_KV_HEADS * HEAD_DIM
GLA_K_DIM = GLA_HEADS * GLA_DK
GLA_V_DIM = GLA_HEADS * GLA_DV
PROJ_DIM = A_Q_DIM + 2 * A_KV_DIM + 2 * GLA_K_DIM + 2 * GLA_V_DIM
LR_PAD = 128

COL_AQ = 0
COL_GV = 1
COL_GG = 2
COL_GQ = 6
COL_GK = 7
COL_AK = 32
COL_AV = 34

MOD_TN = 1024
PROJ_TM = 512
PROJ_TN = 512
OUT_TM = 256
GLA_ROWS = 256
ROUTE_TT = 512
MOE_BLOCK = 256
MOE_SLOTS = N_TOK * 2 + N_EXPERTS * MOE_BLOCK
MOE_NBLK = MOE_SLOTS // MOE_BLOCK
TOK_TM = 256
META_LANES = 128
VMEM_LIMIT = 56 * 1024 * 1024


def _cparams(sem):
    return pltpu.CompilerParams(dimension_semantics=sem, vmem_limit_bytes=VMEM_LIMIT)


def _silu(x):
    return x * jax.nn.sigmoid(x)


def _dot(a, b):
    return jnp.dot(a, b, preferred_element_type=F32)


def _dot_nt(a, b):
    return lax.dot_general(a, b, (((1,), (1,)), ((), ())), preferred_element_type=F32)


def _dot_tn(a, b):
    return lax.dot_general(a, b, (((0,), (0,)), ((), ())), preferred_element_type=F32)


def _rms_modulate(x, shift, scale):
    r = lax.rsqrt(jnp.mean(x * x, axis=-1, keepdims=True) + EPS)
    return (x * r) * (1.0 + scale) + shift


def _mod_kernel(cc_ref, w_ref, b_ref, o_ref):
    a = _silu(cc_ref[...])
    o_ref[0] = _dot(a.astype(BF16), w_ref[0].astype(BF16)) + b_ref[0]


def _modulation(cc, mod_w, mod_b):
    depth, d, n = mod_w.shape
    return pl.pallas_call(
        _mod_kernel,
        out_shape=jax.ShapeDtypeStruct((depth, 8, n), F32),
        grid=(depth, n // MOD_TN),
        in_specs=[
            pl.BlockSpec((8, d), lambda l, j: (0, 0)),
            pl.BlockSpec((1, d, MOD_TN), lambda l, j: (l, 0, j)),
            pl.BlockSpec((1, 1, MOD_TN), lambda l, j: (l, 0, j)),
        ],
        out_specs=pl.BlockSpec((1, 8, MOD_TN), lambda l, j: (l, 0, j)),
        compiler_params=_cparams(("arbitrary", "arbitrary")),
        name="adaln_modulation",
    )(cc, mod_w, mod_b.reshape(depth, 1, n))


def _proj_kernel(with_lr, x_ref, sh_ref, sc_ref, w_ref, *rest):
    if with_lr:
        wlr_ref, o_ref, olr_ref, hn_ref = rest
    else:
        o_ref, hn_ref = rest

    @pl.when(pl.program_id(1) == 0)
    def _():
        hb = _rms_modulate(x_ref[...], sh_ref[0], sc_ref[0]).astype(BF16)
        hn_ref[...] = hb
        if with_lr:
            olr_ref[...] = _dot(hb, wlr_ref[...])

    o_ref[...] = _dot(hn_ref[...], w_ref[...])


def _projection(x2d, shift, scale, w, w_lr, mod_row):
    rows, d = x2d.shape
    n = w.shape[1]
    with_lr = w_lr is not None
    in_specs = [
        pl.BlockSpec((PROJ_TM, d), lambda i, j: (i, 0)),
        pl.BlockSpec((1, 1, d), lambda i, j: (mod_row(i), 0, 0)),
        pl.BlockSpec((1, 1, d), lambda i, j: (mod_row(i), 0, 0)),
        pl.BlockSpec((d, PROJ_TN), lambda i, j: (0, j)),
    ]
    out_shape = [jax.ShapeDtypeStruct((rows, n), F32)]
    out_specs = [pl.BlockSpec((PROJ_TM, PROJ_TN), lambda i, j: (i, j))]
    args = [x2d, shift, scale, w]
    if with_lr:
        in_specs.append(pl.BlockSpec((d, LR_PAD), lambda i, j: (0, 0)))
        out_shape.append(jax.ShapeDtypeStruct((rows, LR_PAD), F32))
        out_specs.append(pl.BlockSpec((PROJ_TM, LR_PAD), lambda i, j: (i, 0)))
        args.append(w_lr)
    res = pl.pallas_call(
        functools.partial(_proj_kernel, with_lr),
        out_shape=out_shape,
        grid=(rows // PROJ_TM, n // PROJ_TN),
        in_specs=in_specs,
        out_specs=out_specs,
        scratch_shapes=[pltpu.VMEM((PROJ_TM, d), BF16)],
        compiler_params=_cparams(("arbitrary", "arbitrary")),
        name="norm_mod_projection",
    )(*args)
    return res if with_lr else res[0]


def _swap_halves32(x):
    lane = lax.broadcasted_iota(I32, x.shape, 1)
    return jnp.where((lane & 63) < 32, pltpu.roll(x, 96, 1), pltpu.roll(x, 32, 1))


def _qk_norm(x, gain):
    return x * lax.rsqrt(jnp.mean(x * x, axis=-1, keepdims=True) + EPS) * gain


def _rope(x, cos, sin_signed):
    return x * cos + _swap_halves32(x) * sin_signed


def _attn_kernel(sink_ref, q_ref, kp_ref, kc_ref, kn_ref, vp_ref, vc_ref, vn_ref, kx_ref, vx_ref,
                 cos_ref, sin_ref, qg_ref, kg_ref, o_ref):
    kvh = pl.program_id(1)
    n = pl.program_id(2)
    nb = pl.num_programs(2)

    def table(ref, blk):
        return ref[pl.ds(pl.multiple_of(blk * A_BLOCK, A_BLOCK), A_BLOCK), :]

    prev = jnp.maximum(n - 1, 0)
    nxt = jnp.minimum(n + 1, nb - 1)
    q_gain = qg_ref[...]
    k_gain = kg_ref[...]
    cos_q, sin_q = table(cos_ref, n), table(sin_ref, n)

    q = q_ref[0]
    qs = []
    for g in range(A_GROUP):
        qh = _rope(_qk_norm(q[:, g * HEAD_DIM:(g + 1) * HEAD_DIM], q_gain), cos_q, sin_q)
        qs.append((qh * HEAD_DIM ** -0.5).astype(BF16))
    qs = jnp.concatenate(qs, axis=0)

    keys = [_qk_norm(kx_ref[0], k_gain)]
    for ref, blk in ((kp_ref, prev), (kc_ref, n), (kn_ref, nxt)):
        keys.append(_rope(_qk_norm(ref[0], k_gain), table(cos_ref, blk), table(sin_ref, blk)))
    keys = jnp.concatenate(keys, axis=0).astype(BF16)
    vals = jnp.concatenate([vx_ref[0], vp_ref[0], vc_ref[0], vn_ref[0]], axis=0).astype(BF16)

    nk = CTX_LEN + 3 * A_BLOCK
    rows = A_GROUP * A_BLOCK
    s = _dot_nt(qs, keys)
    ri = lax.broadcasted_iota(I32, (rows, nk), 0)
    cj = lax.broadcasted_iota(I32, (rows, nk), 1)
    qi = ri & (A_BLOCK - 1)
    wj = cj - CTX_LEN
    kpos = (n - 1) * A_BLOCK + wj
    in_band = (jnp.abs(wj - A_BLOCK - qi) <= WINDOW) & (kpos >= 0) & (kpos < nb * A_BLOCK)
    s = jnp.where((cj < CTX_LEN) | in_band, s, NEG_INF)

    rcol = lax.broadcasted_iota(I32, (rows, 1), 0)
    sink = jnp.zeros((rows, 1), F32)
    for g in range(A_GROUP):
        sink = jnp.where(rcol // A_BLOCK == g, sink_ref[kvh * A_GROUP + g], sink)
    m = jnp.maximum(jnp.max(s, axis=-1, keepdims=True), sink)
    p = jnp.exp(s - m)
    denom = jnp.sum(p, axis=-1, keepdims=True) + jnp.exp(sink - m)
    o = _dot(p.astype(BF16), vals) / denom
    for g in range(A_GROUP):
        o_ref[0, :, g * HEAD_DIM:(g + 1) * HEAD_DIM] = o[g * A_BLOCK:(g + 1) * A_BLOCK].astype(BF16)


def _attention(p_lat, p_ctx, cos, sin_signed, q_gain, k_gain, sink):
    nb = SEQ_LEN // A_BLOCK
    gw = A_GROUP * HEAD_DIM

    def blk3(width, col, which):
        def imap(b, k, n, s):
            r = {"prev": jnp.maximum(n - 1, 0), "cur": n, "next": jnp.minimum(n + 1, nb - 1)}[which]
            return (b, r, col + k)
        return pl.BlockSpec((1, A_BLOCK, width), imap)

    def ctx_blk(col):
        return pl.BlockSpec((1, CTX_LEN, HEAD_DIM), lambda b, k, n, s: (b, 0, col + k))

    full = lambda shape: pl.BlockSpec(shape, lambda b, k, n, s: (0,) * len(shape))
    return pl.pallas_call(
        _attn_kernel,
        out_shape=jax.ShapeDtypeStruct((N_BATCH, SEQ_LEN, A_Q_DIM), BF16),
        grid_spec=pltpu.PrefetchScalarGridSpec(
            num_scalar_prefetch=1,
            grid=(N_BATCH, A_KV_HEADS, nb),
            in_specs=[
                blk3(gw, COL_AQ, "cur"),
                blk3(HEAD_DIM, COL_AK, "prev"), blk3(HEAD_DIM, COL_AK, "cur"), blk3(HEAD_DIM, COL_AK, "next"),
                blk3(HEAD_DIM, COL_AV, "prev"), blk3(HEAD_DIM, COL_AV, "cur"), blk3(HEAD_DIM, COL_AV, "next"),
                ctx_blk(COL_AK), ctx_blk(COL_AV),
                full((SEQ_LEN, HEAD_DIM)), full((SEQ_LEN, HEAD_DIM)),
                full((1, HEAD_DIM)), full((1, HEAD_DIM)),
            ],
            out_specs=pl.BlockSpec((1, A_BLOCK, gw), lambda b, k, n, s: (b, n, k)),
        ),
        compiler_params=_cparams(("arbitrary", "arbitrary", "arbitrary")),
        name="windowed_sink_attention",
    )(sink, p_lat, p_lat, p_lat, p_lat, p_lat, p_lat, p_lat, p_ctx, p_ctx, cos, sin_signed, q_gain, k_gain)


def _split3_bf16(x):
    hi = x.astype(BF16)
    r1 = x - hi.astype(F32)
    mid = r1.astype(BF16)
    lo = (r1 - mid.astype(F32)).astype(BF16)
    return hi, mid, lo


def _gla_kernel(latent, nblk, q_ref, k_ref, v_ref, *rest):
    if latent:
        g_ref, lr_ref, wa_ref, ba_ref, gain_ref, s0_ref, o_ref, st_ref, ofwd_ref = rest
    else:
        lr_ref, wa_ref, ba_ref, s0_ref, o_ref, st_ref = rest
    d = pl.program_id(1)
    j = pl.program_id(2)
    nc = GLA_ROWS // GLA_CHUNK
    blk = jnp.where(d == 0, j, nblk - 1 - j)

    @pl.when(j == 0)
    def _():
        st_ref[...] = s0_ref[0, 0]

    ii = lax.broadcasted_iota(I32, (GLA_CHUNK, GLA_CHUNK), 0)
    jj = lax.broadcasted_iota(I32, (GLA_CHUNK, GLA_CHUNK), 1)
    incl = ((jj - ii) * (1 - 2 * d)) <= 0
    incl_b = jnp.where(incl, 1.0, 0.0).astype(BF16)

    for i in range(nc):
        ci = jnp.where(d == 0, i, nc - 1 - i)
        r0 = pl.multiple_of(ci * GLA_CHUNK, GLA_CHUNK)
        rows = pl.ds(r0, GLA_CHUNK)
        z = _dot(lr_ref[0, rows, :].astype(BF16), wa_ref[0]) + ba_ref[0]
        la = (jnp.minimum(z, 0.0) - jnp.log(1.0 + jnp.exp(-jnp.abs(z)))) / GLA_TAU
        hi, mid, lo = _split3_bf16(la)
        bcum = _dot(incl_b, hi) + _dot(incl_b, mid) + _dot(incl_b, lo)
        blast = jnp.sum(la, axis=0, keepdims=True)
        k = k_ref[0, rows, :]
        kl = (k * jnp.exp(blast - bcum)).astype(BF16)
        decay = jnp.exp(blast)
        v = v_ref[0, rows, :].astype(BF16)
        if latent:
            qf = (q_ref[0, rows, :] * GLA_DK ** -0.5 * jnp.exp(bcum)).astype(BF16)
            kf = (k * jnp.exp(-bcum)).astype(BF16)
        for h in range(GLA_HEADS):
            ks = slice(h * GLA_DK, (h + 1) * GLA_DK)
            vs = slice(h * GLA_DV, (h + 1) * GLA_DV)
            st = st_ref[h]
            if latent:
                att = jnp.where(incl, _dot_nt(qf[:, ks], kf[:, ks]), 0.0)
                o = _dot(att.astype(BF16), v[:, vs]) + _dot_nt(qf[:, ks], st.astype(BF16))
                orow = pl.ds(pl.multiple_of(blk * GLA_ROWS + r0, GLA_CHUNK), GLA_CHUNK)

                @pl.when(d == 0)
                def _():
                    ofwd_ref[orow, vs] = o

                @pl.when(d == 1)
                def _():
                    ot = ofwd_ref[orow, vs] + o
                    on = ot * lax.rsqrt(jnp.mean(ot * ot, axis=-1, keepdims=True) + EPS) * gain_ref[...]
                    o_ref[0, rows, vs] = (on * _silu(g_ref[0, rows, vs])).astype(BF16)

            st_ref[h] = st * decay[:, ks] + _dot_tn(v[:, vs], kl[:, ks])

    if not latent:
        @pl.when(j == nblk - 1)
        def _():
            o_ref[0, 0] = st_ref[...]


def _gla(p3, lr3, wa_pad, ba, o_gain, s0, latent):
    n = p3.shape[1]
    nblk = n // GLA_ROWS
    seq_blk = lambda d, j: jnp.where(d == 0, j, nblk - 1 - j)
    in_specs = [
        pl.BlockSpec((1, GLA_ROWS, GLA_K_DIM), lambda b, d, j: (b, seq_blk(d, j), COL_GQ)),
        pl.BlockSpec((1, GLA_ROWS, GLA_K_DIM), lambda b, d, j: (b, seq_blk(d, j), COL_GK)),
        pl.BlockSpec((1, GLA_ROWS, GLA_V_DIM), lambda b, d, j: (b, seq_blk(d, j), COL_GV)),
    ]
    args = [p3, p3, p3]
    if latent:
        in_specs.append(pl.BlockSpec((1, GLA_ROWS, GLA_V_DIM), lambda b, d, j: (b, seq_blk(d, j), COL_GG)))
        args.append(p3)
    in_specs += [
        pl.BlockSpec((1, GLA_ROWS, LR_PAD), lambda b, d, j: (b, seq_blk(d, j), 0)),
        pl.BlockSpec((1, LR_PAD, GLA_K_DIM), lambda b, d, j: (d, 0, 0)),
        pl.BlockSpec((1, 1, GLA_K_DIM), lambda b, d, j: (d, 0, 0)),
    ]
    args += [lr3, wa_pad, ba]
    if latent:
        in_specs.append(pl.BlockSpec((1, GLA_DV), lambda b, d, j: (0, 0)))
        args.append(o_gain)
    state_spec = pl.BlockSpec((1, 1, GLA_HEADS, GLA_DV, GLA_DK), lambda b, d, j: (b, d, 0, 0, 0))
    in_specs.append(state_spec)
    args.append(s0)
    scratch = [pltpu.VMEM((GLA_HEADS, GLA_DV, GLA_DK), F32)]
    if latent:
        out_shape = jax.ShapeDtypeStruct((N_BATCH, n, GLA_V_DIM), BF16)
        out_spec = pl.BlockSpec((1, GLA_ROWS, GLA_V_DIM),
                                lambda b, d, j: (b, jnp.where(d == 0, nblk - 1, nblk - 1 - j), 0))
        scratch.append(pltpu.VMEM((n, GLA_V_DIM), F32))
    else:
        out_shape = jax.ShapeDtypeStruct(s0.shape, F32)
        out_spec = state_spec
    return pl.pallas_call(
        functools.partial(_gla_kernel, latent, nblk),
        out_shape=out_shape,
        grid=(N_BATCH, 2, nblk),
        in_specs=in_specs,
        out_specs=out_spec,
        scratch_shapes=scratch,
        compiler_params=_cparams(("arbitrary", "arbitrary", "arbitrary")),
        name="gla_latent" if latent else "gla_context_state",
    )(*args)


def _router_logits_t(h, rwt_ref):
    hh = h.astype(BF16)
    hl = (h - hh.astype(F32)).astype(BF16)
    rw = rwt_ref[...]
    rh = rw.astype(BF16)
    rl = (rw - rh.astype(F32)).astype(BF16)
    return _dot_nt(rh, hh) + _dot_nt(rh, hl) + _dot_nt(rl, hh)


def _residual_and_moe_input(mix, x_ref, gate_ref, sh_ref, sc_ref, rwt_ref, x_out_ref, h_out_ref, lg_ref):
    x1 = x_ref[...] + gate_ref[0] * mix
    x_out_ref[...] = x1
    h = _rms_modulate(x1, sh_ref[0], sc_ref[0])
    h_out_ref[...] = h
    lg_ref[...] = _router_logits_t(h, rwt_ref)


def _attn_out_kernel(a1_ref, a2_ref, w_ref, x_ref, gate_ref, sh_ref, sc_ref, rwt_ref, x_out_ref, h_out_ref, lg_ref):
    mix = _dot(a1_ref[...], w_ref[0:A_Q_DIM, :]) + _dot(a2_ref[...], w_ref[A_Q_DIM:, :])
    _residual_and_moe_input(mix, x_ref, gate_ref, sh_ref, sc_ref, rwt_ref, x_out_ref, h_out_ref, lg_ref)


def _conv_out_kernel(gb_ref, gc_ref, u_ref, gcp_ref, up_ref, gcn_ref, un_ref, cw_ref, w_ref,
                     x_ref, gate_ref, sh_ref, sc_ref, rwt_ref, x_out_ref, h_out_ref, lg_ref):
    i = pl.program_id(0)
    tiles_per_seq = SEQ_LEN // OUT_TM
    t = gc_ref[...] * u_ref[...]
    first = (i % tiles_per_seq) == 0
    last = (i % tiles_per_seq) == tiles_per_seq - 1
    t_before = jnp.where(first, 0.0, gcp_ref[7:8, :] * up_ref[7:8, :])
    t_after = jnp.where(last, 0.0, gcn_ref[0:1, :] * un_ref[0:1, :])
    row = lax.broadcasted_iota(I32, t.shape, 0)
    t_up = jnp.where(row == 0, t_before, pltpu.roll(t, 1, 0))
    t_dn = jnp.where(row == OUT_TM - 1, t_after, pltpu.roll(t, OUT_TM - 1, 0))
    y = cw_ref[0:1, :] * t_up + cw_ref[1:2, :] * t + cw_ref[2:3, :] * t_dn
    mix = _dot((gb_ref[...] * y).astype(BF16), w_ref[...])
    _residual_and_moe_input(mix, x_ref, gate_ref, sh_ref, sc_ref, rwt_ref, x_out_ref, h_out_ref, lg_ref)


def _mixer_out(kernel_fn, mixer_specs, mixer_args, x2d, gate, shift, scale, rwt, name):
    d = D_MODEL
    mod_spec = pl.BlockSpec((1, 1, d), lambda i: (i * OUT_TM // SEQ_LEN, 0, 0))
    row_spec = pl.BlockSpec((OUT_TM, d), lambda i: (i, 0))
    return pl.pallas_call(
        kernel_fn,
        out_shape=[jax.ShapeDtypeStruct((N_TOK, d), F32), jax.ShapeDtypeStruct((N_TOK, d), F32),
                   jax.ShapeDtypeStruct((N_EXPERTS, N_TOK), F32)],
        grid=(N_TOK // OUT_TM,),
        in_specs=mixer_specs + [row_spec, mod_spec, mod_spec, mod_spec,
                                pl.BlockSpec((N_EXPERTS, d), lambda i: (0, 0))],
        out_specs=[row_spec, row_spec, pl.BlockSpec((N_EXPERTS, OUT_TM), lambda i: (0, i))],
        compiler_params=_cparams(("arbitrary",)),
        name=name,
    )(*mixer_args, x2d, gate, shift, scale, rwt)


def _attn_out(attn2d, gla2d, w_out, x2d, gate, shift, scale, rwt):
    half = pl.BlockSpec((OUT_TM, A_Q_DIM), lambda i: (i, 0))
    specs = [half, half, pl.BlockSpec((D_MODEL, D_MODEL), lambda i: (0, 0))]
    return _mixer_out(_attn_out_kernel, specs, [attn2d, gla2d, w_out], x2d, gate, shift, scale, rwt,
                      "attn_out_projection")


def _conv_out(g2d, conv_w, w_out, x2d, gate, shift, scale, rwt):
    d = D_MODEL
    sub = OUT_TM // 8
    last8 = N_TOK // 8 - 1
    main = lambda col: pl.BlockSpec((OUT_TM, d), lambda i: (i, col))
    before = lambda col: pl.BlockSpec((8, d), lambda i: (jnp.maximum(i * sub - 1, 0), col))
    after = lambda col: pl.BlockSpec((8, d), lambda i: (jnp.minimum((i + 1) * sub, last8), col))
    specs = [main(0), main(1), main(2), before(1), before(2), after(1), after(2),
             pl.BlockSpec((3, d), lambda i: (0, 0)), pl.BlockSpec((d, d), lambda i: (0, 0))]
    return _mixer_out(_conv_out_kernel, specs, [g2d] * 7 + [conv_w, w_out], x2d, gate, shift, scale, rwt,
                      "conv_out_projection")


def _first_argmax8(x, idx8):
    m = jnp.max(x, axis=0, keepdims=True)
    a = jnp.min(jnp.where(x == m, idx8, float(GROUP_SIZE)), axis=0, keepdims=True)
    return m, a


def _route_kernel(lg_ref, rb_ref, e_ref, w_ref, dest_ref, meta_ref, cnt_ref, carry_ref, pstart_ref):
    phase = pl.program_id(0)
    j = pl.program_id(1)
    tt = ROUTE_TT
    sc = jax.nn.sigmoid(lg_ref[...])
    grp = sc + rb_ref[...]
    idx8 = lax.broadcasted_iota(I32, (GROUP_SIZE, tt), 0).astype(F32)
    groups = [grp[g * GROUP_SIZE:(g + 1) * GROUP_SIZE] for g in range(N_GROUPS)]
    gscore = []
    for x in groups:
        m1, a1 = _first_argmax8(x, idx8)
        m2, _ = _first_argmax8(jnp.where(idx8 == a1, -jnp.inf, x), idx8)
        gscore.append(m1 + m2)
    gmax = functools.reduce(jnp.maximum, gscore)
    gsel = jnp.full((1, tt), float(N_GROUPS), F32)
    for g in reversed(range(N_GROUPS)):
        gsel = jnp.where(gscore[g] == gmax, float(g), gsel)
    in_grp = groups[0]
    for g in range(1, N_GROUPS):
        in_grp = jnp.where(gsel == float(g), groups[g], in_grp)
    _, a1 = _first_argmax8(in_grp, idx8)
    _, a2 = _first_argmax8(jnp.where(idx8 == a1, -jnp.inf, in_grp), idx8)
    e0 = gsel * GROUP_SIZE + a1
    e1 = gsel * GROUP_SIZE + a2
    idx32 = lax.broadcasted_iota(I32, (N_EXPERTS, tt), 0).astype(F32)
    oh0 = idx32 == e0
    oh1 = idx32 == e1
    w0 = jnp.sum(jnp.where(oh0, sc, 0.0), axis=0, keepdims=True)
    w1 = jnp.sum(jnp.where(oh1, sc, 0.0), axis=0, keepdims=True)
    wsum = w0 + w1
    ohs = jnp.where(oh0 | oh1, 1.0, 0.0)
    tile_cnt = jnp.sum(ohs, axis=1, keepdims=True)

    @pl.when(phase == 0)
    def _():
        @pl.when(j == 0)
        def _():
            cnt_ref[...] = jnp.zeros_like(cnt_ref)
        cnt_ref[...] += jnp.broadcast_to(tile_cnt, cnt_ref.shape)

    @pl.when(phase == 1)
    def _():
        @pl.when(j == 0)
        def _():
            cnt = cnt_ref[...].astype(I32)
            pc = ((cnt + (MOE_BLOCK - 1)) // MOE_BLOCK) * MOE_BLOCK
            r = lax.broadcasted_iota(I32, (N_EXPERTS, N_EXPERTS), 0)
            c = lax.broadcasted_iota(I32, (N_EXPERTS, N_EXPERTS), 1)
            strict_lower = jnp.where(c < r, 1.0, 0.0).astype(BF16)
            pcf = pc.astype(F32)
            pstart = _dot(strict_lower, pcf.astype(BF16))
            pstart_ref[...] = pstart
            carry_ref[...] = jnp.zeros_like(carry_ref)
            pends = pstart + pcf
            lane = lax.broadcasted_iota(I32, pends.shape, 1).astype(F32)
            blk_exp = jnp.sum(jnp.where(lane * MOE_BLOCK >= pends, 1.0, 0.0), axis=0, keepdims=True)
            blk_exp = jnp.minimum(blk_exp, float(N_EXPERTS - 1))
            n_used = jnp.sum(pcf, axis=0, keepdims=True) / MOE_BLOCK
            meta = jnp.where(lane[0:1] == float(META_LANES - 1), n_used, blk_exp)
            meta_ref[...] = meta.astype(I32)

        rr = lax.broadcasted_iota(I32, (tt, tt), 0)
        cc = lax.broadcasted_iota(I32, (tt, tt), 1)
        strict_upper = jnp.where(rr < cc, 1.0, 0.0).astype(BF16)
        before = _dot(ohs.astype(BF16), strict_upper)
        pos = before + carry_ref[:, 0:1] + pstart_ref[:, 0:1]
        d0 = jnp.sum(jnp.where(oh0, pos, 0.0), axis=0, keepdims=True)
        d1 = jnp.sum(jnp.where(oh1, pos, 0.0), axis=0, keepdims=True)
        carry_ref[...] += jnp.broadcast_to(tile_cnt, carry_ref.shape)
        e_ref[...] = jnp.concatenate([e0, e1], axis=0).astype(I32)
        w_ref[...] = jnp.concatenate([w0 / wsum, w1 / wsum], axis=0)
        dest_ref[...] = jnp.concatenate([d0, d1], axis=0).astype(I32)


def _route(logits_t, router_b):
    nt = N_TOK // ROUTE_TT
    tok = lambda dt: jax.ShapeDtypeStruct((2, N_TOK), dt)
    tok_spec = pl.BlockSpec((2, ROUTE_TT), lambda p, j: (0, j * p))
    return pl.pallas_call(
        _route_kernel,
        out_shape=[tok(I32), tok(F32), tok(I32), jax.ShapeDtypeStruct((1, META_LANES), I32)],
        grid=(2, nt),
        in_specs=[pl.BlockSpec((N_EXPERTS, ROUTE_TT), lambda p, j: (0, j)),
                  pl.BlockSpec((N_EXPERTS, 1), lambda p, j: (0, 0))],
        out_specs=[tok_spec, tok_spec, tok_spec, pl.BlockSpec((1, META_LANES), lambda p, j: (0, 0))],
        scratch_shapes=[pltpu.VMEM((N_EXPERTS, 128), F32), pltpu.VMEM((N_EXPERTS, 128), F32),
                        pltpu.VMEM((N_EXPERTS, 128), F32)],
        compiler_params=_cparams(("arbitrary", "arbitrary")),
        name="router_slot_assignment",
    )(logits_t, router_b.reshape(N_EXPERTS, 1))


def _dispatch_copy(dest_ref, h_ref, xs_ref, sem, i, r, k):
    dst = dest_ref[k * N_TOK + i * TOK_TM + r]
    return pltpu.make_async_copy(h_ref.at[pl.ds(r, 1)], xs_ref.at[pl.ds(dst, 1)], sem)


def _dispatch_kernel(dest_ref, h_ref, xs_in_ref, xs_ref, sem):
    del xs_in_ref
    i = pl.program_id(0)

    def start(r, carry):
        for k in range(2):
            _dispatch_copy(dest_ref, h_ref, xs_ref, sem, i, r, k).start()
        return carry

    def wait(r, carry):
        for k in range(2):
            _dispatch_copy(dest_ref, h_ref, xs_ref, sem, i, r, k).wait()
        return carry

    lax.fori_loop(0, TOK_TM, start, 0)
    lax.fori_loop(0, TOK_TM, wait, 0)


def _dispatch(dest_flat, h2d):
    d = D_MODEL
    xs0 = jnp.zeros((MOE_SLOTS, d), F32)
    return pl.pallas_call(
        _dispatch_kernel,
        out_shape=jax.ShapeDtypeStruct((MOE_SLOTS, d), F32),
        grid_spec=pltpu.PrefetchScalarGridSpec(
            num_scalar_prefetch=1,
            grid=(N_TOK // TOK_TM,),
            in_specs=[pl.BlockSpec((TOK_TM, d), lambda i, dest: (i, 0)),
                      pl.BlockSpec(memory_space=pl.ANY)],
            out_specs=pl.BlockSpec(memory_space=pl.ANY),
            scratch_shapes=[pltpu.SemaphoreType.DMA],
        ),
        input_output_aliases={2: 0},
        compiler_params=_cparams(("arbitrary",)),
        name="moe_dispatch",
    )(dest_flat, h2d, xs0)


def _expert_kernel(meta_ref, x_ref, w1_ref, w3_ref, w2_ref, y_ref):
    used = pl.program_id(0) < meta_ref[META_LANES - 1]

    @pl.when(used)
    def _():
        x = x_ref[...].astype(BF16)
        a = _silu(_dot(x, w1_ref[0])) * _dot(x, w3_ref[0])
        y_ref[...] = _dot(a.astype(BF16), w2_ref[0])

    @pl.when(jnp.logical_not(used))
    def _():
        y_ref[...] = jnp.zeros_like(y_ref)


def _experts(meta, xs, w1, w3, w2):
    d = D_MODEL
    used = lambda i, meta: jnp.minimum(i, meta[META_LANES - 1] - 1)
    row_spec = pl.BlockSpec((MOE_BLOCK, d), lambda i, meta: (used(i, meta), 0))
    w_in_spec = pl.BlockSpec((1, d, D_EXPERT), lambda i, meta: (meta[used(i, meta)], 0, 0))
    w_out_spec = pl.BlockSpec((1, D_EXPERT, d), lambda i, meta: (meta[used(i, meta)], 0, 0))
    return pl.pallas_call(
        _expert_kernel,
        out_shape=jax.ShapeDtypeStruct((MOE_SLOTS, d), F32),
        grid_spec=pltpu.PrefetchScalarGridSpec(
            num_scalar_prefetch=1,
            grid=(MOE_NBLK,),
            in_specs=[row_spec, w_in_spec, w_in_spec, w_out_spec],
            out_specs=pl.BlockSpec((MOE_BLOCK, d), lambda i, meta: (i, 0)),
        ),
        compiler_params=_cparams(("arbitrary",)),
        name="moe_experts",
    )(meta, xs, w1, w3, w2)


def _combine_copy(dest_ref, yb_ref, buf_ref, sem, i, r, k):
    src = dest_ref[k * N_TOK + i * TOK_TM + r]
    return pltpu.make_async_copy(yb_ref.at[pl.ds(src, 1)], buf_ref.at[k, pl.ds(r, 1)], sem)


def _combine_kernel(dest_ref, yb_ref, x_ref, gate_ref, w_ref, o_ref, buf_ref, sem):
    i = pl.program_id(0)

    def start(r, carry):
        for k in range(2):
            _combine_copy(dest_ref, yb_ref, buf_ref, sem, i, r, k).start()
        return carry

    def wait(r, carry):
        for k in range(2):
            _combine_copy(dest_ref, yb_ref, buf_ref, sem, i, r, k).wait()
        return carry

    lax.fori_loop(0, TOK_TM, start, 0)
    lax.fori_loop(0, TOK_TM, wait, 0)
    w = w_ref[...]
    y = w[:, 0:1] * buf_ref[0] + w[:, 1:2] * buf_ref[1]
    o_ref[...] = x_ref[...] + gate_ref[0] * y


def _combine(dest_flat, yb, x2d, gate, w_tok):
    d = D_MODEL
    return pl.pallas_call(
        _combine_kernel,
        out_shape=jax.ShapeDtypeStruct((N_TOK, d), F32),
        grid_spec=pltpu.PrefetchScalarGridSpec(
            num_scalar_prefetch=1,
            grid=(N_TOK // TOK_TM,),
            in_specs=[pl.BlockSpec(memory_space=pl.ANY),
                      pl.BlockSpec((TOK_TM, d), lambda i, dest: (i, 0)),
                      pl.BlockSpec((1, 1, d), lambda i, dest: (i * TOK_TM // SEQ_LEN, 0, 0)),
                      pl.BlockSpec((TOK_TM, 2), lambda i, dest: (i, 0))],
            out_specs=pl.BlockSpec((TOK_TM, d), lambda i, dest: (i, 0)),
            scratch_shapes=[pltpu.VMEM((2, TOK_TM, d), F32), pltpu.SemaphoreType.DMA],
        ),
        compiler_params=_cparams(("arbitrary",)),
        name="moe_combine",
    )(dest_flat, yb, x2d, gate, w_tok)


def _moe(h2d, logits_t, x2d, gate, router_b, w1, w3, w2):
    _, w_sel, dest, meta = _route(logits_t, router_b)
    dest_flat = dest.reshape(2 * N_TOK)
    xs = _dispatch(dest_flat, h2d)
    yb = _experts(meta.reshape(META_LANES), xs, w1.astype(BF16), w3.astype(BF16), w2.astype(BF16))
    return _combine(dest_flat, yb, x2d, gate, w_sel.T)


def _rope_tables():
    f = HEAD_DIM // 4
    inv_freq = ROPE_THETA ** (-jnp.arange(f, dtype=F32) / f)
    pos = jnp.arange(SEQ_LEN)
    row = (pos // GRID_W).astype(F32)[:, None] * inv_freq
    col = (pos % GRID_W).astype(F32)[:, None] * inv_freq
    cos = jnp.concatenate([jnp.cos(row), jnp.cos(row), jnp.cos(col), jnp.cos(col)], axis=-1)
    sin = jnp.concatenate([-jnp.sin(row), jnp.sin(row), -jnp.sin(col), jnp.sin(col)], axis=-1)
    return cos, sin


def _permute_attn_w_in(w):
    o = [0, A_Q_DIM, A_KV_DIM, A_KV_DIM, GLA_K_DIM, GLA_K_DIM, GLA_V_DIM, GLA_V_DIM]
    s = [sum(o[:i + 1]) for i in range(len(o))]
    seg = lambda i: w[:, s[i]:s[i + 1]] if i + 1 < len(s) else w[:, s[i]:]
    aq, ak, av, gq, gk, gv, gg = (seg(i) for i in range(7))
    main = jnp.concatenate([aq, gv, gg, gq, gk, ak, av], axis=1).astype(BF16)
    lr = jnp.pad(w[:, PROJ_DIM:], ((0, 0), (0, LR_PAD - 2 * GLA_RANK))).astype(BF16)
    return main, lr


def kernel(x, c, ctx, c_ctx, mod_w, mod_b, attn_w_in, attn_q_norm, attn_k_norm, attn_sink, gla_wa2, gla_ba,
           gla_norm, attn_w_out, conv_w_in, conv_w, conv_w_out, router_w, router_b, exp_w1, exp_w3, exp_w2):
    d = D_MODEL
    cc = jnp.concatenate([c, c_ctx[None], jnp.zeros((8 - N_BATCH - 1, d), F32)], axis=0)
    m = _modulation(cc, mod_w, mod_b)
    mods = [[m[l, :, i * d:(i + 1) * d].reshape(8, 1, d) for i in range(6)] for l in range(mod_w.shape[0])]
    lat_row = lambda i: i * PROJ_TM // SEQ_LEN
    ctx_row = lambda i: N_BATCH
    rwt = router_w.T
    x2d = x.reshape(N_TOK, d)

    w_main, w_lr = _permute_attn_w_in(attn_w_in[0])
    p_lat, lr_lat = _projection(x2d, mods[0][0], mods[0][1], w_main, w_lr, lat_row)
    p_ctx, lr_ctx = _projection(ctx.reshape(N_BATCH * CTX_LEN, d), mods[0][0], mods[0][1], w_main, w_lr, ctx_row)
    p_lat = p_lat.reshape(N_BATCH, SEQ_LEN, PROJ_DIM)
    p_ctx = p_ctx.reshape(N_BATCH, CTX_LEN, PROJ_DIM)
    cos, sin_signed = _rope_tables()
    attn = _attention(p_lat, p_ctx, cos, sin_signed, attn_q_norm[0][None], attn_k_norm[0][None], attn_sink[0])
    wa_pad = jnp.zeros((2, LR_PAD, GLA_K_DIM), F32)
    for di in range(2):
        wa_pad = wa_pad.at[di, di * GLA_RANK:(di + 1) * GLA_RANK].set(gla_wa2[0, di])
    wa_pad = wa_pad.astype(BF16)
    ba = gla_ba[0].reshape(2, 1, GLA_K_DIM)
    s_zero = jnp.zeros((N_BATCH, 2, GLA_HEADS, GLA_DV, GLA_DK), F32)
    s_ctx = _gla(p_ctx, lr_ctx.reshape(N_BATCH, CTX_LEN, LR_PAD), wa_pad, ba, None, s_zero, latent=False)
    gla = _gla(p_lat, lr_lat.reshape(N_BATCH, SEQ_LEN, LR_PAD), wa_pad, ba, gla_norm[0][None], s_ctx, latent=True)
    x1, h2, lg = _attn_out(attn.reshape(N_TOK, A_Q_DIM), gla.reshape(N_TOK, GLA_V_DIM), attn_w_out[0].astype(BF16),
                           x2d, mods[0][2], mods[0][3], mods[0][4], rwt)
    x2 = _moe(h2, lg, x1, mods[0][5], router_b, exp_w1[0], exp_w3[0], exp_w2[0])

    g = _projection(x2, mods[1][0], mods[1][1], conv_w_in[0].astype(BF16), None, lat_row)
    x3, h2, lg = _conv_out(g, conv_w[0], conv_w_out[0].astype(BF16), x2, mods[1][2], mods[1][3], mods[1][4], rwt)
    x4 = _moe(h2, lg, x3, mods[1][5], router_b, exp_w1[1], exp_w3[1], exp_w2[1])
    return x4.reshape(N_BATCH, SEQ_LEN, d)
```

```python
import functools

import jax
import jax.numpy as jnp
from jax import lax
from jax.experimental import pallas as pl
from jax.experimental.pallas import tpu as pltpu

F32 = jnp.float32
BF16 = jnp.bfloat16
I32 = jnp.int32

D_MODEL = 2048
N_BATCH = 4
SEQ_LEN = 2048
CTX_LEN = 256
N_TOK = N_BATCH * SEQ_LEN
GRID_W = 64
HEAD_DIM = 128
A_Q_HEADS = 8
A_KV_HEADS = 2
A_GROUP = A_Q_HEADS // A_KV_HEADS
WINDOW = 128
A_BLOCK = 128
ROPE_THETA = 10000.0
GLA_HEADS = 4
GLA_DK = 128
GLA_DV = 256
GLA_RANK = 16
GLA_TAU = 16.0
GLA_CHUNK = 64
N_EXPERTS = 32
N_GROUPS = 4
GROUP_SIZE = N_EXPERTS // N_GROUPS
D_EXPERT = 1024
EPS = 1e-6
NEG_INF = -1e30
A_Q_DIM = A_Q_HEADS * HEAD_DIM
A_KV_DIM = A_KV_HEADS * HEAD_DIM
GLA_K_DIM = GLA_HEADS * GLA_DK
GLA_V_DIM = GLA_HEADS * GLA_DV
PROJ_DIM = A_Q_DIM + 2 * A_KV_DIM + 2 * GLA_K_DIM + 2 * GLA_V_DIM
LR_PAD = 128

COL_AQ = 0
COL_GV = 1
COL_GG = 2
COL_GQ = 6
COL_GK = 7
COL_AK = 32
COL_AV = 34

MOD_TN = 1024
PROJ_TM = 512
PROJ_TN = 512
OUT_TM = 256
GLA_ROWS = 256
ROUTE_TT = 512
MOE_BLOCK = 512
MOE_SUB = 256
MOE_FC = 256
MOE_SLOTS = N_TOK * 2 + N_EXPERTS * MOE_BLOCK
MOE_NBLK = MOE_SLOTS // MOE_BLOCK
TOK_TM = 256
META_LANES = 128
VMEM_LIMIT = 56 * 1024 * 1024


def _cparams(sem):
    return pltpu.CompilerParams(dimension_semantics=sem, vmem_limit_bytes=VMEM_LIMIT)


def _silu(x):
    return x * jax.nn.sigmoid(x)


def _dot(a, b):
    return jnp.dot(a, b, preferred_element_type=F32)


def _dot_nt(a, b):
    return lax.dot_general(a, b, (((1,), (1,)), ((), ())), preferred_element_type=F32)


def _dot_tn(a, b):
    return lax.dot_general(a, b, (((0,), (0,)), ((), ())), preferred_element_type=F32)


def _rms_modulate(x, shift, scale):
    r = lax.rsqrt(jnp.mean(x * x, axis=-1, keepdims=True) + EPS)
    return (x * r) * (1.0 + scale) + shift


def _mod_kernel(cc_ref, w_ref, b_ref, o_ref):
    a = _silu(cc_ref[...])
    o_ref[0] = _dot(a.astype(BF16), w_ref[0].astype(BF16)) + b_ref[0]


def _modulation(cc, mod_w, mod_b):
    depth, d, n = mod_w.shape
    return pl.pallas_call(
        _mod_kernel,
        out_shape=jax.ShapeDtypeStruct((depth, 8, n), F32),
        grid=(depth, n // MOD_TN),
        in_specs=[
            pl.BlockSpec((8, d), lambda l, j: (0, 0)),
            pl.BlockSpec((1, d, MOD_TN), lambda l, j: (l, 0, j)),
            pl.BlockSpec((1, 1, MOD_TN), lambda l, j: (l, 0, j)),
        ],
        out_specs=pl.BlockSpec((1, 8, MOD_TN), lambda l, j: (l, 0, j)),
        compiler_params=_cparams(("arbitrary", "arbitrary")),
        name="adaln_modulation",
    )(cc, mod_w, mod_b.reshape(depth, 1, n))


def _proj_kernel(with_lr, x_ref, sh_ref, sc_ref, w_ref, *rest):
    if with_lr:
        wlr_ref, o_ref, olr_ref, hn_ref = rest
    else:
        o_ref, hn_ref = rest

    @pl.when(pl.program_id(1) == 0)
    def _():
        hb = _rms_modulate(x_ref[...], sh_ref[0], sc_ref[0]).astype(BF16)
        hn_ref[...] = hb
        if with_lr:
            olr_ref[...] = _dot(hb, wlr_ref[...])

    o_ref[...] = _dot(hn_ref[...], w_ref[...])


def _projection(x2d, shift, scale, w, w_lr, mod_row):
    rows, d = x2d.shape
    n = w.shape[1]
    with_lr = w_lr is not None
    in_specs = [
        pl.BlockSpec((PROJ_TM, d), lambda i, j: (i, 0)),
        pl.BlockSpec((1, 1, d), lambda i, j: (mod_row(i), 0, 0)),
        pl.BlockSpec((1, 1, d), lambda i, j: (mod_row(i), 0, 0)),
        pl.BlockSpec((d, PROJ_TN), lambda i, j: (0, j)),
    ]
    out_shape = [jax.ShapeDtypeStruct((rows, n), F32)]
    out_specs = [pl.BlockSpec((PROJ_TM, PROJ_TN), lambda i, j: (i, j))]
    args = [x2d, shift, scale, w]
    if with_lr:
        in_specs.append(pl.BlockSpec((d, LR_PAD), lambda i, j: (0, 0)))
        out_shape.append(jax.ShapeDtypeStruct((rows, LR_PAD), F32))
        out_specs.append(pl.BlockSpec((PROJ_TM, LR_PAD), lambda i, j: (i, 0)))
        args.append(w_lr)
    res = pl.pallas_call(
        functools.partial(_proj_kernel, with_lr),
        out_shape=out_shape,
        grid=(rows // PROJ_TM, n // PROJ_TN),
        in_specs=in_specs,
        out_specs=out_specs,
        scratch_shapes=[pltpu.VMEM((PROJ_TM, d), BF16)],
        compiler_params=_cparams(("arbitrary", "arbitrary")),
        name="norm_mod_projection",
    )(*args)
    return res if with_lr else res[0]


def _swap_halves32(x):
    lane = lax.broadcasted_iota(I32, x.shape, 1)
    return jnp.where((lane & 63) < 32, pltpu.roll(x, 96, 1), pltpu.roll(x, 32, 1))


def _qk_norm(x, gain):
    return x * lax.rsqrt(jnp.mean(x * x, axis=-1, keepdims=True) + EPS) * gain


def _rope(x, cos, sin_signed):
    return x * cos + _swap_halves32(x) * sin_signed


def _attn_kernel(sink_ref, q_ref, kp_ref, kc_ref, kn_ref, vp_ref, vc_ref, vn_ref, kx_ref, vx_ref,
                 cos_ref, sin_ref, qg_ref, kg_ref, o_ref):
    kvh = pl.program_id(1)
    n = pl.program_id(2)
    nb = pl.num_programs(2)

    def table(ref, blk):
        return ref[pl.ds(pl.multiple_of(blk * A_BLOCK, A_BLOCK), A_BLOCK), :]

    prev = jnp.maximum(n - 1, 0)
    nxt = jnp.minimum(n + 1, nb - 1)
    q_gain = qg_ref[...]
    k_gain = kg_ref[...]
    cos_q, sin_q = table(cos_ref, n), table(sin_ref, n)

    q = q_ref[0]
    qs = []
    for g in range(A_GROUP):
        qh = _rope(_qk_norm(q[:, g * HEAD_DIM:(g + 1) * HEAD_DIM], q_gain), cos_q, sin_q)
        qs.append((qh * HEAD_DIM ** -0.5).astype(BF16))
    qs = jnp.concatenate(qs, axis=0)

    keys = [_qk_norm(kx_ref[0], k_gain)]
    for ref, blk in ((kp_ref, prev), (kc_ref, n), (kn_ref, nxt)):
        keys.append(_rope(_qk_norm(ref[0], k_gain), table(cos_ref, blk), table(sin_ref, blk)))
    keys = jnp.concatenate(keys, axis=0).astype(BF16)
    vals = jnp.concatenate([vx_ref[0], vp_ref[0], vc_ref[0], vn_ref[0]], axis=0).astype(BF16)

    nk = CTX_LEN + 3 * A_BLOCK
    rows = A_GROUP * A_BLOCK
    s = _dot_nt(qs, keys)
    ri = lax.broadcasted_iota(I32, (rows, nk), 0)
    cj = lax.broadcasted_iota(I32, (rows, nk), 1)
    qi = ri & (A_BLOCK - 1)
    wj = cj - CTX_LEN
    kpos = (n - 1) * A_BLOCK + wj
    in_band = (jnp.abs(wj - A_BLOCK - qi) <= WINDOW) & (kpos >= 0) & (kpos < nb * A_BLOCK)
    s = jnp.where((cj < CTX_LEN) | in_band, s, NEG_INF)

    rcol = lax.broadcasted_iota(I32, (rows, 1), 0)
    sink = jnp.zeros((rows, 1), F32)
    for g in range(A_GROUP):
        sink = jnp.where(rcol // A_BLOCK == g, sink_ref[kvh * A_GROUP + g], sink)
    m = jnp.maximum(jnp.max(s, axis=-1, keepdims=True), sink)
    p = jnp.exp(s - m)
    denom = jnp.sum(p, axis=-1, keepdims=True) + jnp.exp(sink - m)
    o = _dot(p.astype(BF16), vals) / denom
    for g in range(A_GROUP):
        o_ref[0, :, g * HEAD_DIM:(g + 1) * HEAD_DIM] = o[g * A_BLOCK:(g + 1) * A_BLOCK].astype(BF16)


def _attention(p_lat, p_ctx, cos, sin_signed, q_gain, k_gain, sink):
    nb = SEQ_LEN // A_BLOCK
    gw = A_GROUP * HEAD_DIM

    def blk3(width, col, which):
        def imap(b, k, n, s):
            r = {"prev": jnp.maximum(n - 1, 0), "cur": n, "next": jnp.minimum(n + 1, nb - 1)}[which]
            return (b, r, col + k)
        return pl.BlockSpec((1, A_BLOCK, width), imap)

    def ctx_blk(col):
        return pl.BlockSpec((1, CTX_LEN, HEAD_DIM), lambda b, k, n, s: (b, 0, col + k))

    full = lambda shape: pl.BlockSpec(shape, lambda b, k, n, s: (0,) * len(shape))
    return pl.pallas_call(
        _attn_kernel,
        out_shape=jax.ShapeDtypeStruct((N_BATCH, SEQ_LEN, A_Q_DIM), BF16),
        grid_spec=pltpu.PrefetchScalarGridSpec(
            num_scalar_prefetch=1,
            grid=(N_BATCH, A_KV_HEADS, nb),
            in_specs=[
                blk3(gw, COL_AQ, "cur"),
                blk3(HEAD_DIM, COL_AK, "prev"), blk3(HEAD_DIM, COL_AK, "cur"), blk3(HEAD_DIM, COL_AK, "next"),
                blk3(HEAD_DIM, COL_AV, "prev"), blk3(HEAD_DIM, COL_AV, "cur"), blk3(HEAD_DIM, COL_AV, "next"),
                ctx_blk(COL_AK), ctx_blk(COL_AV),
                full((SEQ_LEN, HEAD_DIM)), full((SEQ_LEN, HEAD_DIM)),
                full((1, HEAD_DIM)), full((1, HEAD_DIM)),
            ],
            out_specs=pl.BlockSpec((1, A_BLOCK, gw), lambda b, k, n, s: (b, n, k)),
        ),
        compiler_params=_cparams(("arbitrary", "arbitrary", "arbitrary")),
        name="windowed_sink_attention",
    )(sink, p_lat, p_lat, p_lat, p_lat, p_lat, p_lat, p_lat, p_ctx, p_ctx, cos, sin_signed, q_gain, k_gain)


def _split3_bf16(x):
    hi = x.astype(BF16)
    r1 = x - hi.astype(F32)
    mid = r1.astype(BF16)
    lo = (r1 - mid.astype(F32)).astype(BF16)
    return hi, mid, lo


def _gla_kernel(latent, nblk, q_ref, k_ref, v_ref, *rest):
    if latent:
        g_ref, lr_ref, wa_ref, ba_ref, gain_ref, s0_ref, o_ref, st_ref, ofwd_ref = rest
    else:
        lr_ref, wa_ref, ba_ref, s0_ref, o_ref, st_ref = rest
    d = pl.program_id(1)
    j = pl.program_id(2)
    nc = GLA_ROWS // GLA_CHUNK
    blk = jnp.where(d == 0, j, nblk - 1 - j)

    @pl.when(j == 0)
    def _():
        st_ref[...] = s0_ref[0, 0]

    if latent:
        @pl.when((pl.program_id(0) == 0) & (d == 0) & (j == 0))
        def _():
            ofwd_ref[...] = jnp.zeros_like(ofwd_ref)

    ii = lax.broadcasted_iota(I32, (GLA_CHUNK, GLA_CHUNK), 0)
    jj = lax.broadcasted_iota(I32, (GLA_CHUNK, GLA_CHUNK), 1)
    incl = ((jj - ii) * (1 - 2 * d)) <= 0
    incl_b = jnp.where(incl, 1.0, 0.0).astype(BF16)

    for i in range(nc):
        ci = jnp.where(d == 0, i, nc - 1 - i)
        r0 = pl.multiple_of(ci * GLA_CHUNK, GLA_CHUNK)
        rows = pl.ds(r0, GLA_CHUNK)
        z = _dot(lr_ref[0, rows, :].astype(BF16), wa_ref[0]) + ba_ref[0]
        la = (jnp.minimum(z, 0.0) - jnp.log(1.0 + jnp.exp(-jnp.abs(z)))) / GLA_TAU
        hi, mid, lo = _split3_bf16(la)
        bcum = _dot(incl_b, hi) + _dot(incl_b, mid) + _dot(incl_b, lo)
        blast = jnp.sum(la, axis=0, keepdims=True)
        k = k_ref[0, rows, :]
        kl = (k * jnp.exp(blast - bcum)).astype(BF16)
        decay = jnp.exp(blast)
        v = v_ref[0, rows, :].astype(BF16)
        if latent:
            qf = (q_ref[0, rows, :] * GLA_DK ** -0.5 * jnp.exp(bcum)).astype(BF16)
            kf = (k * jnp.exp(-bcum)).astype(BF16)
        for h in range(GLA_HEADS):
            ks = slice(h * GLA_DK, (h + 1) * GLA_DK)
            vs = slice(h * GLA_DV, (h + 1) * GLA_DV)
            st = st_ref[h]
            if latent:
                att = jnp.where(incl, _dot_nt(qf[:, ks], kf[:, ks]), 0.0)
                o = _dot(att.astype(BF16), v[:, vs]) + _dot_nt(qf[:, ks], st.astype(BF16))
                orow = pl.ds(pl.multiple_of(blk * GLA_ROWS + r0, GLA_CHUNK), GLA_CHUNK)
                ot = o + jnp.where(d == 1, ofwd_ref[orow, vs], 0.0)
                ofwd_ref[orow, vs] = ot
                on = ot * lax.rsqrt(jnp.mean(ot * ot, axis=-1, keepdims=True) + EPS) * gain_ref[...]
                o_ref[0, rows, vs] = (on * _silu(g_ref[0, rows, vs])).astype(BF16)

            st_ref[h] = st * decay[:, ks] + _dot_tn(v[:, vs], kl[:, ks])

    if not latent:
        @pl.when(j == nblk - 1)
        def _():
            o_ref[0, 0] = st_ref[...]


def _gla(p3, lr3, wa_pad, ba, o_gain, s0, latent):
    n = p3.shape[1]
    nblk = n // GLA_ROWS
    seq_blk = lambda d, j: jnp.where(d == 0, j, nblk - 1 - j)
    in_specs = [
        pl.BlockSpec((1, GLA_ROWS, GLA_K_DIM), lambda b, d, j: (b, seq_blk(d, j), COL_GQ)),
        pl.BlockSpec((1, GLA_ROWS, GLA_K_DIM), lambda b, d, j: (b, seq_blk(d, j), COL_GK)),
        pl.BlockSpec((1, GLA_ROWS, GLA_V_DIM), lambda b, d, j: (b, seq_blk(d, j), COL_GV)),
    ]
    args = [p3, p3, p3]
    if latent:
        in_specs.append(pl.BlockSpec((1, GLA_ROWS, GLA_V_DIM), lambda b, d, j: (b, seq_blk(d, j), COL_GG)))
        args.append(p3)
    in_specs += [
        pl.BlockSpec((1, GLA_ROWS, LR_PAD), lambda b, d, j: (b, seq_blk(d, j), 0)),
        pl.BlockSpec((1, LR_PAD, GLA_K_DIM), lambda b, d, j: (d, 0, 0)),
        pl.BlockSpec((1, 1, GLA_K_DIM), lambda b, d, j: (d, 0, 0)),
    ]
    args += [lr3, wa_pad, ba]
    if latent:
        in_specs.append(pl.BlockSpec((1, GLA_DV), lambda b, d, j: (0, 0)))
        args.append(o_gain)
    state_spec = pl.BlockSpec((1, 1, GLA_HEADS, GLA_DV, GLA_DK), lambda b, d, j: (b, d, 0, 0, 0))
    in_specs.append(state_spec)
    args.append(s0)
    scratch = [pltpu.VMEM((GLA_HEADS, GLA_DV, GLA_DK), F32)]
    if latent:
        out_shape = jax.ShapeDtypeStruct((N_BATCH, n, GLA_V_DIM), BF16)
        out_spec = pl.BlockSpec((1, GLA_ROWS, GLA_V_DIM),
                                lambda b, d, j: (b, jnp.where(d == 0, nblk - 1, nblk - 1 - j), 0))
        scratch.append(pltpu.VMEM((n, GLA_V_DIM), F32))
    else:
        out_shape = jax.ShapeDtypeStruct(s0.shape, F32)
        out_spec = state_spec
    return pl.pallas_call(
        functools.partial(_gla_kernel, latent, nblk),
        out_shape=out_shape,
        grid=(N_BATCH, 2, nblk),
        in_specs=in_specs,
        out_specs=out_spec,
        scratch_shapes=scratch,
        compiler_params=_cparams(("arbitrary", "arbitrary", "arbitrary")),
        name="gla_latent" if latent else "gla_context_state",
    )(*args)


def _router_logits_t(h, rwt_ref):
    hh = h.astype(BF16)
    hl = (h - hh.astype(F32)).astype(BF16)
    rw = rwt_ref[...]
    rh = rw.astype(BF16)
    rl = (rw - rh.astype(F32)).astype(BF16)
    return _dot_nt(rh, hh) + _dot_nt(rh, hl) + _dot_nt(rl, hh)


def _residual_and_moe_input(mix, x_ref, gate_ref, sh_ref, sc_ref, rwt_ref, x_out_ref, h_out_ref, lg_ref):
    x1 = x_ref[...] + gate_ref[0] * mix
    x_out_ref[...] = x1
    h = _rms_modulate(x1, sh_ref[0], sc_ref[0])
    h_out_ref[...] = h
    lg_ref[...] = _router_logits_t(h, rwt_ref)


def _attn_out_kernel(a1_ref, a2_ref, w_ref, x_ref, gate_ref, sh_ref, sc_ref, rwt_ref, x_out_ref, h_out_ref, lg_ref):
    mix = _dot(a1_ref[...], w_ref[0:A_Q_DIM, :]) + _dot(a2_ref[...], w_ref[A_Q_DIM:, :])
    _residual_and_moe_input(mix, x_ref, gate_ref, sh_ref, sc_ref, rwt_ref, x_out_ref, h_out_ref, lg_ref)


def _conv_out_kernel(gb_ref, gc_ref, u_ref, gcp_ref, up_ref, gcn_ref, un_ref, cw_ref, w_ref,
                     x_ref, gate_ref, sh_ref, sc_ref, rwt_ref, x_out_ref, h_out_ref, lg_ref):
    i = pl.program_id(0)
    tiles_per_seq = SEQ_LEN // OUT_TM
    t = gc_ref[...] * u_ref[...]
    first = (i % tiles_per_seq) == 0
    last = (i % tiles_per_seq) == tiles_per_seq - 1
    t_before = jnp.where(first, 0.0, gcp_ref[7:8, :] * up_ref[7:8, :])
    t_after = jnp.where(last, 0.0, gcn_ref[0:1, :] * un_ref[0:1, :])
    row = lax.broadcasted_iota(I32, t.shape, 0)
    t_up = jnp.where(row == 0, t_before, pltpu.roll(t, 1, 0))
    t_dn = jnp.where(row == OUT_TM - 1, t_after, pltpu.roll(t, OUT_TM - 1, 0))
    y = cw_ref[0:1, :] * t_up + cw_ref[1:2, :] * t + cw_ref[2:3, :] * t_dn
    mix = _dot((gb_ref[...] * y).astype(BF16), w_ref[...])
    _residual_and_moe_input(mix, x_ref, gate_ref, sh_ref, sc_ref, rwt_ref, x_out_ref, h_out_ref, lg_ref)


def _mixer_out(kernel_fn, mixer_specs, mixer_args, x2d, gate, shift, scale, rwt, name):
    d = D_MODEL
    mod_spec = pl.BlockSpec((1, 1, d), lambda i: (i * OUT_TM // SEQ_LEN, 0, 0))
    row_spec = pl.BlockSpec((OUT_TM, d), lambda i: (i, 0))
    return pl.pallas_call(
        kernel_fn,
        out_shape=[jax.ShapeDtypeStruct((N_TOK, d), F32), jax.ShapeDtypeStruct((N_TOK, d), F32),
                   jax.ShapeDtypeStruct((N_EXPERTS, N_TOK), F32)],
        grid=(N_TOK // OUT_TM,),
        in_specs=mixer_specs + [row_spec, mod_spec, mod_spec, mod_spec,
                                pl.BlockSpec((N_EXPERTS, d), lambda i: (0, 0))],
        out_specs=[row_spec, row_spec, pl.BlockSpec((N_EXPERTS, OUT_TM), lambda i: (0, i))],
        compiler_params=_cparams(("arbitrary",)),
        name=name,
    )(*mixer_args, x2d, gate, shift, scale, rwt)


def _attn_out(attn2d, gla2d, w_out, x2d, gate, shift, scale, rwt):
    half = pl.BlockSpec((OUT_TM, A_Q_DIM), lambda i: (i, 0))
    specs = [half, half, pl.BlockSpec((D_MODEL, D_MODEL), lambda i: (0, 0))]
    return _mixer_out(_attn_out_kernel, specs, [attn2d, gla2d, w_out], x2d, gate, shift, scale, rwt,
                      "attn_out_projection")


def _conv_out(g2d, conv_w, w_out, x2d, gate, shift, scale, rwt):
    d = D_MODEL
    sub = OUT_TM // 8
    last8 = N_TOK // 8 - 1
    main = lambda col: pl.BlockSpec((OUT_TM, d), lambda i: (i, col))
    before = lambda col: pl.BlockSpec((8, d), lambda i: (jnp.maximum(i * sub - 1, 0), col))
    after = lambda col: pl.BlockSpec((8, d), lambda i: (jnp.minimum((i + 1) * sub, last8), col))
    specs = [main(0), main(1), main(2), before(1), before(2), after(1), after(2),
             pl.BlockSpec((3, d), lambda i: (0, 0)), pl.BlockSpec((d, d), lambda i: (0, 0))]
    return _mixer_out(_conv_out_kernel, specs, [g2d] * 7 + [conv_w, w_out], x2d, gate, shift, scale, rwt,
                      "conv_out_projection")


def _first_argmax8(x, idx8):
    m = jnp.max(x, axis=0, keepdims=True)
    a = jnp.min(jnp.where(x == m, idx8, float(GROUP_SIZE)), axis=0, keepdims=True)
    return m, a


def _route_kernel(lg_ref, rb_ref, e_ref, w_ref, dest_ref, meta_ref, cnt_ref, carry_ref, pstart_ref):
    phase = pl.program_id(0)
    j = pl.program_id(1)
    tt = ROUTE_TT
    sc = jax.nn.sigmoid(lg_ref[...])
    grp = sc + rb_ref[...]
    idx8 = lax.broadcasted_iota(I32, (GROUP_SIZE, tt), 0).astype(F32)
    groups = [grp[g * GROUP_SIZE:(g + 1) * GROUP_SIZE] for g in range(N_GROUPS)]
    gscore = []
    for x in groups:
        m1, a1 = _first_argmax8(x, idx8)
        m2, _ = _first_argmax8(jnp.where(idx8 == a1, -jnp.inf, x), idx8)
        gscore.append(m1 + m2)
    gmax = functools.reduce(jnp.maximum, gscore)
    gsel = jnp.full((1, tt), float(N_GROUPS), F32)
    for g in reversed(range(N_GROUPS)):
        gsel = jnp.where(gscore[g] == gmax, float(g), gsel)
    in_grp = groups[0]
    for g in range(1, N_GROUPS):
        in_grp = jnp.where(gsel == float(g), groups[g], in_grp)
    _, a1 = _first_argmax8(in_grp, idx8)
    _, a2 = _first_argmax8(jnp.where(idx8 == a1, -jnp.inf, in_grp), idx8)
    e0 = gsel * GROUP_SIZE + a1
    e1 = gsel * GROUP_SIZE + a2
    idx32 = lax.broadcasted_iota(I32, (N_EXPERTS, tt), 0).astype(F32)
    oh0 = idx32 == e0
    oh1 = idx32 == e1
    w0 = jnp.sum(jnp.where(oh0, sc, 0.0), axis=0, keepdims=True)
    w1 = jnp.sum(jnp.where(oh1, sc, 0.0), axis=0, keepdims=True)
    wsum = w0 + w1
    ohs = jnp.where(oh0 | oh1, 1.0, 0.0)
    tile_cnt = jnp.sum(ohs, axis=1, keepdims=True)

    @pl.when(phase == 0)
    def _():
        @pl.when(j == 0)
        def _():
            cnt_ref[...] = jnp.zeros_like(cnt_ref)
        cnt_ref[...] += jnp.broadcast_to(tile_cnt, cnt_ref.shape)

    @pl.when(phase == 1)
    def _():
        @pl.when(j == 0)
        def _():
            cnt = cnt_ref[...].astype(I32)
            pc = ((cnt + (MOE_BLOCK - 1)) // MOE_BLOCK) * MOE_BLOCK
            r = lax.broadcasted_iota(I32, (N_EXPERTS, N_EXPERTS), 0)
            c = lax.broadcasted_iota(I32, (N_EXPERTS, N_EXPERTS), 1)
            strict_lower = jnp.where(c < r, 1.0, 0.0).astype(BF16)
            pcf = pc.astype(F32)
            pstart = _dot(strict_lower, pcf.astype(BF16))
            pstart_ref[...] = pstart
            carry_ref[...] = jnp.zeros_like(carry_ref)
            pends = pstart + pcf
            lane = lax.broadcasted_iota(I32, pends.shape, 1).astype(F32)
            blk_exp = jnp.sum(jnp.where(lane * MOE_BLOCK >= pends, 1.0, 0.0), axis=0, keepdims=True)
            blk_exp = jnp.minimum(blk_exp, float(N_EXPERTS - 1))
            n_used = jnp.sum(pcf, axis=0, keepdims=True) / MOE_BLOCK
            own = lax.broadcasted_iota(I32, pends.shape, 0).astype(F32) == blk_exp
            cnt_blk = jnp.sum(jnp.where(own, cnt_ref[...], 0.0), axis=0, keepdims=True)
            start_blk = jnp.sum(jnp.where(own, pstart, 0.0), axis=0, keepdims=True)
            valid = jnp.clip(cnt_blk - (lane[0:1] * MOE_BLOCK - start_blk), 0.0, float(MOE_BLOCK))
            meta = jnp.where(lane[0:1] == float(META_LANES - 1), n_used, blk_exp)
            meta_ref[...] = jnp.concatenate([meta, valid], axis=0).astype(I32)

        rr = lax.broadcasted_iota(I32, (tt, tt), 0)
        cc = lax.broadcasted_iota(I32, (tt, tt), 1)
        strict_upper = jnp.where(rr < cc, 1.0, 0.0).astype(BF16)
        before = _dot(ohs.astype(BF16), strict_upper)
        pos = before + carry_ref[:, 0:1] + pstart_ref[:, 0:1]
        d0 = jnp.sum(jnp.where(oh0, pos, 0.0), axis=0, keepdims=True)
        d1 = jnp.sum(jnp.where(oh1, pos, 0.0), axis=0, keepdims=True)
        carry_ref[...] += jnp.broadcast_to(tile_cnt, carry_ref.shape)
        e_ref[...] = jnp.concatenate([e0, e1], axis=0).astype(I32)
        w_ref[...] = jnp.concatenate([w0 / wsum, w1 / wsum], axis=0)
        dest_ref[...] = jnp.concatenate([d0, d1], axis=0).astype(I32)


def _route(logits_t, router_b):
    nt = N_TOK // ROUTE_TT
    tok = lambda dt: jax.ShapeDtypeStruct((2, N_TOK), dt)
    tok_spec = pl.BlockSpec((2, ROUTE_TT), lambda p, j: (0, j * p))
    return pl.pallas_call(
        _route_kernel,
        out_shape=[tok(I32), tok(F32), tok(I32), jax.ShapeDtypeStruct((2, META_LANES), I32)],
        grid=(2, nt),
        in_specs=[pl.BlockSpec((N_EXPERTS, ROUTE_TT), lambda p, j: (0, j)),
                  pl.BlockSpec((N_EXPERTS, 1), lambda p, j: (0, 0))],
        out_specs=[tok_spec, tok_spec, tok_spec, pl.BlockSpec((2, META_LANES), lambda p, j: (0, 0))],
        scratch_shapes=[pltpu.VMEM((N_EXPERTS, 128), F32), pltpu.VMEM((N_EXPERTS, 128), F32),
                        pltpu.VMEM((N_EXPERTS, 128), F32)],
        compiler_params=_cparams(("arbitrary", "arbitrary")),
        name="router_slot_assignment",
    )(logits_t, router_b.reshape(N_EXPERTS, 1))


def _dispatch_copy(dest_ref, h_ref, xs_ref, sem, i, r, k):
    dst = dest_ref[k * N_TOK + i * TOK_TM + r]
    return pltpu.make_async_copy(h_ref.at[pl.ds(r, 1)], xs_ref.at[pl.ds(dst, 1)], sem)


def _dispatch_kernel(dest_ref, h_ref, xs_in_ref, xs_ref, sem):
    del xs_in_ref
    i = pl.program_id(0)

    def start(r, carry):
        for k in range(2):
            _dispatch_copy(dest_ref, h_ref, xs_ref, sem, i, r, k).start()
        return carry

    def wait(r, carry):
        for k in range(2):
            _dispatch_copy(dest_ref, h_ref, xs_ref, sem, i, r, k).wait()
        return carry

    lax.fori_loop(0, TOK_TM, start, 0)
    lax.fori_loop(0, TOK_TM, wait, 0)


def _dispatch(dest_flat, h2d):
    d = D_MODEL
    xs0 = jnp.zeros((MOE_SLOTS, d), F32)
    return pl.pallas_call(
        _dispatch_kernel,
        out_shape=jax.ShapeDtypeStruct((MOE_SLOTS, d), F32),
        grid_spec=pltpu.PrefetchScalarGridSpec(
            num_scalar_prefetch=1,
            grid=(N_TOK // TOK_TM,),
            in_specs=[pl.BlockSpec((TOK_TM, d), lambda i, dest: (i, 0)),
                      pl.BlockSpec(memory_space=pl.ANY)],
            out_specs=pl.BlockSpec(memory_space=pl.ANY),
            scratch_shapes=[pltpu.SemaphoreType.DMA],
        ),
        input_output_aliases={2: 0},
        compiler_params=_cparams(("arbitrary",)),
        name="moe_dispatch",
    )(dest_flat, h2d, xs0)


def _expert_kernel(meta_ref, x_ref, w1_ref, w3_ref, w2_ref, y_ref, xb_ref):
    blk = pl.program_id(0)
    f = pl.program_id(1)
    valid = jnp.where(blk < meta_ref[META_LANES - 1], meta_ref[META_LANES + blk], 0)
    n_sub = (valid + (MOE_SUB - 1)) // MOE_SUB

    @pl.when(f == 0)
    def _():
        xb_ref[...] = x_ref[...].astype(BF16)
        for s in range(MOE_BLOCK // MOE_SUB):
            @pl.when(s >= n_sub)
            def _():
                y_ref[s * MOE_SUB:(s + 1) * MOE_SUB, :] = jnp.zeros((MOE_SUB, y_ref.shape[1]), F32)

    w1 = w1_ref[0, 0].astype(BF16)
    w3 = w3_ref[0, 0].astype(BF16)
    w2 = w2_ref[0, 0].astype(BF16)

    def sub_block(s, carry):
        rows = pl.ds(pl.multiple_of(s * MOE_SUB, MOE_SUB), MOE_SUB)
        x = xb_ref[rows, :]
        a = _silu(_dot(x, w1)) * _dot(x, w3)
        y = _dot(a.astype(BF16), w2)

        @pl.when(f == 0)
        def _():
            y_ref[rows, :] = y

        @pl.when(f > 0)
        def _():
            y_ref[rows, :] += y

        return carry

    lax.fori_loop(0, n_sub, sub_block, 0)


def _experts(meta, xs, w1, w3, w2, layer):
    d = D_MODEL
    nf = D_EXPERT // MOE_FC
    n_used = lambda meta: meta[META_LANES - 1]
    used = lambda i, meta: jnp.minimum(i, n_used(meta) - 1)
    chunk = lambda i, f, meta: jnp.where(i < n_used(meta), f, nf - 1)
    w_in_spec = pl.BlockSpec((1, 1, d, MOE_FC), lambda i, f, meta: (layer, meta[used(i, meta)], 0, chunk(i, f, meta)))
    w_out_spec = pl.BlockSpec((1, 1, MOE_FC, d), lambda i, f, meta: (layer, meta[used(i, meta)], chunk(i, f, meta), 0))
    return pl.pallas_call(
        _expert_kernel,
        out_shape=jax.ShapeDtypeStruct((MOE_SLOTS, d), F32),
        grid_spec=pltpu.PrefetchScalarGridSpec(
            num_scalar_prefetch=1,
            grid=(MOE_NBLK, nf),
            in_specs=[pl.BlockSpec((MOE_BLOCK, d), lambda i, f, meta: (used(i, meta), 0)),
                      w_in_spec, w_in_spec, w_out_spec],
            out_specs=pl.BlockSpec((MOE_BLOCK, d), lambda i, f, meta: (i, 0)),
            scratch_shapes=[pltpu.VMEM((MOE_BLOCK, d), BF16)],
        ),
        compiler_params=_cparams(("arbitrary", "arbitrary")),
        name="moe_experts",
    )(meta, xs, w1, w3, w2)


def _combine_copy(dest_ref, yb_ref, buf_ref, sem, i, r, k):
    src = dest_ref[k * N_TOK + i * TOK_TM + r]
    return pltpu.make_async_copy(yb_ref.at[pl.ds(src, 1)], buf_ref.at[k, pl.ds(r, 1)], sem)


def _combine_kernel(dest_ref, yb_ref, x_ref, gate_ref, w_ref, o_ref, buf_ref, sem):
    i = pl.program_id(0)

    def start(r, carry):
        for k in range(2):
            _combine_copy(dest_ref, yb_ref, buf_ref, sem, i, r, k).start()
        return carry

    def wait(r, carry):
        for k in range(2):
            _combine_copy(dest_ref, yb_ref, buf_ref, sem, i, r, k).wait()
        return carry

    lax.fori_loop(0, TOK_TM, start, 0)
    lax.fori_loop(0, TOK_TM, wait, 0)
    w = w_ref[...]
    y = w[:, 0:1] * buf_ref[0] + w[:, 1:2] * buf_ref[1]
    o_ref[...] = x_ref[...] + gate_ref[0] * y


def _combine(dest_flat, yb, x2d, gate, w_tok):
    d = D_MODEL
    return pl.pallas_call(
        _combine_kernel,
        out_shape=jax.ShapeDtypeStruct((N_TOK, d), F32),
        grid_spec=pltpu.PrefetchScalarGridSpec(
            num_scalar_prefetch=1,
            grid=(N_TOK // TOK_TM,),
            in_specs=[pl.BlockSpec(memory_space=pl.ANY),
                      pl.BlockSpec((TOK_TM, d), lambda i, dest: (i, 0)),
                      pl.BlockSpec((1, 1, d), lambda i, dest: (i * TOK_TM // SEQ_LEN, 0, 0)),
                      pl.BlockSpec((TOK_TM, 2), lambda i, dest: (i, 0))],
            out_specs=pl.BlockSpec((TOK_TM, d), lambda i, dest: (i, 0)),
            scratch_shapes=[pltpu.VMEM((2, TOK_TM, d), F32), pltpu.SemaphoreType.DMA],
        ),
        compiler_params=_cparams(("arbitrary",)),
        name="moe_combine",
    )(dest_flat, yb, x2d, gate, w_tok)


def _moe(h2d, logits_t, x2d, gate, router_b, w1, w3, w2, layer):
    _, w_sel, dest, meta = _route(logits_t, router_b)
    dest_flat = dest.reshape(2 * N_TOK)
    xs = _dispatch(dest_flat, h2d)
    yb = _experts(meta.reshape(2 * META_LANES), xs, w1, w3, w2, layer)
    return _combine(dest_flat, yb, x2d, gate, w_sel.T)


def _rope_tables():
    f = HEAD_DIM // 4
    inv_freq = ROPE_THETA ** (-jnp.arange(f, dtype=F32) / f)
    pos = jnp.arange(SEQ_LEN)
    row = (pos // GRID_W).astype(F32)[:, None] * inv_freq
    col = (pos % GRID_W).astype(F32)[:, None] * inv_freq
    cos = jnp.concatenate([jnp.cos(row), jnp.cos(row), jnp.cos(col), jnp.cos(col)], axis=-1)
    sin = jnp.concatenate([-jnp.sin(row), jnp.sin(row), -jnp.sin(col), jnp.sin(col)], axis=-1)
    return cos, sin


def _permute_attn_w_in(w):
    o = [0, A_Q_DIM, A_KV_DIM, A_KV_DIM, GLA_K_DIM, GLA_K_DIM, GLA_V_DIM, GLA_V_DIM]
    s = [sum(o[:i + 1]) for i in range(len(o))]
    seg = lambda i: w[:, s[i]:s[i + 1]] if i + 1 < len(s) else w[:, s[i]:]
    aq, ak, av, gq, gk, gv, gg = (seg(i) for i in range(7))
    main = jnp.concatenate([aq, gv, gg, gq, gk, ak, av], axis=1).astype(BF16)
    lr = jnp.pad(w[:, PROJ_DIM:], ((0, 0), (0, LR_PAD - 2 * GLA_RANK))).astype(BF16)
    return main, lr


def kernel(x, c, ctx, c_ctx, mod_w, mod_b, attn_w_in, attn_q_norm, attn_k_norm, attn_sink, gla_wa2, gla_ba,
           gla_norm, attn_w_out, conv_w_in, conv_w, conv_w_out, router_w, router_b, exp_w1, exp_w3, exp_w2):
    d = D_MODEL
    cc = jnp.concatenate([c, c_ctx[None], jnp.zeros((8 - N_BATCH - 1, d), F32)], axis=0)
    m = _modulation(cc, mod_w, mod_b)
    mods = [[m[l, :, i * d:(i + 1) * d].reshape(8, 1, d) for i in range(6)] for l in range(mod_w.shape[0])]
    lat_row = lambda i: i * PROJ_TM // SEQ_LEN
    ctx_row = lambda i: N_BATCH
    rwt = router_w.T
    x2d = x.reshape(N_TOK, d)

    w_main, w_lr = _permute_attn_w_in(attn_w_in[0])
    p_lat, lr_lat = _projection(x2d, mods[0][0], mods[0][1], w_main, w_lr, lat_row)
    p_ctx, lr_ctx = _projection(ctx.reshape(N_BATCH * CTX_LEN, d), mods[0][0], mods[0][1], w_main, w_lr, ctx_row)
    p_lat = p_lat.reshape(N_BATCH, SEQ_LEN, PROJ_DIM)
    p_ctx = p_ctx.reshape(N_BATCH, CTX_LEN, PROJ_DIM)
    cos, sin_signed = _rope_tables()
    attn = _attention(p_lat, p_ctx, cos, sin_signed, attn_q_norm[0][None], attn_k_norm[0][None], attn_sink[0])
    wa_pad = jnp.zeros((2, LR_PAD, GLA_K_DIM), F32)
    for di in range(2):
        wa_pad = wa_pad.at[di, di * GLA_RANK:(di + 1) * GLA_RANK].set(gla_wa2[0, di])
    wa_pad = wa_pad.astype(BF16)
    ba = gla_ba[0].reshape(2, 1, GLA_K_DIM)
    s_zero = jnp.zeros((N_BATCH, 2, GLA_HEADS, GLA_DV, GLA_DK), F32)
    s_ctx = _gla(p_ctx, lr_ctx.reshape(N_BATCH, CTX_LEN, LR_PAD), wa_pad, ba, None, s_zero, latent=False)
    gla = _gla(p_lat, lr_lat.reshape(N_BATCH, SEQ_LEN, LR_PAD), wa_pad, ba, gla_norm[0][None], s_ctx, latent=True)
    x1, h2, lg = _attn_out(attn.reshape(N_TOK, A_Q_DIM), gla.reshape(N_TOK, GLA_V_DIM), attn_w_out[0].astype(BF16),
                           x2d, mods[0][2], mods[0][3], mods[0][4], rwt)
    x2 = _moe(h2, lg, x1, mods[0][5], router_b, exp_w1, exp_w3, exp_w2, 0)

    g = _projection(x2, mods[1][0], mods[1][1], conv_w_in[0].astype(BF16), None, lat_row)
    x3, h2, lg = _conv_out(g, conv_w[0], conv_w_out[0].astype(BF16), x2, mods[1][2], mods[1][3], mods[1][4], rwt)
    x4 = _moe(h2, lg, x3, mods[1][5], router_b, exp_w1, exp_w3, exp_w2, 1)
    return x4.reshape(N_BATCH, SEQ_LEN, d)
```

```python
import functools

import jax
import jax.numpy as jnp
from jax import lax
from jax.experimental import pallas as pl
from jax.experimental.pallas import tpu as pltpu

F32 = jnp.float32
BF16 = jnp.bfloat16
I32 = jnp.int32

D_MODEL = 2048
N_BATCH = 4
SEQ_LEN = 2048
CTX_LEN = 256
N_TOK = N_BATCH * SEQ_LEN
GRID_W = 64
HEAD_DIM = 128
A_Q_HEADS = 8
A_KV_HEADS = 2
A_GROUP = A_Q_HEADS // A_KV_HEADS
WINDOW = 128
A_BLOCK = 128
ROPE_THETA = 10000.0
GLA_HEADS = 4
GLA_DK = 128
GLA_DV = 256
GLA_RANK = 16
GLA_TAU = 16.0
GLA_CHUNK = 64
N_EXPERTS = 32
N_GROUPS = 4
GROUP_SIZE = N_EXPERTS // N_GROUPS
D_EXPERT = 1024
EPS = 1e-6
NEG_INF = -1e30
A_Q_DIM = A_Q_HEADS * HEAD_DIM
A_KV_DIM = A_KV_HEADS * HEAD_DIM
GLA_K_DIM = GLA_HEADS * GLA_DK
GLA_V_DIM = GLA_HEADS * GLA_DV
PROJ_DIM = A_Q_DIM + 2 * A_KV_DIM + 2 * GLA_K_DIM + 2 * GLA_V_DIM
LR_PAD = 128

COL_AQ = 0
COL_GV = 1
COL_GG = 2
COL_GQ = 6
COL_GK = 7
COL_AK = 32
COL_AV = 34

MOD_TN = 1024
PROJ_TM = 512
PROJ_TN = 512
OUT_TM = 256
CONV_HALO = 16
GLA_ROWS = 256
ROUTE_TT = 512
ROUTE_LANES = 128
MOE_BLOCK = 512
MOE_SUB = 256
MOE_FC = 256
MOE_SLOTS = N_TOK * 2 + N_EXPERTS * MOE_BLOCK
MOE_NBLK = MOE_SLOTS // MOE_BLOCK
TOK_TM = 256
META_ROWS = 3
META_LANES = 128
VMEM_LIMIT = 56 * 1024 * 1024


def _cparams(sem):
    return pltpu.CompilerParams(dimension_semantics=sem, vmem_limit_bytes=VMEM_LIMIT)


def _silu(x):
    return x * jax.nn.sigmoid(x)


def _dot(a, b):
    return jnp.dot(a, b, preferred_element_type=F32)


def _dot_nt(a, b):
    return lax.dot_general(a, b, (((1,), (1,)), ((), ())), preferred_element_type=F32)


def _dot_tn(a, b):
    return lax.dot_general(a, b, (((0,), (0,)), ((), ())), preferred_element_type=F32)


def _rms_modulate(x, shift, scale):
    r = lax.rsqrt(jnp.mean(x * x, axis=-1, keepdims=True) + EPS)
    return (x * r) * (1.0 + scale) + shift


def _mod_kernel(cc_ref, w_ref, b_ref, o_ref):
    a = _silu(cc_ref[...])
    o_ref[0] = _dot(a.astype(BF16), w_ref[0].astype(BF16)) + b_ref[0]


def _modulation(cc, mod_w, mod_b):
    depth, d, n = mod_w.shape
    return pl.pallas_call(
        _mod_kernel,
        out_shape=jax.ShapeDtypeStruct((depth, 8, n), F32),
        grid=(depth, n // MOD_TN),
        in_specs=[
            pl.BlockSpec((8, d), lambda l, j: (0, 0)),
            pl.BlockSpec((1, d, MOD_TN), lambda l, j: (l, 0, j)),
            pl.BlockSpec((1, 1, MOD_TN), lambda l, j: (l, 0, j)),
        ],
        out_specs=pl.BlockSpec((1, 8, MOD_TN), lambda l, j: (l, 0, j)),
        compiler_params=_cparams(("arbitrary", "arbitrary")),
        name="adaln_modulation",
    )(cc, mod_w, mod_b.reshape(depth, 1, n))


def _proj_kernel(with_lr, x_ref, sh_ref, sc_ref, w_ref, *rest):
    if with_lr:
        wlr_ref, o_ref, olr_ref = rest
    else:
        (o_ref,) = rest
    hb = _rms_modulate(x_ref[...], sh_ref[0], sc_ref[0]).astype(BF16)
    for c in range(w_ref.shape[1] // PROJ_TN):
        cols = slice(c * PROJ_TN, (c + 1) * PROJ_TN)
        o_ref[:, cols] = _dot(hb, w_ref[:, cols]).astype(BF16)
    if with_lr:
        olr_ref[...] = _dot(hb, wlr_ref[...]).astype(BF16)


def _projection(x2d, shift, scale, w, w_lr, mod_row):
    rows, d = x2d.shape
    n = w.shape[1]
    with_lr = w_lr is not None
    resident = lambda shape: pl.BlockSpec(shape, lambda i: (0, 0), pipeline_mode=pl.Buffered(1))
    in_specs = [
        pl.BlockSpec((PROJ_TM, d), lambda i: (i, 0)),
        pl.BlockSpec((1, 1, d), lambda i: (mod_row(i), 0, 0)),
        pl.BlockSpec((1, 1, d), lambda i: (mod_row(i), 0, 0)),
        resident((d, n)),
    ]
    out_shape = [jax.ShapeDtypeStruct((rows, n), BF16)]
    out_specs = [pl.BlockSpec((PROJ_TM, n), lambda i: (i, 0))]
    args = [x2d, shift, scale, w]
    if with_lr:
        in_specs.append(resident((d, LR_PAD)))
        out_shape.append(jax.ShapeDtypeStruct((rows, LR_PAD), BF16))
        out_specs.append(pl.BlockSpec((PROJ_TM, LR_PAD), lambda i: (i, 0)))
        args.append(w_lr)
    res = pl.pallas_call(
        functools.partial(_proj_kernel, with_lr),
        out_shape=out_shape,
        grid=(rows // PROJ_TM,),
        in_specs=in_specs,
        out_specs=out_specs,
        compiler_params=_cparams(("arbitrary",)),
        name="norm_mod_projection",
    )(*args)
    return res if with_lr else res[0]


def _swap_halves32(x):
    lane = lax.broadcasted_iota(I32, x.shape, 1)
    return jnp.where((lane & 63) < 32, pltpu.roll(x, 96, 1), pltpu.roll(x, 32, 1))


def _qk_norm(x, gain):
    return x * lax.rsqrt(jnp.mean(x * x, axis=-1, keepdims=True) + EPS) * gain


def _rope(x, cos, sin_signed):
    return x * cos + _swap_halves32(x) * sin_signed


ATT_WIN0 = CTX_LEN
ATT_LAT0 = CTX_LEN + A_BLOCK
ATT_ROWS = CTX_LEN + SEQ_LEN + 2 * A_BLOCK
ATT_WIN = 3 * A_BLOCK


def _attn_kernel(sink_ref, q_ref, k_ref, v_ref, kx_ref, vx_ref, cos_ref, sin_ref, qg_ref, kg_ref, o_ref,
                 kn_ref, vn_ref, band_ref):
    kvh = pl.program_id(1)
    nb = SEQ_LEN // A_BLOCK
    k_gain = kg_ref[...]
    q_gain = qg_ref[...]

    pad = jnp.zeros((A_BLOCK, HEAD_DIM), BF16)
    kn_ref[0:CTX_LEN, :] = _qk_norm(kx_ref[0].astype(F32), k_gain).astype(BF16)
    vn_ref[0:CTX_LEN, :] = vx_ref[0]
    for ref in (kn_ref, vn_ref):
        ref[ATT_WIN0:ATT_LAT0, :] = pad
        ref[ATT_LAT0 + SEQ_LEN:, :] = pad
    vn_ref[ATT_LAT0:ATT_LAT0 + SEQ_LEN, :] = v_ref[0]
    prep_rows = 4 * A_BLOCK
    for c in range(SEQ_LEN // prep_rows):
        r = slice(c * prep_rows, (c + 1) * prep_rows)
        kc = _rope(_qk_norm(k_ref[0, r, :].astype(F32), k_gain), cos_ref[r, :], sin_ref[r, :])
        kn_ref[ATT_LAT0 + c * prep_rows:ATT_LAT0 + (c + 1) * prep_rows, :] = kc.astype(BF16)

    qi = lax.broadcasted_iota(I32, (A_BLOCK, ATT_WIN), 0)
    wj = lax.broadcasted_iota(I32, (A_BLOCK, ATT_WIN), 1)
    band = jnp.where(jnp.abs(wj - A_BLOCK - qi) <= WINDOW, 1.0, 0.0)
    band_ref[0] = jnp.where(wj >= A_BLOCK, band, 0.0)
    band_ref[1] = band
    band_ref[2] = jnp.where(wj < 2 * A_BLOCK, band, 0.0)

    rows = A_GROUP * A_BLOCK
    rcol = lax.broadcasted_iota(I32, (rows, 1), 0)
    sink = jnp.zeros((rows, 1), F32)
    for g in range(A_GROUP):
        sink = jnp.where((rcol >= g * A_BLOCK) & (rcol < (g + 1) * A_BLOCK), sink_ref[kvh * A_GROUP + g], sink)

    def query_block(n, carry):
        qrows = pl.ds(pl.multiple_of(n * A_BLOCK, A_BLOCK), A_BLOCK)
        wrows = pl.ds(pl.multiple_of(ATT_WIN0 + n * A_BLOCK, A_BLOCK), ATT_WIN)
        cos_q, sin_q = cos_ref[qrows, :], sin_ref[qrows, :]
        q = q_ref[0, qrows, :].astype(F32)
        qs = []
        for g in range(A_GROUP):
            qh = _rope(_qk_norm(q[:, g * HEAD_DIM:(g + 1) * HEAD_DIM], q_gain), cos_q, sin_q)
            qs.append((qh * HEAD_DIM ** -0.5).astype(BF16))
        qs = jnp.concatenate(qs, axis=0)
        mask = band_ref[jnp.where(n == 0, 0, jnp.where(n == nb - 1, 2, 1))]
        mask = jnp.concatenate([mask] * A_GROUP, axis=0)
        s_ctx = _dot_nt(qs, kn_ref[0:CTX_LEN, :])
        s_win = jnp.where(mask > 0.5, _dot_nt(qs, kn_ref[wrows, :]), NEG_INF)
        m = jnp.maximum(jnp.max(s_ctx, axis=-1, keepdims=True), jnp.max(s_win, axis=-1, keepdims=True))
        m = jnp.maximum(m, sink)
        p_ctx = jnp.exp(s_ctx - m)
        p_win = jnp.exp(s_win - m)
        denom = (jnp.sum(p_ctx, axis=-1, keepdims=True) + jnp.sum(p_win, axis=-1, keepdims=True)
                 + jnp.exp(sink - m))
        o = (_dot(p_ctx.astype(BF16), vn_ref[0:CTX_LEN, :]) + _dot(p_win.astype(BF16), vn_ref[wrows, :])) / denom
        for g in range(A_GROUP):
            o_ref[0, qrows, g * HEAD_DIM:(g + 1) * HEAD_DIM] = o[g * A_BLOCK:(g + 1) * A_BLOCK].astype(BF16)
        return carry

    lax.fori_loop(0, nb, query_block, 0)


def _attention(p_lat, p_ctx, cos, sin_signed, q_gain, k_gain, sink):
    gw = A_GROUP * HEAD_DIM
    lat = lambda width, col: pl.BlockSpec((1, SEQ_LEN, width), lambda b, k, s: (b, 0, col + k))
    ctx_blk = lambda col: pl.BlockSpec((1, CTX_LEN, HEAD_DIM), lambda b, k, s: (b, 0, col + k))
    full = lambda shape: pl.BlockSpec(shape, lambda b, k, s: (0,) * len(shape))
    return pl.pallas_call(
        _attn_kernel,
        out_shape=jax.ShapeDtypeStruct((N_BATCH, SEQ_LEN, A_Q_DIM), BF16),
        grid_spec=pltpu.PrefetchScalarGridSpec(
            num_scalar_prefetch=1,
            grid=(N_BATCH, A_KV_HEADS),
            in_specs=[
                lat(gw, COL_AQ), lat(HEAD_DIM, COL_AK), lat(HEAD_DIM, COL_AV),
                ctx_blk(COL_AK), ctx_blk(COL_AV),
                full((SEQ_LEN, HEAD_DIM)), full((SEQ_LEN, HEAD_DIM)),
                full((1, HEAD_DIM)), full((1, HEAD_DIM)),
            ],
            out_specs=pl.BlockSpec((1, SEQ_LEN, gw), lambda b, k, s: (b, 0, k)),
            scratch_shapes=[pltpu.VMEM((ATT_ROWS, HEAD_DIM), BF16), pltpu.VMEM((ATT_ROWS, HEAD_DIM), BF16),
                            pltpu.VMEM((3, A_BLOCK, ATT_WIN), F32)],
        ),
        compiler_params=_cparams(("arbitrary", "arbitrary")),
        name="windowed_sink_attention",
    )(sink, p_lat, p_lat, p_lat, p_ctx, p_ctx, cos, sin_signed, q_gain, k_gain)


def _split3_bf16(x):
    hi = x.astype(BF16)
    r1 = x - hi.astype(F32)
    mid = r1.astype(BF16)
    lo = (r1 - mid.astype(F32)).astype(BF16)
    return hi, mid, lo


def _gla_kernel(latent, nblk, q_ref, k_ref, v_ref, *rest):
    if latent:
        g_ref, lr_ref, wa_ref, ba_ref, gain_ref, s0_ref, o_ref, st_ref, ofwd_ref = rest
    else:
        lr_ref, wa_ref, ba_ref, s0_ref, o_ref, st_ref = rest
    d = pl.program_id(1)
    j = pl.program_id(2)
    nc = GLA_ROWS // GLA_CHUNK
    blk = jnp.where(d == 0, j, nblk - 1 - j)

    @pl.when(j == 0)
    def _():
        st_ref[...] = s0_ref[0, 0]

    if latent:
        @pl.when((pl.program_id(0) == 0) & (d == 0) & (j == 0))
        def _():
            ofwd_ref[...] = jnp.zeros_like(ofwd_ref)

    ii = lax.broadcasted_iota(I32, (GLA_CHUNK, GLA_CHUNK), 0)
    jj = lax.broadcasted_iota(I32, (GLA_CHUNK, GLA_CHUNK), 1)
    incl = ((jj - ii) * (1 - 2 * d)) <= 0
    incl_b = jnp.where(incl, 1.0, 0.0).astype(BF16)

    for i in range(nc):
        ci = jnp.where(d == 0, i, nc - 1 - i)
        r0 = pl.multiple_of(ci * GLA_CHUNK, GLA_CHUNK)
        rows = pl.ds(r0, GLA_CHUNK)
        z = _dot(lr_ref[0, rows, :], wa_ref[0]) + ba_ref[0]
        la = (jnp.minimum(z, 0.0) - jnp.log(1.0 + jnp.exp(-jnp.abs(z)))) / GLA_TAU
        hi, mid, lo = _split3_bf16(la)
        bcum = _dot(incl_b, hi) + _dot(incl_b, mid) + _dot(incl_b, lo)
        blast = jnp.sum(la, axis=0, keepdims=True)
        k = k_ref[0, rows, :].astype(F32)
        kl = (k * jnp.exp(blast - bcum)).astype(BF16)
        decay = jnp.exp(blast)
        v = v_ref[0, rows, :]
        if latent:
            qf = (q_ref[0, rows, :].astype(F32) * GLA_DK ** -0.5 * jnp.exp(bcum)).astype(BF16)
            kf = (k * jnp.exp(-bcum)).astype(BF16)
        for h in range(GLA_HEADS):
            ks = slice(h * GLA_DK, (h + 1) * GLA_DK)
            vs = slice(h * GLA_DV, (h + 1) * GLA_DV)
            st = st_ref[h]
            if latent:
                att = jnp.where(incl, _dot_nt(qf[:, ks], kf[:, ks]), 0.0)
                o = _dot(att.astype(BF16), v[:, vs]) + _dot_nt(qf[:, ks], st.astype(BF16))
                orow = pl.ds(pl.multiple_of(blk * GLA_ROWS + r0, GLA_CHUNK), GLA_CHUNK)
                ot = o + jnp.where(d == 1, ofwd_ref[orow, vs], 0.0)
                ofwd_ref[orow, vs] = ot
                on = ot * lax.rsqrt(jnp.mean(ot * ot, axis=-1, keepdims=True) + EPS) * gain_ref[...]
                o_ref[0, rows, vs] = (on * _silu(g_ref[0, rows, vs].astype(F32))).astype(BF16)

            st_ref[h] = st * decay[:, ks] + _dot_tn(v[:, vs], kl[:, ks])

    if not latent:
        @pl.when(j == nblk - 1)
        def _():
            o_ref[0, 0] = st_ref[...]


def _gla(p3, lr3, wa_pad, ba, o_gain, s0, latent):
    n = p3.shape[1]
    nblk = n // GLA_ROWS
    seq_blk = lambda d, j: jnp.where(d == 0, j, nblk - 1 - j)
    in_specs = [
        pl.BlockSpec((1, GLA_ROWS, GLA_K_DIM), lambda b, d, j: (b, seq_blk(d, j), COL_GQ)),
        pl.BlockSpec((1, GLA_ROWS, GLA_K_DIM), lambda b, d, j: (b, seq_blk(d, j), COL_GK)),
        pl.BlockSpec((1, GLA_ROWS, GLA_V_DIM), lambda b, d, j: (b, seq_blk(d, j), COL_GV)),
    ]
    args = [p3, p3, p3]
    if latent:
        in_specs.append(pl.BlockSpec((1, GLA_ROWS, GLA_V_DIM), lambda b, d, j: (b, seq_blk(d, j), COL_GG)))
        args.append(p3)
    in_specs += [
        pl.BlockSpec((1, GLA_ROWS, LR_PAD), lambda b, d, j: (b, seq_blk(d, j), 0)),
        pl.BlockSpec((1, LR_PAD, GLA_K_DIM), lambda b, d, j: (d, 0, 0)),
        pl.BlockSpec((1, 1, GLA_K_DIM), lambda b, d, j: (d, 0, 0)),
    ]
    args += [lr3, wa_pad, ba]
    if latent:
        in_specs.append(pl.BlockSpec((1, GLA_DV), lambda b, d, j: (0, 0)))
        args.append(o_gain)
    state_spec = pl.BlockSpec((1, 1, GLA_HEADS, GLA_DV, GLA_DK), lambda b, d, j: (b, d, 0, 0, 0))
    in_specs.append(state_spec)
    args.append(s0)
    scratch = [pltpu.VMEM((GLA_HEADS, GLA_DV, GLA_DK), F32)]
    if latent:
        out_shape = jax.ShapeDtypeStruct((N_BATCH, n, GLA_V_DIM), BF16)
        out_spec = pl.BlockSpec((1, GLA_ROWS, GLA_V_DIM),
                                lambda b, d, j: (b, jnp.where(d == 0, nblk - 1, nblk - 1 - j), 0))
        scratch.append(pltpu.VMEM((n, GLA_V_DIM), F32))
    else:
        out_shape = jax.ShapeDtypeStruct(s0.shape, F32)
        out_spec = state_spec
    return pl.pallas_call(
        functools.partial(_gla_kernel, latent, nblk),
        out_shape=out_shape,
        grid=(N_BATCH, 2, nblk),
        in_specs=in_specs,
        out_specs=out_spec,
        scratch_shapes=scratch,
        compiler_params=_cparams(("arbitrary", "arbitrary", "arbitrary")),
        name="gla_latent" if latent else "gla_context_state",
    )(*args)


def _router_logits(h, rw_ref):
    hh = h.astype(BF16)
    hl = (h - hh.astype(F32)).astype(BF16)
    return _dot(hh, rw_ref[0]) + _dot(hl, rw_ref[0]) + _dot(hh, rw_ref[1])


def _residual_and_moe_input(mix, x_ref, gate_ref, sh_ref, sc_ref, rwt_ref, x_out_ref, h_out_ref, lg_ref):
    x1 = x_ref[...] + gate_ref[0] * mix
    x_out_ref[...] = x1
    h = _rms_modulate(x1, sh_ref[0], sc_ref[0])
    h_out_ref[...] = h
    lg_ref[...] = _router_logits(h, rwt_ref)


def _attn_out_kernel(a1_ref, a2_ref, w_ref, x_ref, gate_ref, sh_ref, sc_ref, rwt_ref, x_out_ref, h_out_ref, lg_ref):
    mix = _dot(a1_ref[...], w_ref[0:A_Q_DIM, :]) + _dot(a2_ref[...], w_ref[A_Q_DIM:, :])
    _residual_and_moe_input(mix, x_ref, gate_ref, sh_ref, sc_ref, rwt_ref, x_out_ref, h_out_ref, lg_ref)


def _conv_out_kernel(gb_ref, gc_ref, u_ref, gcp_ref, up_ref, gcn_ref, un_ref, cw_ref, w_ref,
                     x_ref, gate_ref, sh_ref, sc_ref, rwt_ref, x_out_ref, h_out_ref, lg_ref):
    i = pl.program_id(0)
    tiles_per_seq = SEQ_LEN // OUT_TM
    t = gc_ref[...].astype(F32) * u_ref[...].astype(F32)
    first = (i % tiles_per_seq) == 0
    last = (i % tiles_per_seq) == tiles_per_seq - 1
    halo_last = slice(CONV_HALO - 1, CONV_HALO)
    t_before = jnp.where(first, 0.0, gcp_ref[halo_last, :].astype(F32) * up_ref[halo_last, :].astype(F32))
    t_after = jnp.where(last, 0.0, gcn_ref[0:1, :].astype(F32) * un_ref[0:1, :].astype(F32))
    row = lax.broadcasted_iota(I32, t.shape, 0)
    t_up = jnp.where(row == 0, t_before, pltpu.roll(t, 1, 0))
    t_dn = jnp.where(row == OUT_TM - 1, t_after, pltpu.roll(t, OUT_TM - 1, 0))
    y = cw_ref[0:1, :] * t_up + cw_ref[1:2, :] * t + cw_ref[2:3, :] * t_dn
    mix = _dot((gb_ref[...].astype(F32) * y).astype(BF16), w_ref[...])
    _residual_and_moe_input(mix, x_ref, gate_ref, sh_ref, sc_ref, rwt_ref, x_out_ref, h_out_ref, lg_ref)


def _mixer_out(kernel_fn, mixer_specs, mixer_args, x2d, gate, shift, scale, rwt, name):
    d = D_MODEL
    mod_spec = pl.BlockSpec((1, 1, d), lambda i: (i * OUT_TM // SEQ_LEN, 0, 0))
    row_spec = pl.BlockSpec((OUT_TM, d), lambda i: (i, 0))
    return pl.pallas_call(
        kernel_fn,
        out_shape=[jax.ShapeDtypeStruct((N_TOK, d), F32), jax.ShapeDtypeStruct((N_TOK, d), F32),
                   jax.ShapeDtypeStruct((N_TOK, ROUTE_LANES), F32)],
        grid=(N_TOK // OUT_TM,),
        in_specs=mixer_specs + [row_spec, mod_spec, mod_spec, mod_spec,
                                pl.BlockSpec((2, d, ROUTE_LANES), lambda i: (0, 0, 0))],
        out_specs=[row_spec, row_spec, pl.BlockSpec((OUT_TM, ROUTE_LANES), lambda i: (i, 0))],
        compiler_params=_cparams(("arbitrary",)),
        name=name,
    )(*mixer_args, x2d, gate, shift, scale, rwt)


def _attn_out(attn2d, gla2d, w_out, x2d, gate, shift, scale, rwt):
    half = pl.BlockSpec((OUT_TM, A_Q_DIM), lambda i: (i, 0))
    specs = [half, half, pl.BlockSpec((D_MODEL, D_MODEL), lambda i: (0, 0))]
    return _mixer_out(_attn_out_kernel, specs, [attn2d, gla2d, w_out], x2d, gate, shift, scale, rwt,
                      "attn_out_projection")


def _conv_out(g2d, conv_w, w_out, x2d, gate, shift, scale, rwt):
    d = D_MODEL
    sub = OUT_TM // CONV_HALO
    last_halo = N_TOK // CONV_HALO - 1
    main = lambda col: pl.BlockSpec((OUT_TM, d), lambda i: (i, col))
    before = lambda col: pl.BlockSpec((CONV_HALO, d), lambda i: (jnp.maximum(i * sub - 1, 0), col))
    after = lambda col: pl.BlockSpec((CONV_HALO, d), lambda i: (jnp.minimum((i + 1) * sub, last_halo), col))
    specs = [main(0), main(1), main(2), before(1), before(2), after(1), after(2),
             pl.BlockSpec((3, d), lambda i: (0, 0)), pl.BlockSpec((d, d), lambda i: (0, 0))]
    return _mixer_out(_conv_out_kernel, specs, [g2d] * 7 + [conv_w, w_out], x2d, gate, shift, scale, rwt,
                      "conv_out_projection")


def _first_argmax8(x, idx8):
    m = jnp.max(x, axis=0, keepdims=True)
    a = jnp.min(jnp.where(x == m, idx8, float(GROUP_SIZE)), axis=0, keepdims=True)
    return m, a


def _route_kernel(lg_ref, rb_ref, e_ref, w_ref, dest_ref, meta_ref, cnt_ref, carry_ref, pstart_ref):
    phase = pl.program_id(0)
    j = pl.program_id(1)
    tt = ROUTE_TT
    sc = jax.nn.sigmoid(lg_ref[...].T[:N_EXPERTS])
    grp = sc + rb_ref[...]
    idx8 = lax.broadcasted_iota(I32, (GROUP_SIZE, tt), 0).astype(F32)
    groups = [grp[g * GROUP_SIZE:(g + 1) * GROUP_SIZE] for g in range(N_GROUPS)]
    gscore = []
    for x in groups:
        m1, a1 = _first_argmax8(x, idx8)
        m2, _ = _first_argmax8(jnp.where(idx8 == a1, -jnp.inf, x), idx8)
        gscore.append(m1 + m2)
    gmax = functools.reduce(jnp.maximum, gscore)
    gsel = jnp.full((1, tt), float(N_GROUPS), F32)
    for g in reversed(range(N_GROUPS)):
        gsel = jnp.where(gscore[g] == gmax, float(g), gsel)
    in_grp = groups[0]
    for g in range(1, N_GROUPS):
        in_grp = jnp.where(gsel == float(g), groups[g], in_grp)
    _, a1 = _first_argmax8(in_grp, idx8)
    _, a2 = _first_argmax8(jnp.where(idx8 == a1, -jnp.inf, in_grp), idx8)
    e0 = gsel * GROUP_SIZE + a1
    e1 = gsel * GROUP_SIZE + a2
    idx32 = lax.broadcasted_iota(I32, (N_EXPERTS, tt), 0).astype(F32)
    oh0 = idx32 == e0
    oh1 = idx32 == e1
    w0 = jnp.sum(jnp.where(oh0, sc, 0.0), axis=0, keepdims=True)
    w1 = jnp.sum(jnp.where(oh1, sc, 0.0), axis=0, keepdims=True)
    wsum = w0 + w1
    ohs = jnp.where(oh0 | oh1, 1.0, 0.0)
    tile_cnt = jnp.sum(ohs, axis=1, keepdims=True)

    @pl.when(phase == 0)
    def _():
        @pl.when(j == 0)
        def _():
            cnt_ref[...] = jnp.zeros_like(cnt_ref)
        cnt_ref[...] += jnp.broadcast_to(tile_cnt, cnt_ref.shape)

    @pl.when(phase == 1)
    def _():
        @pl.when(j == 0)
        def _():
            cnt = cnt_ref[...].astype(I32)
            pc = ((cnt + (MOE_BLOCK - 1)) // MOE_BLOCK) * MOE_BLOCK
            r = lax.broadcasted_iota(I32, (N_EXPERTS, N_EXPERTS), 0)
            c = lax.broadcasted_iota(I32, (N_EXPERTS, N_EXPERTS), 1)
            strict_lower = jnp.where(c < r, 1.0, 0.0).astype(BF16)
            pcf = pc.astype(F32)
            pstart = _dot(strict_lower, pcf.astype(BF16))
            pstart_ref[...] = pstart
            carry_ref[...] = jnp.zeros_like(carry_ref)
            pends = pstart + pcf
            lane = lax.broadcasted_iota(I32, pends.shape, 1).astype(F32)
            blk_exp = jnp.sum(jnp.where(lane * MOE_BLOCK >= pends, 1.0, 0.0), axis=0, keepdims=True)
            blk_exp = jnp.minimum(blk_exp, float(N_EXPERTS - 1))
            n_used = jnp.sum(pcf, axis=0, keepdims=True) / MOE_BLOCK
            own = lax.broadcasted_iota(I32, pends.shape, 0).astype(F32) == blk_exp
            cnt_blk = jnp.sum(jnp.where(own, cnt_ref[...], 0.0), axis=0, keepdims=True)
            start_blk = jnp.sum(jnp.where(own, pstart, 0.0), axis=0, keepdims=True)
            valid = jnp.clip(cnt_blk - (lane[0:1] * MOE_BLOCK - start_blk), 0.0, float(MOE_BLOCK))
            meta = jnp.where(lane[0:1] == float(META_LANES - 1), n_used, blk_exp)
            tail = jnp.where(cnt > 0, pends - MOE_BLOCK, -1.0)
            expert_lane = lax.broadcasted_iota(I32, pends.shape, 0).astype(F32) == lane
            tail = jnp.sum(jnp.where(expert_lane, tail, 0.0), axis=0, keepdims=True)
            meta_ref[...] = jnp.concatenate([meta, valid, tail], axis=0).astype(I32)

        rr = lax.broadcasted_iota(I32, (tt, tt), 0)
        cc = lax.broadcasted_iota(I32, (tt, tt), 1)
        strict_upper = jnp.where(rr < cc, 1.0, 0.0).astype(BF16)
        before = _dot(ohs.astype(BF16), strict_upper)
        pos = before + carry_ref[:, 0:1] + pstart_ref[:, 0:1]
        d0 = jnp.sum(jnp.where(oh0, pos, 0.0), axis=0, keepdims=True)
        d1 = jnp.sum(jnp.where(oh1, pos, 0.0), axis=0, keepdims=True)
        carry_ref[...] += jnp.broadcast_to(tile_cnt, carry_ref.shape)
        e_ref[...] = jnp.concatenate([e0, e1], axis=0).astype(I32)
        w_ref[...] = jnp.concatenate([w0 / wsum, w1 / wsum], axis=0)
        dest_ref[...] = jnp.concatenate([d0, d1], axis=0).astype(I32)


def _route(logits_t, router_b):
    nt = N_TOK // ROUTE_TT
    tok = lambda dt: jax.ShapeDtypeStruct((2, N_TOK), dt)
    tok_spec = pl.BlockSpec((2, ROUTE_TT), lambda p, j: (0, j * p))
    return pl.pallas_call(
        _route_kernel,
        out_shape=[tok(I32), tok(F32), tok(I32), jax.ShapeDtypeStruct((META_ROWS, META_LANES), I32)],
        grid=(2, nt),
        in_specs=[pl.BlockSpec((ROUTE_TT, ROUTE_LANES), lambda p, j: (j, 0)),
                  pl.BlockSpec((N_EXPERTS, 1), lambda p, j: (0, 0))],
        out_specs=[tok_spec, tok_spec, tok_spec, pl.BlockSpec((META_ROWS, META_LANES), lambda p, j: (0, 0))],
        scratch_shapes=[pltpu.VMEM((N_EXPERTS, 128), F32), pltpu.VMEM((N_EXPERTS, 128), F32),
                        pltpu.VMEM((N_EXPERTS, 128), F32)],
        compiler_params=_cparams(("arbitrary", "arbitrary")),
        name="router_slot_assignment",
    )(logits_t, router_b.reshape(N_EXPERTS, 1))


def _dispatch_copy(dest_ref, h_ref, xs_ref, sem, i, r, k):
    dst = dest_ref[k * N_TOK + i * TOK_TM + r]
    return pltpu.make_async_copy(h_ref.at[pl.ds(r, 1)], xs_ref.at[pl.ds(dst, 1)], sem)


def _zero_block_copy(zero_ref, xs_ref, sem, row):
    return pltpu.make_async_copy(zero_ref, xs_ref.at[pl.ds(pl.multiple_of(row, MOE_BLOCK), MOE_BLOCK)], sem)


def _zero_fill_padding(meta_ref, zero_ref, xs_ref, sem):
    zero_ref[...] = jnp.zeros_like(zero_ref)
    n_used = meta_ref[META_LANES - 1]

    def for_each_block(fn):
        def expert_tail(e, carry):
            row = meta_ref[2 * META_LANES + e]

            @pl.when(row >= 0)
            def _():
                fn(_zero_block_copy(zero_ref, xs_ref, sem, row))
            return carry

        def unused(b, carry):
            fn(_zero_block_copy(zero_ref, xs_ref, sem, b * MOE_BLOCK))
            return carry

        lax.fori_loop(0, N_EXPERTS, expert_tail, 0)
        lax.fori_loop(n_used, MOE_NBLK, unused, 0)

    for_each_block(lambda copy: copy.start())
    for_each_block(lambda copy: copy.wait())


def _dispatch_kernel(dest_ref, meta_ref, h_ref, xs_ref, zero_ref, sem):
    i = pl.program_id(0)

    @pl.when(i == 0)
    def _():
        _zero_fill_padding(meta_ref, zero_ref, xs_ref, sem)

    unroll = 8

    def start(g, carry):
        for u in range(unroll):
            for k in range(2):
                _dispatch_copy(dest_ref, h_ref, xs_ref, sem, i, g * unroll + u, k).start()
        return carry

    def wait(g, carry):
        for u in range(unroll):
            for k in range(2):
                _dispatch_copy(dest_ref, h_ref, xs_ref, sem, i, g * unroll + u, k).wait()
        return carry

    lax.fori_loop(0, TOK_TM // unroll, start, 0)
    lax.fori_loop(0, TOK_TM // unroll, wait, 0)


def _dispatch(dest_flat, meta_flat, h2d):
    d = D_MODEL
    return pl.pallas_call(
        _dispatch_kernel,
        out_shape=jax.ShapeDtypeStruct((MOE_SLOTS, d), F32),
        grid_spec=pltpu.PrefetchScalarGridSpec(
            num_scalar_prefetch=2,
            grid=(N_TOK // TOK_TM,),
            in_specs=[pl.BlockSpec((TOK_TM, d), lambda i, dest, meta: (i, 0))],
            out_specs=pl.BlockSpec(memory_space=pl.ANY),
            scratch_shapes=[pltpu.VMEM((MOE_BLOCK, d), F32), pltpu.SemaphoreType.DMA],
        ),
        compiler_params=_cparams(("arbitrary",)),
        name="moe_dispatch",
    )(dest_flat, meta_flat, h2d)


def _expert_kernel(meta_ref, x_ref, w1_ref, w3_ref, w2_ref, y_ref, xb_ref):
    blk = pl.program_id(0)
    f = pl.program_id(1)
    valid = jnp.where(blk < meta_ref[META_LANES - 1], meta_ref[META_LANES + blk], 0)

    @pl.when(f == 0)
    def _():
        xb_ref[...] = x_ref[...].astype(BF16)

        @pl.when(valid <= MOE_SUB)
        def _():
            y_ref[MOE_SUB:, :] = jnp.zeros((MOE_BLOCK - MOE_SUB, y_ref.shape[1]), F32)

        @pl.when(valid == 0)
        def _():
            y_ref[:MOE_SUB, :] = jnp.zeros((MOE_SUB, y_ref.shape[1]), F32)

    def compute(nrows):
        x = xb_ref[:nrows, :]
        a = _silu(_dot(x, w1_ref[0, 0].astype(BF16))) * _dot(x, w3_ref[0, 0].astype(BF16))
        y = _dot(a.astype(BF16), w2_ref[0, 0].astype(BF16))

        @pl.when(f == 0)
        def _():
            y_ref[:nrows, :] = y

        @pl.when(f > 0)
        def _():
            y_ref[:nrows, :] += y

    @pl.when(valid > MOE_SUB)
    def _():
        compute(MOE_BLOCK)

    @pl.when((valid > 0) & (valid <= MOE_SUB))
    def _():
        compute(MOE_SUB)


def _experts(meta, xs, w1, w3, w2, layer):
    d = D_MODEL
    nf = D_EXPERT // MOE_FC
    n_used = lambda meta: meta[META_LANES - 1]
    used = lambda i, meta: jnp.minimum(i, n_used(meta) - 1)
    chunk = lambda i, f, meta: jnp.where(i < n_used(meta), f, nf - 1)
    w_in_spec = pl.BlockSpec((1, 1, d, MOE_FC), lambda i, f, meta: (layer, meta[used(i, meta)], 0, chunk(i, f, meta)))
    w_out_spec = pl.BlockSpec((1, 1, MOE_FC, d), lambda i, f, meta: (layer, meta[used(i, meta)], chunk(i, f, meta), 0))
    return pl.pallas_call(
        _expert_kernel,
        out_shape=jax.ShapeDtypeStruct((MOE_SLOTS, d), F32),
        grid_spec=pltpu.PrefetchScalarGridSpec(
            num_scalar_prefetch=1,
            grid=(MOE_NBLK, nf),
            in_specs=[pl.BlockSpec((MOE_BLOCK, d), lambda i, f, meta: (used(i, meta), 0)),
                      w_in_spec, w_in_spec, w_out_spec],
            out_specs=pl.BlockSpec((MOE_BLOCK, d), lambda i, f, meta: (i, 0)),
            scratch_shapes=[pltpu.VMEM((MOE_BLOCK, d), BF16)],
        ),
        compiler_params=_cparams(("arbitrary", "arbitrary")),
        name="moe_experts",
    )(meta, xs, w1, w3, w2)


def _combine_copy(dest_ref, yb_ref, buf_ref, sem, i, r, k):
    src = dest_ref[k * N_TOK + i * TOK_TM + r]
    return pltpu.make_async_copy(yb_ref.at[pl.ds(src, 1)], buf_ref.at[k, pl.ds(r, 1)], sem)


def _combine_kernel(dest_ref, yb_ref, x_ref, gate_ref, w_ref, o_ref, buf_ref, sem):
    i = pl.program_id(0)

    unroll = 8

    def start(g, carry):
        for u in range(unroll):
            for k in range(2):
                _combine_copy(dest_ref, yb_ref, buf_ref, sem, i, g * unroll + u, k).start()
        return carry

    def wait(g, carry):
        for u in range(unroll):
            for k in range(2):
                _combine_copy(dest_ref, yb_ref, buf_ref, sem, i, g * unroll + u, k).wait()
        return carry

    lax.fori_loop(0, TOK_TM // unroll, start, 0)
    lax.fori_loop(0, TOK_TM // unroll, wait, 0)
    w = w_ref[...]
    y = w[:, 0:1] * buf_ref[0] + w[:, 1:2] * buf_ref[1]
    o_ref[...] = x_ref[...] + gate_ref[0] * y


def _combine(dest_flat, yb, x2d, gate, w_tok):
    d = D_MODEL
    return pl.pallas_call(
        _combine_kernel,
        out_shape=jax.ShapeDtypeStruct((N_TOK, d), F32),
        grid_spec=pltpu.PrefetchScalarGridSpec(
            num_scalar_prefetch=1,
            grid=(N_TOK // TOK_TM,),
            in_specs=[pl.BlockSpec(memory_space=pl.ANY),
                      pl.BlockSpec((TOK_TM, d), lambda i, dest: (i, 0)),
                      pl.BlockSpec((1, 1, d), lambda i, dest: (i * TOK_TM // SEQ_LEN, 0, 0)),
                      pl.BlockSpec((TOK_TM, 2), lambda i, dest: (i, 0))],
            out_specs=pl.BlockSpec((TOK_TM, d), lambda i, dest: (i, 0)),
            scratch_shapes=[pltpu.VMEM((2, TOK_TM, d), F32), pltpu.SemaphoreType.DMA],
        ),
        compiler_params=_cparams(("arbitrary",)),
        name="moe_combine",
    )(dest_flat, yb, x2d, gate, w_tok)


def _moe(h2d, logits_t, x2d, gate, router_b, w1, w3, w2, layer):
    _, w_sel, dest, meta = _route(logits_t, router_b)
    dest_flat = dest.reshape(2 * N_TOK)
    meta_flat = meta.reshape(META_ROWS * META_LANES)
    xs = _dispatch(dest_flat, meta_flat, h2d)
    yb = _experts(meta_flat, xs, w1, w3, w2, layer)
    return _combine(dest_flat, yb, x2d, gate, w_sel.T)


def _rope_tables():
    f = HEAD_DIM // 4
    inv_freq = ROPE_THETA ** (-jnp.arange(f, dtype=F32) / f)
    pos = jnp.arange(SEQ_LEN)
    row = (pos // GRID_W).astype(F32)[:, None] * inv_freq
    col = (pos % GRID_W).astype(F32)[:, None] * inv_freq
    cos = jnp.concatenate([jnp.cos(row), jnp.cos(row), jnp.cos(col), jnp.cos(col)], axis=-1)
    sin = jnp.concatenate([-jnp.sin(row), jnp.sin(row), -jnp.sin(col), jnp.sin(col)], axis=-1)
    return cos, sin


def _permute_attn_w_in(w):
    o = [0, A_Q_DIM, A_KV_DIM, A_KV_DIM, GLA_K_DIM, GLA_K_DIM, GLA_V_DIM, GLA_V_DIM]
    s = [sum(o[:i + 1]) for i in range(len(o))]
    seg = lambda i: w[:, s[i]:s[i + 1]] if i + 1 < len(s) else w[:, s[i]:]
    aq, ak, av, gq, gk, gv, gg = (seg(i) for i in range(7))
    main = jnp.concatenate([aq, gv, gg, gq, gk, ak, av], axis=1).astype(BF16)
    lr = jnp.pad(w[:, PROJ_DIM:], ((0, 0), (0, LR_PAD - 2 * GLA_RANK))).astype(BF16)
    return main, lr


def kernel(x, c, ctx, c_ctx, mod_w, mod_b, attn_w_in, attn_q_norm, attn_k_norm, attn_sink, gla_wa2, gla_ba,
           gla_norm, attn_w_out, conv_w_in, conv_w, conv_w_out, router_w, router_b, exp_w1, exp_w3, exp_w2):
    d = D_MODEL
    cc = jnp.concatenate([c, c_ctx[None], jnp.zeros((8 - N_BATCH - 1, d), F32)], axis=0)
    m = _modulation(cc, mod_w, mod_b)
    mods = [[m[l, :, i * d:(i + 1) * d].reshape(8, 1, d) for i in range(6)] for l in range(mod_w.shape[0])]
    lat_row = lambda i: i * PROJ_TM // SEQ_LEN
    ctx_row = lambda i: N_BATCH
    rw_pad = jnp.pad(router_w, ((0, 0), (0, ROUTE_LANES - N_EXPERTS)))
    rw_hi = rw_pad.astype(BF16)
    rwt = jnp.stack([rw_hi, (rw_pad - rw_hi.astype(F32)).astype(BF16)])
    x2d = x.reshape(N_TOK, d)

    w_main, w_lr = _permute_attn_w_in(attn_w_in[0])
    p_lat, lr_lat = _projection(x2d, mods[0][0], mods[0][1], w_main, w_lr, lat_row)
    p_ctx, lr_ctx = _projection(ctx.reshape(N_BATCH * CTX_LEN, d), mods[0][0], mods[0][1], w_main, w_lr, ctx_row)
    p_lat = p_lat.reshape(N_BATCH, SEQ_LEN, PROJ_DIM)
    p_ctx = p_ctx.reshape(N_BATCH, CTX_LEN, PROJ_DIM)
    cos, sin_signed = _rope_tables()
    attn = _attention(p_lat, p_ctx, cos, sin_signed, attn_q_norm[0][None], attn_k_norm[0][None], attn_sink[0])
    wa_pad = jnp.zeros((2, LR_PAD, GLA_K_DIM), F32)
    for di in range(2):
        wa_pad = wa_pad.at[di, di * GLA_RANK:(di + 1) * GLA_RANK].set(gla_wa2[0, di])
    wa_pad = wa_pad.astype(BF16)
    ba = gla_ba[0].reshape(2, 1, GLA_K_DIM)
    s_zero = jnp.zeros((N_BATCH, 2, GLA_HEADS, GLA_DV, GLA_DK), F32)
    s_ctx = _gla(p_ctx, lr_ctx.reshape(N_BATCH, CTX_LEN, LR_PAD), wa_pad, ba, None, s_zero, latent=False)
    gla = _gla(p_lat, lr_lat.reshape(N_BATCH, SEQ_LEN, LR_PAD), wa_pad, ba, gla_norm[0][None], s_ctx, latent=True)
    x1, h2, lg = _attn_out(attn.reshape(N_TOK, A_Q_DIM), gla.reshape(N_TOK, GLA_V_DIM), attn_w_out[0].astype(BF16),
                           x2d, mods[0][2], mods[0][3], mods[0][4], rwt)
    x2 = _moe(h2, lg, x1, mods[0][5], router_b, exp_w1, exp_w3, exp_w2, 0)

    g = _projection(x2, mods[1][0], mods[1][1], conv_w_in[0].astype(BF16), None, lat_row)
    x3, h2, lg = _conv_out(g, conv_w[0], conv_w_out[0].astype(BF16), x2, mods[1][2], mods[1][3], mods[1][4], rwt)
    x4 = _moe(h2, lg, x3, mods[1][5], router_b, exp_w1, exp_w3, exp_w2, 1)
    return x4.reshape(N_BATCH, SEQ_LEN, d)
```

```python
import functools

import jax
import jax.numpy as jnp
from jax import lax
from jax.experimental import pallas as pl
from jax.experimental.pallas import tpu as pltpu

F32 = jnp.float32
BF16 = jnp.bfloat16
I32 = jnp.int32

D_MODEL = 2048
N_BATCH = 4
SEQ_LEN = 2048
CTX_LEN = 256
N_TOK = N_BATCH * SEQ_LEN
GRID_W = 64
HEAD_DIM = 128
A_Q_HEADS = 8
A_KV_HEADS = 2
A_GROUP = A_Q_HEADS // A_KV_HEADS
WINDOW = 128
A_BLOCK = 128
ROPE_THETA = 10000.0
GLA_HEADS = 4
GLA_DK = 128
GLA_DV = 256
GLA_RANK = 16
GLA_TAU = 16.0
GLA_CHUNK = 64
N_EXPERTS = 32
N_GROUPS = 4
GROUP_SIZE = N_EXPERTS // N_GROUPS
D_EXPERT = 1024
EPS = 1e-6
NEG_INF = -1e30
A_Q_DIM = A_Q_HEADS * HEAD_DIM
A_KV_DIM = A_KV_HEADS * HEAD_DIM
GLA_K_DIM = GLA_HEADS * GLA_DK
GLA_V_DIM = GLA_HEADS * GLA_DV
PROJ_DIM = A_Q_DIM + 2 * A_KV_DIM + 2 * GLA_K_DIM + 2 * GLA_V_DIM
LR_PAD = 128

COL_AQ = 0
COL_GV = 1
COL_GG = 2
COL_GQ = 6
COL_GK = 7
COL_AK = 32
COL_AV = 34

MOD_TN = 1024
PROJ_TM = 512
PROJ_TN = 512
OUT_TM = 256
CONV_HALO = 16
GLA_ROWS = 256
ROUTE_TT = 512
ROUTE_LANES = 128
MOE_SUB = 256
MOE_BLOCK = 3 * MOE_SUB
MOE_FC = 256
MOE_NBLK = -(-(N_TOK * 2 + N_EXPERTS * (MOE_BLOCK - 1)) // MOE_BLOCK)
MOE_SLOTS = MOE_NBLK * MOE_BLOCK
TOK_TM = 256
META_ROWS = 3
META_LANES = 128
VMEM_LIMIT = 56 * 1024 * 1024


def _cparams(sem):
    return pltpu.CompilerParams(dimension_semantics=sem, vmem_limit_bytes=VMEM_LIMIT)


def _silu(x):
    return x * jax.nn.sigmoid(x)


def _dot(a, b):
    return jnp.dot(a, b, preferred_element_type=F32)


def _dot_nt(a, b):
    return lax.dot_general(a, b, (((1,), (1,)), ((), ())), preferred_element_type=F32)


def _dot_tn(a, b):
    return lax.dot_general(a, b, (((0,), (0,)), ((), ())), preferred_element_type=F32)


def _rms_modulate(x, shift, scale):
    r = lax.rsqrt(jnp.mean(x * x, axis=-1, keepdims=True) + EPS)
    return (x * r) * (1.0 + scale) + shift


def _mod_kernel(cc_ref, w_ref, b_ref, o_ref):
    a = _silu(cc_ref[...])
    o_ref[0] = _dot(a.astype(BF16), w_ref[0].astype(BF16)) + b_ref[0]


def _modulation(cc, mod_w, mod_b):
    depth, d, n = mod_w.shape
    return pl.pallas_call(
        _mod_kernel,
        out_shape=jax.ShapeDtypeStruct((depth, 8, n), F32),
        grid=(depth, n // MOD_TN),
        in_specs=[
            pl.BlockSpec((8, d), lambda l, j: (0, 0)),
            pl.BlockSpec((1, d, MOD_TN), lambda l, j: (l, 0, j)),
            pl.BlockSpec((1, 1, MOD_TN), lambda l, j: (l, 0, j)),
        ],
        out_specs=pl.BlockSpec((1, 8, MOD_TN), lambda l, j: (l, 0, j)),
        compiler_params=_cparams(("arbitrary", "arbitrary")),
        name="adaln_modulation",
    )(cc, mod_w, mod_b.reshape(depth, 1, n))


def _proj_kernel(with_lr, x_ref, sh_ref, sc_ref, w_ref, *rest):
    if with_lr:
        wlr_ref, o_ref, olr_ref = rest
    else:
        (o_ref,) = rest
    hb = _rms_modulate(x_ref[...], sh_ref[0], sc_ref[0]).astype(BF16)
    for c in range(w_ref.shape[1] // PROJ_TN):
        cols = slice(c * PROJ_TN, (c + 1) * PROJ_TN)
        o_ref[:, cols] = _dot(hb, w_ref[:, cols]).astype(BF16)
    if with_lr:
        olr_ref[...] = _dot(hb, wlr_ref[...]).astype(BF16)


def _projection(x2d, shift, scale, w, w_lr, mod_row):
    rows, d = x2d.shape
    n = w.shape[1]
    with_lr = w_lr is not None
    resident = lambda shape: pl.BlockSpec(shape, lambda i: (0, 0), pipeline_mode=pl.Buffered(1))
    in_specs = [
        pl.BlockSpec((PROJ_TM, d), lambda i: (i, 0)),
        pl.BlockSpec((1, 1, d), lambda i: (mod_row(i), 0, 0)),
        pl.BlockSpec((1, 1, d), lambda i: (mod_row(i), 0, 0)),
        resident((d, n)),
    ]
    out_shape = [jax.ShapeDtypeStruct((rows, n), BF16)]
    out_specs = [pl.BlockSpec((PROJ_TM, n), lambda i: (i, 0))]
    args = [x2d, shift, scale, w]
    if with_lr:
        in_specs.append(resident((d, LR_PAD)))
        out_shape.append(jax.ShapeDtypeStruct((rows, LR_PAD), BF16))
        out_specs.append(pl.BlockSpec((PROJ_TM, LR_PAD), lambda i: (i, 0)))
        args.append(w_lr)
    res = pl.pallas_call(
        functools.partial(_proj_kernel, with_lr),
        out_shape=out_shape,
        grid=(rows // PROJ_TM,),
        in_specs=in_specs,
        out_specs=out_specs,
        compiler_params=_cparams(("arbitrary",)),
        name="norm_mod_projection",
    )(*args)
    return res if with_lr else res[0]


def _swap_halves32(x):
    lane = lax.broadcasted_iota(I32, x.shape, 1)
    return jnp.where((lane & 63) < 32, pltpu.roll(x, 96, 1), pltpu.roll(x, 32, 1))


def _qk_norm(x, gain):
    return x * lax.rsqrt(jnp.mean(x * x, axis=-1, keepdims=True) + EPS) * gain


def _rope(x, cos, sin_signed):
    return x * cos + _swap_halves32(x) * sin_signed


ATT_WIN0 = CTX_LEN
ATT_LAT0 = CTX_LEN + A_BLOCK
ATT_ROWS = CTX_LEN + SEQ_LEN + 2 * A_BLOCK
ATT_WIN = 3 * A_BLOCK


def _attn_kernel(sink_ref, q_ref, k_ref, v_ref, kx_ref, vx_ref, cos_ref, sin_ref, qg_ref, kg_ref, o_ref,
                 kn_ref, vn_ref, band_ref):
    kvh = pl.program_id(1)
    nb = SEQ_LEN // A_BLOCK
    k_gain = kg_ref[...]
    q_gain = qg_ref[...]

    pad = jnp.zeros((A_BLOCK, HEAD_DIM), BF16)
    kn_ref[0:CTX_LEN, :] = _qk_norm(kx_ref[0].astype(F32), k_gain).astype(BF16)
    vn_ref[0:CTX_LEN, :] = vx_ref[0]
    for ref in (kn_ref, vn_ref):
        ref[ATT_WIN0:ATT_LAT0, :] = pad
        ref[ATT_LAT0 + SEQ_LEN:, :] = pad
    vn_ref[ATT_LAT0:ATT_LAT0 + SEQ_LEN, :] = v_ref[0]
    prep_rows = 4 * A_BLOCK
    for c in range(SEQ_LEN // prep_rows):
        r = slice(c * prep_rows, (c + 1) * prep_rows)
        kc = _rope(_qk_norm(k_ref[0, r, :].astype(F32), k_gain), cos_ref[r, :], sin_ref[r, :])
        kn_ref[ATT_LAT0 + c * prep_rows:ATT_LAT0 + (c + 1) * prep_rows, :] = kc.astype(BF16)

    qi = lax.broadcasted_iota(I32, (A_BLOCK, ATT_WIN), 0)
    wj = lax.broadcasted_iota(I32, (A_BLOCK, ATT_WIN), 1)
    band = jnp.where(jnp.abs(wj - A_BLOCK - qi) <= WINDOW, 1.0, 0.0)
    band_ref[0] = jnp.where(wj >= A_BLOCK, band, 0.0)
    band_ref[1] = band
    band_ref[2] = jnp.where(wj < 2 * A_BLOCK, band, 0.0)

    rows = A_GROUP * A_BLOCK
    rcol = lax.broadcasted_iota(I32, (rows, 1), 0)
    sink = jnp.zeros((rows, 1), F32)
    for g in range(A_GROUP):
        sink = jnp.where((rcol >= g * A_BLOCK) & (rcol < (g + 1) * A_BLOCK), sink_ref[kvh * A_GROUP + g], sink)

    def query_block(n, carry):
        qrows = pl.ds(pl.multiple_of(n * A_BLOCK, A_BLOCK), A_BLOCK)
        wrows = pl.ds(pl.multiple_of(ATT_WIN0 + n * A_BLOCK, A_BLOCK), ATT_WIN)
        cos_q, sin_q = cos_ref[qrows, :], sin_ref[qrows, :]
        q = q_ref[0, qrows, :].astype(F32)
        qs = []
        for g in range(A_GROUP):
            qh = _rope(_qk_norm(q[:, g * HEAD_DIM:(g + 1) * HEAD_DIM], q_gain), cos_q, sin_q)
            qs.append((qh * HEAD_DIM ** -0.5).astype(BF16))
        qs = jnp.concatenate(qs, axis=0)
        mask = band_ref[jnp.where(n == 0, 0, jnp.where(n == nb - 1, 2, 1))]
        mask = jnp.concatenate([mask] * A_GROUP, axis=0)
        s_ctx = _dot_nt(qs, kn_ref[0:CTX_LEN, :])
        s_win = jnp.where(mask > 0.5, _dot_nt(qs, kn_ref[wrows, :]), NEG_INF)
        m = jnp.maximum(jnp.max(s_ctx, axis=-1, keepdims=True), jnp.max(s_win, axis=-1, keepdims=True))
        m = jnp.maximum(m, sink)
        p_ctx = jnp.exp(s_ctx - m)
        p_win = jnp.exp(s_win - m)
        denom = (jnp.sum(p_ctx, axis=-1, keepdims=True) + jnp.sum(p_win, axis=-1, keepdims=True)
                 + jnp.exp(sink - m))
        o = (_dot(p_ctx.astype(BF16), vn_ref[0:CTX_LEN, :]) + _dot(p_win.astype(BF16), vn_ref[wrows, :])) / denom
        for g in range(A_GROUP):
            o_ref[0, qrows, g * HEAD_DIM:(g + 1) * HEAD_DIM] = o[g * A_BLOCK:(g + 1) * A_BLOCK].astype(BF16)
        return carry

    lax.fori_loop(0, nb, query_block, 0)


def _attention(p_lat, p_ctx, cos, sin_signed, q_gain, k_gain, sink):
    gw = A_GROUP * HEAD_DIM
    lat = lambda width, col: pl.BlockSpec((1, SEQ_LEN, width), lambda b, k, s: (b, 0, col + k))
    ctx_blk = lambda col: pl.BlockSpec((1, CTX_LEN, HEAD_DIM), lambda b, k, s: (b, 0, col + k))
    full = lambda shape: pl.BlockSpec(shape, lambda b, k, s: (0,) * len(shape))
    return pl.pallas_call(
        _attn_kernel,
        out_shape=jax.ShapeDtypeStruct((N_BATCH, SEQ_LEN, A_Q_DIM), BF16),
        grid_spec=pltpu.PrefetchScalarGridSpec(
            num_scalar_prefetch=1,
            grid=(N_BATCH, A_KV_HEADS),
            in_specs=[
                lat(gw, COL_AQ), lat(HEAD_DIM, COL_AK), lat(HEAD_DIM, COL_AV),
                ctx_blk(COL_AK), ctx_blk(COL_AV),
                full((SEQ_LEN, HEAD_DIM)), full((SEQ_LEN, HEAD_DIM)),
                full((1, HEAD_DIM)), full((1, HEAD_DIM)),
            ],
            out_specs=pl.BlockSpec((1, SEQ_LEN, gw), lambda b, k, s: (b, 0, k)),
            scratch_shapes=[pltpu.VMEM((ATT_ROWS, HEAD_DIM), BF16), pltpu.VMEM((ATT_ROWS, HEAD_DIM), BF16),
                            pltpu.VMEM((3, A_BLOCK, ATT_WIN), F32)],
        ),
        compiler_params=_cparams(("arbitrary", "arbitrary")),
        name="windowed_sink_attention",
    )(sink, p_lat, p_lat, p_lat, p_ctx, p_ctx, cos, sin_signed, q_gain, k_gain)


def _split3_bf16(x):
    hi = x.astype(BF16)
    r1 = x - hi.astype(F32)
    mid = r1.astype(BF16)
    lo = (r1 - mid.astype(F32)).astype(BF16)
    return hi, mid, lo


def _gla_kernel(latent, nblk, q_ref, k_ref, v_ref, *rest):
    if latent:
        g_ref, lr_ref, wa_ref, ba_ref, gain_ref, s0_ref, o_ref, st_ref, ofwd_ref = rest
    else:
        lr_ref, wa_ref, ba_ref, s0_ref, o_ref, st_ref = rest
    d = pl.program_id(1)
    j = pl.program_id(2)
    nc = GLA_ROWS // GLA_CHUNK
    blk = jnp.where(d == 0, j, nblk - 1 - j)

    @pl.when(j == 0)
    def _():
        st_ref[...] = s0_ref[0, 0]

    if latent:
        @pl.when((pl.program_id(0) == 0) & (d == 0) & (j == 0))
        def _():
            ofwd_ref[...] = jnp.zeros_like(ofwd_ref)

    ii = lax.broadcasted_iota(I32, (GLA_CHUNK, GLA_CHUNK), 0)
    jj = lax.broadcasted_iota(I32, (GLA_CHUNK, GLA_CHUNK), 1)
    incl = ((jj - ii) * (1 - 2 * d)) <= 0
    incl_b = jnp.where(incl, 1.0, 0.0).astype(BF16)

    for i in range(nc):
        ci = jnp.where(d == 0, i, nc - 1 - i)
        r0 = pl.multiple_of(ci * GLA_CHUNK, GLA_CHUNK)
        rows = pl.ds(r0, GLA_CHUNK)
        z = _dot(lr_ref[0, rows, :], wa_ref[0]) + ba_ref[0]
        la = (jnp.minimum(z, 0.0) - jnp.log(1.0 + jnp.exp(-jnp.abs(z)))) / GLA_TAU
        hi, mid, lo = _split3_bf16(la)
        bcum = _dot(incl_b, hi) + _dot(incl_b, mid) + _dot(incl_b, lo)
        blast = jnp.sum(la, axis=0, keepdims=True)
        k = k_ref[0, rows, :].astype(F32)
        kl = (k * jnp.exp(blast - bcum)).astype(BF16)
        decay = jnp.exp(blast)
        v = v_ref[0, rows, :]
        if latent:
            qf = (q_ref[0, rows, :].astype(F32) * GLA_DK ** -0.5 * jnp.exp(bcum)).astype(BF16)
            kf = (k * jnp.exp(-bcum)).astype(BF16)
        for h in range(GLA_HEADS):
            ks = slice(h * GLA_DK, (h + 1) * GLA_DK)
            vs = slice(h * GLA_DV, (h + 1) * GLA_DV)
            st = st_ref[h]
            if latent:
                att = jnp.where(incl, _dot_nt(qf[:, ks], kf[:, ks]), 0.0)
                o = _dot(att.astype(BF16), v[:, vs]) + _dot_nt(qf[:, ks], st.astype(BF16))
                orow = pl.ds(pl.multiple_of(blk * GLA_ROWS + r0, GLA_CHUNK), GLA_CHUNK)
                ot = o + jnp.where(d == 1, ofwd_ref[orow, vs], 0.0)
                ofwd_ref[orow, vs] = ot
                on = ot * lax.rsqrt(jnp.mean(ot * ot, axis=-1, keepdims=True) + EPS) * gain_ref[...]
                o_ref[0, rows, vs] = (on * _silu(g_ref[0, rows, vs].astype(F32))).astype(BF16)

            st_ref[h] = st * decay[:, ks] + _dot_tn(v[:, vs], kl[:, ks])

    if not latent:
        @pl.when(j == nblk - 1)
        def _():
            o_ref[0, 0] = st_ref[...]


def _gla(p3, lr3, wa_pad, ba, o_gain, s0, latent):
    n = p3.shape[1]
    nblk = n // GLA_ROWS
    seq_blk = lambda d, j: jnp.where(d == 0, j, nblk - 1 - j)
    in_specs = [
        pl.BlockSpec((1, GLA_ROWS, GLA_K_DIM), lambda b, d, j: (b, seq_blk(d, j), COL_GQ)),
        pl.BlockSpec((1, GLA_ROWS, GLA_K_DIM), lambda b, d, j: (b, seq_blk(d, j), COL_GK)),
        pl.BlockSpec((1, GLA_ROWS, GLA_V_DIM), lambda b, d, j: (b, seq_blk(d, j), COL_GV)),
    ]
    args = [p3, p3, p3]
    if latent:
        in_specs.append(pl.BlockSpec((1, GLA_ROWS, GLA_V_DIM), lambda b, d, j: (b, seq_blk(d, j), COL_GG)))
        args.append(p3)
    in_specs += [
        pl.BlockSpec((1, GLA_ROWS, LR_PAD), lambda b, d, j: (b, seq_blk(d, j), 0)),
        pl.BlockSpec((1, LR_PAD, GLA_K_DIM), lambda b, d, j: (d, 0, 0)),
        pl.BlockSpec((1, 1, GLA_K_DIM), lambda b, d, j: (d, 0, 0)),
    ]
    args += [lr3, wa_pad, ba]
    if latent:
        in_specs.append(pl.BlockSpec((1, GLA_DV), lambda b, d, j: (0, 0)))
        args.append(o_gain)
    state_spec = pl.BlockSpec((1, 1, GLA_HEADS, GLA_DV, GLA_DK), lambda b, d, j: (b, d, 0, 0, 0))
    in_specs.append(state_spec)
    args.append(s0)
    scratch = [pltpu.VMEM((GLA_HEADS, GLA_DV, GLA_DK), F32)]
    if latent:
        out_shape = jax.ShapeDtypeStruct((N_BATCH, n, GLA_V_DIM), BF16)
        out_spec = pl.BlockSpec((1, GLA_ROWS, GLA_V_DIM),
                                lambda b, d, j: (b, jnp.where(d == 0, nblk - 1, nblk - 1 - j), 0))
        scratch.append(pltpu.VMEM((n, GLA_V_DIM), F32))
    else:
        out_shape = jax.ShapeDtypeStruct(s0.shape, F32)
        out_spec = state_spec
    return pl.pallas_call(
        functools.partial(_gla_kernel, latent, nblk),
        out_shape=out_shape,
        grid=(N_BATCH, 2, nblk),
        in_specs=in_specs,
        out_specs=out_spec,
        scratch_shapes=scratch,
        compiler_params=_cparams(("arbitrary", "arbitrary", "arbitrary")),
        name="gla_latent" if latent else "gla_context_state",
    )(*args)


def _router_logits(h, rw_ref):
    hh = h.astype(BF16)
    hl = (h - hh.astype(F32)).astype(BF16)
    return _dot(hh, rw_ref[0]) + _dot(hl, rw_ref[0]) + _dot(hh, rw_ref[1])


def _residual_and_moe_input(mix, x_ref, gate_ref, sh_ref, sc_ref, rwt_ref, x_out_ref, h_out_ref, lg_ref):
    x1 = x_ref[...] + gate_ref[0] * mix
    x_out_ref[...] = x1
    h = _rms_modulate(x1, sh_ref[0], sc_ref[0])
    h_out_ref[...] = h
    lg_ref[...] = _router_logits(h, rwt_ref)


def _attn_out_kernel(a1_ref, a2_ref, w_ref, x_ref, gate_ref, sh_ref, sc_ref, rwt_ref, x_out_ref, h_out_ref, lg_ref):
    mix = _dot(a1_ref[...], w_ref[0:A_Q_DIM, :]) + _dot(a2_ref[...], w_ref[A_Q_DIM:, :])
    _residual_and_moe_input(mix, x_ref, gate_ref, sh_ref, sc_ref, rwt_ref, x_out_ref, h_out_ref, lg_ref)


def _conv_out_kernel(gb_ref, gc_ref, u_ref, gcp_ref, up_ref, gcn_ref, un_ref, cw_ref, w_ref,
                     x_ref, gate_ref, sh_ref, sc_ref, rwt_ref, x_out_ref, h_out_ref, lg_ref):
    i = pl.program_id(0)
    tiles_per_seq = SEQ_LEN // OUT_TM
    t = gc_ref[...].astype(F32) * u_ref[...].astype(F32)
    first = (i % tiles_per_seq) == 0
    last = (i % tiles_per_seq) == tiles_per_seq - 1
    halo_last = slice(CONV_HALO - 1, CONV_HALO)
    t_before = jnp.where(first, 0.0, gcp_ref[halo_last, :].astype(F32) * up_ref[halo_last, :].astype(F32))
    t_after = jnp.where(last, 0.0, gcn_ref[0:1, :].astype(F32) * un_ref[0:1, :].astype(F32))
    row = lax.broadcasted_iota(I32, t.shape, 0)
    t_up = jnp.where(row == 0, t_before, pltpu.roll(t, 1, 0))
    t_dn = jnp.where(row == OUT_TM - 1, t_after, pltpu.roll(t, OUT_TM - 1, 0))
    y = cw_ref[0:1, :] * t_up + cw_ref[1:2, :] * t + cw_ref[2:3, :] * t_dn
    mix = _dot((gb_ref[...].astype(F32) * y).astype(BF16), w_ref[...])
    _residual_and_moe_input(mix, x_ref, gate_ref, sh_ref, sc_ref, rwt_ref, x_out_ref, h_out_ref, lg_ref)


def _mixer_out(kernel_fn, mixer_specs, mixer_args, x2d, gate, shift, scale, rwt, name):
    d = D_MODEL
    mod_spec = pl.BlockSpec((1, 1, d), lambda i: (i * OUT_TM // SEQ_LEN, 0, 0))
    row_spec = pl.BlockSpec((OUT_TM, d), lambda i: (i, 0))
    return pl.pallas_call(
        kernel_fn,
        out_shape=[jax.ShapeDtypeStruct((N_TOK, d), F32), jax.ShapeDtypeStruct((N_TOK, d), F32),
                   jax.ShapeDtypeStruct((N_TOK, ROUTE_LANES), F32)],
        grid=(N_TOK // OUT_TM,),
        in_specs=mixer_specs + [row_spec, mod_spec, mod_spec, mod_spec,
                                pl.BlockSpec((2, d, ROUTE_LANES), lambda i: (0, 0, 0))],
        out_specs=[row_spec, row_spec, pl.BlockSpec((OUT_TM, ROUTE_LANES), lambda i: (i, 0))],
        compiler_params=_cparams(("arbitrary",)),
        name=name,
    )(*mixer_args, x2d, gate, shift, scale, rwt)


def _attn_out(attn2d, gla2d, w_out, x2d, gate, shift, scale, rwt):
    half = pl.BlockSpec((OUT_TM, A_Q_DIM), lambda i: (i, 0))
    specs = [half, half, pl.BlockSpec((D_MODEL, D_MODEL), lambda i: (0, 0))]
    return _mixer_out(_attn_out_kernel, specs, [attn2d, gla2d, w_out], x2d, gate, shift, scale, rwt,
                      "attn_out_projection")


def _conv_out(g2d, conv_w, w_out, x2d, gate, shift, scale, rwt):
    d = D_MODEL
    sub = OUT_TM // CONV_HALO
    last_halo = N_TOK // CONV_HALO - 1
    main = lambda col: pl.BlockSpec((OUT_TM, d), lambda i: (i, col))
    before = lambda col: pl.BlockSpec((CONV_HALO, d), lambda i: (jnp.maximum(i * sub - 1, 0), col))
    after = lambda col: pl.BlockSpec((CONV_HALO, d), lambda i: (jnp.minimum((i + 1) * sub, last_halo), col))
    specs = [main(0), main(1), main(2), before(1), before(2), after(1), after(2),
             pl.BlockSpec((3, d), lambda i: (0, 0)), pl.BlockSpec((d, d), lambda i: (0, 0))]
    return _mixer_out(_conv_out_kernel, specs, [g2d] * 7 + [conv_w, w_out], x2d, gate, shift, scale, rwt,
                      "conv_out_projection")


def _first_argmax8(x, idx8):
    m = jnp.max(x, axis=0, keepdims=True)
    a = jnp.min(jnp.where(x == m, idx8, float(GROUP_SIZE)), axis=0, keepdims=True)
    return m, a


def _route_kernel(lg_ref, rb_ref, e_ref, w_ref, dest_ref, meta_ref, cnt_ref, carry_ref, pstart_ref):
    phase = pl.program_id(0)
    j = pl.program_id(1)
    tt = ROUTE_TT
    sc = jax.nn.sigmoid(lg_ref[...].T[:N_EXPERTS])
    grp = sc + rb_ref[...]
    idx8 = lax.broadcasted_iota(I32, (GROUP_SIZE, tt), 0).astype(F32)
    groups = [grp[g * GROUP_SIZE:(g + 1) * GROUP_SIZE] for g in range(N_GROUPS)]
    gscore = []
    for x in groups:
        m1, a1 = _first_argmax8(x, idx8)
        m2, _ = _first_argmax8(jnp.where(idx8 == a1, -jnp.inf, x), idx8)
        gscore.append(m1 + m2)
    gmax = functools.reduce(jnp.maximum, gscore)
    gsel = jnp.full((1, tt), float(N_GROUPS), F32)
    for g in reversed(range(N_GROUPS)):
        gsel = jnp.where(gscore[g] == gmax, float(g), gsel)
    in_grp = groups[0]
    for g in range(1, N_GROUPS):
        in_grp = jnp.where(gsel == float(g), groups[g], in_grp)
    _, a1 = _first_argmax8(in_grp, idx8)
    _, a2 = _first_argmax8(jnp.where(idx8 == a1, -jnp.inf, in_grp), idx8)
    e0 = gsel * GROUP_SIZE + a1
    e1 = gsel * GROUP_SIZE + a2
    idx32 = lax.broadcasted_iota(I32, (N_EXPERTS, tt), 0).astype(F32)
    oh0 = idx32 == e0
    oh1 = idx32 == e1
    w0 = jnp.sum(jnp.where(oh0, sc, 0.0), axis=0, keepdims=True)
    w1 = jnp.sum(jnp.where(oh1, sc, 0.0), axis=0, keepdims=True)
    wsum = w0 + w1
    ohs = jnp.where(oh0 | oh1, 1.0, 0.0)
    tile_cnt = jnp.sum(ohs, axis=1, keepdims=True)

    @pl.when(phase == 0)
    def _():
        @pl.when(j == 0)
        def _():
            cnt_ref[...] = jnp.zeros_like(cnt_ref)
        cnt_ref[...] += jnp.broadcast_to(tile_cnt, cnt_ref.shape)

    @pl.when(phase == 1)
    def _():
        @pl.when(j == 0)
        def _():
            cnt = cnt_ref[...]
            nblk_e = jnp.zeros_like(cnt)
            for kb in range(-(-N_TOK // MOE_BLOCK)):
                nblk_e += jnp.where(cnt > float(kb * MOE_BLOCK), 1.0, 0.0)
            r = lax.broadcasted_iota(I32, (N_EXPERTS, N_EXPERTS), 0)
            c = lax.broadcasted_iota(I32, (N_EXPERTS, N_EXPERTS), 1)
            strict_lower = jnp.where(c < r, 1.0, 0.0).astype(BF16)
            pcf = nblk_e * MOE_BLOCK
            pstart = _dot(strict_lower, nblk_e.astype(BF16)) * MOE_BLOCK
            pstart_ref[...] = pstart
            carry_ref[...] = jnp.zeros_like(carry_ref)
            pends = pstart + pcf
            lane = lax.broadcasted_iota(I32, pends.shape, 1).astype(F32)
            blk_exp = jnp.sum(jnp.where(lane * MOE_BLOCK >= pends, 1.0, 0.0), axis=0, keepdims=True)
            blk_exp = jnp.minimum(blk_exp, float(N_EXPERTS - 1))
            n_used = jnp.sum(nblk_e, axis=0, keepdims=True)
            own = lax.broadcasted_iota(I32, pends.shape, 0).astype(F32) == blk_exp
            cnt_blk = jnp.sum(jnp.where(own, cnt, 0.0), axis=0, keepdims=True)
            start_blk = jnp.sum(jnp.where(own, pstart, 0.0), axis=0, keepdims=True)
            valid = jnp.clip(cnt_blk - (lane[0:1] * MOE_BLOCK - start_blk), 0.0, float(MOE_BLOCK))
            meta = jnp.where(lane[0:1] == float(META_LANES - 1), n_used, blk_exp)
            tail = jnp.where(cnt > 0, pends - MOE_BLOCK, -1.0)
            expert_lane = lax.broadcasted_iota(I32, pends.shape, 0).astype(F32) == lane
            tail = jnp.sum(jnp.where(expert_lane, tail, 0.0), axis=0, keepdims=True)
            meta_ref[...] = jnp.concatenate([meta, valid, tail], axis=0).astype(I32)

        rr = lax.broadcasted_iota(I32, (tt, tt), 0)
        cc = lax.broadcasted_iota(I32, (tt, tt), 1)
        strict_upper = jnp.where(rr < cc, 1.0, 0.0).astype(BF16)
        before = _dot(ohs.astype(BF16), strict_upper)
        pos = before + carry_ref[:, 0:1] + pstart_ref[:, 0:1]
        d0 = jnp.sum(jnp.where(oh0, pos, 0.0), axis=0, keepdims=True)
        d1 = jnp.sum(jnp.where(oh1, pos, 0.0), axis=0, keepdims=True)
        carry_ref[...] += jnp.broadcast_to(tile_cnt, carry_ref.shape)
        e_ref[...] = jnp.concatenate([e0, e1], axis=0).astype(I32)
        w_ref[...] = jnp.concatenate([w0 / wsum, w1 / wsum], axis=0)
        dest_ref[...] = jnp.concatenate([d0, d1], axis=0).astype(I32)


def _route(logits_t, router_b):
    nt = N_TOK // ROUTE_TT
    tok = lambda dt: jax.ShapeDtypeStruct((2, N_TOK), dt)
    tok_spec = pl.BlockSpec((2, ROUTE_TT), lambda p, j: (0, j * p))
    return pl.pallas_call(
        _route_kernel,
        out_shape=[tok(I32), tok(F32), tok(I32), jax.ShapeDtypeStruct((META_ROWS, META_LANES), I32)],
        grid=(2, nt),
        in_specs=[pl.BlockSpec((ROUTE_TT, ROUTE_LANES), lambda p, j: (j, 0)),
                  pl.BlockSpec((N_EXPERTS, 1), lambda p, j: (0, 0))],
        out_specs=[tok_spec, tok_spec, tok_spec, pl.BlockSpec((META_ROWS, META_LANES), lambda p, j: (0, 0))],
        scratch_shapes=[pltpu.VMEM((N_EXPERTS, 128), F32), pltpu.VMEM((N_EXPERTS, 128), F32),
                        pltpu.VMEM((N_EXPERTS, 128), F32)],
        compiler_params=_cparams(("arbitrary", "arbitrary")),
        name="router_slot_assignment",
    )(logits_t, router_b.reshape(N_EXPERTS, 1))


def _dispatch_copy(dest_ref, h_ref, xs_ref, sem, i, r, k):
    dst = dest_ref[k * N_TOK + i * TOK_TM + r]
    return pltpu.make_async_copy(h_ref.at[pl.ds(r, 1)], xs_ref.at[pl.ds(dst, 1)], sem)


def _zero_block_copy(zero_ref, xs_ref, sem, row):
    return pltpu.make_async_copy(zero_ref, xs_ref.at[pl.ds(pl.multiple_of(row, MOE_BLOCK), MOE_BLOCK)], sem)


def _zero_fill_padding(meta_ref, zero_ref, xs_ref, sem):
    zero_ref[...] = jnp.zeros_like(zero_ref)
    n_used = meta_ref[META_LANES - 1]

    def for_each_block(fn):
        def expert_tail(e, carry):
            row = meta_ref[2 * META_LANES + e]

            @pl.when(row >= 0)
            def _():
                fn(_zero_block_copy(zero_ref, xs_ref, sem, row))
            return carry

        def unused(b, carry):
            fn(_zero_block_copy(zero_ref, xs_ref, sem, b * MOE_BLOCK))
            return carry

        lax.fori_loop(0, N_EXPERTS, expert_tail, 0)
        lax.fori_loop(n_used, MOE_NBLK, unused, 0)

    for_each_block(lambda copy: copy.start())
    for_each_block(lambda copy: copy.wait())


def _dispatch_kernel(dest_ref, meta_ref, h_ref, xs_ref, zero_ref, sem):
    i = pl.program_id(0)

    @pl.when(i == 0)
    def _():
        _zero_fill_padding(meta_ref, zero_ref, xs_ref, sem)

    unroll = 8

    def start(g, carry):
        for u in range(unroll):
            for k in range(2):
                _dispatch_copy(dest_ref, h_ref, xs_ref, sem, i, g * unroll + u, k).start()
        return carry

    def wait(g, carry):
        for u in range(unroll):
            for k in range(2):
                _dispatch_copy(dest_ref, h_ref, xs_ref, sem, i, g * unroll + u, k).wait()
        return carry

    lax.fori_loop(0, TOK_TM // unroll, start, 0)
    lax.fori_loop(0, TOK_TM // unroll, wait, 0)


def _dispatch(dest_flat, meta_flat, h2d):
    d = D_MODEL
    return pl.pallas_call(
        _dispatch_kernel,
        out_shape=jax.ShapeDtypeStruct((MOE_SLOTS, d), F32),
        grid_spec=pltpu.PrefetchScalarGridSpec(
            num_scalar_prefetch=2,
            grid=(N_TOK // TOK_TM,),
            in_specs=[pl.BlockSpec((TOK_TM, d), lambda i, dest, meta: (i, 0))],
            out_specs=pl.BlockSpec(memory_space=pl.ANY),
            scratch_shapes=[pltpu.VMEM((MOE_BLOCK, d), F32), pltpu.SemaphoreType.DMA],
        ),
        compiler_params=_cparams(("arbitrary",)),
        name="moe_dispatch",
    )(dest_flat, meta_flat, h2d)


def _expert_kernel(meta_ref, x_ref, w1_ref, w3_ref, w2_ref, y_ref, xb_ref):
    blk = pl.program_id(0)
    f = pl.program_id(1)
    valid = jnp.where(blk < meta_ref[META_LANES - 1], meta_ref[META_LANES + blk], 0)

    @pl.when(f == 0)
    def _():
        xb_ref[...] = x_ref[...].astype(BF16)
        for u in range(MOE_BLOCK // MOE_SUB):
            @pl.when(valid <= u * MOE_SUB)
            def _():
                y_ref[u * MOE_SUB:(u + 1) * MOE_SUB, :] = jnp.zeros((MOE_SUB, y_ref.shape[1]), F32)

    def compute(r0, nrows):
        x = xb_ref[r0:r0 + nrows, :]
        a = _silu(_dot(x, w1_ref[0, 0].astype(BF16))) * _dot(x, w3_ref[0, 0].astype(BF16))
        y = _dot(a.astype(BF16), w2_ref[0, 0].astype(BF16))

        @pl.when(f == 0)
        def _():
            y_ref[r0:r0 + nrows, :] = y

        @pl.when(f > 0)
        def _():
            y_ref[r0:r0 + nrows, :] += y

    @pl.when(valid > MOE_SUB)
    def _():
        compute(0, 2 * MOE_SUB)

    @pl.when((valid > 0) & (valid <= MOE_SUB))
    def _():
        compute(0, MOE_SUB)

    @pl.when(valid > 2 * MOE_SUB)
    def _():
        compute(2 * MOE_SUB, MOE_SUB)


def _experts(meta, xs, w1, w3, w2, layer):
    d = D_MODEL
    nf = D_EXPERT // MOE_FC
    n_used = lambda meta: meta[META_LANES - 1]
    used = lambda i, meta: jnp.minimum(i, n_used(meta) - 1)
    chunk = lambda i, f, meta: jnp.where(i < n_used(meta), f, nf - 1)
    w_in_spec = pl.BlockSpec((1, 1, d, MOE_FC), lambda i, f, meta: (layer, meta[used(i, meta)], 0, chunk(i, f, meta)))
    w_out_spec = pl.BlockSpec((1, 1, MOE_FC, d), lambda i, f, meta: (layer, meta[used(i, meta)], chunk(i, f, meta), 0))
    return pl.pallas_call(
        _expert_kernel,
        out_shape=jax.ShapeDtypeStruct((MOE_SLOTS, d), F32),
        grid_spec=pltpu.PrefetchScalarGridSpec(
            num_scalar_prefetch=1,
            grid=(MOE_NBLK, nf),
            in_specs=[pl.BlockSpec((MOE_BLOCK, d), lambda i, f, meta: (used(i, meta), 0)),
                      w_in_spec, w_in_spec, w_out_spec],
            out_specs=pl.BlockSpec((MOE_BLOCK, d), lambda i, f, meta: (i, 0)),
            scratch_shapes=[pltpu.VMEM((MOE_BLOCK, d), BF16)],
        ),
        compiler_params=_cparams(("arbitrary", "arbitrary")),
        name="moe_experts",
    )(meta, xs, w1, w3, w2)


def _combine_copy(dest_ref, yb_ref, buf_ref, sem, i, r, k):
    src = dest_ref[k * N_TOK + i * TOK_TM + r]
    return pltpu.make_async_copy(yb_ref.at[pl.ds(src, 1)], buf_ref.at[k, pl.ds(r, 1)], sem)


def _combine_kernel(dest_ref, yb_ref, x_ref, gate_ref, w_ref, o_ref, buf_ref, sem):
    i = pl.program_id(0)

    unroll = 8

    def start(g, carry):
        for u in range(unroll):
            for k in range(2):
                _combine_copy(dest_ref, yb_ref, buf_ref, sem, i, g * unroll + u, k).start()
        return carry

    def wait(g, carry):
        for u in range(unroll):
            for k in range(2):
                _combine_copy(dest_ref, yb_ref, buf_ref, sem, i, g * unroll + u, k).wait()
        return carry

    lax.fori_loop(0, TOK_TM // unroll, start, 0)
    lax.fori_loop(0, TOK_TM // unroll, wait, 0)
    w = w_ref[...]
    y = w[:, 0:1] * buf_ref[0] + w[:, 1:2] * buf_ref[1]
    o_ref[...] = x_ref[...] + gate_ref[0] * y


def _combine(dest_flat, yb, x2d, gate, w_tok):
    d = D_MODEL
    return pl.pallas_call(
        _combine_kernel,
        out_shape=jax.ShapeDtypeStruct((N_TOK, d), F32),
        grid_spec=pltpu.PrefetchScalarGridSpec(
            num_scalar_prefetch=1,
            grid=(N_TOK // TOK_TM,),
            in_specs=[pl.BlockSpec(memory_space=pl.ANY),
                      pl.BlockSpec((TOK_TM, d), lambda i, dest: (i, 0)),
                      pl.BlockSpec((1, 1, d), lambda i, dest: (i * TOK_TM // SEQ_LEN, 0, 0)),
                      pl.BlockSpec((TOK_TM, 2), lambda i, dest: (i, 0))],
            out_specs=pl.BlockSpec((TOK_TM, d), lambda i, dest: (i, 0)),
            scratch_shapes=[pltpu.VMEM((2, TOK_TM, d), F32), pltpu.SemaphoreType.DMA],
        ),
        compiler_params=_cparams(("arbitrary",)),
        name="moe_combine",
    )(dest_flat, yb, x2d, gate, w_tok)


def _moe(h2d, logits_t, x2d, gate, router_b, w1, w3, w2, layer):
    _, w_sel, dest, meta = _route(logits_t, router_b)
    dest_flat = dest.reshape(2 * N_TOK)
    meta_flat = meta.reshape(META_ROWS * META_LANES)
    xs = _dispatch(dest_flat, meta_flat, h2d)
    yb = _experts(meta_flat, xs, w1, w3, w2, layer)
    return _combine(dest_flat, yb, x2d, gate, w_sel.T)


def _rope_tables():
    f = HEAD_DIM // 4
    inv_freq = ROPE_THETA ** (-jnp.arange(f, dtype=F32) / f)
    pos = jnp.arange(SEQ_LEN)
    row = (pos // GRID_W).astype(F32)[:, None] * inv_freq
    col = (pos % GRID_W).astype(F32)[:, None] * inv_freq
    cos = jnp.concatenate([jnp.cos(row), jnp.cos(row), jnp.cos(col), jnp.cos(col)], axis=-1)
    sin = jnp.concatenate([-jnp.sin(row), jnp.sin(row), -jnp.sin(col), jnp.sin(col)], axis=-1)
    return cos, sin


def _permute_attn_w_in(w):
    o = [0, A_Q_DIM, A_KV_DIM, A_KV_DIM, GLA_K_DIM, GLA_K_DIM, GLA_V_DIM, GLA_V_DIM]
    s = [sum(o[:i + 1]) for i in range(len(o))]
    seg = lambda i: w[:, s[i]:s[i + 1]] if i + 1 < len(s) else w[:, s[i]:]
    aq, ak, av, gq, gk, gv, gg = (seg(i) for i in range(7))
    main = jnp.concatenate([aq, gv, gg, gq, gk, ak, av], axis=1).astype(BF16)
    lr = jnp.pad(w[:, PROJ_DIM:], ((0, 0), (0, LR_PAD - 2 * GLA_RANK))).astype(BF16)
    return main, lr


def kernel(x, c, ctx, c_ctx, mod_w, mod_b, attn_w_in, attn_q_norm, attn_k_norm, attn_sink, gla_wa2, gla_ba,
           gla_norm, attn_w_out, conv_w_in, conv_w, conv_w_out, router_w, router_b, exp_w1, exp_w3, exp_w2):
    d = D_MODEL
    cc = jnp.concatenate([c, c_ctx[None], jnp.zeros((8 - N_BATCH - 1, d), F32)], axis=0)
    m = _modulation(cc, mod_w, mod_b)
    mods = [[m[l, :, i * d:(i + 1) * d].reshape(8, 1, d) for i in range(6)] for l in range(mod_w.shape[0])]
    lat_row = lambda i: i * PROJ_TM // SEQ_LEN
    ctx_row = lambda i: N_BATCH
    rw_pad = jnp.pad(router_w, ((0, 0), (0, ROUTE_LANES - N_EXPERTS)))
    rw_hi = rw_pad.astype(BF16)
    rwt = jnp.stack([rw_hi, (rw_pad - rw_hi.astype(F32)).astype(BF16)])
    x2d = x.reshape(N_TOK, d)

    w_main, w_lr = _permute_attn_w_in(attn_w_in[0])
    p_lat, lr_lat = _projection(x2d, mods[0][0], mods[0][1], w_main, w_lr, lat_row)
    p_ctx, lr_ctx = _projection(ctx.reshape(N_BATCH * CTX_LEN, d), mods[0][0], mods[0][1], w_main, w_lr, ctx_row)
    p_lat = p_lat.reshape(N_BATCH, SEQ_LEN, PROJ_DIM)
    p_ctx = p_ctx.reshape(N_BATCH, CTX_LEN, PROJ_DIM)
    cos, sin_signed = _rope_tables()
    attn = _attention(p_lat, p_ctx, cos, sin_signed, attn_q_norm[0][None], attn_k_norm[0][None], attn_sink[0])
    wa_pad = jnp.zeros((2, LR_PAD, GLA_K_DIM), F32)
    for di in range(2):
        wa_pad = wa_pad.at[di, di * GLA_RANK:(di + 1) * GLA_RANK].set(gla_wa2[0, di])
    wa_pad = wa_pad.astype(BF16)
    ba = gla_ba[0].reshape(2, 1, GLA_K_DIM)
    s_zero = jnp.zeros((N_BATCH, 2, GLA_HEADS, GLA_DV, GLA_DK), F32)
    s_ctx = _gla(p_ctx, lr_ctx.reshape(N_BATCH, CTX_LEN, LR_PAD), wa_pad, ba, None, s_zero, latent=False)
    gla = _gla(p_lat, lr_lat.reshape(N_BATCH, SEQ_LEN, LR_PAD), wa_pad, ba, gla_norm[0][None], s_ctx, latent=True)
    x1, h2, lg = _attn_out(attn.reshape(N_TOK, A_Q_DIM), gla.reshape(N_TOK, GLA_V_DIM), attn_w_out[0].astype(BF16),
                           x2d, mods[0][2], mods[0][3], mods[0][4], rwt)
    x2 = _moe(h2, lg, x1, mods[0][5], router_b, exp_w1, exp_w3, exp_w2, 0)

    g = _projection(x2, mods[1][0], mods[1][1], conv_w_in[0].astype(BF16), None, lat_row)
    x3, h2, lg = _conv_out(g, conv_w[0], conv_w_out[0].astype(BF16), x2, mods[1][2], mods[1][3], mods[1][4], rwt)
    x4 = _moe(h2, lg, x3, mods[1][5], router_b, exp_w1, exp_w3, exp_w2, 1)
    return x4.reshape(N_BATCH, SEQ_LEN, d)
```

```python
import functools

import jax
import jax.numpy as jnp
from jax import lax
from jax.experimental import pallas as pl
from jax.experimental.pallas import tpu as pltpu

F32 = jnp.float32
BF16 = jnp.bfloat16
I32 = jnp.int32

D_MODEL = 2048
N_BATCH = 4
SEQ_LEN = 2048
CTX_LEN = 256
N_TOK = N_BATCH * SEQ_LEN
GRID_W = 64
HEAD_DIM = 128
A_Q_HEADS = 8
A_KV_HEADS = 2
A_GROUP = A_Q_HEADS // A_KV_HEADS
WINDOW = 128
A_BLOCK = 128
ROPE_THETA = 10000.0
GLA_HEADS = 4
GLA_DK = 128
GLA_DV = 256
GLA_RANK = 16
GLA_TAU = 16.0
GLA_CHUNK = 64
N_EXPERTS = 32
N_GROUPS = 4
GROUP_SIZE = N_EXPERTS // N_GROUPS
D_EXPERT = 1024
EPS = 1e-6
NEG_INF = -1e30
A_Q_DIM = A_Q_HEADS * HEAD_DIM
A_KV_DIM = A_KV_HEADS * HEAD_DIM
GLA_K_DIM = GLA_HEADS * GLA_DK
GLA_V_DIM = GLA_HEADS * GLA_DV
PROJ_DIM = A_Q_DIM + 2 * A_KV_DIM + 2 * GLA_K_DIM + 2 * GLA_V_DIM
LR_PAD = 128

COL_AQ = 0
COL_GV = 1
COL_GG = 2
COL_GQ = 6
COL_GK = 7
COL_AK = 32
COL_AV = 34

MOD_TN = 1024
PROJ_TM = 512
PROJ_TN = 512
OUT_TM = 256
CONV_HALO = 16
GLA_ROWS = 256
ROUTE_TT = 512
ROUTE_LANES = 128
MOE_SUB = 256
MOE_BLOCK = 3 * MOE_SUB
MOE_ROWS = 128
MOE_FC = 256
MOE_NBLK = -(-(N_TOK * 2 + N_EXPERTS * (MOE_BLOCK - 1)) // MOE_BLOCK)
MOE_SLOTS = MOE_NBLK * MOE_BLOCK
TOK_TM = 256
META_ROWS = 4
META_LANES = 128
VMEM_LIMIT = 56 * 1024 * 1024


def _cparams(sem):
    return pltpu.CompilerParams(dimension_semantics=sem, vmem_limit_bytes=VMEM_LIMIT)


def _silu(x):
    return x * jax.nn.sigmoid(x)


def _dot(a, b):
    return jnp.dot(a, b, preferred_element_type=F32)


def _dot_nt(a, b):
    return lax.dot_general(a, b, (((1,), (1,)), ((), ())), preferred_element_type=F32)


def _dot_tn(a, b):
    return lax.dot_general(a, b, (((0,), (0,)), ((), ())), preferred_element_type=F32)


def _rms_modulate(x, shift, scale):
    r = lax.rsqrt(jnp.mean(x * x, axis=-1, keepdims=True) + EPS)
    return (x * r) * (1.0 + scale) + shift


def _mod_kernel(cc_ref, w_ref, b_ref, o_ref):
    a = _silu(cc_ref[...])
    o_ref[0] = _dot(a.astype(BF16), w_ref[0].astype(BF16)) + b_ref[0]


def _modulation(cc, mod_w, mod_b):
    depth, d, n = mod_w.shape
    return pl.pallas_call(
        _mod_kernel,
        out_shape=jax.ShapeDtypeStruct((depth, 8, n), F32),
        grid=(depth, n // MOD_TN),
        in_specs=[
            pl.BlockSpec((8, d), lambda l, j: (0, 0)),
            pl.BlockSpec((1, d, MOD_TN), lambda l, j: (l, 0, j)),
            pl.BlockSpec((1, 1, MOD_TN), lambda l, j: (l, 0, j)),
        ],
        out_specs=pl.BlockSpec((1, 8, MOD_TN), lambda l, j: (l, 0, j)),
        compiler_params=_cparams(("arbitrary", "arbitrary")),
        name="adaln_modulation",
    )(cc, mod_w, mod_b.reshape(depth, 1, n))


def _proj_kernel(with_lr, x_ref, sh_ref, sc_ref, w_ref, *rest):
    if with_lr:
        wlr_ref, o_ref, olr_ref = rest
    else:
        (o_ref,) = rest
    hb = _rms_modulate(x_ref[...], sh_ref[0], sc_ref[0]).astype(BF16)
    for c in range(w_ref.shape[1] // PROJ_TN):
        cols = slice(c * PROJ_TN, (c + 1) * PROJ_TN)
        o_ref[:, cols] = _dot(hb, w_ref[:, cols]).astype(BF16)
    if with_lr:
        olr_ref[...] = _dot(hb, wlr_ref[...]).astype(BF16)


def _projection(x2d, shift, scale, w, w_lr, mod_row):
    rows, d = x2d.shape
    n = w.shape[1]
    with_lr = w_lr is not None
    resident = lambda shape: pl.BlockSpec(shape, lambda i: (0, 0), pipeline_mode=pl.Buffered(1))
    in_specs = [
        pl.BlockSpec((PROJ_TM, d), lambda i: (i, 0)),
        pl.BlockSpec((1, 1, d), lambda i: (mod_row(i), 0, 0)),
        pl.BlockSpec((1, 1, d), lambda i: (mod_row(i), 0, 0)),
        resident((d, n)),
    ]
    out_shape = [jax.ShapeDtypeStruct((rows, n), BF16)]
    out_specs = [pl.BlockSpec((PROJ_TM, n), lambda i: (i, 0))]
    args = [x2d, shift, scale, w]
    if with_lr:
        in_specs.append(resident((d, LR_PAD)))
        out_shape.append(jax.ShapeDtypeStruct((rows, LR_PAD), BF16))
        out_specs.append(pl.BlockSpec((PROJ_TM, LR_PAD), lambda i: (i, 0)))
        args.append(w_lr)
    res = pl.pallas_call(
        functools.partial(_proj_kernel, with_lr),
        out_shape=out_shape,
        grid=(rows // PROJ_TM,),
        in_specs=in_specs,
        out_specs=out_specs,
        compiler_params=_cparams(("arbitrary",)),
        name="norm_mod_projection",
    )(*args)
    return res if with_lr else res[0]


def _swap_halves32(x):
    lane = lax.broadcasted_iota(I32, x.shape, 1)
    return jnp.where((lane & 63) < 32, pltpu.roll(x, 96, 1), pltpu.roll(x, 32, 1))


def _qk_norm(x, gain):
    return x * lax.rsqrt(jnp.mean(x * x, axis=-1, keepdims=True) + EPS) * gain


def _rope(x, cos, sin_signed):
    return x * cos + _swap_halves32(x) * sin_signed


ATT_WIN0 = CTX_LEN
ATT_LAT0 = CTX_LEN + A_BLOCK
ATT_ROWS = CTX_LEN + SEQ_LEN + 2 * A_BLOCK
ATT_WIN = 3 * A_BLOCK


def _attn_kernel(sink_ref, q_ref, k_ref, v_ref, kx_ref, vx_ref, cos_ref, sin_ref, qg_ref, kg_ref, o_ref,
                 kn_ref, vn_ref, band_ref):
    kvh = pl.program_id(1)
    nb = SEQ_LEN // A_BLOCK
    k_gain = kg_ref[...]
    q_gain = qg_ref[...]

    pad = jnp.zeros((A_BLOCK, HEAD_DIM), BF16)
    kn_ref[0:CTX_LEN, :] = _qk_norm(kx_ref[0].astype(F32), k_gain).astype(BF16)
    vn_ref[0:CTX_LEN, :] = vx_ref[0]
    for ref in (kn_ref, vn_ref):
        ref[ATT_WIN0:ATT_LAT0, :] = pad
        ref[ATT_LAT0 + SEQ_LEN:, :] = pad
    vn_ref[ATT_LAT0:ATT_LAT0 + SEQ_LEN, :] = v_ref[0]
    prep_rows = 4 * A_BLOCK
    for c in range(SEQ_LEN // prep_rows):
        r = slice(c * prep_rows, (c + 1) * prep_rows)
        kc = _rope(_qk_norm(k_ref[0, r, :].astype(F32), k_gain), cos_ref[r, :], sin_ref[r, :])
        kn_ref[ATT_LAT0 + c * prep_rows:ATT_LAT0 + (c + 1) * prep_rows, :] = kc.astype(BF16)

    qi = lax.broadcasted_iota(I32, (A_BLOCK, ATT_WIN), 0)
    wj = lax.broadcasted_iota(I32, (A_BLOCK, ATT_WIN), 1)
    band = jnp.where(jnp.abs(wj - A_BLOCK - qi) <= WINDOW, 1.0, 0.0)
    band_ref[0] = jnp.where(wj >= A_BLOCK, band, 0.0)
    band_ref[1] = band
    band_ref[2] = jnp.where(wj < 2 * A_BLOCK, band, 0.0)

    rows = A_GROUP * A_BLOCK
    rcol = lax.broadcasted_iota(I32, (rows, 1), 0)
    sink = jnp.zeros((rows, 1), F32)
    for g in range(A_GROUP):
        sink = jnp.where((rcol >= g * A_BLOCK) & (rcol < (g + 1) * A_BLOCK), sink_ref[kvh * A_GROUP + g], sink)

    def query_block(n):
        qrows = pl.ds(pl.multiple_of(n * A_BLOCK, A_BLOCK), A_BLOCK)
        wrows = pl.ds(pl.multiple_of(ATT_WIN0 + n * A_BLOCK, A_BLOCK), ATT_WIN)
        cos_q, sin_q = cos_ref[qrows, :], sin_ref[qrows, :]
        q = q_ref[0, qrows, :].astype(F32)
        qs = []
        for g in range(A_GROUP):
            qh = _rope(_qk_norm(q[:, g * HEAD_DIM:(g + 1) * HEAD_DIM], q_gain), cos_q, sin_q)
            qs.append((qh * HEAD_DIM ** -0.5).astype(BF16))
        qs = jnp.concatenate(qs, axis=0)
        mask = band_ref[jnp.where(n == 0, 0, jnp.where(n == nb - 1, 2, 1))]
        mask = jnp.concatenate([mask] * A_GROUP, axis=0)
        s_ctx = _dot_nt(qs, kn_ref[0:CTX_LEN, :])
        s_win = jnp.where(mask > 0.5, _dot_nt(qs, kn_ref[wrows, :]), NEG_INF)
        m = jnp.maximum(jnp.max(s_ctx, axis=-1, keepdims=True), jnp.max(s_win, axis=-1, keepdims=True))
        m = jnp.maximum(m, sink)
        p_ctx = jnp.exp(s_ctx - m)
        p_win = jnp.exp(s_win - m)
        denom = (jnp.sum(p_ctx, axis=-1, keepdims=True) + jnp.sum(p_win, axis=-1, keepdims=True)
                 + jnp.exp(sink - m))
        o = (_dot(p_ctx.astype(BF16), vn_ref[0:CTX_LEN, :]) + _dot(p_win.astype(BF16), vn_ref[wrows, :])) / denom
        for g in range(A_GROUP):
            o_ref[0, qrows, g * HEAD_DIM:(g + 1) * HEAD_DIM] = o[g * A_BLOCK:(g + 1) * A_BLOCK].astype(BF16)

    def query_block_pair(i, carry):
        query_block(2 * i)
        query_block(2 * i + 1)
        return carry

    lax.fori_loop(0, nb // 2, query_block_pair, 0)


def _attention(p_lat, p_ctx, cos, sin_signed, q_gain, k_gain, sink):
    gw = A_GROUP * HEAD_DIM
    lat = lambda width, col: pl.BlockSpec((1, SEQ_LEN, width), lambda b, k, s: (b, 0, col + k))
    ctx_blk = lambda col: pl.BlockSpec((1, CTX_LEN, HEAD_DIM), lambda b, k, s: (b, 0, col + k))
    full = lambda shape: pl.BlockSpec(shape, lambda b, k, s: (0,) * len(shape))
    return pl.pallas_call(
        _attn_kernel,
        out_shape=jax.ShapeDtypeStruct((N_BATCH, SEQ_LEN, A_Q_DIM), BF16),
        grid_spec=pltpu.PrefetchScalarGridSpec(
            num_scalar_prefetch=1,
            grid=(N_BATCH, A_KV_HEADS),
            in_specs=[
                lat(gw, COL_AQ), lat(HEAD_DIM, COL_AK), lat(HEAD_DIM, COL_AV),
                ctx_blk(COL_AK), ctx_blk(COL_AV),
                full((SEQ_LEN, HEAD_DIM)), full((SEQ_LEN, HEAD_DIM)),
                full((1, HEAD_DIM)), full((1, HEAD_DIM)),
            ],
            out_specs=pl.BlockSpec((1, SEQ_LEN, gw), lambda b, k, s: (b, 0, k)),
            scratch_shapes=[pltpu.VMEM((ATT_ROWS, HEAD_DIM), BF16), pltpu.VMEM((ATT_ROWS, HEAD_DIM), BF16),
                            pltpu.VMEM((3, A_BLOCK, ATT_WIN), F32)],
        ),
        compiler_params=_cparams(("arbitrary", "arbitrary")),
        name="windowed_sink_attention",
    )(sink, p_lat, p_lat, p_lat, p_ctx, p_ctx, cos, sin_signed, q_gain, k_gain)


def _split3_bf16(x):
    hi = x.astype(BF16)
    r1 = x - hi.astype(F32)
    mid = r1.astype(BF16)
    lo = (r1 - mid.astype(F32)).astype(BF16)
    return hi, mid, lo


def _gla_kernel(latent, nblk, q_ref, k_ref, v_ref, *rest):
    if latent:
        g_ref, lr_ref, wa_ref, ba_ref, gain_ref, s0_ref, o_ref, st_ref, ofwd_ref = rest
    else:
        lr_ref, wa_ref, ba_ref, s0_ref, o_ref, st_ref = rest
    d = pl.program_id(1)
    j = pl.program_id(2)
    nc = GLA_ROWS // GLA_CHUNK
    blk = jnp.where(d == 0, j, nblk - 1 - j)

    @pl.when(j == 0)
    def _():
        st_ref[...] = s0_ref[0, 0]

    if latent:
        @pl.when((pl.program_id(0) == 0) & (d == 0) & (j == 0))
        def _():
            ofwd_ref[...] = jnp.zeros_like(ofwd_ref)

    ii = lax.broadcasted_iota(I32, (GLA_CHUNK, GLA_CHUNK), 0)
    jj = lax.broadcasted_iota(I32, (GLA_CHUNK, GLA_CHUNK), 1)
    incl = ((jj - ii) * (1 - 2 * d)) <= 0
    incl_b = jnp.where(incl, 1.0, 0.0).astype(BF16)

    for i in range(nc):
        ci = jnp.where(d == 0, i, nc - 1 - i)
        r0 = pl.multiple_of(ci * GLA_CHUNK, GLA_CHUNK)
        rows = pl.ds(r0, GLA_CHUNK)
        z = _dot(lr_ref[0, rows, :], wa_ref[0]) + ba_ref[0]
        la = (jnp.minimum(z, 0.0) - jnp.log(1.0 + jnp.exp(-jnp.abs(z)))) / GLA_TAU
        hi, mid, lo = _split3_bf16(la)
        bcum = _dot(incl_b, hi) + _dot(incl_b, mid) + _dot(incl_b, lo)
        blast = jnp.sum(la, axis=0, keepdims=True)
        k = k_ref[0, rows, :].astype(F32)
        kl = (k * jnp.exp(blast - bcum)).astype(BF16)
        decay = jnp.exp(blast)
        v = v_ref[0, rows, :]
        if latent:
            qf = (q_ref[0, rows, :].astype(F32) * GLA_DK ** -0.5 * jnp.exp(bcum)).astype(BF16)
            kf = (k * jnp.exp(-bcum)).astype(BF16)
        for h in range(GLA_HEADS):
            ks = slice(h * GLA_DK, (h + 1) * GLA_DK)
            vs = slice(h * GLA_DV, (h + 1) * GLA_DV)
            st = st_ref[h]
            if latent:
                att = jnp.where(incl, _dot_nt(qf[:, ks], kf[:, ks]), 0.0)
                o = _dot(att.astype(BF16), v[:, vs]) + _dot_nt(qf[:, ks], st.astype(BF16))
                orow = pl.ds(pl.multiple_of(blk * GLA_ROWS + r0, GLA_CHUNK), GLA_CHUNK)
                ot = o + jnp.where(d == 1, ofwd_ref[orow, vs], 0.0)
                ofwd_ref[orow, vs] = ot
                on = ot * lax.rsqrt(jnp.mean(ot * ot, axis=-1, keepdims=True) + EPS) * gain_ref[...]
                o_ref[0, rows, vs] = (on * _silu(g_ref[0, rows, vs].astype(F32))).astype(BF16)

            st_ref[h] = st * decay[:, ks] + _dot_tn(v[:, vs], kl[:, ks])

    if not latent:
        @pl.when(j == nblk - 1)
        def _():
            o_ref[0, 0] = st_ref[...]


def _gla(p3, lr3, wa_pad, ba, o_gain, s0, latent):
    n = p3.shape[1]
    nblk = n // GLA_ROWS
    seq_blk = lambda d, j: jnp.where(d == 0, j, nblk - 1 - j)
    in_specs = [
        pl.BlockSpec((1, GLA_ROWS, GLA_K_DIM), lambda b, d, j: (b, seq_blk(d, j), COL_GQ)),
        pl.BlockSpec((1, GLA_ROWS, GLA_K_DIM), lambda b, d, j: (b, seq_blk(d, j), COL_GK)),
        pl.BlockSpec((1, GLA_ROWS, GLA_V_DIM), lambda b, d, j: (b, seq_blk(d, j), COL_GV)),
    ]
    args = [p3, p3, p3]
    if latent:
        in_specs.append(pl.BlockSpec((1, GLA_ROWS, GLA_V_DIM), lambda b, d, j: (b, seq_blk(d, j), COL_GG)))
        args.append(p3)
    in_specs += [
        pl.BlockSpec((1, GLA_ROWS, LR_PAD), lambda b, d, j: (b, seq_blk(d, j), 0)),
        pl.BlockSpec((1, LR_PAD, GLA_K_DIM), lambda b, d, j: (d, 0, 0)),
        pl.BlockSpec((1, 1, GLA_K_DIM), lambda b, d, j: (d, 0, 0)),
    ]
    args += [lr3, wa_pad, ba]
    if latent:
        in_specs.append(pl.BlockSpec((1, GLA_DV), lambda b, d, j: (0, 0)))
        args.append(o_gain)
    state_spec = pl.BlockSpec((1, 1, GLA_HEADS, GLA_DV, GLA_DK), lambda b, d, j: (b, d, 0, 0, 0))
    in_specs.append(state_spec)
    args.append(s0)
    scratch = [pltpu.VMEM((GLA_HEADS, GLA_DV, GLA_DK), F32)]
    if latent:
        out_shape = jax.ShapeDtypeStruct((N_BATCH, n, GLA_V_DIM), BF16)
        out_spec = pl.BlockSpec((1, GLA_ROWS, GLA_V_DIM),
                                lambda b, d, j: (b, jnp.where(d == 0, nblk - 1, nblk - 1 - j), 0))
        scratch.append(pltpu.VMEM((n, GLA_V_DIM), F32))
    else:
        out_shape = jax.ShapeDtypeStruct(s0.shape, F32)
        out_spec = state_spec
    return pl.pallas_call(
        functools.partial(_gla_kernel, latent, nblk),
        out_shape=out_shape,
        grid=(N_BATCH, 2, nblk),
        in_specs=in_specs,
        out_specs=out_spec,
        scratch_shapes=scratch,
        compiler_params=_cparams(("arbitrary", "arbitrary", "arbitrary")),
        name="gla_latent" if latent else "gla_context_state",
    )(*args)


def _router_logits(h, rw_ref):
    hh = h.astype(BF16)
    hl = (h - hh.astype(F32)).astype(BF16)
    return _dot(hh, rw_ref[0]) + _dot(hl, rw_ref[0]) + _dot(hh, rw_ref[1])


def _residual_and_moe_input(mix, x_ref, gate_ref, sh_ref, sc_ref, rwt_ref, x_out_ref, h_out_ref, lg_ref):
    x1 = x_ref[...] + gate_ref[0] * mix
    x_out_ref[...] = x1
    h = _rms_modulate(x1, sh_ref[0], sc_ref[0])
    h_out_ref[...] = h
    lg_ref[...] = _router_logits(h, rwt_ref)


def _attn_out_kernel(a1_ref, a2_ref, w_ref, x_ref, gate_ref, sh_ref, sc_ref, rwt_ref, x_out_ref, h_out_ref, lg_ref):
    mix = _dot(a1_ref[...], w_ref[0:A_Q_DIM, :]) + _dot(a2_ref[...], w_ref[A_Q_DIM:, :])
    _residual_and_moe_input(mix, x_ref, gate_ref, sh_ref, sc_ref, rwt_ref, x_out_ref, h_out_ref, lg_ref)


def _conv_out_kernel(gb_ref, gc_ref, u_ref, gcp_ref, up_ref, gcn_ref, un_ref, cw_ref, w_ref,
                     x_ref, gate_ref, sh_ref, sc_ref, rwt_ref, x_out_ref, h_out_ref, lg_ref):
    i = pl.program_id(0)
    tiles_per_seq = SEQ_LEN // OUT_TM
    t = gc_ref[...].astype(F32) * u_ref[...].astype(F32)
    first = (i % tiles_per_seq) == 0
    last = (i % tiles_per_seq) == tiles_per_seq - 1
    halo_last = slice(CONV_HALO - 1, CONV_HALO)
    t_before = jnp.where(first, 0.0, gcp_ref[halo_last, :].astype(F32) * up_ref[halo_last, :].astype(F32))
    t_after = jnp.where(last, 0.0, gcn_ref[0:1, :].astype(F32) * un_ref[0:1, :].astype(F32))
    row = lax.broadcasted_iota(I32, t.shape, 0)
    t_up = jnp.where(row == 0, t_before, pltpu.roll(t, 1, 0))
    t_dn = jnp.where(row == OUT_TM - 1, t_after, pltpu.roll(t, OUT_TM - 1, 0))
    y = cw_ref[0:1, :] * t_up + cw_ref[1:2, :] * t + cw_ref[2:3, :] * t_dn
    mix = _dot((gb_ref[...].astype(F32) * y).astype(BF16), w_ref[...])
    _residual_and_moe_input(mix, x_ref, gate_ref, sh_ref, sc_ref, rwt_ref, x_out_ref, h_out_ref, lg_ref)


def _mixer_out(kernel_fn, mixer_specs, mixer_args, x2d, gate, shift, scale, rwt, name):
    d = D_MODEL
    mod_spec = pl.BlockSpec((1, 1, d), lambda i: (i * OUT_TM // SEQ_LEN, 0, 0))
    row_spec = pl.BlockSpec((OUT_TM, d), lambda i: (i, 0))
    return pl.pallas_call(
        kernel_fn,
        out_shape=[jax.ShapeDtypeStruct((N_TOK, d), F32), jax.ShapeDtypeStruct((N_TOK, d), F32),
                   jax.ShapeDtypeStruct((N_TOK, ROUTE_LANES), F32)],
        grid=(N_TOK // OUT_TM,),
        in_specs=mixer_specs + [row_spec, mod_spec, mod_spec, mod_spec,
                                pl.BlockSpec((2, d, ROUTE_LANES), lambda i: (0, 0, 0))],
        out_specs=[row_spec, row_spec, pl.BlockSpec((OUT_TM, ROUTE_LANES), lambda i: (i, 0))],
        compiler_params=_cparams(("arbitrary",)),
        name=name,
    )(*mixer_args, x2d, gate, shift, scale, rwt)


def _attn_out(attn2d, gla2d, w_out, x2d, gate, shift, scale, rwt):
    half = pl.BlockSpec((OUT_TM, A_Q_DIM), lambda i: (i, 0))
    specs = [half, half, pl.BlockSpec((D_MODEL, D_MODEL), lambda i: (0, 0))]
    return _mixer_out(_attn_out_kernel, specs, [attn2d, gla2d, w_out], x2d, gate, shift, scale, rwt,
                      "attn_out_projection")


def _conv_out(g2d, conv_w, w_out, x2d, gate, shift, scale, rwt):
    d = D_MODEL
    sub = OUT_TM // CONV_HALO
    last_halo = N_TOK // CONV_HALO - 1
    main = lambda col: pl.BlockSpec((OUT_TM, d), lambda i: (i, col))
    before = lambda col: pl.BlockSpec((CONV_HALO, d), lambda i: (jnp.maximum(i * sub - 1, 0), col))
    after = lambda col: pl.BlockSpec((CONV_HALO, d), lambda i: (jnp.minimum((i + 1) * sub, last_halo), col))
    specs = [main(0), main(1), main(2), before(1), before(2), after(1), after(2),
             pl.BlockSpec((3, d), lambda i: (0, 0)), pl.BlockSpec((d, d), lambda i: (0, 0))]
    return _mixer_out(_conv_out_kernel, specs, [g2d] * 7 + [conv_w, w_out], x2d, gate, shift, scale, rwt,
                      "conv_out_projection")


def _first_argmax8(x, idx8):
    m = jnp.max(x, axis=0, keepdims=True)
    a = jnp.min(jnp.where(x == m, idx8, float(GROUP_SIZE)), axis=0, keepdims=True)
    return m, a


def _route_kernel(lg_ref, rb_ref, e_ref, w_ref, dest_ref, meta_ref, cnt_ref, carry_ref, pstart_ref):
    phase = pl.program_id(0)
    j = pl.program_id(1)
    tt = ROUTE_TT
    sc = jax.nn.sigmoid(lg_ref[...].T[:N_EXPERTS])
    grp = sc + rb_ref[...]
    idx8 = lax.broadcasted_iota(I32, (GROUP_SIZE, tt), 0).astype(F32)
    groups = [grp[g * GROUP_SIZE:(g + 1) * GROUP_SIZE] for g in range(N_GROUPS)]
    gscore = []
    for x in groups:
        m1, a1 = _first_argmax8(x, idx8)
        m2, _ = _first_argmax8(jnp.where(idx8 == a1, -jnp.inf, x), idx8)
        gscore.append(m1 + m2)
    gmax = functools.reduce(jnp.maximum, gscore)
    gsel = jnp.full((1, tt), float(N_GROUPS), F32)
    for g in reversed(range(N_GROUPS)):
        gsel = jnp.where(gscore[g] == gmax, float(g), gsel)
    in_grp = groups[0]
    for g in range(1, N_GROUPS):
        in_grp = jnp.where(gsel == float(g), groups[g], in_grp)
    _, a1 = _first_argmax8(in_grp, idx8)
    _, a2 = _first_argmax8(jnp.where(idx8 == a1, -jnp.inf, in_grp), idx8)
    e0 = gsel * GROUP_SIZE + a1
    e1 = gsel * GROUP_SIZE + a2
    idx32 = lax.broadcasted_iota(I32, (N_EXPERTS, tt), 0).astype(F32)
    oh0 = idx32 == e0
    oh1 = idx32 == e1
    w0 = jnp.sum(jnp.where(oh0, sc, 0.0), axis=0, keepdims=True)
    w1 = jnp.sum(jnp.where(oh1, sc, 0.0), axis=0, keepdims=True)
    wsum = w0 + w1
    ohs = jnp.where(oh0 | oh1, 1.0, 0.0)
    tile_cnt = jnp.sum(ohs, axis=1, keepdims=True)

    @pl.when(phase == 0)
    def _():
        @pl.when(j == 0)
        def _():
            cnt_ref[...] = jnp.zeros_like(cnt_ref)
        cnt_ref[...] += jnp.broadcast_to(tile_cnt, cnt_ref.shape)

    @pl.when(phase == 1)
    def _():
        @pl.when(j == 0)
        def _():
            cnt = cnt_ref[...]
            nblk_e = jnp.zeros_like(cnt)
            for kb in range(-(-N_TOK // MOE_BLOCK)):
                nblk_e += jnp.where(cnt > float(kb * MOE_BLOCK), 1.0, 0.0)
            r = lax.broadcasted_iota(I32, (N_EXPERTS, N_EXPERTS), 0)
            c = lax.broadcasted_iota(I32, (N_EXPERTS, N_EXPERTS), 1)
            strict_lower = jnp.where(c < r, 1.0, 0.0).astype(BF16)
            pcf = nblk_e * MOE_BLOCK
            pstart = _dot(strict_lower, nblk_e.astype(BF16)) * MOE_BLOCK
            pstart_ref[...] = pstart
            carry_ref[...] = jnp.zeros_like(carry_ref)
            pends = pstart + pcf
            lane = lax.broadcasted_iota(I32, pends.shape, 1).astype(F32)
            blk_exp = jnp.sum(jnp.where(lane * MOE_BLOCK >= pends, 1.0, 0.0), axis=0, keepdims=True)
            blk_exp = jnp.minimum(blk_exp, float(N_EXPERTS - 1))
            n_used = jnp.sum(nblk_e, axis=0, keepdims=True)
            own = lax.broadcasted_iota(I32, pends.shape, 0).astype(F32) == blk_exp
            cnt_blk = jnp.sum(jnp.where(own, cnt, 0.0), axis=0, keepdims=True)
            start_blk = jnp.sum(jnp.where(own, pstart, 0.0), axis=0, keepdims=True)
            valid = jnp.clip(cnt_blk - (lane[0:1] * MOE_BLOCK - start_blk), 0.0, float(MOE_BLOCK))
            meta = jnp.where(lane[0:1] == float(META_LANES - 1), n_used, blk_exp)
            tail = jnp.where(cnt > 0, pends - MOE_BLOCK, -1.0)
            valid_tail = cnt - (pcf - MOE_BLOCK)
            full_units = jnp.zeros_like(cnt)
            for u in range(1, MOE_BLOCK // MOE_SUB + 1):
                full_units += jnp.where(valid_tail >= float(u * MOE_SUB), 1.0, 0.0)
            expert_lane = lax.broadcasted_iota(I32, pends.shape, 0).astype(F32) == lane
            to_lane = lambda col: jnp.sum(jnp.where(expert_lane, col, 0.0), axis=0, keepdims=True)
            meta_ref[...] = jnp.concatenate([meta, valid, to_lane(tail), to_lane(full_units)], axis=0).astype(I32)

        rr = lax.broadcasted_iota(I32, (tt, tt), 0)
        cc = lax.broadcasted_iota(I32, (tt, tt), 1)
        strict_upper = jnp.where(rr < cc, 1.0, 0.0).astype(BF16)
        before = _dot(ohs.astype(BF16), strict_upper)
        pos = before + carry_ref[:, 0:1] + pstart_ref[:, 0:1]
        d0 = jnp.sum(jnp.where(oh0, pos, 0.0), axis=0, keepdims=True)
        d1 = jnp.sum(jnp.where(oh1, pos, 0.0), axis=0, keepdims=True)
        carry_ref[...] += jnp.broadcast_to(tile_cnt, carry_ref.shape)
        e_ref[...] = jnp.concatenate([e0, e1], axis=0).astype(I32)
        w_ref[...] = jnp.concatenate([w0 / wsum, w1 / wsum], axis=0)
        dest_ref[...] = jnp.concatenate([d0, d1], axis=0).astype(I32)


def _route(logits_t, router_b):
    nt = N_TOK // ROUTE_TT
    tok = lambda dt: jax.ShapeDtypeStruct((2, N_TOK), dt)
    tok_spec = pl.BlockSpec((2, ROUTE_TT), lambda p, j: (0, j * p))
    return pl.pallas_call(
        _route_kernel,
        out_shape=[tok(I32), tok(F32), tok(I32), jax.ShapeDtypeStruct((META_ROWS, META_LANES), I32)],
        grid=(2, nt),
        in_specs=[pl.BlockSpec((ROUTE_TT, ROUTE_LANES), lambda p, j: (j, 0)),
                  pl.BlockSpec((N_EXPERTS, 1), lambda p, j: (0, 0))],
        out_specs=[tok_spec, tok_spec, tok_spec, pl.BlockSpec((META_ROWS, META_LANES), lambda p, j: (0, 0))],
        scratch_shapes=[pltpu.VMEM((N_EXPERTS, 128), F32), pltpu.VMEM((N_EXPERTS, 128), F32),
                        pltpu.VMEM((N_EXPERTS, 128), F32)],
        compiler_params=_cparams(("arbitrary", "arbitrary")),
        name="router_slot_assignment",
    )(logits_t, router_b.reshape(N_EXPERTS, 1))


def _dispatch_copy(dest_ref, h_ref, xs_ref, sem, i, r, k):
    dst = dest_ref[k * N_TOK + i * TOK_TM + r]
    return pltpu.make_async_copy(h_ref.at[pl.ds(r, 1)], xs_ref.at[pl.ds(dst, 1)], sem)


def _zero_unit_copy(zero_ref, xs_ref, sem, row):
    return pltpu.make_async_copy(zero_ref, xs_ref.at[pl.ds(pl.multiple_of(row, MOE_SUB), MOE_SUB)], sem)


def _zero_fill_padding(meta_ref, zero_ref, xs_ref, sem):
    zero_ref[...] = jnp.zeros_like(zero_ref)
    n_used = meta_ref[META_LANES - 1]
    units = MOE_BLOCK // MOE_SUB

    def for_each_unit(fn):
        def expert_tail(e, carry):
            row = meta_ref[2 * META_LANES + e]
            first = meta_ref[3 * META_LANES + e]
            for u in range(units):
                @pl.when((row >= 0) & (u >= first))
                def _():
                    fn(_zero_unit_copy(zero_ref, xs_ref, sem, row + u * MOE_SUB))
            return carry

        def unused(b, carry):
            for u in range(units):
                fn(_zero_unit_copy(zero_ref, xs_ref, sem, b * MOE_BLOCK + u * MOE_SUB))
            return carry

        lax.fori_loop(0, N_EXPERTS, expert_tail, 0)
        lax.fori_loop(n_used, MOE_NBLK, unused, 0)

    for_each_unit(lambda copy: copy.start())
    for_each_unit(lambda copy: copy.wait())


def _dispatch_kernel(dest_ref, meta_ref, h_ref, xs_ref, zero_ref, sem):
    i = pl.program_id(0)

    @pl.when(i == 0)
    def _():
        _zero_fill_padding(meta_ref, zero_ref, xs_ref, sem)

    unroll = 8

    def start(g, carry):
        for u in range(unroll):
            for k in range(2):
                _dispatch_copy(dest_ref, h_ref, xs_ref, sem, i, g * unroll + u, k).start()
        return carry

    def wait(g, carry):
        for u in range(unroll):
            for k in range(2):
                _dispatch_copy(dest_ref, h_ref, xs_ref, sem, i, g * unroll + u, k).wait()
        return carry

    lax.fori_loop(0, TOK_TM // unroll, start, 0)
    lax.fori_loop(0, TOK_TM // unroll, wait, 0)


def _dispatch(dest_flat, meta_flat, h2d):
    d = D_MODEL
    return pl.pallas_call(
        _dispatch_kernel,
        out_shape=jax.ShapeDtypeStruct((MOE_SLOTS, d), F32),
        grid_spec=pltpu.PrefetchScalarGridSpec(
            num_scalar_prefetch=2,
            grid=(N_TOK // TOK_TM,),
            in_specs=[pl.BlockSpec((TOK_TM, d), lambda i, dest, meta: (i, 0))],
            out_specs=pl.BlockSpec(memory_space=pl.ANY),
            scratch_shapes=[pltpu.VMEM((MOE_SUB, d), F32), pltpu.SemaphoreType.DMA],
        ),
        compiler_params=_cparams(("arbitrary",)),
        name="moe_dispatch",
    )(dest_flat, meta_flat, h2d)


def _expert_kernel(meta_ref, x_ref, w1_ref, w3_ref, w2_ref, y_ref, xb_ref):
    blk = pl.program_id(0)
    f = pl.program_id(1)
    valid = jnp.where(blk < meta_ref[META_LANES - 1], meta_ref[META_LANES + blk], 0)

    @pl.when(f == 0)
    def _():
        xb_ref[...] = x_ref[...].astype(BF16)
        for u in range(MOE_BLOCK // MOE_ROWS):
            @pl.when(valid <= u * MOE_ROWS)
            def _():
                y_ref[u * MOE_ROWS:(u + 1) * MOE_ROWS, :] = jnp.zeros((MOE_ROWS, y_ref.shape[1]), F32)

    def compute(r0, nrows):
        x = xb_ref[r0:r0 + nrows, :]
        a = _silu(_dot(x, w1_ref[0, 0].astype(BF16))) * _dot(x, w3_ref[0, 0].astype(BF16))
        y = _dot(a.astype(BF16), w2_ref[0, 0].astype(BF16))

        @pl.when(f == 0)
        def _():
            y_ref[r0:r0 + nrows, :] = y

        @pl.when(f > 0)
        def _():
            y_ref[r0:r0 + nrows, :] += y

    first = 2 * MOE_SUB
    for r0, span in ((0, first), (first, MOE_BLOCK - first)):
        for nrows in range(MOE_ROWS, span + 1, MOE_ROWS):
            needed = valid > r0 + nrows - MOE_ROWS
            if nrows < span:
                needed = needed & (valid <= r0 + nrows)

            @pl.when(needed)
            def _():
                compute(r0, nrows)


def _experts(meta, xs, w1, w3, w2, layer):
    d = D_MODEL
    nf = D_EXPERT // MOE_FC
    n_used = lambda meta: meta[META_LANES - 1]
    used = lambda i, meta: jnp.minimum(i, n_used(meta) - 1)
    chunk = lambda i, f, meta: jnp.where(i < n_used(meta), f, nf - 1)
    w_in_spec = pl.BlockSpec((1, 1, d, MOE_FC), lambda i, f, meta: (layer, meta[used(i, meta)], 0, chunk(i, f, meta)))
    w_out_spec = pl.BlockSpec((1, 1, MOE_FC, d), lambda i, f, meta: (layer, meta[used(i, meta)], chunk(i, f, meta), 0))
    return pl.pallas_call(
        _expert_kernel,
        out_shape=jax.ShapeDtypeStruct((MOE_SLOTS, d), F32),
        grid_spec=pltpu.PrefetchScalarGridSpec(
            num_scalar_prefetch=1,
            grid=(MOE_NBLK, nf),
            in_specs=[pl.BlockSpec((MOE_BLOCK, d), lambda i, f, meta: (used(i, meta), 0)),
                      w_in_spec, w_in_spec, w_out_spec],
            out_specs=pl.BlockSpec((MOE_BLOCK, d), lambda i, f, meta: (i, 0)),
            scratch_shapes=[pltpu.VMEM((MOE_BLOCK, d), BF16)],
        ),
        compiler_params=_cparams(("arbitrary", "arbitrary")),
        name="moe_experts",
    )(meta, xs, w1, w3, w2)


def _combine_copy(dest_ref, yb_ref, buf_ref, sem, i, r, k):
    src = dest_ref[k * N_TOK + i * TOK_TM + r]
    return pltpu.make_async_copy(yb_ref.at[pl.ds(src, 1)], buf_ref.at[k, pl.ds(r, 1)], sem)


def _combine_kernel(dest_ref, yb_ref, x_ref, gate_ref, w_ref, o_ref, buf_ref, sem):
    i = pl.program_id(0)

    unroll = 8

    def start(g, carry):
        for u in range(unroll):
            for k in range(2):
                _combine_copy(dest_ref, yb_ref, buf_ref, sem, i, g * unroll + u, k).start()
        return carry

    def wait(g, carry):
        for u in range(unroll):
            for k in range(2):
                _combine_copy(dest_ref, yb_ref, buf_ref, sem, i, g * unroll + u, k).wait()
        return carry

    lax.fori_loop(0, TOK_TM // unroll, start, 0)
    lax.fori_loop(0, TOK_TM // unroll, wait, 0)
    w = w_ref[...]
    y = w[:, 0:1] * buf_ref[0] + w[:, 1:2] * buf_ref[1]
    o_ref[...] = x_ref[...] + gate_ref[0] * y


def _combine(dest_flat, yb, x2d, gate, w_tok):
    d = D_MODEL
    return pl.pallas_call(
        _combine_kernel,
        out_shape=jax.ShapeDtypeStruct((N_TOK, d), F32),
        grid_spec=pltpu.PrefetchScalarGridSpec(
            num_scalar_prefetch=1,
            grid=(N_TOK // TOK_TM,),
            in_specs=[pl.BlockSpec(memory_space=pl.ANY),
                      pl.BlockSpec((TOK_TM, d), lambda i, dest: (i, 0)),
                      pl.BlockSpec((1, 1, d), lambda i, dest: (i * TOK_TM // SEQ_LEN, 0, 0)),
                      pl.BlockSpec((TOK_TM, 2), lambda i, dest: (i, 0))],
            out_specs=pl.BlockSpec((TOK_TM, d), lambda i, dest: (i, 0)),
            scratch_shapes=[pltpu.VMEM((2, TOK_TM, d), F32), pltpu.SemaphoreType.DMA],
        ),
        compiler_params=_cparams(("arbitrary",)),
        name="moe_combine",
    )(dest_flat, yb, x2d, gate, w_tok)


def _moe(h2d, logits_t, x2d, gate, router_b, w1, w3, w2, layer):
    _, w_sel, dest, meta = _route(logits_t, router_b)
    dest_flat = dest.reshape(2 * N_TOK)
    meta_flat = meta.reshape(META_ROWS * META_LANES)
    xs = _dispatch(dest_flat, meta_flat, h2d)
    yb = _experts(meta_flat, xs, w1, w3, w2, layer)
    return _combine(dest_flat, yb, x2d, gate, w_sel.T)


def _rope_tables():
    f = HEAD_DIM // 4
    inv_freq = ROPE_THETA ** (-jnp.arange(f, dtype=F32) / f)
    pos = jnp.arange(SEQ_LEN)
    row = (pos // GRID_W).astype(F32)[:, None] * inv_freq
    col = (pos % GRID_W).astype(F32)[:, None] * inv_freq
    cos = jnp.concatenate([jnp.cos(row), jnp.cos(row), jnp.cos(col), jnp.cos(col)], axis=-1)
    sin = jnp.concatenate([-jnp.sin(row), jnp.sin(row), -jnp.sin(col), jnp.sin(col)], axis=-1)
    return cos, sin


def _permute_attn_w_in(w):
    o = [0, A_Q_DIM, A_KV_DIM, A_KV_DIM, GLA_K_DIM, GLA_K_DIM, GLA_V_DIM, GLA_V_DIM]
    s = [sum(o[:i + 1]) for i in range(len(o))]
    seg = lambda i: w[:, s[i]:s[i + 1]] if i + 1 < len(s) else w[:, s[i]:]
    aq, ak, av, gq, gk, gv, gg = (seg(i) for i in range(7))
    main = jnp.concatenate([aq, gv, gg, gq, gk, ak, av], axis=1).astype(BF16)
    lr = jnp.pad(w[:, PROJ_DIM:], ((0, 0), (0, LR_PAD - 2 * GLA_RANK))).astype(BF16)
    return main, lr


def kernel(x, c, ctx, c_ctx, mod_w, mod_b, attn_w_in, attn_q_norm, attn_k_norm, attn_sink, gla_wa2, gla_ba,
           gla_norm, attn_w_out, conv_w_in, conv_w, conv_w_out, router_w, router_b, exp_w1, exp_w3, exp_w2):
    d = D_MODEL
    cc = jnp.concatenate([c, c_ctx[None], jnp.zeros((8 - N_BATCH - 1, d), F32)], axis=0)
    m = _modulation(cc, mod_w, mod_b)
    mods = [[m[l, :, i * d:(i + 1) * d].reshape(8, 1, d) for i in range(6)] for l in range(mod_w.shape[0])]
    lat_row = lambda i: i * PROJ_TM // SEQ_LEN
    ctx_row = lambda i: N_BATCH
    rw_pad = jnp.pad(router_w, ((0, 0), (0, ROUTE_LANES - N_EXPERTS)))
    rw_hi = rw_pad.astype(BF16)
    rwt = jnp.stack([rw_hi, (rw_pad - rw_hi.astype(F32)).astype(BF16)])
    x2d = x.reshape(N_TOK, d)

    w_main, w_lr = _permute_attn_w_in(attn_w_in[0])
    p_lat, lr_lat = _projection(x2d, mods[0][0], mods[0][1], w_main, w_lr, lat_row)
    p_ctx, lr_ctx = _projection(ctx.reshape(N_BATCH * CTX_LEN, d), mods[0][0], mods[0][1], w_main, w_lr, ctx_row)
    p_lat = p_lat.reshape(N_BATCH, SEQ_LEN, PROJ_DIM)
    p_ctx = p_ctx.reshape(N_BATCH, CTX_LEN, PROJ_DIM)
    cos, sin_signed = _rope_tables()
    attn = _attention(p_lat, p_ctx, cos, sin_signed, attn_q_norm[0][None], attn_k_norm[0][None], attn_sink[0])
    wa_pad = jnp.zeros((2, LR_PAD, GLA_K_DIM), F32)
    for di in range(2):
        wa_pad = wa_pad.at[di, di * GLA_RANK:(di + 1) * GLA_RANK].set(gla_wa2[0, di])
    wa_pad = wa_pad.astype(BF16)
    ba = gla_ba[0].reshape(2, 1, GLA_K_DIM)
    s_zero = jnp.zeros((N_BATCH, 2, GLA_HEADS, GLA_DV, GLA_DK), F32)
    s_ctx = _gla(p_ctx, lr_ctx.reshape(N_BATCH, CTX_LEN, LR_PAD), wa_pad, ba, None, s_zero, latent=False)
    gla = _gla(p_lat, lr_lat.reshape(N_BATCH, SEQ_LEN, LR_PAD), wa_pad, ba, gla_norm[0][None], s_ctx, latent=True)
    x1, h2, lg = _attn_out(attn.reshape(N_TOK, A_Q_DIM), gla.reshape(N_TOK, GLA_V_DIM), attn_w_out[0].astype(BF16),
                           x2d, mods[0][2], mods[0][3], mods[0][4], rwt)
    x2 = _moe(h2, lg, x1, mods[0][5], router_b, exp_w1, exp_w3, exp_w2, 0)

    g = _projection(x2, mods[1][0], mods[1][1], conv_w_in[0].astype(BF16), None, lat_row)
    x3, h2, lg = _conv_out(g, conv_w[0], conv_w_out[0].astype(BF16), x2, mods[1][2], mods[1][3], mods[1][4], rwt)
    x4 = _moe(h2, lg, x3, mods[1][5], router_b, exp_w1, exp_w3, exp_w2, 1)
    return x4.reshape(N_BATCH, SEQ_LEN, d)
```

```python
import functools

import jax
import jax.numpy as jnp
from jax import lax
from jax.experimental import pallas as pl
from jax.experimental.pallas import tpu as pltpu

F32 = jnp.float32
BF16 = jnp.bfloat16
I32 = jnp.int32

D_MODEL = 2048
N_BATCH = 4
SEQ_LEN = 2048
CTX_LEN = 256
N_TOK = N_BATCH * SEQ_LEN
GRID_W = 64
HEAD_DIM = 128
A_Q_HEADS = 8
A_KV_HEADS = 2
A_GROUP = A_Q_HEADS // A_KV_HEADS
WINDOW = 128
A_BLOCK = 128
ROPE_THETA = 10000.0
GLA_HEADS = 4
GLA_DK = 128
GLA_DV = 256
GLA_RANK = 16
GLA_TAU = 16.0
GLA_CHUNK = 64
N_EXPERTS = 32
N_GROUPS = 4
GROUP_SIZE = N_EXPERTS // N_GROUPS
D_EXPERT = 1024
EPS = 1e-6
NEG_INF = -1e30
A_Q_DIM = A_Q_HEADS * HEAD_DIM
A_KV_DIM = A_KV_HEADS * HEAD_DIM
GLA_K_DIM = GLA_HEADS * GLA_DK
GLA_V_DIM = GLA_HEADS * GLA_DV
PROJ_DIM = A_Q_DIM + 2 * A_KV_DIM + 2 * GLA_K_DIM + 2 * GLA_V_DIM
LR_PAD = 128

COL_AQ = 0
COL_GV = 1
COL_GG = 2
COL_GQ = 6
COL_GK = 7
COL_AK = 32
COL_AV = 34

MOD_TN = 1024
PROJ_TM = 512
PROJ_TN = 512
OUT_TM = 256
CONV_HALO = 16
GLA_ROWS = 256
ROUTE_TT = 512
ROUTE_LANES = 128
MOE_ALIGN = 8
MOE_BLOCK = 768
MOE_ROWS = 128
MOE_FC = 256
MOE_NVIS = N_EXPERTS + (2 * N_TOK) // MOE_BLOCK
MOE_TAIL_PIECES = (256, 128, 64, 32, 16, 8)
MOE_SLOTS = 2 * N_TOK + sum(MOE_TAIL_PIECES)
assert MOE_SLOTS >= 2 * N_TOK + N_EXPERTS * (MOE_ALIGN - 1) + MOE_ROWS
TOK_TM = 256
META_ROWS = 5
META_LANES = 128
VMEM_LIMIT = 56 * 1024 * 1024


def _cparams(sem):
    return pltpu.CompilerParams(dimension_semantics=sem, vmem_limit_bytes=VMEM_LIMIT)


def _silu(x):
    return x * jax.nn.sigmoid(x)


def _dot(a, b):
    return jnp.dot(a, b, preferred_element_type=F32)


def _dot_nt(a, b):
    return lax.dot_general(a, b, (((1,), (1,)), ((), ())), preferred_element_type=F32)


def _dot_tn(a, b):
    return lax.dot_general(a, b, (((0,), (0,)), ((), ())), preferred_element_type=F32)


def _rms_modulate(x, shift, scale):
    r = lax.rsqrt(jnp.mean(x * x, axis=-1, keepdims=True) + EPS)
    return (x * r) * (1.0 + scale) + shift


def _mod_kernel(cc_ref, w_ref, b_ref, o_ref):
    a = _silu(cc_ref[...])
    o_ref[0] = _dot(a.astype(BF16), w_ref[0].astype(BF16)) + b_ref[0]


def _modulation(cc, mod_w, mod_b):
    depth, d, n = mod_w.shape
    return pl.pallas_call(
        _mod_kernel,
        out_shape=jax.ShapeDtypeStruct((depth, 8, n), F32),
        grid=(depth, n // MOD_TN),
        in_specs=[
            pl.BlockSpec((8, d), lambda l, j: (0, 0)),
            pl.BlockSpec((1, d, MOD_TN), lambda l, j: (l, 0, j)),
            pl.BlockSpec((1, 1, MOD_TN), lambda l, j: (l, 0, j)),
        ],
        out_specs=pl.BlockSpec((1, 8, MOD_TN), lambda l, j: (l, 0, j)),
        compiler_params=_cparams(("arbitrary", "arbitrary")),
        name="adaln_modulation",
    )(cc, mod_w, mod_b.reshape(depth, 1, n))


def _proj_kernel(with_lr, x_ref, sh_ref, sc_ref, w_ref, *rest):
    if with_lr:
        wlr_ref, o_ref, olr_ref = rest
    else:
        (o_ref,) = rest
    hb = _rms_modulate(x_ref[...], sh_ref[0], sc_ref[0]).astype(BF16)
    for c in range(w_ref.shape[1] // PROJ_TN):
        cols = slice(c * PROJ_TN, (c + 1) * PROJ_TN)
        o_ref[:, cols] = _dot(hb, w_ref[:, cols]).astype(BF16)
    if with_lr:
        olr_ref[...] = _dot(hb, wlr_ref[...]).astype(BF16)


def _projection(x2d, shift, scale, w, w_lr, mod_row):
    rows, d = x2d.shape
    n = w.shape[1]
    with_lr = w_lr is not None
    resident = lambda shape: pl.BlockSpec(shape, lambda i: (0, 0), pipeline_mode=pl.Buffered(1))
    in_specs = [
        pl.BlockSpec((PROJ_TM, d), lambda i: (i, 0)),
        pl.BlockSpec((1, 1, d), lambda i: (mod_row(i), 0, 0)),
        pl.BlockSpec((1, 1, d), lambda i: (mod_row(i), 0, 0)),
        resident((d, n)),
    ]
    out_shape = [jax.ShapeDtypeStruct((rows, n), BF16)]
    out_specs = [pl.BlockSpec((PROJ_TM, n), lambda i: (i, 0))]
    args = [x2d, shift, scale, w]
    if with_lr:
        in_specs.append(resident((d, LR_PAD)))
        out_shape.append(jax.ShapeDtypeStruct((rows, LR_PAD), BF16))
        out_specs.append(pl.BlockSpec((PROJ_TM, LR_PAD), lambda i: (i, 0)))
        args.append(w_lr)
    res = pl.pallas_call(
        functools.partial(_proj_kernel, with_lr),
        out_shape=out_shape,
        grid=(rows // PROJ_TM,),
        in_specs=in_specs,
        out_specs=out_specs,
        compiler_params=_cparams(("arbitrary",)),
        name="norm_mod_projection",
    )(*args)
    return res if with_lr else res[0]


def _swap_halves32(x):
    lane = lax.broadcasted_iota(I32, x.shape, 1)
    return jnp.where((lane & 63) < 32, pltpu.roll(x, 96, 1), pltpu.roll(x, 32, 1))


def _qk_norm(x, gain):
    return x * lax.rsqrt(jnp.mean(x * x, axis=-1, keepdims=True) + EPS) * gain


def _rope(x, cos, sin_signed):
    return x * cos + _swap_halves32(x) * sin_signed


ATT_WIN0 = CTX_LEN
ATT_LAT0 = CTX_LEN + A_BLOCK
ATT_ROWS = CTX_LEN + SEQ_LEN + 2 * A_BLOCK
ATT_WIN = 3 * A_BLOCK


def _attn_kernel(sink_ref, q_ref, k_ref, v_ref, kx_ref, vx_ref, cos_ref, sin_ref, qg_ref, kg_ref, o_ref,
                 kn_ref, vn_ref, band_ref):
    kvh = pl.program_id(1)
    nb = SEQ_LEN // A_BLOCK
    k_gain = kg_ref[...]
    q_gain = qg_ref[...]

    pad = jnp.zeros((A_BLOCK, HEAD_DIM), BF16)
    kn_ref[0:CTX_LEN, :] = _qk_norm(kx_ref[0].astype(F32), k_gain).astype(BF16)
    vn_ref[0:CTX_LEN, :] = vx_ref[0]
    for ref in (kn_ref, vn_ref):
        ref[ATT_WIN0:ATT_LAT0, :] = pad
        ref[ATT_LAT0 + SEQ_LEN:, :] = pad
    vn_ref[ATT_LAT0:ATT_LAT0 + SEQ_LEN, :] = v_ref[0]
    prep_rows = 4 * A_BLOCK
    for c in range(SEQ_LEN // prep_rows):
        r = slice(c * prep_rows, (c + 1) * prep_rows)
        kc = _rope(_qk_norm(k_ref[0, r, :].astype(F32), k_gain), cos_ref[r, :], sin_ref[r, :])
        kn_ref[ATT_LAT0 + c * prep_rows:ATT_LAT0 + (c + 1) * prep_rows, :] = kc.astype(BF16)

    qi = lax.broadcasted_iota(I32, (A_BLOCK, ATT_WIN), 0)
    wj = lax.broadcasted_iota(I32, (A_BLOCK, ATT_WIN), 1)
    band = jnp.where(jnp.abs(wj - A_BLOCK - qi) <= WINDOW, 1.0, 0.0)
    band_ref[0] = jnp.where(wj >= A_BLOCK, band, 0.0)
    band_ref[1] = band
    band_ref[2] = jnp.where(wj < 2 * A_BLOCK, band, 0.0)

    rows = A_GROUP * A_BLOCK
    rcol = lax.broadcasted_iota(I32, (rows, 1), 0)
    sink = jnp.zeros((rows, 1), F32)
    for g in range(A_GROUP):
        sink = jnp.where((rcol >= g * A_BLOCK) & (rcol < (g + 1) * A_BLOCK), sink_ref[kvh * A_GROUP + g], sink)

    def query_block(n):
        qrows = pl.ds(pl.multiple_of(n * A_BLOCK, A_BLOCK), A_BLOCK)
        wrows = pl.ds(pl.multiple_of(ATT_WIN0 + n * A_BLOCK, A_BLOCK), ATT_WIN)
        cos_q, sin_q = cos_ref[qrows, :], sin_ref[qrows, :]
        q = q_ref[0, qrows, :].astype(F32)
        qs = []
        for g in range(A_GROUP):
            qh = _rope(_qk_norm(q[:, g * HEAD_DIM:(g + 1) * HEAD_DIM], q_gain), cos_q, sin_q)
            qs.append((qh * HEAD_DIM ** -0.5).astype(BF16))
        qs = jnp.concatenate(qs, axis=0)
        mask = band_ref[jnp.where(n == 0, 0, jnp.where(n == nb - 1, 2, 1))]
        mask = jnp.concatenate([mask] * A_GROUP, axis=0)
        s_ctx = _dot_nt(qs, kn_ref[0:CTX_LEN, :])
        s_win = jnp.where(mask > 0.5, _dot_nt(qs, kn_ref[wrows, :]), NEG_INF)
        m = jnp.maximum(jnp.max(s_ctx, axis=-1, keepdims=True), jnp.max(s_win, axis=-1, keepdims=True))
        m = jnp.maximum(m, sink)
        p_ctx = jnp.exp(s_ctx - m)
        p_win = jnp.exp(s_win - m)
        denom = (jnp.sum(p_ctx, axis=-1, keepdims=True) + jnp.sum(p_win, axis=-1, keepdims=True)
                 + jnp.exp(sink - m))
        o = (_dot(p_ctx.astype(BF16), vn_ref[0:CTX_LEN, :]) + _dot(p_win.astype(BF16), vn_ref[wrows, :])) / denom
        for g in range(A_GROUP):
            o_ref[0, qrows, g * HEAD_DIM:(g + 1) * HEAD_DIM] = o[g * A_BLOCK:(g + 1) * A_BLOCK].astype(BF16)

    def query_block_pair(i, carry):
        query_block(2 * i)
        query_block(2 * i + 1)
        return carry

    lax.fori_loop(0, nb // 2, query_block_pair, 0)


def _attention(p_lat, p_ctx, cos, sin_signed, q_gain, k_gain, sink):
    gw = A_GROUP * HEAD_DIM
    lat = lambda width, col: pl.BlockSpec((1, SEQ_LEN, width), lambda b, k, s: (b, 0, col + k))
    ctx_blk = lambda col: pl.BlockSpec((1, CTX_LEN, HEAD_DIM), lambda b, k, s: (b, 0, col + k))
    full = lambda shape: pl.BlockSpec(shape, lambda b, k, s: (0,) * len(shape))
    return pl.pallas_call(
        _attn_kernel,
        out_shape=jax.ShapeDtypeStruct((N_BATCH, SEQ_LEN, A_Q_DIM), BF16),
        grid_spec=pltpu.PrefetchScalarGridSpec(
            num_scalar_prefetch=1,
            grid=(N_BATCH, A_KV_HEADS),
            in_specs=[
                lat(gw, COL_AQ), lat(HEAD_DIM, COL_AK), lat(HEAD_DIM, COL_AV),
                ctx_blk(COL_AK), ctx_blk(COL_AV),
                full((SEQ_LEN, HEAD_DIM)), full((SEQ_LEN, HEAD_DIM)),
                full((1, HEAD_DIM)), full((1, HEAD_DIM)),
            ],
            out_specs=pl.BlockSpec((1, SEQ_LEN, gw), lambda b, k, s: (b, 0, k)),
            scratch_shapes=[pltpu.VMEM((ATT_ROWS, HEAD_DIM), BF16), pltpu.VMEM((ATT_ROWS, HEAD_DIM), BF16),
                            pltpu.VMEM((3, A_BLOCK, ATT_WIN), F32)],
        ),
        compiler_params=_cparams(("arbitrary", "arbitrary")),
        name="windowed_sink_attention",
    )(sink, p_lat, p_lat, p_lat, p_ctx, p_ctx, cos, sin_signed, q_gain, k_gain)


def _split3_bf16(x):
    hi = x.astype(BF16)
    r1 = x - hi.astype(F32)
    mid = r1.astype(BF16)
    lo = (r1 - mid.astype(F32)).astype(BF16)
    return hi, mid, lo


def _gla_kernel(latent, nblk, q_ref, k_ref, v_ref, *rest):
    if latent:
        g_ref, lr_ref, wa_ref, ba_ref, gain_ref, s0_ref, o_ref, st_ref, ofwd_ref = rest
    else:
        lr_ref, wa_ref, ba_ref, s0_ref, o_ref, st_ref = rest
    d = pl.program_id(1)
    j = pl.program_id(2)
    nc = GLA_ROWS // GLA_CHUNK
    blk = jnp.where(d == 0, j, nblk - 1 - j)

    @pl.when(j == 0)
    def _():
        st_ref[...] = s0_ref[0, 0]

    if latent:
        @pl.when((pl.program_id(0) == 0) & (d == 0) & (j == 0))
        def _():
            ofwd_ref[...] = jnp.zeros_like(ofwd_ref)

    ii = lax.broadcasted_iota(I32, (GLA_CHUNK, GLA_CHUNK), 0)
    jj = lax.broadcasted_iota(I32, (GLA_CHUNK, GLA_CHUNK), 1)
    incl = ((jj - ii) * (1 - 2 * d)) <= 0
    incl_b = jnp.where(incl, 1.0, 0.0).astype(BF16)

    for i in range(nc):
        ci = jnp.where(d == 0, i, nc - 1 - i)
        r0 = pl.multiple_of(ci * GLA_CHUNK, GLA_CHUNK)
        rows = pl.ds(r0, GLA_CHUNK)
        z = _dot(lr_ref[0, rows, :], wa_ref[0]) + ba_ref[0]
        la = (jnp.minimum(z, 0.0) - jnp.log(1.0 + jnp.exp(-jnp.abs(z)))) / GLA_TAU
        hi, mid, lo = _split3_bf16(la)
        bcum = _dot(incl_b, hi) + _dot(incl_b, mid) + _dot(incl_b, lo)
        blast = jnp.sum(la, axis=0, keepdims=True)
        k = k_ref[0, rows, :].astype(F32)
        kl = (k * jnp.exp(blast - bcum)).astype(BF16)
        decay = jnp.exp(blast)
        v = v_ref[0, rows, :]
        if latent:
            qf = (q_ref[0, rows, :].astype(F32) * GLA_DK ** -0.5 * jnp.exp(bcum)).astype(BF16)
            kf = (k * jnp.exp(-bcum)).astype(BF16)
        for h in range(GLA_HEADS):
            ks = slice(h * GLA_DK, (h + 1) * GLA_DK)
            vs = slice(h * GLA_DV, (h + 1) * GLA_DV)
            st = st_ref[h]
            if latent:
                att = jnp.where(incl, _dot_nt(qf[:, ks], kf[:, ks]), 0.0)
                o = _dot(att.astype(BF16), v[:, vs]) + _dot_nt(qf[:, ks], st.astype(BF16))
                orow = pl.ds(pl.multiple_of(blk * GLA_ROWS + r0, GLA_CHUNK), GLA_CHUNK)
                ot = o + jnp.where(d == 1, ofwd_ref[orow, vs], 0.0)
                ofwd_ref[orow, vs] = ot
                on = ot * lax.rsqrt(jnp.mean(ot * ot, axis=-1, keepdims=True) + EPS) * gain_ref[...]
                o_ref[0, rows, vs] = (on * _silu(g_ref[0, rows, vs].astype(F32))).astype(BF16)

            st_ref[h] = st * decay[:, ks] + _dot_tn(v[:, vs], kl[:, ks])

    if not latent:
        @pl.when(j == nblk - 1)
        def _():
            o_ref[0, 0] = st_ref[...]


def _gla(p3, lr3, wa_pad, ba, o_gain, s0, latent):
    n = p3.shape[1]
    nblk = n // GLA_ROWS
    seq_blk = lambda d, j: jnp.where(d == 0, j, nblk - 1 - j)
    in_specs = [
        pl.BlockSpec((1, GLA_ROWS, GLA_K_DIM), lambda b, d, j: (b, seq_blk(d, j), COL_GQ)),
        pl.BlockSpec((1, GLA_ROWS, GLA_K_DIM), lambda b, d, j: (b, seq_blk(d, j), COL_GK)),
        pl.BlockSpec((1, GLA_ROWS, GLA_V_DIM), lambda b, d, j: (b, seq_blk(d, j), COL_GV)),
    ]
    args = [p3, p3, p3]
    if latent:
        in_specs.append(pl.BlockSpec((1, GLA_ROWS, GLA_V_DIM), lambda b, d, j: (b, seq_blk(d, j), COL_GG)))
        args.append(p3)
    in_specs += [
        pl.BlockSpec((1, GLA_ROWS, LR_PAD), lambda b, d, j: (b, seq_blk(d, j), 0)),
        pl.BlockSpec((1, LR_PAD, GLA_K_DIM), lambda b, d, j: (d, 0, 0)),
        pl.BlockSpec((1, 1, GLA_K_DIM), lambda b, d, j: (d, 0, 0)),
    ]
    args += [lr3, wa_pad, ba]
    if latent:
        in_specs.append(pl.BlockSpec((1, GLA_DV), lambda b, d, j: (0, 0)))
        args.append(o_gain)
    state_spec = pl.BlockSpec((1, 1, GLA_HEADS, GLA_DV, GLA_DK), lambda b, d, j: (b, d, 0, 0, 0))
    in_specs.append(state_spec)
    args.append(s0)
    scratch = [pltpu.VMEM((GLA_HEADS, GLA_DV, GLA_DK), F32)]
    if latent:
        out_shape = jax.ShapeDtypeStruct((N_BATCH, n, GLA_V_DIM), BF16)
        out_spec = pl.BlockSpec((1, GLA_ROWS, GLA_V_DIM),
                                lambda b, d, j: (b, jnp.where(d == 0, nblk - 1, nblk - 1 - j), 0))
        scratch.append(pltpu.VMEM((n, GLA_V_DIM), F32))
    else:
        out_shape = jax.ShapeDtypeStruct(s0.shape, F32)
        out_spec = state_spec
    return pl.pallas_call(
        functools.partial(_gla_kernel, latent, nblk),
        out_shape=out_shape,
        grid=(N_BATCH, 2, nblk),
        in_specs=in_specs,
        out_specs=out_spec,
        scratch_shapes=scratch,
        compiler_params=_cparams(("arbitrary", "arbitrary", "arbitrary")),
        name="gla_latent" if latent else "gla_context_state",
    )(*args)


def _router_logits(h, rw_ref):
    hh = h.astype(BF16)
    hl = (h - hh.astype(F32)).astype(BF16)
    return _dot(hh, rw_ref[0]) + _dot(hl, rw_ref[0]) + _dot(hh, rw_ref[1])


def _residual_and_moe_input(mix, x_ref, gate_ref, sh_ref, sc_ref, rwt_ref, x_out_ref, h_out_ref, lg_ref):
    x1 = x_ref[...] + gate_ref[0] * mix
    x_out_ref[...] = x1
    h = _rms_modulate(x1, sh_ref[0], sc_ref[0])
    h_out_ref[...] = h
    lg_ref[...] = _router_logits(h, rwt_ref)


def _attn_out_kernel(a1_ref, a2_ref, w_ref, x_ref, gate_ref, sh_ref, sc_ref, rwt_ref, x_out_ref, h_out_ref, lg_ref):
    mix = _dot(a1_ref[...], w_ref[0:A_Q_DIM, :]) + _dot(a2_ref[...], w_ref[A_Q_DIM:, :])
    _residual_and_moe_input(mix, x_ref, gate_ref, sh_ref, sc_ref, rwt_ref, x_out_ref, h_out_ref, lg_ref)


def _conv_out_kernel(gb_ref, gc_ref, u_ref, gcp_ref, up_ref, gcn_ref, un_ref, cw_ref, w_ref,
                     x_ref, gate_ref, sh_ref, sc_ref, rwt_ref, x_out_ref, h_out_ref, lg_ref):
    i = pl.program_id(0)
    tiles_per_seq = SEQ_LEN // OUT_TM
    t = gc_ref[...].astype(F32) * u_ref[...].astype(F32)
    first = (i % tiles_per_seq) == 0
    last = (i % tiles_per_seq) == tiles_per_seq - 1
    halo_last = slice(CONV_HALO - 1, CONV_HALO)
    t_before = jnp.where(first, 0.0, gcp_ref[halo_last, :].astype(F32) * up_ref[halo_last, :].astype(F32))
    t_after = jnp.where(last, 0.0, gcn_ref[0:1, :].astype(F32) * un_ref[0:1, :].astype(F32))
    row = lax.broadcasted_iota(I32, t.shape, 0)
    t_up = jnp.where(row == 0, t_before, pltpu.roll(t, 1, 0))
    t_dn = jnp.where(row == OUT_TM - 1, t_after, pltpu.roll(t, OUT_TM - 1, 0))
    y = cw_ref[0:1, :] * t_up + cw_ref[1:2, :] * t + cw_ref[2:3, :] * t_dn
    mix = _dot((gb_ref[...].astype(F32) * y).astype(BF16), w_ref[...])
    _residual_and_moe_input(mix, x_ref, gate_ref, sh_ref, sc_ref, rwt_ref, x_out_ref, h_out_ref, lg_ref)


def _mixer_out(kernel_fn, mixer_specs, mixer_args, x2d, gate, shift, scale, rwt, name):
    d = D_MODEL
    mod_spec = pl.BlockSpec((1, 1, d), lambda i: (i * OUT_TM // SEQ_LEN, 0, 0))
    row_spec = pl.BlockSpec((OUT_TM, d), lambda i: (i, 0))
    return pl.pallas_call(
        kernel_fn,
        out_shape=[jax.ShapeDtypeStruct((N_TOK, d), F32), jax.ShapeDtypeStruct((N_TOK, d), F32),
                   jax.ShapeDtypeStruct((N_TOK, ROUTE_LANES), F32)],
        grid=(N_TOK // OUT_TM,),
        in_specs=mixer_specs + [row_spec, mod_spec, mod_spec, mod_spec,
                                pl.BlockSpec((2, d, ROUTE_LANES), lambda i: (0, 0, 0))],
        out_specs=[row_spec, row_spec, pl.BlockSpec((OUT_TM, ROUTE_LANES), lambda i: (i, 0))],
        compiler_params=_cparams(("arbitrary",)),
        name=name,
    )(*mixer_args, x2d, gate, shift, scale, rwt)


def _attn_out(attn2d, gla2d, w_out, x2d, gate, shift, scale, rwt):
    half = pl.BlockSpec((OUT_TM, A_Q_DIM), lambda i: (i, 0))
    specs = [half, half, pl.BlockSpec((D_MODEL, D_MODEL), lambda i: (0, 0))]
    return _mixer_out(_attn_out_kernel, specs, [attn2d, gla2d, w_out], x2d, gate, shift, scale, rwt,
                      "attn_out_projection")


def _conv_out(g2d, conv_w, w_out, x2d, gate, shift, scale, rwt):
    d = D_MODEL
    sub = OUT_TM // CONV_HALO
    last_halo = N_TOK // CONV_HALO - 1
    main = lambda col: pl.BlockSpec((OUT_TM, d), lambda i: (i, col))
    before = lambda col: pl.BlockSpec((CONV_HALO, d), lambda i: (jnp.maximum(i * sub - 1, 0), col))
    after = lambda col: pl.BlockSpec((CONV_HALO, d), lambda i: (jnp.minimum((i + 1) * sub, last_halo), col))
    specs = [main(0), main(1), main(2), before(1), before(2), after(1), after(2),
             pl.BlockSpec((3, d), lambda i: (0, 0)), pl.BlockSpec((d, d), lambda i: (0, 0))]
    return _mixer_out(_conv_out_kernel, specs, [g2d] * 7 + [conv_w, w_out], x2d, gate, shift, scale, rwt,
                      "conv_out_projection")


def _first_argmax8(x, idx8):
    m = jnp.max(x, axis=0, keepdims=True)
    a = jnp.min(jnp.where(x == m, idx8, float(GROUP_SIZE)), axis=0, keepdims=True)
    return m, a


def _route_kernel(lg_ref, rb_ref, e_ref, w_ref, dest_ref, meta_ref, cnt_ref, carry_ref, pstart_ref):
    phase = pl.program_id(0)
    j = pl.program_id(1)
    tt = ROUTE_TT
    sc = jax.nn.sigmoid(lg_ref[...].T[:N_EXPERTS])
    grp = sc + rb_ref[...]
    idx8 = lax.broadcasted_iota(I32, (GROUP_SIZE, tt), 0).astype(F32)
    groups = [grp[g * GROUP_SIZE:(g + 1) * GROUP_SIZE] for g in range(N_GROUPS)]
    gscore = []
    for x in groups:
        m1, a1 = _first_argmax8(x, idx8)
        m2, _ = _first_argmax8(jnp.where(idx8 == a1, -jnp.inf, x), idx8)
        gscore.append(m1 + m2)
    gmax = functools.reduce(jnp.maximum, gscore)
    gsel = jnp.full((1, tt), float(N_GROUPS), F32)
    for g in reversed(range(N_GROUPS)):
        gsel = jnp.where(gscore[g] == gmax, float(g), gsel)
    in_grp = groups[0]
    for g in range(1, N_GROUPS):
        in_grp = jnp.where(gsel == float(g), groups[g], in_grp)
    _, a1 = _first_argmax8(in_grp, idx8)
    _, a2 = _first_argmax8(jnp.where(idx8 == a1, -jnp.inf, in_grp), idx8)
    e0 = gsel * GROUP_SIZE + a1
    e1 = gsel * GROUP_SIZE + a2
    idx32 = lax.broadcasted_iota(I32, (N_EXPERTS, tt), 0).astype(F32)
    oh0 = idx32 == e0
    oh1 = idx32 == e1
    w0 = jnp.sum(jnp.where(oh0, sc, 0.0), axis=0, keepdims=True)
    w1 = jnp.sum(jnp.where(oh1, sc, 0.0), axis=0, keepdims=True)
    wsum = w0 + w1
    ohs = jnp.where(oh0 | oh1, 1.0, 0.0)
    tile_cnt = jnp.sum(ohs, axis=1, keepdims=True)

    @pl.when(phase == 0)
    def _():
        @pl.when(j == 0)
        def _():
            cnt_ref[...] = jnp.zeros_like(cnt_ref)
        cnt_ref[...] += jnp.broadcast_to(tile_cnt, cnt_ref.shape)

    @pl.when(phase == 1)
    def _():
        @pl.when(j == 0)
        def _():
            cnt = cnt_ref[...]
            expert = lax.broadcasted_iota(I32, cnt.shape, 0)
            lane = lax.broadcasted_iota(I32, cnt.shape, 1).astype(F32)

            def exclusive_cumsum(v):
                inc = v
                for s in (1, 2, 4, 8, 16):
                    inc = inc + jnp.where(expert >= s, pltpu.roll(inc, s, 0), 0.0)
                return inc - v

            cnt_pad = jnp.floor((cnt + (MOE_ALIGN - 1)) * (1.0 / MOE_ALIGN)) * MOE_ALIGN
            pstart = exclusive_cumsum(cnt_pad)
            pstart_ref[...] = pstart
            carry_ref[...] = jnp.zeros_like(carry_ref)
            nvis_e = jnp.zeros_like(cnt)
            for kb in range(-(-N_TOK // MOE_BLOCK)):
                nvis_e += jnp.where(cnt > float(kb * MOE_BLOCK), 1.0, 0.0)
            vstart = exclusive_cumsum(nvis_e)
            vis_exp = jnp.sum(jnp.where(lane >= vstart + nvis_e, 1.0, 0.0), axis=0, keepdims=True)
            vis_exp = jnp.minimum(vis_exp, float(N_EXPERTS - 1))
            own = expert.astype(F32) == vis_exp
            of_visit = lambda col: jnp.sum(jnp.where(own, col, 0.0), axis=0, keepdims=True)
            kth = lane[0:1] - of_visit(vstart)
            vis_rows = jnp.clip(of_visit(cnt_pad) - kth * MOE_BLOCK, 0.0, float(MOE_BLOCK))
            vis_row0 = of_visit(pstart) + kth * MOE_BLOCK
            last_lane = lane[0:1] == float(META_LANES - 1)
            expert_lane = expert.astype(F32) == lane
            to_lane = lambda col: jnp.sum(jnp.where(expert_lane, col, 0.0), axis=0, keepdims=True)
            meta_ref[...] = jnp.concatenate([
                jnp.where(last_lane, jnp.sum(nvis_e, axis=0, keepdims=True), vis_exp),
                vis_rows,
                vis_row0,
                to_lane(pstart + cnt),
                jnp.where(last_lane, jnp.sum(cnt_pad, axis=0, keepdims=True), to_lane(cnt_pad - cnt)),
            ], axis=0).astype(I32)

        rr = lax.broadcasted_iota(I32, (tt, tt), 0)
        cc = lax.broadcasted_iota(I32, (tt, tt), 1)
        strict_upper = jnp.where(rr < cc, 1.0, 0.0).astype(BF16)
        before = _dot(ohs.astype(BF16), strict_upper)
        pos = before + carry_ref[:, 0:1] + pstart_ref[:, 0:1]
        d0 = jnp.sum(jnp.where(oh0, pos, 0.0), axis=0, keepdims=True)
        d1 = jnp.sum(jnp.where(oh1, pos, 0.0), axis=0, keepdims=True)
        carry_ref[...] += jnp.broadcast_to(tile_cnt, carry_ref.shape)
        e_ref[...] = jnp.concatenate([e0, e1], axis=0).astype(I32)
        w_ref[...] = jnp.concatenate([w0 / wsum, w1 / wsum], axis=0)
        dest_ref[...] = jnp.concatenate([d0, d1], axis=0).astype(I32)


def _route(logits_t, router_b):
    nt = N_TOK // ROUTE_TT
    tok = lambda dt: jax.ShapeDtypeStruct((2, N_TOK), dt)
    tok_spec = pl.BlockSpec((2, ROUTE_TT), lambda p, j: (0, j * p))
    return pl.pallas_call(
        _route_kernel,
        out_shape=[tok(I32), tok(F32), tok(I32), jax.ShapeDtypeStruct((META_ROWS, META_LANES), I32)],
        grid=(2, nt),
        in_specs=[pl.BlockSpec((ROUTE_TT, ROUTE_LANES), lambda p, j: (j, 0)),
                  pl.BlockSpec((N_EXPERTS, 1), lambda p, j: (0, 0))],
        out_specs=[tok_spec, tok_spec, tok_spec, pl.BlockSpec((META_ROWS, META_LANES), lambda p, j: (0, 0))],
        scratch_shapes=[pltpu.VMEM((N_EXPERTS, 128), F32), pltpu.VMEM((N_EXPERTS, 128), F32),
                        pltpu.VMEM((N_EXPERTS, 128), F32)],
        compiler_params=_cparams(("arbitrary", "arbitrary")),
        name="router_slot_assignment",
    )(logits_t, router_b.reshape(N_EXPERTS, 1))


def _dispatch_copy(dest_ref, h_ref, xs_ref, sem, i, r, k):
    dst = dest_ref[k * N_TOK + i * TOK_TM + r]
    return pltpu.make_async_copy(h_ref.at[pl.ds(r, 1)], xs_ref.at[pl.ds(dst, 1)], sem)


def _for_each_tail_piece(meta_ref, zero_ref, slots_ref, sem, fn):
    used = meta_ref[4 * META_LANES + META_LANES - 1]
    tail = MOE_SLOTS - used
    row = used
    for size in MOE_TAIL_PIECES:
        take = tail & size

        @pl.when(take != 0)
        def _():
            fn(pltpu.make_async_copy(zero_ref.at[pl.ds(0, size)],
                                     slots_ref.at[pl.ds(pl.multiple_of(row, MOE_ALIGN), size)], sem))
        row = row + take


def _for_each_padding_row(meta_ref, zero_ref, xs_ref, sem, fn):
    def expert_padding(e, carry):
        row = meta_ref[3 * META_LANES + e]
        npad = meta_ref[4 * META_LANES + e]
        for r in range(MOE_ALIGN - 1):
            @pl.when(r < npad)
            def _():
                fn(pltpu.make_async_copy(zero_ref.at[pl.ds(0, 1)], xs_ref.at[pl.ds(row + r, 1)], sem))
        return carry

    lax.fori_loop(0, N_EXPERTS, expert_padding, 0)


def _zero_fill_padding(meta_ref, zero_ref, xs_ref, sem):
    zero_ref[...] = jnp.zeros_like(zero_ref)
    for fn in (lambda copy: copy.start(), lambda copy: copy.wait()):
        _for_each_padding_row(meta_ref, zero_ref, xs_ref, sem, fn)
        _for_each_tail_piece(meta_ref, zero_ref, xs_ref, sem, fn)


def _dispatch_kernel(dest_ref, meta_ref, h_ref, xs_ref, zero_ref, sem):
    i = pl.program_id(0)

    @pl.when(i == 0)
    def _():
        _zero_fill_padding(meta_ref, zero_ref, xs_ref, sem)

    unroll = 8

    def start(g, carry):
        for u in range(unroll):
            for k in range(2):
                _dispatch_copy(dest_ref, h_ref, xs_ref, sem, i, g * unroll + u, k).start()
        return carry

    def wait(g, carry):
        for u in range(unroll):
            for k in range(2):
                _dispatch_copy(dest_ref, h_ref, xs_ref, sem, i, g * unroll + u, k).wait()
        return carry

    lax.fori_loop(0, TOK_TM // unroll, start, 0)
    lax.fori_loop(0, TOK_TM // unroll, wait, 0)


def _dispatch(dest_flat, meta_flat, h2d):
    d = D_MODEL
    return pl.pallas_call(
        _dispatch_kernel,
        out_shape=jax.ShapeDtypeStruct((MOE_SLOTS, d), F32),
        grid_spec=pltpu.PrefetchScalarGridSpec(
            num_scalar_prefetch=2,
            grid=(N_TOK // TOK_TM,),
            in_specs=[pl.BlockSpec((TOK_TM, d), lambda i, dest, meta: (i, 0))],
            out_specs=pl.BlockSpec(memory_space=pl.ANY),
            scratch_shapes=[pltpu.VMEM((MOE_TAIL_PIECES[0], d), F32), pltpu.SemaphoreType.DMA],
        ),
        compiler_params=_cparams(("arbitrary",)),
        name="moe_dispatch",
    )(dest_flat, meta_flat, h2d)


def _for_each_x_piece(meta_ref, xs_ref, xbuf_ref, sem, v, fn):
    rows = meta_ref[META_LANES + v]
    row0 = meta_ref[2 * META_LANES + v]
    slot = v % 2
    for p in range(MOE_BLOCK // MOE_ROWS):
        @pl.when(p * MOE_ROWS < rows)
        def _():
            src = xs_ref.at[pl.ds(pl.multiple_of(row0 + p * MOE_ROWS, MOE_ALIGN), MOE_ROWS)]
            fn(pltpu.make_async_copy(src, xbuf_ref.at[slot, pl.ds(p * MOE_ROWS, MOE_ROWS)], sem.at[slot]))


def _for_each_y_piece(meta_ref, ybuf_ref, yb_ref, sem, v, fn):
    rows = meta_ref[META_LANES + v]
    row0 = meta_ref[2 * META_LANES + v]

    def piece(off, size):
        dst = yb_ref.at[pl.ds(pl.multiple_of(row0 + off, MOE_ALIGN), size)]
        fn(pltpu.make_async_copy(ybuf_ref.at[pl.ds(pl.multiple_of(off, MOE_ALIGN), size)], dst, sem))

    for p in range(MOE_BLOCK // MOE_ROWS):
        @pl.when((p + 1) * MOE_ROWS <= rows)
        def _():
            piece(p * MOE_ROWS, MOE_ROWS)
    off = (rows // MOE_ROWS) * MOE_ROWS
    size = MOE_ROWS // 2
    while size >= MOE_ALIGN:
        take = (rows - off) & size

        @pl.when(take != 0)
        def _():
            piece(off, size)
        off = off + take
        size //= 2


def _expert_kernel(meta_ref, xs_ref, w1_ref, w3_ref, w2_ref, yb_ref, xbuf_ref, xb_ref, ybuf_ref, zero_ref,
                   xsem, ysem, zsem):
    v = pl.program_id(0)
    f = pl.program_id(1)
    nf = pl.num_programs(1)
    valid = meta_ref[META_LANES + v]
    start = lambda copy: copy.start()
    wait = lambda copy: copy.wait()

    @pl.when(f == 0)
    def _():
        @pl.when(v == 0)
        def _():
            zero_ref[...] = jnp.zeros_like(zero_ref)
            _for_each_tail_piece(meta_ref, zero_ref, yb_ref, zsem, start)
            _for_each_tail_piece(meta_ref, zero_ref, yb_ref, zsem, wait)
            _for_each_x_piece(meta_ref, xs_ref, xbuf_ref, xsem, v, start)

        _for_each_x_piece(meta_ref, xs_ref, xbuf_ref, xsem, v, wait)

        @pl.when(v + 1 < MOE_NVIS)
        def _():
            _for_each_x_piece(meta_ref, xs_ref, xbuf_ref, xsem, v + 1, start)

        for p in range(MOE_BLOCK // MOE_ROWS):
            @pl.when(p * MOE_ROWS < valid)
            def _():
                piece = slice(p * MOE_ROWS, (p + 1) * MOE_ROWS)
                xb_ref[piece, :] = xbuf_ref[v % 2, piece, :].astype(BF16)

    def previous_writeback_done():
        @pl.when(v > 0)
        def _():
            _for_each_y_piece(meta_ref, ybuf_ref, yb_ref, ysem, v - 1, wait)

    @pl.when((f == 0) & (valid == 0))
    def _():
        previous_writeback_done()

    def compute(r0, nrows):
        x = xb_ref[r0:r0 + nrows, :]
        a = _silu(_dot(x, w1_ref[0, 0].astype(BF16))) * _dot(x, w3_ref[0, 0].astype(BF16))
        y = _dot(a.astype(BF16), w2_ref[0, 0].astype(BF16))

        @pl.when(f == 0)
        def _():
            if r0 == 0:
                previous_writeback_done()
            ybuf_ref[r0:r0 + nrows, :] = y

        @pl.when(f > 0)
        def _():
            ybuf_ref[r0:r0 + nrows, :] += y

    first = 4 * MOE_ROWS
    for r0, span in ((0, first), (first, MOE_BLOCK - first)):
        for nrows in range(MOE_ROWS, span + 1, MOE_ROWS):
            needed = valid > r0 + nrows - MOE_ROWS
            if nrows < span:
                needed = needed & (valid <= r0 + nrows)

            @pl.when(needed)
            def _():
                compute(r0, nrows)

    @pl.when(f == nf - 1)
    def _():
        _for_each_y_piece(meta_ref, ybuf_ref, yb_ref, ysem, v, start)

        @pl.when(v == MOE_NVIS - 1)
        def _():
            _for_each_y_piece(meta_ref, ybuf_ref, yb_ref, ysem, v, wait)


def _experts(meta, xs, w1, w3, w2, layer):
    d = D_MODEL
    nf = D_EXPERT // MOE_FC
    n_vis = lambda meta: meta[META_LANES - 1]
    used = lambda v, meta: jnp.minimum(v, n_vis(meta) - 1)
    chunk = lambda v, f, meta: jnp.where(v < n_vis(meta), f, nf - 1)
    w_in_spec = pl.BlockSpec((1, 1, d, MOE_FC), lambda v, f, meta: (layer, meta[used(v, meta)], 0, chunk(v, f, meta)))
    w_out_spec = pl.BlockSpec((1, 1, MOE_FC, d), lambda v, f, meta: (layer, meta[used(v, meta)], chunk(v, f, meta), 0))
    return pl.pallas_call(
        _expert_kernel,
        out_shape=jax.ShapeDtypeStruct((MOE_SLOTS, d), F32),
        grid_spec=pltpu.PrefetchScalarGridSpec(
            num_scalar_prefetch=1,
            grid=(MOE_NVIS, nf),
            in_specs=[pl.BlockSpec(memory_space=pl.ANY), w_in_spec, w_in_spec, w_out_spec],
            out_specs=pl.BlockSpec(memory_space=pl.ANY),
            scratch_shapes=[pltpu.VMEM((2, MOE_BLOCK, d), F32), pltpu.VMEM((MOE_BLOCK, d), BF16),
                            pltpu.VMEM((MOE_BLOCK, d), F32), pltpu.VMEM((MOE_TAIL_PIECES[0], d), F32),
                            pltpu.SemaphoreType.DMA((2,)), pltpu.SemaphoreType.DMA, pltpu.SemaphoreType.DMA],
        ),
        compiler_params=_cparams(("arbitrary", "arbitrary")),
        name="moe_experts",
    )(meta, xs, w1, w3, w2)


def _combine_copy(dest_ref, yb_ref, buf_ref, sem, i, r, k):
    src = dest_ref[k * N_TOK + i * TOK_TM + r]
    return pltpu.make_async_copy(yb_ref.at[pl.ds(src, 1)], buf_ref.at[k, pl.ds(r, 1)], sem)


def _combine_kernel(dest_ref, yb_ref, x_ref, gate_ref, w_ref, o_ref, buf_ref, sem):
    i = pl.program_id(0)

    unroll = 8

    def start(g, carry):
        for u in range(unroll):
            for k in range(2):
                _combine_copy(dest_ref, yb_ref, buf_ref, sem, i, g * unroll + u, k).start()
        return carry

    def wait(g, carry):
        for u in range(unroll):
            for k in range(2):
                _combine_copy(dest_ref, yb_ref, buf_ref, sem, i, g * unroll + u, k).wait()
        return carry

    lax.fori_loop(0, TOK_TM // unroll, start, 0)
    lax.fori_loop(0, TOK_TM // unroll, wait, 0)
    w = w_ref[...]
    y = w[:, 0:1] * buf_ref[0] + w[:, 1:2] * buf_ref[1]
    o_ref[...] = x_ref[...] + gate_ref[0] * y


def _combine(dest_flat, yb, x2d, gate, w_tok):
    d = D_MODEL
    return pl.pallas_call(
        _combine_kernel,
        out_shape=jax.ShapeDtypeStruct((N_TOK, d), F32),
        grid_spec=pltpu.PrefetchScalarGridSpec(
            num_scalar_prefetch=1,
            grid=(N_TOK // TOK_TM,),
            in_specs=[pl.BlockSpec(memory_space=pl.ANY),
                      pl.BlockSpec((TOK_TM, d), lambda i, dest: (i, 0)),
                      pl.BlockSpec((1, 1, d), lambda i, dest: (i * TOK_TM // SEQ_LEN, 0, 0)),
                      pl.BlockSpec((TOK_TM, 2), lambda i, dest: (i, 0))],
            out_specs=pl.BlockSpec((TOK_TM, d), lambda i, dest: (i, 0)),
            scratch_shapes=[pltpu.VMEM((2, TOK_TM, d), F32), pltpu.SemaphoreType.DMA],
        ),
        compiler_params=_cparams(("arbitrary",)),
        name="moe_combine",
    )(dest_flat, yb, x2d, gate, w_tok)


def _moe(h2d, logits_t, x2d, gate, router_b, w1, w3, w2, layer):
    _, w_sel, dest, meta = _route(logits_t, router_b)
    dest_flat = dest.reshape(2 * N_TOK)
    meta_flat = meta.reshape(META_ROWS * META_LANES)
    xs = _dispatch(dest_flat, meta_flat, h2d)
    yb = _experts(meta_flat, xs, w1, w3, w2, layer)
    return _combine(dest_flat, yb, x2d, gate, w_sel.T)


def _rope_tables():
    f = HEAD_DIM // 4
    inv_freq = ROPE_THETA ** (-jnp.arange(f, dtype=F32) / f)
    pos = jnp.arange(SEQ_LEN)
    row = (pos // GRID_W).astype(F32)[:, None] * inv_freq
    col = (pos % GRID_W).astype(F32)[:, None] * inv_freq
    cos = jnp.concatenate([jnp.cos(row), jnp.cos(row), jnp.cos(col), jnp.cos(col)], axis=-1)
    sin = jnp.concatenate([-jnp.sin(row), jnp.sin(row), -jnp.sin(col), jnp.sin(col)], axis=-1)
    return cos, sin


def _permute_attn_w_in(w):
    o = [0, A_Q_DIM, A_KV_DIM, A_KV_DIM, GLA_K_DIM, GLA_K_DIM, GLA_V_DIM, GLA_V_DIM]
    s = [sum(o[:i + 1]) for i in range(len(o))]
    seg = lambda i: w[:, s[i]:s[i + 1]] if i + 1 < len(s) else w[:, s[i]:]
    aq, ak, av, gq, gk, gv, gg = (seg(i) for i in range(7))
    main = jnp.concatenate([aq, gv, gg, gq, gk, ak, av], axis=1).astype(BF16)
    lr = jnp.pad(w[:, PROJ_DIM:], ((0, 0), (0, LR_PAD - 2 * GLA_RANK))).astype(BF16)
    return main, lr


def kernel(x, c, ctx, c_ctx, mod_w, mod_b, attn_w_in, attn_q_norm, attn_k_norm, attn_sink, gla_wa2, gla_ba,
           gla_norm, attn_w_out, conv_w_in, conv_w, conv_w_out, router_w, router_b, exp_w1, exp_w3, exp_w2):
    d = D_MODEL
    cc = jnp.concatenate([c, c_ctx[None], jnp.zeros((8 - N_BATCH - 1, d), F32)], axis=0)
    m = _modulation(cc, mod_w, mod_b)
    mods = [[m[l, :, i * d:(i + 1) * d].reshape(8, 1, d) for i in range(6)] for l in range(mod_w.shape[0])]
    lat_row = lambda i: i * PROJ_TM // SEQ_LEN
    ctx_row = lambda i: N_BATCH
    rw_pad = jnp.pad(router_w, ((0, 0), (0, ROUTE_LANES - N_EXPERTS)))
    rw_hi = rw_pad.astype(BF16)
    rwt = jnp.stack([rw_hi, (rw_pad - rw_hi.astype(F32)).astype(BF16)])
    x2d = x.reshape(N_TOK, d)

    w_main, w_lr = _permute_attn_w_in(attn_w_in[0])
    p_lat, lr_lat = _projection(x2d, mods[0][0], mods[0][1], w_main, w_lr, lat_row)
    p_ctx, lr_ctx = _projection(ctx.reshape(N_BATCH * CTX_LEN, d), mods[0][0], mods[0][1], w_main, w_lr, ctx_row)
    p_lat = p_lat.reshape(N_BATCH, SEQ_LEN, PROJ_DIM)
    p_ctx = p_ctx.reshape(N_BATCH, CTX_LEN, PROJ_DIM)
    cos, sin_signed = _rope_tables()
    attn = _attention(p_lat, p_ctx, cos, sin_signed, attn_q_norm[0][None], attn_k_norm[0][None], attn_sink[0])
    wa_pad = jnp.zeros((2, LR_PAD, GLA_K_DIM), F32)
    for di in range(2):
        wa_pad = wa_pad.at[di, di * GLA_RANK:(di + 1) * GLA_RANK].set(gla_wa2[0, di])
    wa_pad = wa_pad.astype(BF16)
    ba = gla_ba[0].reshape(2, 1, GLA_K_DIM)
    s_zero = jnp.zeros((N_BATCH, 2, GLA_HEADS, GLA_DV, GLA_DK), F32)
    s_ctx = _gla(p_ctx, lr_ctx.reshape(N_BATCH, CTX_LEN, LR_PAD), wa_pad, ba, None, s_zero, latent=False)
    gla = _gla(p_lat, lr_lat.reshape(N_BATCH, SEQ_LEN, LR_PAD), wa_pad, ba, gla_norm[0][None], s_ctx, latent=True)
    x1, h2, lg = _attn_out(attn.reshape(N_TOK, A_Q_DIM), gla.reshape(N_TOK, GLA_V_DIM), attn_w_out[0].astype(BF16),
                           x2d, mods[0][2], mods[0][3], mods[0][4], rwt)
    x2 = _moe(h2, lg, x1, mods[0][5], router_b, exp_w1, exp_w3, exp_w2, 0)

    g = _projection(x2, mods[1][0], mods[1][1], conv_w_in[0].astype(BF16), None, lat_row)
    x3, h2, lg = _conv_out(g, conv_w[0], conv_w_out[0].astype(BF16), x2, mods[1][2], mods[1][3], mods[1][4], rwt)
    x4 = _moe(h2, lg, x3, mods[1][5], router_b, exp_w1, exp_w3, exp_w2, 1)
    return x4.reshape(N_BATCH, SEQ_LEN, d)
```

```python
import functools

import jax
import jax.numpy as jnp
from jax import lax
from jax.experimental import pallas as pl
from jax.experimental.pallas import tpu as pltpu

F32 = jnp.float32
BF16 = jnp.bfloat16
I32 = jnp.int32

D_MODEL = 2048
N_BATCH = 4
SEQ_LEN = 2048
CTX_LEN = 256
N_TOK = N_BATCH * SEQ_LEN
GRID_W = 64
HEAD_DIM = 128
A_Q_HEADS = 8
A_KV_HEADS = 2
A_GROUP = A_Q_HEADS // A_KV_HEADS
WINDOW = 128
A_BLOCK = 128
ROPE_THETA = 10000.0
GLA_HEADS = 4
GLA_DK = 128
GLA_DV = 256
GLA_RANK = 16
GLA_TAU = 16.0
GLA_CHUNK = 64
N_EXPERTS = 32
N_GROUPS = 4
GROUP_SIZE = N_EXPERTS // N_GROUPS
D_EXPERT = 1024
EPS = 1e-6
NEG_INF = -1e30
A_Q_DIM = A_Q_HEADS * HEAD_DIM
A_KV_DIM = A_KV_HEADS * HEAD_DIM
GLA_K_DIM = GLA_HEADS * GLA_DK
GLA_V_DIM = GLA_HEADS * GLA_DV
PROJ_DIM = A_Q_DIM + 2 * A_KV_DIM + 2 * GLA_K_DIM + 2 * GLA_V_DIM
LR_PAD = 128

COL_AQ = 0
COL_GV = 1
COL_GG = 2
COL_GQ = 6
COL_GK = 7
COL_AK = 32
COL_AV = 34

MOD_TN = 1024
PROJ_TM = 512
PROJ_TN = 512
OUT_TM = 256
CONV_HALO = 16
GLA_ROWS = 256
ROUTE_TT = 512
ROUTE_LANES = 128
MOE_ALIGN = 8
MOE_BLOCK = 768
MOE_ROWS = 128
MOE_FC = 256
MOE_NVIS = N_EXPERTS + (2 * N_TOK) // MOE_BLOCK
MOE_TAIL_PIECES = (256, 128, 64, 32, 16, 8)
MOE_SLOTS = 2 * N_TOK + sum(MOE_TAIL_PIECES)
assert MOE_SLOTS >= 2 * N_TOK + N_EXPERTS * (MOE_ALIGN - 1) + MOE_ROWS
TOK_TM = 256
META_ROWS = 5
META_LANES = 128
VMEM_LIMIT = 56 * 1024 * 1024


def _cparams(sem):
    return pltpu.CompilerParams(dimension_semantics=sem, vmem_limit_bytes=VMEM_LIMIT)


def _silu(x):
    return x * jax.nn.sigmoid(x)


def _dot(a, b):
    return jnp.dot(a, b, preferred_element_type=F32)


def _dot_nt(a, b):
    return lax.dot_general(a, b, (((1,), (1,)), ((), ())), preferred_element_type=F32)


def _dot_tn(a, b):
    return lax.dot_general(a, b, (((0,), (0,)), ((), ())), preferred_element_type=F32)


def _rms_modulate(x, shift, scale):
    r = lax.rsqrt(jnp.mean(x * x, axis=-1, keepdims=True) + EPS)
    return (x * r) * (1.0 + scale) + shift


def _mod_kernel(cc_ref, w_ref, b_ref, o_ref):
    a = _silu(cc_ref[...])
    o_ref[0] = _dot(a.astype(BF16), w_ref[0].astype(BF16)) + b_ref[0]


def _modulation(cc, mod_w, mod_b):
    depth, d, n = mod_w.shape
    return pl.pallas_call(
        _mod_kernel,
        out_shape=jax.ShapeDtypeStruct((depth, 8, n), F32),
        grid=(depth, n // MOD_TN),
        in_specs=[
            pl.BlockSpec((8, d), lambda l, j: (0, 0)),
            pl.BlockSpec((1, d, MOD_TN), lambda l, j: (l, 0, j)),
            pl.BlockSpec((1, 1, MOD_TN), lambda l, j: (l, 0, j)),
        ],
        out_specs=pl.BlockSpec((1, 8, MOD_TN), lambda l, j: (l, 0, j)),
        compiler_params=_cparams(("arbitrary", "arbitrary")),
        name="adaln_modulation",
    )(cc, mod_w, mod_b.reshape(depth, 1, n))


def _proj_kernel(with_lr, x_ref, sh_ref, sc_ref, w_ref, *rest):
    if with_lr:
        wlr_ref, o_ref, olr_ref = rest
    else:
        (o_ref,) = rest
    hb = _rms_modulate(x_ref[...], sh_ref[0], sc_ref[0]).astype(BF16)
    for c in range(w_ref.shape[1] // PROJ_TN):
        cols = slice(c * PROJ_TN, (c + 1) * PROJ_TN)
        o_ref[:, cols] = _dot(hb, w_ref[:, cols]).astype(BF16)
    if with_lr:
        olr_ref[...] = _dot(hb, wlr_ref[...]).astype(BF16)


def _projection(x2d, shift, scale, w, w_lr, mod_row):
    rows, d = x2d.shape
    n = w.shape[1]
    with_lr = w_lr is not None
    resident = lambda shape: pl.BlockSpec(shape, lambda i: (0, 0), pipeline_mode=pl.Buffered(1))
    in_specs = [
        pl.BlockSpec((PROJ_TM, d), lambda i: (i, 0)),
        pl.BlockSpec((1, 1, d), lambda i: (mod_row(i), 0, 0)),
        pl.BlockSpec((1, 1, d), lambda i: (mod_row(i), 0, 0)),
        resident((d, n)),
    ]
    out_shape = [jax.ShapeDtypeStruct((rows, n), BF16)]
    out_specs = [pl.BlockSpec((PROJ_TM, n), lambda i: (i, 0))]
    args = [x2d, shift, scale, w]
    if with_lr:
        in_specs.append(resident((d, LR_PAD)))
        out_shape.append(jax.ShapeDtypeStruct((rows, LR_PAD), BF16))
        out_specs.append(pl.BlockSpec((PROJ_TM, LR_PAD), lambda i: (i, 0)))
        args.append(w_lr)
    res = pl.pallas_call(
        functools.partial(_proj_kernel, with_lr),
        out_shape=out_shape,
        grid=(rows // PROJ_TM,),
        in_specs=in_specs,
        out_specs=out_specs,
        compiler_params=_cparams(("arbitrary",)),
        name="norm_mod_projection",
    )(*args)
    return res if with_lr else res[0]


def _swap_halves32(x):
    lane = lax.broadcasted_iota(I32, x.shape, 1)
    return jnp.where((lane & 63) < 32, pltpu.roll(x, 96, 1), pltpu.roll(x, 32, 1))


def _qk_norm(x, gain):
    return x * lax.rsqrt(jnp.mean(x * x, axis=-1, keepdims=True) + EPS) * gain


def _rope(x, cos, sin_signed):
    return x * cos + _swap_halves32(x) * sin_signed


ATT_WIN0 = CTX_LEN
ATT_LAT0 = CTX_LEN + A_BLOCK
ATT_ROWS = CTX_LEN + SEQ_LEN + 2 * A_BLOCK
ATT_WIN = 3 * A_BLOCK


def _attn_kernel(sink_ref, q_ref, k_ref, v_ref, kx_ref, vx_ref, cos_ref, sin_ref, qg_ref, kg_ref, o_ref,
                 kn_ref, vn_ref, band_ref):
    kvh = pl.program_id(1)
    nb = SEQ_LEN // A_BLOCK
    k_gain = kg_ref[...]
    q_gain = qg_ref[...]

    pad = jnp.zeros((A_BLOCK, HEAD_DIM), BF16)
    kn_ref[0:CTX_LEN, :] = _qk_norm(kx_ref[0].astype(F32), k_gain).astype(BF16)
    vn_ref[0:CTX_LEN, :] = vx_ref[0]
    for ref in (kn_ref, vn_ref):
        ref[ATT_WIN0:ATT_LAT0, :] = pad
        ref[ATT_LAT0 + SEQ_LEN:, :] = pad
    vn_ref[ATT_LAT0:ATT_LAT0 + SEQ_LEN, :] = v_ref[0]
    prep_rows = 4 * A_BLOCK
    for c in range(SEQ_LEN // prep_rows):
        r = slice(c * prep_rows, (c + 1) * prep_rows)
        kc = _rope(_qk_norm(k_ref[0, r, :].astype(F32), k_gain), cos_ref[r, :], sin_ref[r, :])
        kn_ref[ATT_LAT0 + c * prep_rows:ATT_LAT0 + (c + 1) * prep_rows, :] = kc.astype(BF16)

    qi = lax.broadcasted_iota(I32, (A_BLOCK, ATT_WIN), 0)
    wj = lax.broadcasted_iota(I32, (A_BLOCK, ATT_WIN), 1)
    band = jnp.where(jnp.abs(wj - A_BLOCK - qi) <= WINDOW, 1.0, 0.0)
    band_ref[0] = jnp.where(wj >= A_BLOCK, band, 0.0)
    band_ref[1] = band
    band_ref[2] = jnp.where(wj < 2 * A_BLOCK, band, 0.0)

    rows = A_GROUP * A_BLOCK
    rcol = lax.broadcasted_iota(I32, (rows, 1), 0)
    sink = jnp.zeros((rows, 1), F32)
    for g in range(A_GROUP):
        sink = jnp.where((rcol >= g * A_BLOCK) & (rcol < (g + 1) * A_BLOCK), sink_ref[kvh * A_GROUP + g], sink)

    def query_block(n):
        qrows = pl.ds(pl.multiple_of(n * A_BLOCK, A_BLOCK), A_BLOCK)
        wrows = pl.ds(pl.multiple_of(ATT_WIN0 + n * A_BLOCK, A_BLOCK), ATT_WIN)
        cos_q, sin_q = cos_ref[qrows, :], sin_ref[qrows, :]
        q = q_ref[0, qrows, :].astype(F32)
        qs = []
        for g in range(A_GROUP):
            qh = _rope(_qk_norm(q[:, g * HEAD_DIM:(g + 1) * HEAD_DIM], q_gain), cos_q, sin_q)
            qs.append((qh * HEAD_DIM ** -0.5).astype(BF16))
        qs = jnp.concatenate(qs, axis=0)
        mask = band_ref[jnp.where(n == 0, 0, jnp.where(n == nb - 1, 2, 1))]
        mask = jnp.concatenate([mask] * A_GROUP, axis=0)
        s_ctx = _dot_nt(qs, kn_ref[0:CTX_LEN, :])
        s_win = jnp.where(mask > 0.5, _dot_nt(qs, kn_ref[wrows, :]), NEG_INF)
        m = jnp.maximum(jnp.max(s_ctx, axis=-1, keepdims=True), jnp.max(s_win, axis=-1, keepdims=True))
        m = jnp.maximum(m, sink)
        p_ctx = jnp.exp(s_ctx - m)
        p_win = jnp.exp(s_win - m)
        denom = (jnp.sum(p_ctx, axis=-1, keepdims=True) + jnp.sum(p_win, axis=-1, keepdims=True)
                 + jnp.exp(sink - m))
        o = (_dot(p_ctx.astype(BF16), vn_ref[0:CTX_LEN, :]) + _dot(p_win.astype(BF16), vn_ref[wrows, :])) / denom
        for g in range(A_GROUP):
            o_ref[0, qrows, g * HEAD_DIM:(g + 1) * HEAD_DIM] = o[g * A_BLOCK:(g + 1) * A_BLOCK].astype(BF16)

    def query_block_pair(i, carry):
        query_block(2 * i)
        query_block(2 * i + 1)
        return carry

    lax.fori_loop(0, nb // 2, query_block_pair, 0)


def _attention(p_lat, p_ctx, cos, sin_signed, q_gain, k_gain, sink):
    gw = A_GROUP * HEAD_DIM
    lat = lambda width, col: pl.BlockSpec((1, SEQ_LEN, width), lambda b, k, s: (b, 0, col + k))
    ctx_blk = lambda col: pl.BlockSpec((1, CTX_LEN, HEAD_DIM), lambda b, k, s: (b, 0, col + k))
    full = lambda shape: pl.BlockSpec(shape, lambda b, k, s: (0,) * len(shape))
    return pl.pallas_call(
        _attn_kernel,
        out_shape=jax.ShapeDtypeStruct((N_BATCH, SEQ_LEN, A_Q_DIM), BF16),
        grid_spec=pltpu.PrefetchScalarGridSpec(
            num_scalar_prefetch=1,
            grid=(N_BATCH, A_KV_HEADS),
            in_specs=[
                lat(gw, COL_AQ), lat(HEAD_DIM, COL_AK), lat(HEAD_DIM, COL_AV),
                ctx_blk(COL_AK), ctx_blk(COL_AV),
                full((SEQ_LEN, HEAD_DIM)), full((SEQ_LEN, HEAD_DIM)),
                full((1, HEAD_DIM)), full((1, HEAD_DIM)),
            ],
            out_specs=pl.BlockSpec((1, SEQ_LEN, gw), lambda b, k, s: (b, 0, k)),
            scratch_shapes=[pltpu.VMEM((ATT_ROWS, HEAD_DIM), BF16), pltpu.VMEM((ATT_ROWS, HEAD_DIM), BF16),
                            pltpu.VMEM((3, A_BLOCK, ATT_WIN), F32)],
        ),
        compiler_params=_cparams(("arbitrary", "arbitrary")),
        name="windowed_sink_attention",
    )(sink, p_lat, p_lat, p_lat, p_ctx, p_ctx, cos, sin_signed, q_gain, k_gain)


def _split3_bf16(x):
    hi = x.astype(BF16)
    r1 = x - hi.astype(F32)
    mid = r1.astype(BF16)
    lo = (r1 - mid.astype(F32)).astype(BF16)
    return hi, mid, lo


def _gla_kernel(latent, nblk, q_ref, k_ref, v_ref, *rest):
    if latent:
        g_ref, lr_ref, wa_ref, ba_ref, gain_ref, s0_ref, o_ref, st_ref, ofwd_ref = rest
    else:
        lr_ref, wa_ref, ba_ref, s0_ref, o_ref, st_ref = rest
    d = pl.program_id(1)
    j = pl.program_id(2)
    nc = GLA_ROWS // GLA_CHUNK
    blk = jnp.where(d == 0, j, nblk - 1 - j)

    @pl.when(j == 0)
    def _():
        st_ref[...] = s0_ref[0, 0]

    ii = lax.broadcasted_iota(I32, (GLA_CHUNK, GLA_CHUNK), 0)
    jj = lax.broadcasted_iota(I32, (GLA_CHUNK, GLA_CHUNK), 1)

    def scan_block(backward):
        incl = (jj >= ii) if backward else (jj <= ii)
        incl_b = jnp.where(incl, 1.0, 0.0).astype(BF16)
        for i in range(nc):
            r0 = (nc - 1 - i if backward else i) * GLA_CHUNK
            rows = slice(r0, r0 + GLA_CHUNK)
            z = _dot(lr_ref[0, rows, :], wa_ref[0]) + ba_ref[0]
            la = (jnp.minimum(z, 0.0) - jnp.log(1.0 + jnp.exp(-jnp.abs(z)))) / GLA_TAU
            hi, mid, lo = _split3_bf16(la)
            bcum = _dot(incl_b, hi) + _dot(incl_b, mid) + _dot(incl_b, lo)
            blast = jnp.sum(la, axis=0, keepdims=True)
            k = k_ref[0, rows, :].astype(F32)
            kl = (k * jnp.exp(blast - bcum)).astype(BF16)
            decay = jnp.exp(blast)
            v = v_ref[0, rows, :]
            if latent:
                qf = (q_ref[0, rows, :].astype(F32) * GLA_DK ** -0.5 * jnp.exp(bcum)).astype(BF16)
                kf = (k * jnp.exp(-bcum)).astype(BF16)
            for h in range(GLA_HEADS):
                ks = slice(h * GLA_DK, (h + 1) * GLA_DK)
                vs = slice(h * GLA_DV, (h + 1) * GLA_DV)
                st = st_ref[h]
                if latent:
                    att = jnp.where(incl, _dot_nt(qf[:, ks], kf[:, ks]), 0.0)
                    o = _dot(att.astype(BF16), v[:, vs]) + _dot_nt(qf[:, ks], st.astype(BF16))
                    orow = pl.ds(pl.multiple_of(blk * GLA_ROWS + r0, GLA_CHUNK), GLA_CHUNK)
                    if backward:
                        ot = o + ofwd_ref[orow, vs]
                        on = ot * lax.rsqrt(jnp.mean(ot * ot, axis=-1, keepdims=True) + EPS) * gain_ref[...]
                        o_ref[0, rows, vs] = (on * _silu(g_ref[0, rows, vs].astype(F32))).astype(BF16)
                    else:
                        ofwd_ref[orow, vs] = o
                st_ref[h] = st * decay[:, ks] + _dot_tn(v[:, vs], kl[:, ks])

    for backward in (False, True):
        @pl.when(d == int(backward))
        def _():
            scan_block(backward)

    if not latent:
        @pl.when(j == nblk - 1)
        def _():
            o_ref[0, 0] = st_ref[...]


def _gla(p3, lr3, wa_pad, ba, o_gain, s0, latent):
    n = p3.shape[1]
    nblk = n // GLA_ROWS
    seq_blk = lambda d, j: jnp.where(d == 0, j, nblk - 1 - j)
    in_specs = [
        pl.BlockSpec((1, GLA_ROWS, GLA_K_DIM), lambda b, d, j: (b, seq_blk(d, j), COL_GQ)),
        pl.BlockSpec((1, GLA_ROWS, GLA_K_DIM), lambda b, d, j: (b, seq_blk(d, j), COL_GK)),
        pl.BlockSpec((1, GLA_ROWS, GLA_V_DIM), lambda b, d, j: (b, seq_blk(d, j), COL_GV)),
    ]
    args = [p3, p3, p3]
    if latent:
        in_specs.append(pl.BlockSpec((1, GLA_ROWS, GLA_V_DIM), lambda b, d, j: (b, seq_blk(d, j), COL_GG)))
        args.append(p3)
    in_specs += [
        pl.BlockSpec((1, GLA_ROWS, LR_PAD), lambda b, d, j: (b, seq_blk(d, j), 0)),
        pl.BlockSpec((1, LR_PAD, GLA_K_DIM), lambda b, d, j: (d, 0, 0)),
        pl.BlockSpec((1, 1, GLA_K_DIM), lambda b, d, j: (d, 0, 0)),
    ]
    args += [lr3, wa_pad, ba]
    if latent:
        in_specs.append(pl.BlockSpec((1, GLA_DV), lambda b, d, j: (0, 0)))
        args.append(o_gain)
    state_spec = pl.BlockSpec((1, 1, GLA_HEADS, GLA_DV, GLA_DK), lambda b, d, j: (b, d, 0, 0, 0))
    in_specs.append(state_spec)
    args.append(s0)
    scratch = [pltpu.VMEM((GLA_HEADS, GLA_DV, GLA_DK), F32)]
    if latent:
        out_shape = jax.ShapeDtypeStruct((N_BATCH, n, GLA_V_DIM), BF16)
        out_spec = pl.BlockSpec((1, GLA_ROWS, GLA_V_DIM),
                                lambda b, d, j: (b, jnp.where(d == 0, nblk - 1, nblk - 1 - j), 0))
        scratch.append(pltpu.VMEM((n, GLA_V_DIM), F32))
    else:
        out_shape = jax.ShapeDtypeStruct(s0.shape, F32)
        out_spec = state_spec
    return pl.pallas_call(
        functools.partial(_gla_kernel, latent, nblk),
        out_shape=out_shape,
        grid=(N_BATCH, 2, nblk),
        in_specs=in_specs,
        out_specs=out_spec,
        scratch_shapes=scratch,
        compiler_params=_cparams(("arbitrary", "arbitrary", "arbitrary")),
        name="gla_latent" if latent else "gla_context_state",
    )(*args)


def _router_logits(h, rw_ref):
    hh = h.astype(BF16)
    hl = (h - hh.astype(F32)).astype(BF16)
    return _dot(hh, rw_ref[0]) + _dot(hl, rw_ref[0]) + _dot(hh, rw_ref[1])


def _residual_and_moe_input(mix, x_ref, gate_ref, sh_ref, sc_ref, rwt_ref, x_out_ref, h_out_ref, lg_ref):
    x1 = x_ref[...] + gate_ref[0] * mix
    x_out_ref[...] = x1
    h = _rms_modulate(x1, sh_ref[0], sc_ref[0])
    h_out_ref[...] = h
    lg_ref[...] = _router_logits(h, rwt_ref)


def _attn_out_kernel(a1_ref, a2_ref, w_ref, x_ref, gate_ref, sh_ref, sc_ref, rwt_ref, x_out_ref, h_out_ref, lg_ref):
    mix = _dot(a1_ref[...], w_ref[0:A_Q_DIM, :]) + _dot(a2_ref[...], w_ref[A_Q_DIM:, :])
    _residual_and_moe_input(mix, x_ref, gate_ref, sh_ref, sc_ref, rwt_ref, x_out_ref, h_out_ref, lg_ref)


def _conv_out_kernel(gb_ref, gc_ref, u_ref, gcp_ref, up_ref, gcn_ref, un_ref, cw_ref, w_ref,
                     x_ref, gate_ref, sh_ref, sc_ref, rwt_ref, x_out_ref, h_out_ref, lg_ref):
    i = pl.program_id(0)
    tiles_per_seq = SEQ_LEN // OUT_TM
    t = gc_ref[...].astype(F32) * u_ref[...].astype(F32)
    first = (i % tiles_per_seq) == 0
    last = (i % tiles_per_seq) == tiles_per_seq - 1
    halo_last = slice(CONV_HALO - 1, CONV_HALO)
    t_before = jnp.where(first, 0.0, gcp_ref[halo_last, :].astype(F32) * up_ref[halo_last, :].astype(F32))
    t_after = jnp.where(last, 0.0, gcn_ref[0:1, :].astype(F32) * un_ref[0:1, :].astype(F32))
    row = lax.broadcasted_iota(I32, t.shape, 0)
    t_up = jnp.where(row == 0, t_before, pltpu.roll(t, 1, 0))
    t_dn = jnp.where(row == OUT_TM - 1, t_after, pltpu.roll(t, OUT_TM - 1, 0))
    y = cw_ref[0:1, :] * t_up + cw_ref[1:2, :] * t + cw_ref[2:3, :] * t_dn
    mix = _dot((gb_ref[...].astype(F32) * y).astype(BF16), w_ref[...])
    _residual_and_moe_input(mix, x_ref, gate_ref, sh_ref, sc_ref, rwt_ref, x_out_ref, h_out_ref, lg_ref)


def _mixer_out(kernel_fn, mixer_specs, mixer_args, x2d, gate, shift, scale, rwt, name):
    d = D_MODEL
    mod_spec = pl.BlockSpec((1, 1, d), lambda i: (i * OUT_TM // SEQ_LEN, 0, 0))
    row_spec = pl.BlockSpec((OUT_TM, d), lambda i: (i, 0))
    return pl.pallas_call(
        kernel_fn,
        out_shape=[jax.ShapeDtypeStruct((N_TOK, d), F32), jax.ShapeDtypeStruct((N_TOK, d), F32),
                   jax.ShapeDtypeStruct((N_TOK, ROUTE_LANES), F32)],
        grid=(N_TOK // OUT_TM,),
        in_specs=mixer_specs + [row_spec, mod_spec, mod_spec, mod_spec,
                                pl.BlockSpec((2, d, ROUTE_LANES), lambda i: (0, 0, 0))],
        out_specs=[row_spec, row_spec, pl.BlockSpec((OUT_TM, ROUTE_LANES), lambda i: (i, 0))],
        compiler_params=_cparams(("arbitrary",)),
        name=name,
    )(*mixer_args, x2d, gate, shift, scale, rwt)


def _attn_out(attn2d, gla2d, w_out, x2d, gate, shift, scale, rwt):
    half = pl.BlockSpec((OUT_TM, A_Q_DIM), lambda i: (i, 0))
    specs = [half, half, pl.BlockSpec((D_MODEL, D_MODEL), lambda i: (0, 0))]
    return _mixer_out(_attn_out_kernel, specs, [attn2d, gla2d, w_out], x2d, gate, shift, scale, rwt,
                      "attn_out_projection")


def _conv_out(g2d, conv_w, w_out, x2d, gate, shift, scale, rwt):
    d = D_MODEL
    sub = OUT_TM // CONV_HALO
    last_halo = N_TOK // CONV_HALO - 1
    main = lambda col: pl.BlockSpec((OUT_TM, d), lambda i: (i, col))
    before = lambda col: pl.BlockSpec((CONV_HALO, d), lambda i: (jnp.maximum(i * sub - 1, 0), col))
    after = lambda col: pl.BlockSpec((CONV_HALO, d), lambda i: (jnp.minimum((i + 1) * sub, last_halo), col))
    specs = [main(0), main(1), main(2), before(1), before(2), after(1), after(2),
             pl.BlockSpec((3, d), lambda i: (0, 0)), pl.BlockSpec((d, d), lambda i: (0, 0))]
    return _mixer_out(_conv_out_kernel, specs, [g2d] * 7 + [conv_w, w_out], x2d, gate, shift, scale, rwt,
                      "conv_out_projection")


def _first_argmax8(x, idx8):
    m = jnp.max(x, axis=0, keepdims=True)
    a = jnp.min(jnp.where(x == m, idx8, float(GROUP_SIZE)), axis=0, keepdims=True)
    return m, a


def _route_kernel(lg_ref, rb_ref, e_ref, w_ref, dest_ref, meta_ref, cnt_ref, carry_ref, pstart_ref):
    phase = pl.program_id(0)
    j = pl.program_id(1)
    tt = ROUTE_TT
    sc = jax.nn.sigmoid(lg_ref[...].T[:N_EXPERTS])
    grp = sc + rb_ref[...]
    idx8 = lax.broadcasted_iota(I32, (GROUP_SIZE, tt), 0).astype(F32)
    groups = [grp[g * GROUP_SIZE:(g + 1) * GROUP_SIZE] for g in range(N_GROUPS)]
    gscore = []
    for x in groups:
        m1, a1 = _first_argmax8(x, idx8)
        m2, _ = _first_argmax8(jnp.where(idx8 == a1, -jnp.inf, x), idx8)
        gscore.append(m1 + m2)
    gmax = functools.reduce(jnp.maximum, gscore)
    gsel = jnp.full((1, tt), float(N_GROUPS), F32)
    for g in reversed(range(N_GROUPS)):
        gsel = jnp.where(gscore[g] == gmax, float(g), gsel)
    in_grp = groups[0]
    for g in range(1, N_GROUPS):
        in_grp = jnp.where(gsel == float(g), groups[g], in_grp)
    _, a1 = _first_argmax8(in_grp, idx8)
    _, a2 = _first_argmax8(jnp.where(idx8 == a1, -jnp.inf, in_grp), idx8)
    e0 = gsel * GROUP_SIZE + a1
    e1 = gsel * GROUP_SIZE + a2
    idx32 = lax.broadcasted_iota(I32, (N_EXPERTS, tt), 0).astype(F32)
    oh0 = idx32 == e0
    oh1 = idx32 == e1
    w0 = jnp.sum(jnp.where(oh0, sc, 0.0), axis=0, keepdims=True)
    w1 = jnp.sum(jnp.where(oh1, sc, 0.0), axis=0, keepdims=True)
    wsum = w0 + w1
    ohs = jnp.where(oh0 | oh1, 1.0, 0.0)
    tile_cnt = jnp.sum(ohs, axis=1, keepdims=True)

    @pl.when(phase == 0)
    def _():
        @pl.when(j == 0)
        def _():
            cnt_ref[...] = jnp.zeros_like(cnt_ref)
        cnt_ref[...] += jnp.broadcast_to(tile_cnt, cnt_ref.shape)

    @pl.when(phase == 1)
    def _():
        @pl.when(j == 0)
        def _():
            cnt = cnt_ref[...]
            expert = lax.broadcasted_iota(I32, cnt.shape, 0)
            lane = lax.broadcasted_iota(I32, cnt.shape, 1).astype(F32)

            def exclusive_cumsum(v):
                inc = v
                for s in (1, 2, 4, 8, 16):
                    inc = inc + jnp.where(expert >= s, pltpu.roll(inc, s, 0), 0.0)
                return inc - v

            cnt_pad = jnp.floor((cnt + (MOE_ALIGN - 1)) * (1.0 / MOE_ALIGN)) * MOE_ALIGN
            pstart = exclusive_cumsum(cnt_pad)
            pstart_ref[...] = pstart
            carry_ref[...] = jnp.zeros_like(carry_ref)
            nvis_e = jnp.zeros_like(cnt)
            for kb in range(-(-N_TOK // MOE_BLOCK)):
                nvis_e += jnp.where(cnt > float(kb * MOE_BLOCK), 1.0, 0.0)
            vstart = exclusive_cumsum(nvis_e)
            vis_exp = jnp.sum(jnp.where(lane >= vstart + nvis_e, 1.0, 0.0), axis=0, keepdims=True)
            vis_exp = jnp.minimum(vis_exp, float(N_EXPERTS - 1))
            own = expert.astype(F32) == vis_exp
            of_visit = lambda col: jnp.sum(jnp.where(own, col, 0.0), axis=0, keepdims=True)
            kth = lane[0:1] - of_visit(vstart)
            vis_rows = jnp.clip(of_visit(cnt_pad) - kth * MOE_BLOCK, 0.0, float(MOE_BLOCK))
            vis_row0 = of_visit(pstart) + kth * MOE_BLOCK
            last_lane = lane[0:1] == float(META_LANES - 1)
            expert_lane = expert.astype(F32) == lane
            to_lane = lambda col: jnp.sum(jnp.where(expert_lane, col, 0.0), axis=0, keepdims=True)
            meta_ref[...] = jnp.concatenate([
                jnp.where(last_lane, jnp.sum(nvis_e, axis=0, keepdims=True), vis_exp),
                vis_rows,
                vis_row0,
                to_lane(pstart + cnt),
                jnp.where(last_lane, jnp.sum(cnt_pad, axis=0, keepdims=True), to_lane(cnt_pad - cnt)),
            ], axis=0).astype(I32)

        rr = lax.broadcasted_iota(I32, (tt, tt), 0)
        cc = lax.broadcasted_iota(I32, (tt, tt), 1)
        strict_upper = jnp.where(rr < cc, 1.0, 0.0).astype(BF16)
        before = _dot(ohs.astype(BF16), strict_upper)
        pos = before + carry_ref[:, 0:1] + pstart_ref[:, 0:1]
        d0 = jnp.sum(jnp.where(oh0, pos, 0.0), axis=0, keepdims=True)
        d1 = jnp.sum(jnp.where(oh1, pos, 0.0), axis=0, keepdims=True)
        carry_ref[...] += jnp.broadcast_to(tile_cnt, carry_ref.shape)
        e_ref[...] = jnp.concatenate([e0, e1], axis=0).astype(I32)
        w_ref[...] = jnp.concatenate([w0 / wsum, w1 / wsum], axis=0)
        dest_ref[...] = jnp.concatenate([d0, d1], axis=0).astype(I32)


def _route(logits_t, router_b):
    nt = N_TOK // ROUTE_TT
    tok = lambda dt: jax.ShapeDtypeStruct((2, N_TOK), dt)
    tok_spec = pl.BlockSpec((2, ROUTE_TT), lambda p, j: (0, j * p))
    return pl.pallas_call(
        _route_kernel,
        out_shape=[tok(I32), tok(F32), tok(I32), jax.ShapeDtypeStruct((META_ROWS, META_LANES), I32)],
        grid=(2, nt),
        in_specs=[pl.BlockSpec((ROUTE_TT, ROUTE_LANES), lambda p, j: (j, 0)),
                  pl.BlockSpec((N_EXPERTS, 1), lambda p, j: (0, 0))],
        out_specs=[tok_spec, tok_spec, tok_spec, pl.BlockSpec((META_ROWS, META_LANES), lambda p, j: (0, 0))],
        scratch_shapes=[pltpu.VMEM((N_EXPERTS, 128), F32), pltpu.VMEM((N_EXPERTS, 128), F32),
                        pltpu.VMEM((N_EXPERTS, 128), F32)],
        compiler_params=_cparams(("arbitrary", "arbitrary")),
        name="router_slot_assignment",
    )(logits_t, router_b.reshape(N_EXPERTS, 1))


def _dispatch_copy(dest_ref, h_ref, xs_ref, sem, i, r, k):
    dst = dest_ref[k * N_TOK + i * TOK_TM + r]
    return pltpu.make_async_copy(h_ref.at[pl.ds(r, 1)], xs_ref.at[pl.ds(dst, 1)], sem)


def _for_each_tail_piece(meta_ref, zero_ref, slots_ref, sem, fn):
    used = meta_ref[4 * META_LANES + META_LANES - 1]
    tail = MOE_SLOTS - used
    row = used
    for size in MOE_TAIL_PIECES:
        take = tail & size

        @pl.when(take != 0)
        def _():
            fn(pltpu.make_async_copy(zero_ref.at[pl.ds(0, size)],
                                     slots_ref.at[pl.ds(pl.multiple_of(row, MOE_ALIGN), size)], sem))
        row = row + take


def _for_each_padding_row(meta_ref, zero_ref, xs_ref, sem, fn):
    def expert_padding(e, carry):
        row = meta_ref[3 * META_LANES + e]
        npad = meta_ref[4 * META_LANES + e]
        for r in range(MOE_ALIGN - 1):
            @pl.when(r < npad)
            def _():
                fn(pltpu.make_async_copy(zero_ref.at[pl.ds(0, 1)], xs_ref.at[pl.ds(row + r, 1)], sem))
        return carry

    lax.fori_loop(0, N_EXPERTS, expert_padding, 0)


def _zero_fill_padding(meta_ref, zero_ref, xs_ref, sem):
    zero_ref[...] = jnp.zeros_like(zero_ref)
    for fn in (lambda copy: copy.start(), lambda copy: copy.wait()):
        _for_each_padding_row(meta_ref, zero_ref, xs_ref, sem, fn)
        _for_each_tail_piece(meta_ref, zero_ref, xs_ref, sem, fn)


def _dispatch_kernel(dest_ref, meta_ref, h_ref, xs_ref, zero_ref, sem):
    i = pl.program_id(0)

    @pl.when(i == 0)
    def _():
        _zero_fill_padding(meta_ref, zero_ref, xs_ref, sem)

    unroll = 8

    def start(g, carry):
        for u in range(unroll):
            for k in range(2):
                _dispatch_copy(dest_ref, h_ref, xs_ref, sem, i, g * unroll + u, k).start()
        return carry

    def wait(g, carry):
        for u in range(unroll):
            for k in range(2):
                _dispatch_copy(dest_ref, h_ref, xs_ref, sem, i, g * unroll + u, k).wait()
        return carry

    lax.fori_loop(0, TOK_TM // unroll, start, 0)
    lax.fori_loop(0, TOK_TM // unroll, wait, 0)


def _dispatch(dest_flat, meta_flat, h2d):
    d = D_MODEL
    return pl.pallas_call(
        _dispatch_kernel,
        out_shape=jax.ShapeDtypeStruct((MOE_SLOTS, d), F32),
        grid_spec=pltpu.PrefetchScalarGridSpec(
            num_scalar_prefetch=2,
            grid=(N_TOK // TOK_TM,),
            in_specs=[pl.BlockSpec((TOK_TM, d), lambda i, dest, meta: (i, 0))],
            out_specs=pl.BlockSpec(memory_space=pl.ANY),
            scratch_shapes=[pltpu.VMEM((MOE_TAIL_PIECES[0], d), F32), pltpu.SemaphoreType.DMA],
        ),
        compiler_params=_cparams(("arbitrary",)),
        name="moe_dispatch",
    )(dest_flat, meta_flat, h2d)


def _for_each_x_piece(meta_ref, xs_ref, xbuf_ref, sem, v, fn):
    rows = meta_ref[META_LANES + v]
    row0 = meta_ref[2 * META_LANES + v]
    slot = v % 2
    for p in range(MOE_BLOCK // MOE_ROWS):
        @pl.when(p * MOE_ROWS < rows)
        def _():
            src = xs_ref.at[pl.ds(pl.multiple_of(row0 + p * MOE_ROWS, MOE_ALIGN), MOE_ROWS)]
            fn(pltpu.make_async_copy(src, xbuf_ref.at[slot, pl.ds(p * MOE_ROWS, MOE_ROWS)], sem.at[slot]))


def _for_each_y_piece(meta_ref, ybuf_ref, yb_ref, sem, v, fn):
    rows = meta_ref[META_LANES + v]
    row0 = meta_ref[2 * META_LANES + v]

    def piece(off, size):
        dst = yb_ref.at[pl.ds(pl.multiple_of(row0 + off, MOE_ALIGN), size)]
        fn(pltpu.make_async_copy(ybuf_ref.at[pl.ds(pl.multiple_of(off, MOE_ALIGN), size)], dst, sem))

    for p in range(MOE_BLOCK // MOE_ROWS):
        @pl.when((p + 1) * MOE_ROWS <= rows)
        def _():
            piece(p * MOE_ROWS, MOE_ROWS)
    off = (rows // MOE_ROWS) * MOE_ROWS
    size = MOE_ROWS // 2
    while size >= MOE_ALIGN:
        take = (rows - off) & size

        @pl.when(take != 0)
        def _():
            piece(off, size)
        off = off + take
        size //= 2


def _expert_kernel(meta_ref, xs_ref, w1_ref, w3_ref, w2_ref, yb_ref, xbuf_ref, xb_ref, ybuf_ref, zero_ref,
                   xsem, ysem, zsem):
    v = pl.program_id(0)
    f = pl.program_id(1)
    nf = pl.num_programs(1)
    valid = meta_ref[META_LANES + v]
    start = lambda copy: copy.start()
    wait = lambda copy: copy.wait()

    @pl.when(f == 0)
    def _():
        @pl.when(v == 0)
        def _():
            zero_ref[...] = jnp.zeros_like(zero_ref)
            _for_each_tail_piece(meta_ref, zero_ref, yb_ref, zsem, start)
            _for_each_tail_piece(meta_ref, zero_ref, yb_ref, zsem, wait)
            _for_each_x_piece(meta_ref, xs_ref, xbuf_ref, xsem, v, start)

        _for_each_x_piece(meta_ref, xs_ref, xbuf_ref, xsem, v, wait)

        @pl.when(v + 1 < MOE_NVIS)
        def _():
            _for_each_x_piece(meta_ref, xs_ref, xbuf_ref, xsem, v + 1, start)

        for p in range(MOE_BLOCK // MOE_ROWS):
            @pl.when(p * MOE_ROWS < valid)
            def _():
                piece = slice(p * MOE_ROWS, (p + 1) * MOE_ROWS)
                xb_ref[piece, :] = xbuf_ref[v % 2, piece, :].astype(BF16)

    def previous_writeback_done():
        @pl.when(v > 0)
        def _():
            _for_each_y_piece(meta_ref, ybuf_ref, yb_ref, ysem, v - 1, wait)

    @pl.when((f == 0) & (valid == 0))
    def _():
        previous_writeback_done()

    def compute(nrows):
        x = xb_ref[:nrows, :]
        a = _silu(_dot(x, w1_ref[0, 0].astype(BF16))) * _dot(x, w3_ref[0, 0].astype(BF16))
        y = _dot(a.astype(BF16), w2_ref[0, 0].astype(BF16))

        @pl.when(f == 0)
        def _():
            previous_writeback_done()
            ybuf_ref[:nrows, :] = y

        @pl.when(f > 0)
        def _():
            ybuf_ref[:nrows, :] = y + ybuf_ref[:nrows, :]

    for nrows in range(MOE_ROWS, MOE_BLOCK + 1, MOE_ROWS):
        @pl.when((valid > nrows - MOE_ROWS) & (valid <= nrows))
        def _():
            compute(nrows)

    @pl.when(f == nf - 1)
    def _():
        _for_each_y_piece(meta_ref, ybuf_ref, yb_ref, ysem, v, start)

        @pl.when(v == MOE_NVIS - 1)
        def _():
            _for_each_y_piece(meta_ref, ybuf_ref, yb_ref, ysem, v, wait)


def _experts(meta, xs, w1, w3, w2, layer):
    d = D_MODEL
    nf = D_EXPERT // MOE_FC
    n_vis = lambda meta: meta[META_LANES - 1]
    used = lambda v, meta: jnp.minimum(v, n_vis(meta) - 1)
    chunk = lambda v, f, meta: jnp.where(v < n_vis(meta), f, nf - 1)
    w_in_spec = pl.BlockSpec((1, 1, d, MOE_FC), lambda v, f, meta: (layer, meta[used(v, meta)], 0, chunk(v, f, meta)))
    w_out_spec = pl.BlockSpec((1, 1, MOE_FC, d), lambda v, f, meta: (layer, meta[used(v, meta)], chunk(v, f, meta), 0))
    return pl.pallas_call(
        _expert_kernel,
        out_shape=jax.ShapeDtypeStruct((MOE_SLOTS, d), F32),
        grid_spec=pltpu.PrefetchScalarGridSpec(
            num_scalar_prefetch=1,
            grid=(MOE_NVIS, nf),
            in_specs=[pl.BlockSpec(memory_space=pl.ANY), w_in_spec, w_in_spec, w_out_spec],
            out_specs=pl.BlockSpec(memory_space=pl.ANY),
            scratch_shapes=[pltpu.VMEM((2, MOE_BLOCK, d), F32), pltpu.VMEM((MOE_BLOCK, d), BF16),
                            pltpu.VMEM((MOE_BLOCK, d), F32), pltpu.VMEM((MOE_TAIL_PIECES[0], d), F32),
                            pltpu.SemaphoreType.DMA((2,)), pltpu.SemaphoreType.DMA, pltpu.SemaphoreType.DMA],
        ),
        compiler_params=_cparams(("arbitrary", "arbitrary")),
        name="moe_experts",
    )(meta, xs, w1, w3, w2)


def _combine_copy(dest_ref, yb_ref, buf_ref, sem, i, r, k):
    src = dest_ref[k * N_TOK + i * TOK_TM + r]
    return pltpu.make_async_copy(yb_ref.at[pl.ds(src, 1)], buf_ref.at[k, pl.ds(r, 1)], sem)


def _combine_kernel(dest_ref, yb_ref, x_ref, gate_ref, w_ref, o_ref, buf_ref, sem):
    i = pl.program_id(0)

    unroll = 8

    def start(g, carry):
        for u in range(unroll):
            for k in range(2):
                _combine_copy(dest_ref, yb_ref, buf_ref, sem, i, g * unroll + u, k).start()
        return carry

    def wait(g, carry):
        for u in range(unroll):
            for k in range(2):
                _combine_copy(dest_ref, yb_ref, buf_ref, sem, i, g * unroll + u, k).wait()
        return carry

    lax.fori_loop(0, TOK_TM // unroll, start, 0)
    lax.fori_loop(0, TOK_TM // unroll, wait, 0)
    w = w_ref[...]
    y = w[:, 0:1] * buf_ref[0] + w[:, 1:2] * buf_ref[1]
    o_ref[...] = x_ref[...] + gate_ref[0] * y


def _combine(dest_flat, yb, x2d, gate, w_tok):
    d = D_MODEL
    return pl.pallas_call(
        _combine_kernel,
        out_shape=jax.ShapeDtypeStruct((N_TOK, d), F32),
        grid_spec=pltpu.PrefetchScalarGridSpec(
            num_scalar_prefetch=1,
            grid=(N_TOK // TOK_TM,),
            in_specs=[pl.BlockSpec(memory_space=pl.ANY),
                      pl.BlockSpec((TOK_TM, d), lambda i, dest: (i, 0)),
                      pl.BlockSpec((1, 1, d), lambda i, dest: (i * TOK_TM // SEQ_LEN, 0, 0)),
                      pl.BlockSpec((TOK_TM, 2), lambda i, dest: (i, 0))],
            out_specs=pl.BlockSpec((TOK_TM, d), lambda i, dest: (i, 0)),
            scratch_shapes=[pltpu.VMEM((2, TOK_TM, d), F32), pltpu.SemaphoreType.DMA],
        ),
        compiler_params=_cparams(("arbitrary",)),
        name="moe_combine",
    )(dest_flat, yb, x2d, gate, w_tok)


def _moe(h2d, logits_t, x2d, gate, router_b, w1, w3, w2, layer):
    _, w_sel, dest, meta = _route(logits_t, router_b)
    dest_flat = dest.reshape(2 * N_TOK)
    meta_flat = meta.reshape(META_ROWS * META_LANES)
    xs = _dispatch(dest_flat, meta_flat, h2d)
    yb = _experts(meta_flat, xs, w1, w3, w2, layer)
    return _combine(dest_flat, yb, x2d, gate, w_sel.T)


def _rope_tables():
    f = HEAD_DIM // 4
    inv_freq = ROPE_THETA ** (-jnp.arange(f, dtype=F32) / f)
    pos = jnp.arange(SEQ_LEN)
    row = (pos // GRID_W).astype(F32)[:, None] * inv_freq
    col = (pos % GRID_W).astype(F32)[:, None] * inv_freq
    cos = jnp.concatenate([jnp.cos(row), jnp.cos(row), jnp.cos(col), jnp.cos(col)], axis=-1)
    sin = jnp.concatenate([-jnp.sin(row), jnp.sin(row), -jnp.sin(col), jnp.sin(col)], axis=-1)
    return cos, sin


def _permute_attn_w_in(w):
    o = [0, A_Q_DIM, A_KV_DIM, A_KV_DIM, GLA_K_DIM, GLA_K_DIM, GLA_V_DIM, GLA_V_DIM]
    s = [sum(o[:i + 1]) for i in range(len(o))]
    seg = lambda i: w[:, s[i]:s[i + 1]] if i + 1 < len(s) else w[:, s[i]:]
    aq, ak, av, gq, gk, gv, gg = (seg(i) for i in range(7))
    main = jnp.concatenate([aq, gv, gg, gq, gk, ak, av], axis=1).astype(BF16)
    lr = jnp.pad(w[:, PROJ_DIM:], ((0, 0), (0, LR_PAD - 2 * GLA_RANK))).astype(BF16)
    return main, lr


def kernel(x, c, ctx, c_ctx, mod_w, mod_b, attn_w_in, attn_q_norm, attn_k_norm, attn_sink, gla_wa2, gla_ba,
           gla_norm, attn_w_out, conv_w_in, conv_w, conv_w_out, router_w, router_b, exp_w1, exp_w3, exp_w2):
    d = D_MODEL
    cc = jnp.concatenate([c, c_ctx[None], jnp.zeros((8 - N_BATCH - 1, d), F32)], axis=0)
    m = _modulation(cc, mod_w, mod_b)
    mods = [[m[l, :, i * d:(i + 1) * d].reshape(8, 1, d) for i in range(6)] for l in range(mod_w.shape[0])]
    lat_row = lambda i: i * PROJ_TM // SEQ_LEN
    ctx_row = lambda i: N_BATCH
    rw_pad = jnp.pad(router_w, ((0, 0), (0, ROUTE_LANES - N_EXPERTS)))
    rw_hi = rw_pad.astype(BF16)
    rwt = jnp.stack([rw_hi, (rw_pad - rw_hi.astype(F32)).astype(BF16)])
    x2d = x.reshape(N_TOK, d)

    w_main, w_lr = _permute_attn_w_in(attn_w_in[0])
    p_lat, lr_lat = _projection(x2d, mods[0][0], mods[0][1], w_main, w_lr, lat_row)
    p_ctx, lr_ctx = _projection(ctx.reshape(N_BATCH * CTX_LEN, d), mods[0][0], mods[0][1], w_main, w_lr, ctx_row)
    p_lat = p_lat.reshape(N_BATCH, SEQ_LEN, PROJ_DIM)
    p_ctx = p_ctx.reshape(N_BATCH, CTX_LEN, PROJ_DIM)
    cos, sin_signed = _rope_tables()
    attn = _attention(p_lat, p_ctx, cos, sin_signed, attn_q_norm[0][None], attn_k_norm[0][None], attn_sink[0])
    wa_pad = jnp.zeros((2, LR_PAD, GLA_K_DIM), F32)
    for di in range(2):
        wa_pad = wa_pad.at[di, di * GLA_RANK:(di + 1) * GLA_RANK].set(gla_wa2[0, di])
    wa_pad = wa_pad.astype(BF16)
    ba = gla_ba[0].reshape(2, 1, GLA_K_DIM)
    s_zero = jnp.zeros((N_BATCH, 2, GLA_HEADS, GLA_DV, GLA_DK), F32)
    s_ctx = _gla(p_ctx, lr_ctx.reshape(N_BATCH, CTX_LEN, LR_PAD), wa_pad, ba, None, s_zero, latent=False)
    gla = _gla(p_lat, lr_lat.reshape(N_BATCH, SEQ_LEN, LR_PAD), wa_pad, ba, gla_norm[0][None], s_ctx, latent=True)
    x1, h2, lg = _attn_out(attn.reshape(N_TOK, A_Q_DIM), gla.reshape(N_TOK, GLA_V_DIM), attn_w_out[0].astype(BF16),
                           x2d, mods[0][2], mods[0][3], mods[0][4], rwt)
    x2 = _moe(h2, lg, x1, mods[0][5], router_b, exp_w1, exp_w3, exp_w2, 0)

    g = _projection(x2, mods[1][0], mods[1][1], conv_w_in[0].astype(BF16), None, lat_row)
    x3, h2, lg = _conv_out(g, conv_w[0], conv_w_out[0].astype(BF16), x2, mods[1][2], mods[1][3], mods[1][4], rwt)
    x4 = _moe(h2, lg, x3, mods[1][5], router_b, exp_w1, exp_w3, exp_w2, 1)
    return x4.reshape(N_BATCH, SEQ_LEN, d)
```

```python
import functools

import jax
import jax.numpy as jnp
from jax import lax
from jax.experimental import pallas as pl
from jax.experimental.pallas import tpu as pltpu

F32 = jnp.float32
BF16 = jnp.bfloat16
I32 = jnp.int32

D_MODEL = 2048
N_BATCH = 4
SEQ_LEN = 2048
CTX_LEN = 256
N_TOK = N_BATCH * SEQ_LEN
GRID_W = 64
HEAD_DIM = 128
A_Q_HEADS = 8
A_KV_HEADS = 2
A_GROUP = A_Q_HEADS // A_KV_HEADS
WINDOW = 128
A_BLOCK = 128
ROPE_THETA = 10000.0
GLA_HEADS = 4
GLA_DK = 128
GLA_DV = 256
GLA_RANK = 16
GLA_TAU = 16.0
GLA_CHUNK = 64
N_EXPERTS = 32
N_GROUPS = 4
GROUP_SIZE = N_EXPERTS // N_GROUPS
D_EXPERT = 1024
EPS = 1e-6
NEG_INF = -1e30
A_Q_DIM = A_Q_HEADS * HEAD_DIM
A_KV_DIM = A_KV_HEADS * HEAD_DIM
GLA_K_DIM = GLA_HEADS * GLA_DK
GLA_V_DIM = GLA_HEADS * GLA_DV
PROJ_DIM = A_Q_DIM + 2 * A_KV_DIM + 2 * GLA_K_DIM + 2 * GLA_V_DIM
LR_PAD = 128

COL_AQ = 0
COL_GV = 1
COL_GG = 2
COL_GQ = 6
COL_GK = 7
COL_AK = 32
COL_AV = 34

MOD_TN = 1024
PROJ_TM = 512
PROJ_TN = 512
OUT_TM = 256
CONV_HALO = 16
GLA_ROWS = 512
ROUTE_TT = 512
ROUTE_LANES = 128
MOE_ALIGN = 8
MOE_BLOCK = 768
MOE_ROWS = 128
MOE_FC = 256
MOE_NVIS = N_EXPERTS + (2 * N_TOK) // MOE_BLOCK
MOE_TAIL_PIECES = (256, 128, 64, 32, 16, 8)
MOE_SLOTS = 2 * N_TOK + sum(MOE_TAIL_PIECES)
assert MOE_SLOTS >= 2 * N_TOK + N_EXPERTS * (MOE_ALIGN - 1) + MOE_ROWS
TOK_TM = 256
ROW_GROUP = 8
META_ROWS = 5
META_LANES = 128
VMEM_LIMIT = 56 * 1024 * 1024


def _cparams(sem):
    return pltpu.CompilerParams(dimension_semantics=sem, vmem_limit_bytes=VMEM_LIMIT)


def _silu(x):
    return x * jax.nn.sigmoid(x)


def _dot(a, b):
    return jnp.dot(a, b, preferred_element_type=F32)


def _dot_nt(a, b):
    return lax.dot_general(a, b, (((1,), (1,)), ((), ())), preferred_element_type=F32)


def _dot_tn(a, b):
    return lax.dot_general(a, b, (((0,), (0,)), ((), ())), preferred_element_type=F32)


def _rms_modulate(x, shift, scale):
    r = lax.rsqrt(jnp.mean(x * x, axis=-1, keepdims=True) + EPS)
    return (x * r) * (1.0 + scale) + shift


def _mod_kernel(cc_ref, w_ref, b_ref, o_ref):
    a = _silu(cc_ref[...])
    o_ref[0] = _dot(a.astype(BF16), w_ref[0].astype(BF16)) + b_ref[0]


def _modulation(cc, mod_w, mod_b):
    depth, d, n = mod_w.shape
    return pl.pallas_call(
        _mod_kernel,
        out_shape=jax.ShapeDtypeStruct((depth, 8, n), F32),
        grid=(depth, n // MOD_TN),
        in_specs=[
            pl.BlockSpec((8, d), lambda l, j: (0, 0)),
            pl.BlockSpec((1, d, MOD_TN), lambda l, j: (l, 0, j)),
            pl.BlockSpec((1, 1, MOD_TN), lambda l, j: (l, 0, j)),
        ],
        out_specs=pl.BlockSpec((1, 8, MOD_TN), lambda l, j: (l, 0, j)),
        compiler_params=_cparams(("arbitrary", "arbitrary")),
        name="adaln_modulation",
    )(cc, mod_w, mod_b.reshape(depth, 1, n))


def _proj_kernel(with_lr, x_ref, sh_ref, sc_ref, w_ref, *rest):
    if with_lr:
        wlr_ref, o_ref, olr_ref = rest
    else:
        (o_ref,) = rest
    hb = _rms_modulate(x_ref[...], sh_ref[0], sc_ref[0]).astype(BF16)
    for c in range(w_ref.shape[1] // PROJ_TN):
        cols = slice(c * PROJ_TN, (c + 1) * PROJ_TN)
        o_ref[:, cols] = _dot(hb, w_ref[:, cols]).astype(BF16)
    if with_lr:
        olr_ref[...] = _dot(hb, wlr_ref[...]).astype(BF16)


def _projection(x2d, shift, scale, w, w_lr, mod_row):
    rows, d = x2d.shape
    n = w.shape[1]
    with_lr = w_lr is not None
    resident = lambda shape: pl.BlockSpec(shape, lambda i: (0, 0), pipeline_mode=pl.Buffered(1))
    in_specs = [
        pl.BlockSpec((PROJ_TM, d), lambda i: (i, 0)),
        pl.BlockSpec((1, 1, d), lambda i: (mod_row(i), 0, 0)),
        pl.BlockSpec((1, 1, d), lambda i: (mod_row(i), 0, 0)),
        resident((d, n)),
    ]
    out_shape = [jax.ShapeDtypeStruct((rows, n), BF16)]
    out_specs = [pl.BlockSpec((PROJ_TM, n), lambda i: (i, 0))]
    args = [x2d, shift, scale, w]
    if with_lr:
        in_specs.append(resident((d, LR_PAD)))
        out_shape.append(jax.ShapeDtypeStruct((rows, LR_PAD), BF16))
        out_specs.append(pl.BlockSpec((PROJ_TM, LR_PAD), lambda i: (i, 0)))
        args.append(w_lr)
    res = pl.pallas_call(
        functools.partial(_proj_kernel, with_lr),
        out_shape=out_shape,
        grid=(rows // PROJ_TM,),
        in_specs=in_specs,
        out_specs=out_specs,
        compiler_params=_cparams(("arbitrary",)),
        name="norm_mod_projection",
    )(*args)
    return res if with_lr else res[0]


def _swap_halves32(x):
    lane = lax.broadcasted_iota(I32, x.shape, 1)
    return jnp.where((lane & 63) < 32, pltpu.roll(x, 96, 1), pltpu.roll(x, 32, 1))


def _qk_norm(x, gain):
    return x * lax.rsqrt(jnp.mean(x * x, axis=-1, keepdims=True) + EPS) * gain


def _rope(x, cos, sin_signed):
    return x * cos + _swap_halves32(x) * sin_signed


ATT_WIN0 = CTX_LEN
ATT_LAT0 = CTX_LEN + A_BLOCK
ATT_ROWS = CTX_LEN + SEQ_LEN + 2 * A_BLOCK
ATT_WIN = 3 * A_BLOCK


def _attn_kernel(sink_ref, q_ref, k_ref, v_ref, kx_ref, vx_ref, cos_ref, sin_ref, qg_ref, kg_ref, o_ref,
                 kn_ref, vn_ref, band_ref):
    kvh = pl.program_id(1)
    nb = SEQ_LEN // A_BLOCK
    k_gain = kg_ref[...]
    q_gain = qg_ref[...]

    pad = jnp.zeros((A_BLOCK, HEAD_DIM), BF16)
    kn_ref[0:CTX_LEN, :] = _qk_norm(kx_ref[0].astype(F32), k_gain).astype(BF16)
    vn_ref[0:CTX_LEN, :] = vx_ref[0]
    for ref in (kn_ref, vn_ref):
        ref[ATT_WIN0:ATT_LAT0, :] = pad
        ref[ATT_LAT0 + SEQ_LEN:, :] = pad
    vn_ref[ATT_LAT0:ATT_LAT0 + SEQ_LEN, :] = v_ref[0]
    prep_rows = 4 * A_BLOCK
    for c in range(SEQ_LEN // prep_rows):
        r = slice(c * prep_rows, (c + 1) * prep_rows)
        kc = _rope(_qk_norm(k_ref[0, r, :].astype(F32), k_gain), cos_ref[r, :], sin_ref[r, :])
        kn_ref[ATT_LAT0 + c * prep_rows:ATT_LAT0 + (c + 1) * prep_rows, :] = kc.astype(BF16)

    qi = lax.broadcasted_iota(I32, (A_BLOCK, ATT_WIN), 0)
    wj = lax.broadcasted_iota(I32, (A_BLOCK, ATT_WIN), 1)
    band = jnp.where(jnp.abs(wj - A_BLOCK - qi) <= WINDOW, 1.0, 0.0)
    band_ref[0] = jnp.where(wj >= A_BLOCK, band, 0.0)
    band_ref[1] = band
    band_ref[2] = jnp.where(wj < 2 * A_BLOCK, band, 0.0)

    rows = A_GROUP * A_BLOCK
    rcol = lax.broadcasted_iota(I32, (rows, 1), 0)
    sink = jnp.zeros((rows, 1), F32)
    for g in range(A_GROUP):
        sink = jnp.where((rcol >= g * A_BLOCK) & (rcol < (g + 1) * A_BLOCK), sink_ref[kvh * A_GROUP + g], sink)

    def query_block(n):
        qrows = pl.ds(pl.multiple_of(n * A_BLOCK, A_BLOCK), A_BLOCK)
        wrows = pl.ds(pl.multiple_of(ATT_WIN0 + n * A_BLOCK, A_BLOCK), ATT_WIN)
        cos_q, sin_q = cos_ref[qrows, :], sin_ref[qrows, :]
        q = q_ref[0, qrows, :].astype(F32)
        qs = []
        for g in range(A_GROUP):
            qh = _rope(_qk_norm(q[:, g * HEAD_DIM:(g + 1) * HEAD_DIM], q_gain), cos_q, sin_q)
            qs.append((qh * HEAD_DIM ** -0.5).astype(BF16))
        qs = jnp.concatenate(qs, axis=0)
        mask = band_ref[jnp.where(n == 0, 0, jnp.where(n == nb - 1, 2, 1))]
        mask = jnp.concatenate([mask] * A_GROUP, axis=0)
        s_ctx = _dot_nt(qs, kn_ref[0:CTX_LEN, :])
        s_win = jnp.where(mask > 0.5, _dot_nt(qs, kn_ref[wrows, :]), NEG_INF)
        m = jnp.maximum(jnp.max(s_ctx, axis=-1, keepdims=True), jnp.max(s_win, axis=-1, keepdims=True))
        m = jnp.maximum(m, sink)
        p_ctx = jnp.exp(s_ctx - m)
        p_win = jnp.exp(s_win - m)
        denom = (jnp.sum(p_ctx, axis=-1, keepdims=True) + jnp.sum(p_win, axis=-1, keepdims=True)
                 + jnp.exp(sink - m))
        o = (_dot(p_ctx.astype(BF16), vn_ref[0:CTX_LEN, :]) + _dot(p_win.astype(BF16), vn_ref[wrows, :])) / denom
        for g in range(A_GROUP):
            o_ref[0, qrows, g * HEAD_DIM:(g + 1) * HEAD_DIM] = o[g * A_BLOCK:(g + 1) * A_BLOCK].astype(BF16)

    def query_block_pair(i, carry):
        query_block(2 * i)
        query_block(2 * i + 1)
        return carry

    lax.fori_loop(0, nb // 2, query_block_pair, 0)


def _attention(p_lat, p_ctx, cos, sin_signed, q_gain, k_gain, sink):
    gw = A_GROUP * HEAD_DIM
    lat = lambda width, col: pl.BlockSpec((1, SEQ_LEN, width), lambda b, k, s: (b, 0, col + k))
    ctx_blk = lambda col: pl.BlockSpec((1, CTX_LEN, HEAD_DIM), lambda b, k, s: (b, 0, col + k))
    full = lambda shape: pl.BlockSpec(shape, lambda b, k, s: (0,) * len(shape))
    return pl.pallas_call(
        _attn_kernel,
        out_shape=jax.ShapeDtypeStruct((N_BATCH, SEQ_LEN, A_Q_DIM), BF16),
        grid_spec=pltpu.PrefetchScalarGridSpec(
            num_scalar_prefetch=1,
            grid=(N_BATCH, A_KV_HEADS),
            in_specs=[
                lat(gw, COL_AQ), lat(HEAD_DIM, COL_AK), lat(HEAD_DIM, COL_AV),
                ctx_blk(COL_AK), ctx_blk(COL_AV),
                full((SEQ_LEN, HEAD_DIM)), full((SEQ_LEN, HEAD_DIM)),
                full((1, HEAD_DIM)), full((1, HEAD_DIM)),
            ],
            out_specs=pl.BlockSpec((1, SEQ_LEN, gw), lambda b, k, s: (b, 0, k)),
            scratch_shapes=[pltpu.VMEM((ATT_ROWS, HEAD_DIM), BF16), pltpu.VMEM((ATT_ROWS, HEAD_DIM), BF16),
                            pltpu.VMEM((3, A_BLOCK, ATT_WIN), F32)],
        ),
        compiler_params=_cparams(("arbitrary", "arbitrary")),
        name="windowed_sink_attention",
    )(sink, p_lat, p_lat, p_lat, p_ctx, p_ctx, cos, sin_signed, q_gain, k_gain)


def _split3_bf16(x):
    hi = x.astype(BF16)
    r1 = x - hi.astype(F32)
    mid = r1.astype(BF16)
    lo = (r1 - mid.astype(F32)).astype(BF16)
    return hi, mid, lo


def _gla_kernel(latent, nblk, q_ref, k_ref, v_ref, *rest):
    if latent:
        g_ref, lr_ref, wa_ref, ba_ref, gain_ref, s0_ref, o_ref, st_ref, ofwd_ref = rest
    else:
        lr_ref, wa_ref, ba_ref, s0_ref, o_ref, st_ref = rest
    d = pl.program_id(1)
    j = pl.program_id(2)
    blk_rows = k_ref.shape[1]
    nc = blk_rows // GLA_CHUNK
    blk = jnp.where(d == 0, j, nblk - 1 - j)

    @pl.when(j == 0)
    def _():
        st_ref[...] = s0_ref[0, 0]

    ii = lax.broadcasted_iota(I32, (GLA_CHUNK, GLA_CHUNK), 0)
    jj = lax.broadcasted_iota(I32, (GLA_CHUNK, GLA_CHUNK), 1)

    def scan_block(backward):
        incl = (jj >= ii) if backward else (jj <= ii)
        incl_b = jnp.where(incl, 1.0, 0.0).astype(BF16)
        for i in range(nc):
            r0 = (nc - 1 - i if backward else i) * GLA_CHUNK
            rows = slice(r0, r0 + GLA_CHUNK)
            z = _dot(lr_ref[0, rows, :], wa_ref[0]) + ba_ref[0]
            la = (jnp.minimum(z, 0.0) - jnp.log(1.0 + jnp.exp(-jnp.abs(z)))) / GLA_TAU
            hi, mid, lo = _split3_bf16(la)
            bcum = _dot(incl_b, hi) + _dot(incl_b, mid) + _dot(incl_b, lo)
            blast = jnp.sum(la, axis=0, keepdims=True)
            k = k_ref[0, rows, :].astype(F32)
            kl = (k * jnp.exp(blast - bcum)).astype(BF16)
            decay = jnp.exp(blast)
            v = v_ref[0, rows, :]
            if latent:
                qf = (q_ref[0, rows, :].astype(F32) * GLA_DK ** -0.5 * jnp.exp(bcum)).astype(BF16)
                kf = (k * jnp.exp(-bcum)).astype(BF16)
            for h in range(GLA_HEADS):
                ks = slice(h * GLA_DK, (h + 1) * GLA_DK)
                vs = slice(h * GLA_DV, (h + 1) * GLA_DV)
                st = st_ref[h]
                if latent:
                    att = jnp.where(incl, _dot_nt(qf[:, ks], kf[:, ks]), 0.0)
                    o = _dot(att.astype(BF16), v[:, vs]) + _dot_nt(qf[:, ks], st.astype(BF16))
                    orow = pl.ds(pl.multiple_of(blk * blk_rows + r0, GLA_CHUNK), GLA_CHUNK)
                    if backward:
                        ot = o + ofwd_ref[orow, vs]
                        on = ot * lax.rsqrt(jnp.mean(ot * ot, axis=-1, keepdims=True) + EPS) * gain_ref[...]
                        o_ref[0, rows, vs] = (on * _silu(g_ref[0, rows, vs].astype(F32))).astype(BF16)
                    else:
                        ofwd_ref[orow, vs] = o
                st_ref[h] = st * decay[:, ks] + _dot_tn(v[:, vs], kl[:, ks])

    for backward in (False, True):
        @pl.when(d == int(backward))
        def _():
            scan_block(backward)

    if not latent:
        @pl.when(j == nblk - 1)
        def _():
            o_ref[0, 0] = st_ref[...]


def _gla(p3, lr3, wa_pad, ba, o_gain, s0, latent):
    n = p3.shape[1]
    blk_rows = min(GLA_ROWS, n)
    nblk = n // blk_rows
    seq_blk = lambda d, j: jnp.where(d == 0, j, nblk - 1 - j)
    in_specs = [
        pl.BlockSpec((1, blk_rows, GLA_K_DIM), lambda b, d, j: (b, seq_blk(d, j), COL_GQ)),
        pl.BlockSpec((1, blk_rows, GLA_K_DIM), lambda b, d, j: (b, seq_blk(d, j), COL_GK)),
        pl.BlockSpec((1, blk_rows, GLA_V_DIM), lambda b, d, j: (b, seq_blk(d, j), COL_GV)),
    ]
    args = [p3, p3, p3]
    if latent:
        in_specs.append(pl.BlockSpec((1, blk_rows, GLA_V_DIM), lambda b, d, j: (b, seq_blk(d, j), COL_GG)))
        args.append(p3)
    in_specs += [
        pl.BlockSpec((1, blk_rows, LR_PAD), lambda b, d, j: (b, seq_blk(d, j), 0)),
        pl.BlockSpec((1, LR_PAD, GLA_K_DIM), lambda b, d, j: (d, 0, 0)),
        pl.BlockSpec((1, 1, GLA_K_DIM), lambda b, d, j: (d, 0, 0)),
    ]
    args += [lr3, wa_pad, ba]
    if latent:
        in_specs.append(pl.BlockSpec((1, GLA_DV), lambda b, d, j: (0, 0)))
        args.append(o_gain)
    state_spec = pl.BlockSpec((1, 1, GLA_HEADS, GLA_DV, GLA_DK), lambda b, d, j: (b, d, 0, 0, 0))
    in_specs.append(state_spec)
    args.append(s0)
    scratch = [pltpu.VMEM((GLA_HEADS, GLA_DV, GLA_DK), F32)]
    if latent:
        out_shape = jax.ShapeDtypeStruct((N_BATCH, n, GLA_V_DIM), BF16)
        out_spec = pl.BlockSpec((1, blk_rows, GLA_V_DIM),
                                lambda b, d, j: (b, jnp.where(d == 0, nblk - 1, nblk - 1 - j), 0))
        scratch.append(pltpu.VMEM((n, GLA_V_DIM), F32))
    else:
        out_shape = jax.ShapeDtypeStruct(s0.shape, F32)
        out_spec = state_spec
    return pl.pallas_call(
        functools.partial(_gla_kernel, latent, nblk),
        out_shape=out_shape,
        grid=(N_BATCH, 2, nblk),
        in_specs=in_specs,
        out_specs=out_spec,
        scratch_shapes=scratch,
        compiler_params=_cparams(("arbitrary", "arbitrary", "arbitrary")),
        name="gla_latent" if latent else "gla_context_state",
    )(*args)


def _router_logits(h, rw_ref):
    hh = h.astype(BF16)
    hl = (h - hh.astype(F32)).astype(BF16)
    return _dot(hh, rw_ref[0]) + _dot(hl, rw_ref[0]) + _dot(hh, rw_ref[1])


def _residual_and_moe_input(mix, x_ref, gate_ref, sh_ref, sc_ref, rwt_ref, x_out_ref, h_out_ref, lg_ref):
    x1 = x_ref[...] + gate_ref[0] * mix
    x_out_ref[...] = x1
    h = _rms_modulate(x1, sh_ref[0], sc_ref[0])
    h_out_ref[...] = h
    lg_ref[...] = _router_logits(h, rwt_ref)


def _attn_out_kernel(a1_ref, a2_ref, w_ref, x_ref, gate_ref, sh_ref, sc_ref, rwt_ref, x_out_ref, h_out_ref, lg_ref):
    mix = _dot(a1_ref[...], w_ref[0:A_Q_DIM, :]) + _dot(a2_ref[...], w_ref[A_Q_DIM:, :])
    _residual_and_moe_input(mix, x_ref, gate_ref, sh_ref, sc_ref, rwt_ref, x_out_ref, h_out_ref, lg_ref)


def _conv_out_kernel(gb_ref, gc_ref, u_ref, gcp_ref, up_ref, gcn_ref, un_ref, cw_ref, w_ref,
                     x_ref, gate_ref, sh_ref, sc_ref, rwt_ref, x_out_ref, h_out_ref, lg_ref):
    i = pl.program_id(0)
    tiles_per_seq = SEQ_LEN // OUT_TM
    t = gc_ref[...].astype(F32) * u_ref[...].astype(F32)
    first = (i % tiles_per_seq) == 0
    last = (i % tiles_per_seq) == tiles_per_seq - 1
    halo_last = slice(CONV_HALO - 1, CONV_HALO)
    t_before = jnp.where(first, 0.0, gcp_ref[halo_last, :].astype(F32) * up_ref[halo_last, :].astype(F32))
    t_after = jnp.where(last, 0.0, gcn_ref[0:1, :].astype(F32) * un_ref[0:1, :].astype(F32))
    row = lax.broadcasted_iota(I32, t.shape, 0)
    t_up = jnp.where(row == 0, t_before, pltpu.roll(t, 1, 0))
    t_dn = jnp.where(row == OUT_TM - 1, t_after, pltpu.roll(t, OUT_TM - 1, 0))
    y = cw_ref[0:1, :] * t_up + cw_ref[1:2, :] * t + cw_ref[2:3, :] * t_dn
    mix = _dot((gb_ref[...].astype(F32) * y).astype(BF16), w_ref[...])
    _residual_and_moe_input(mix, x_ref, gate_ref, sh_ref, sc_ref, rwt_ref, x_out_ref, h_out_ref, lg_ref)


def _mixer_out(kernel_fn, mixer_specs, mixer_args, x2d, gate, shift, scale, rwt, name):
    d = D_MODEL
    mod_spec = pl.BlockSpec((1, 1, d), lambda i: (i * OUT_TM // SEQ_LEN, 0, 0))
    row_spec = pl.BlockSpec((OUT_TM, d), lambda i: (i, 0))
    return pl.pallas_call(
        kernel_fn,
        out_shape=[jax.ShapeDtypeStruct((N_TOK, d), F32), jax.ShapeDtypeStruct((N_TOK, d), F32),
                   jax.ShapeDtypeStruct((N_TOK, ROUTE_LANES), F32)],
        grid=(N_TOK // OUT_TM,),
        in_specs=mixer_specs + [row_spec, mod_spec, mod_spec, mod_spec,
                                pl.BlockSpec((2, d, ROUTE_LANES), lambda i: (0, 0, 0))],
        out_specs=[row_spec, row_spec, pl.BlockSpec((OUT_TM, ROUTE_LANES), lambda i: (i, 0))],
        compiler_params=_cparams(("arbitrary",)),
        name=name,
    )(*mixer_args, x2d, gate, shift, scale, rwt)


def _attn_out(attn2d, gla2d, w_out, x2d, gate, shift, scale, rwt):
    half = pl.BlockSpec((OUT_TM, A_Q_DIM), lambda i: (i, 0))
    specs = [half, half, pl.BlockSpec((D_MODEL, D_MODEL), lambda i: (0, 0))]
    return _mixer_out(_attn_out_kernel, specs, [attn2d, gla2d, w_out], x2d, gate, shift, scale, rwt,
                      "attn_out_projection")


def _conv_out(g2d, conv_w, w_out, x2d, gate, shift, scale, rwt):
    d = D_MODEL
    sub = OUT_TM // CONV_HALO
    last_halo = N_TOK // CONV_HALO - 1
    main = lambda col: pl.BlockSpec((OUT_TM, d), lambda i: (i, col))
    before = lambda col: pl.BlockSpec((CONV_HALO, d), lambda i: (jnp.maximum(i * sub - 1, 0), col))
    after = lambda col: pl.BlockSpec((CONV_HALO, d), lambda i: (jnp.minimum((i + 1) * sub, last_halo), col))
    specs = [main(0), main(1), main(2), before(1), before(2), after(1), after(2),
             pl.BlockSpec((3, d), lambda i: (0, 0)), pl.BlockSpec((d, d), lambda i: (0, 0))]
    return _mixer_out(_conv_out_kernel, specs, [g2d] * 7 + [conv_w, w_out], x2d, gate, shift, scale, rwt,
                      "conv_out_projection")


def _first_argmax8(x, idx8):
    m = jnp.max(x, axis=0, keepdims=True)
    a = jnp.min(jnp.where(x == m, idx8, float(GROUP_SIZE)), axis=0, keepdims=True)
    return m, a


def _route_kernel(lg_ref, rb_ref, e_ref, w_ref, dest_ref, meta_ref, cnt_ref, carry_ref, pstart_ref):
    phase = pl.program_id(0)
    j = pl.program_id(1)
    tt = ROUTE_TT
    sc = jax.nn.sigmoid(lg_ref[...].T[:N_EXPERTS])
    grp = sc + rb_ref[...]
    idx8 = lax.broadcasted_iota(I32, (GROUP_SIZE, tt), 0).astype(F32)
    groups = [grp[g * GROUP_SIZE:(g + 1) * GROUP_SIZE] for g in range(N_GROUPS)]
    gscore = []
    for x in groups:
        m1, a1 = _first_argmax8(x, idx8)
        m2, _ = _first_argmax8(jnp.where(idx8 == a1, -jnp.inf, x), idx8)
        gscore.append(m1 + m2)
    gmax = functools.reduce(jnp.maximum, gscore)
    gsel = jnp.full((1, tt), float(N_GROUPS), F32)
    for g in reversed(range(N_GROUPS)):
        gsel = jnp.where(gscore[g] == gmax, float(g), gsel)
    in_grp = groups[0]
    for g in range(1, N_GROUPS):
        in_grp = jnp.where(gsel == float(g), groups[g], in_grp)
    _, a1 = _first_argmax8(in_grp, idx8)
    _, a2 = _first_argmax8(jnp.where(idx8 == a1, -jnp.inf, in_grp), idx8)
    e0 = gsel * GROUP_SIZE + a1
    e1 = gsel * GROUP_SIZE + a2
    idx32 = lax.broadcasted_iota(I32, (N_EXPERTS, tt), 0).astype(F32)
    oh0 = idx32 == e0
    oh1 = idx32 == e1
    w0 = jnp.sum(jnp.where(oh0, sc, 0.0), axis=0, keepdims=True)
    w1 = jnp.sum(jnp.where(oh1, sc, 0.0), axis=0, keepdims=True)
    wsum = w0 + w1
    ohs = jnp.where(oh0 | oh1, 1.0, 0.0)
    tile_cnt = jnp.sum(ohs, axis=1, keepdims=True)

    @pl.when(phase == 0)
    def _():
        @pl.when(j == 0)
        def _():
            cnt_ref[...] = jnp.zeros_like(cnt_ref)
        cnt_ref[...] += jnp.broadcast_to(tile_cnt, cnt_ref.shape)

    @pl.when(phase == 1)
    def _():
        @pl.when(j == 0)
        def _():
            cnt = cnt_ref[...]
            expert = lax.broadcasted_iota(I32, cnt.shape, 0)
            lane = lax.broadcasted_iota(I32, cnt.shape, 1).astype(F32)

            def exclusive_cumsum(v):
                inc = v
                for s in (1, 2, 4, 8, 16):
                    inc = inc + jnp.where(expert >= s, pltpu.roll(inc, s, 0), 0.0)
                return inc - v

            cnt_pad = jnp.floor((cnt + (MOE_ALIGN - 1)) * (1.0 / MOE_ALIGN)) * MOE_ALIGN
            pstart = exclusive_cumsum(cnt_pad)
            pstart_ref[...] = pstart
            carry_ref[...] = jnp.zeros_like(carry_ref)
            nvis_e = jnp.zeros_like(cnt)
            for kb in range(-(-N_TOK // MOE_BLOCK)):
                nvis_e += jnp.where(cnt > float(kb * MOE_BLOCK), 1.0, 0.0)
            vstart = exclusive_cumsum(nvis_e)
            vis_exp = jnp.sum(jnp.where(lane >= vstart + nvis_e, 1.0, 0.0), axis=0, keepdims=True)
            vis_exp = jnp.minimum(vis_exp, float(N_EXPERTS - 1))
            own = expert.astype(F32) == vis_exp
            of_visit = lambda col: jnp.sum(jnp.where(own, col, 0.0), axis=0, keepdims=True)
            kth = lane[0:1] - of_visit(vstart)
            vis_rows = jnp.clip(of_visit(cnt_pad) - kth * MOE_BLOCK, 0.0, float(MOE_BLOCK))
            vis_row0 = of_visit(pstart) + kth * MOE_BLOCK
            last_lane = lane[0:1] == float(META_LANES - 1)
            expert_lane = expert.astype(F32) == lane
            to_lane = lambda col: jnp.sum(jnp.where(expert_lane, col, 0.0), axis=0, keepdims=True)
            meta_ref[...] = jnp.concatenate([
                jnp.where(last_lane, jnp.sum(nvis_e, axis=0, keepdims=True), vis_exp),
                vis_rows,
                vis_row0,
                to_lane(pstart + cnt),
                jnp.where(last_lane, jnp.sum(cnt_pad, axis=0, keepdims=True), to_lane(cnt_pad - cnt)),
            ], axis=0).astype(I32)

        rr = lax.broadcasted_iota(I32, (tt, tt), 0)
        cc = lax.broadcasted_iota(I32, (tt, tt), 1)
        strict_upper = jnp.where(rr < cc, 1.0, 0.0).astype(BF16)
        before = _dot(ohs.astype(BF16), strict_upper)
        pos = before + carry_ref[:, 0:1] + pstart_ref[:, 0:1]
        d0 = jnp.sum(jnp.where(oh0, pos, 0.0), axis=0, keepdims=True)
        d1 = jnp.sum(jnp.where(oh1, pos, 0.0), axis=0, keepdims=True)
        carry_ref[...] += jnp.broadcast_to(tile_cnt, carry_ref.shape)
        e_ref[...] = jnp.concatenate([e0, e1], axis=0).astype(I32)
        w_ref[...] = jnp.concatenate([w0 / wsum, w1 / wsum], axis=0)
        dest_ref[...] = jnp.concatenate([d0, d1], axis=0).astype(I32)


def _route(logits_t, router_b):
    nt = N_TOK // ROUTE_TT
    tok = lambda dt: jax.ShapeDtypeStruct((2, N_TOK), dt)
    tok_spec = pl.BlockSpec((2, ROUTE_TT), lambda p, j: (0, j * p))
    return pl.pallas_call(
        _route_kernel,
        out_shape=[tok(I32), tok(F32), tok(I32), jax.ShapeDtypeStruct((META_ROWS, META_LANES), I32)],
        grid=(2, nt),
        in_specs=[pl.BlockSpec((ROUTE_TT, ROUTE_LANES), lambda p, j: (j, 0)),
                  pl.BlockSpec((N_EXPERTS, 1), lambda p, j: (0, 0))],
        out_specs=[tok_spec, tok_spec, tok_spec, pl.BlockSpec((META_ROWS, META_LANES), lambda p, j: (0, 0))],
        scratch_shapes=[pltpu.VMEM((N_EXPERTS, 128), F32), pltpu.VMEM((N_EXPERTS, 128), F32),
                        pltpu.VMEM((N_EXPERTS, 128), F32)],
        compiler_params=_cparams(("arbitrary", "arbitrary")),
        name="router_slot_assignment",
    )(logits_t, router_b.reshape(N_EXPERTS, 1))


def _dispatch_copy(dest_ref, h_ref, xs_ref, sem, i, g, u, k):
    dst = dest_ref[k * N_TOK + i * TOK_TM + g * ROW_GROUP + u]
    return pltpu.make_async_copy(h_ref.at[g, pl.ds(u, 1)], xs_ref.at[pl.ds(dst, 1)], sem)


def _for_each_tail_piece(meta_ref, zero_ref, slots_ref, sem, fn):
    used = meta_ref[4 * META_LANES + META_LANES - 1]
    tail = MOE_SLOTS - used
    row = used
    for size in MOE_TAIL_PIECES:
        take = tail & size

        @pl.when(take != 0)
        def _():
            fn(pltpu.make_async_copy(zero_ref.at[pl.ds(0, size)],
                                     slots_ref.at[pl.ds(pl.multiple_of(row, MOE_ALIGN), size)], sem))
        row = row + take


def _for_each_padding_row(meta_ref, zero_ref, xs_ref, sem, fn):
    def expert_padding(e, carry):
        row = meta_ref[3 * META_LANES + e]
        npad = meta_ref[4 * META_LANES + e]
        for r in range(MOE_ALIGN - 1):
            @pl.when(r < npad)
            def _():
                fn(pltpu.make_async_copy(zero_ref.at[pl.ds(0, 1)], xs_ref.at[pl.ds(row + r, 1)], sem))
        return carry

    lax.fori_loop(0, N_EXPERTS, expert_padding, 0)


def _zero_fill_padding(meta_ref, zero_ref, xs_ref, sem):
    zero_ref[...] = jnp.zeros_like(zero_ref)
    for fn in (lambda copy: copy.start(), lambda copy: copy.wait()):
        _for_each_padding_row(meta_ref, zero_ref, xs_ref, sem, fn)
        _for_each_tail_piece(meta_ref, zero_ref, xs_ref, sem, fn)


def _dispatch_kernel(dest_ref, meta_ref, h_ref, xs_ref, zero_ref, sem):
    i = pl.program_id(0)

    @pl.when(i == 0)
    def _():
        _zero_fill_padding(meta_ref, zero_ref, xs_ref, sem)

    def start(g, carry):
        for u in range(ROW_GROUP):
            for k in range(2):
                _dispatch_copy(dest_ref, h_ref, xs_ref, sem, i, g, u, k).start()
        return carry

    def wait(g, carry):
        for u in range(ROW_GROUP):
            for k in range(2):
                _dispatch_copy(dest_ref, h_ref, xs_ref, sem, i, g, u, k).wait()
        return carry

    lax.fori_loop(0, TOK_TM // ROW_GROUP, start, 0)
    lax.fori_loop(0, TOK_TM // ROW_GROUP, wait, 0)


def _dispatch(dest_flat, meta_flat, h2d):
    d = D_MODEL
    return pl.pallas_call(
        _dispatch_kernel,
        out_shape=jax.ShapeDtypeStruct((MOE_SLOTS, d), F32),
        grid_spec=pltpu.PrefetchScalarGridSpec(
            num_scalar_prefetch=2,
            grid=(N_TOK // TOK_TM,),
            in_specs=[pl.BlockSpec((TOK_TM // ROW_GROUP, ROW_GROUP, d), lambda i, dest, meta: (i, 0, 0))],
            out_specs=pl.BlockSpec(memory_space=pl.ANY),
            scratch_shapes=[pltpu.VMEM((MOE_TAIL_PIECES[0], d), F32), pltpu.SemaphoreType.DMA],
        ),
        compiler_params=_cparams(("arbitrary",)),
        name="moe_dispatch",
    )(dest_flat, meta_flat, h2d.reshape(N_TOK // ROW_GROUP, ROW_GROUP, d))


def _for_each_x_piece(meta_ref, xs_ref, xbuf_ref, sem, v, fn):
    rows = meta_ref[META_LANES + v]
    row0 = meta_ref[2 * META_LANES + v]
    slot = v % 2
    for p in range(MOE_BLOCK // MOE_ROWS):
        @pl.when(p * MOE_ROWS < rows)
        def _():
            src = xs_ref.at[pl.ds(pl.multiple_of(row0 + p * MOE_ROWS, MOE_ALIGN), MOE_ROWS)]
            fn(pltpu.make_async_copy(src, xbuf_ref.at[slot, pl.ds(p * MOE_ROWS, MOE_ROWS)], sem.at[slot]))


def _for_each_y_piece(meta_ref, ybuf_ref, yb_ref, sem, v, fn):
    rows = meta_ref[META_LANES + v]
    row0 = meta_ref[2 * META_LANES + v]

    def piece(off, size):
        dst = yb_ref.at[pl.ds(pl.multiple_of(row0 + off, MOE_ALIGN), size)]
        fn(pltpu.make_async_copy(ybuf_ref.at[pl.ds(pl.multiple_of(off, MOE_ALIGN), size)], dst, sem))

    for p in range(MOE_BLOCK // MOE_ROWS):
        @pl.when((p + 1) * MOE_ROWS <= rows)
        def _():
            piece(p * MOE_ROWS, MOE_ROWS)
    off = (rows // MOE_ROWS) * MOE_ROWS
    size = MOE_ROWS // 2
    while size >= MOE_ALIGN:
        take = (rows - off) & size

        @pl.when(take != 0)
        def _():
            piece(off, size)
        off = off + take
        size //= 2


def _expert_kernel(meta_ref, xs_ref, w1_ref, w3_ref, w2_ref, yb_ref, xbuf_ref, xb_ref, ybuf_ref, zero_ref,
                   xsem, ysem, zsem):
    v = pl.program_id(0)
    f = pl.program_id(1)
    nf = pl.num_programs(1)
    valid = meta_ref[META_LANES + v]
    start = lambda copy: copy.start()
    wait = lambda copy: copy.wait()

    @pl.when(f == 0)
    def _():
        @pl.when(v == 0)
        def _():
            zero_ref[...] = jnp.zeros_like(zero_ref)
            _for_each_tail_piece(meta_ref, zero_ref, yb_ref, zsem, start)
            _for_each_tail_piece(meta_ref, zero_ref, yb_ref, zsem, wait)
            _for_each_x_piece(meta_ref, xs_ref, xbuf_ref, xsem, v, start)

        _for_each_x_piece(meta_ref, xs_ref, xbuf_ref, xsem, v, wait)

        @pl.when(v + 1 < MOE_NVIS)
        def _():
            _for_each_x_piece(meta_ref, xs_ref, xbuf_ref, xsem, v + 1, start)

        for p in range(MOE_BLOCK // MOE_ROWS):
            @pl.when(p * MOE_ROWS < valid)
            def _():
                piece = slice(p * MOE_ROWS, (p + 1) * MOE_ROWS)
                xb_ref[piece, :] = xbuf_ref[v % 2, piece, :].astype(BF16)

    def previous_writeback_done():
        @pl.when(v > 0)
        def _():
            _for_each_y_piece(meta_ref, ybuf_ref, yb_ref, ysem, v - 1, wait)

    @pl.when((f == 0) & (valid == 0))
    def _():
        previous_writeback_done()

    def compute(nrows):
        x = xb_ref[:nrows, :]
        a = _silu(_dot(x, w1_ref[0, 0].astype(BF16))) * _dot(x, w3_ref[0, 0].astype(BF16))
        y = _dot(a.astype(BF16), w2_ref[0, 0].astype(BF16))

        @pl.when(f == 0)
        def _():
            previous_writeback_done()
            ybuf_ref[:nrows, :] = y

        @pl.when(f > 0)
        def _():
            ybuf_ref[:nrows, :] = y + ybuf_ref[:nrows, :]

    for nrows in range(MOE_ROWS, MOE_BLOCK + 1, MOE_ROWS):
        @pl.when((valid > nrows - MOE_ROWS) & (valid <= nrows))
        def _():
            compute(nrows)

    @pl.when(f == nf - 1)
    def _():
        _for_each_y_piece(meta_ref, ybuf_ref, yb_ref, ysem, v, start)

        @pl.when(v == MOE_NVIS - 1)
        def _():
            _for_each_y_piece(meta_ref, ybuf_ref, yb_ref, ysem, v, wait)


def _experts(meta, xs, w1, w3, w2, layer):
    d = D_MODEL
    nf = D_EXPERT // MOE_FC
    n_vis = lambda meta: meta[META_LANES - 1]
    used = lambda v, meta: jnp.minimum(v, n_vis(meta) - 1)
    chunk = lambda v, f, meta: jnp.where(v < n_vis(meta), f, nf - 1)
    w_in_spec = pl.BlockSpec((1, 1, d, MOE_FC), lambda v, f, meta: (layer, meta[used(v, meta)], 0, chunk(v, f, meta)))
    w_out_spec = pl.BlockSpec((1, 1, MOE_FC, d), lambda v, f, meta: (layer, meta[used(v, meta)], chunk(v, f, meta), 0))
    return pl.pallas_call(
        _expert_kernel,
        out_shape=jax.ShapeDtypeStruct((MOE_SLOTS, d), F32),
        grid_spec=pltpu.PrefetchScalarGridSpec(
            num_scalar_prefetch=1,
            grid=(MOE_NVIS, nf),
            in_specs=[pl.BlockSpec(memory_space=pl.ANY), w_in_spec, w_in_spec, w_out_spec],
            out_specs=pl.BlockSpec(memory_space=pl.ANY),
            scratch_shapes=[pltpu.VMEM((2, MOE_BLOCK, d), F32), pltpu.VMEM((MOE_BLOCK, d), BF16),
                            pltpu.VMEM((MOE_BLOCK, d), F32), pltpu.VMEM((MOE_TAIL_PIECES[0], d), F32),
                            pltpu.SemaphoreType.DMA((2,)), pltpu.SemaphoreType.DMA, pltpu.SemaphoreType.DMA],
        ),
        compiler_params=_cparams(("arbitrary", "arbitrary")),
        name="moe_experts",
    )(meta, xs, w1, w3, w2)


def _combine_copy(dest_ref, yb_ref, buf_ref, sem, i, g, u, k):
    src = dest_ref[k * N_TOK + i * TOK_TM + g * ROW_GROUP + u]
    return pltpu.make_async_copy(yb_ref.at[pl.ds(src, 1)], buf_ref.at[k, g, pl.ds(u, 1)], sem)


def _combine_kernel(dest_ref, yb_ref, x_ref, gate_ref, w_ref, o_ref, buf_ref, sem):
    i = pl.program_id(0)

    def start(g, carry):
        for u in range(ROW_GROUP):
            for k in range(2):
                _combine_copy(dest_ref, yb_ref, buf_ref, sem, i, g, u, k).start()
        return carry

    def wait(g, carry):
        for u in range(ROW_GROUP):
            for k in range(2):
                _combine_copy(dest_ref, yb_ref, buf_ref, sem, i, g, u, k).wait()
        return carry

    lax.fori_loop(0, TOK_TM // ROW_GROUP, start, 0)
    lax.fori_loop(0, TOK_TM // ROW_GROUP, wait, 0)
    w = w_ref[...]
    rows = lambda k: buf_ref[k].reshape(TOK_TM, buf_ref.shape[-1])
    y = w[:, 0:1] * rows(0) + w[:, 1:2] * rows(1)
    o_ref[...] = x_ref[...] + gate_ref[0] * y


def _combine(dest_flat, yb, x2d, gate, w_tok):
    d = D_MODEL
    return pl.pallas_call(
        _combine_kernel,
        out_shape=jax.ShapeDtypeStruct((N_TOK, d), F32),
        grid_spec=pltpu.PrefetchScalarGridSpec(
            num_scalar_prefetch=1,
            grid=(N_TOK // TOK_TM,),
            in_specs=[pl.BlockSpec(memory_space=pl.ANY),
                      pl.BlockSpec((TOK_TM, d), lambda i, dest: (i, 0)),
                      pl.BlockSpec((1, 1, d), lambda i, dest: (i * TOK_TM // SEQ_LEN, 0, 0)),
                      pl.BlockSpec((TOK_TM, 2), lambda i, dest: (i, 0))],
            out_specs=pl.BlockSpec((TOK_TM, d), lambda i, dest: (i, 0)),
            scratch_shapes=[pltpu.VMEM((2, TOK_TM // ROW_GROUP, ROW_GROUP, d), F32), pltpu.SemaphoreType.DMA],
        ),
        compiler_params=_cparams(("arbitrary",)),
        name="moe_combine",
    )(dest_flat, yb, x2d, gate, w_tok)


def _moe(h2d, logits_t, x2d, gate, router_b, w1, w3, w2, layer):
    _, w_sel, dest, meta = _route(logits_t, router_b)
    dest_flat = dest.reshape(2 * N_TOK)
    meta_flat = meta.reshape(META_ROWS * META_LANES)
    xs = _dispatch(dest_flat, meta_flat, h2d)
    yb = _experts(meta_flat, xs, w1, w3, w2, layer)
    return _combine(dest_flat, yb, x2d, gate, w_sel.T)


def _rope_tables():
    f = HEAD_DIM // 4
    inv_freq = ROPE_THETA ** (-jnp.arange(f, dtype=F32) / f)
    pos = jnp.arange(SEQ_LEN)
    row = (pos // GRID_W).astype(F32)[:, None] * inv_freq
    col = (pos % GRID_W).astype(F32)[:, None] * inv_freq
    cos = jnp.concatenate([jnp.cos(row), jnp.cos(row), jnp.cos(col), jnp.cos(col)], axis=-1)
    sin = jnp.concatenate([-jnp.sin(row), jnp.sin(row), -jnp.sin(col), jnp.sin(col)], axis=-1)
    return cos, sin


def _cast_block_kernel(src_ref, x_ref, o_ref):
    del src_ref
    o_ref[...] = x_ref[...].astype(BF16)


def _permute_attn_w_in(w):
    widths = dict(aq=A_Q_DIM, ak=A_KV_DIM, av=A_KV_DIM, gq=GLA_K_DIM, gk=GLA_K_DIM, gv=GLA_V_DIM, gg=GLA_V_DIM)
    starts, col = {}, 0
    for name in ("aq", "ak", "av", "gq", "gk", "gv", "gg"):
        starts[name], col = col, col + widths[name]
    blk = A_KV_DIM
    src = [c // blk for name in ("aq", "gv", "gg", "gq", "gk", "ak", "av")
           for c in range(starts[name], starts[name] + widths[name], blk)]
    d = w.shape[0]
    main = pl.pallas_call(
        _cast_block_kernel,
        out_shape=jax.ShapeDtypeStruct((d, PROJ_DIM), BF16),
        grid_spec=pltpu.PrefetchScalarGridSpec(
            num_scalar_prefetch=1,
            grid=(len(src),),
            in_specs=[pl.BlockSpec((d, blk), lambda j, src: (0, src[j]))],
            out_specs=pl.BlockSpec((d, blk), lambda j, src: (0, j)),
        ),
        compiler_params=_cparams(("arbitrary",)),
        name="permute_cast_w_in",
    )(jnp.asarray(src, I32), w)
    lr = jnp.pad(w[:, PROJ_DIM:], ((0, 0), (0, LR_PAD - 2 * GLA_RANK))).astype(BF16)
    return main, lr


def kernel(x, c, ctx, c_ctx, mod_w, mod_b, attn_w_in, attn_q_norm, attn_k_norm, attn_sink, gla_wa2, gla_ba,
           gla_norm, attn_w_out, conv_w_in, conv_w, conv_w_out, router_w, router_b, exp_w1, exp_w3, exp_w2):
    d = D_MODEL
    cc = jnp.concatenate([c, c_ctx[None], jnp.zeros((8 - N_BATCH - 1, d), F32)], axis=0)
    m = _modulation(cc, mod_w, mod_b)
    mods = [[m[l, :, i * d:(i + 1) * d].reshape(8, 1, d) for i in range(6)] for l in range(mod_w.shape[0])]
    lat_row = lambda i: i * PROJ_TM // SEQ_LEN
    ctx_row = lambda i: N_BATCH
    rw_pad = jnp.pad(router_w, ((0, 0), (0, ROUTE_LANES - N_EXPERTS)))
    rw_hi = rw_pad.astype(BF16)
    rwt = jnp.stack([rw_hi, (rw_pad - rw_hi.astype(F32)).astype(BF16)])
    x2d = x.reshape(N_TOK, d)

    w_main, w_lr = _permute_attn_w_in(attn_w_in[0])
    p_lat, lr_lat = _projection(x2d, mods[0][0], mods[0][1], w_main, w_lr, lat_row)
    p_ctx, lr_ctx = _projection(ctx.reshape(N_BATCH * CTX_LEN, d), mods[0][0], mods[0][1], w_main, w_lr, ctx_row)
    p_lat = p_lat.reshape(N_BATCH, SEQ_LEN, PROJ_DIM)
    p_ctx = p_ctx.reshape(N_BATCH, CTX_LEN, PROJ_DIM)
    cos, sin_signed = _rope_tables()
    attn = _attention(p_lat, p_ctx, cos, sin_signed, attn_q_norm[0][None], attn_k_norm[0][None], attn_sink[0])
    wa_pad = jnp.zeros((2, LR_PAD, GLA_K_DIM), F32)
    for di in range(2):
        wa_pad = wa_pad.at[di, di * GLA_RANK:(di + 1) * GLA_RANK].set(gla_wa2[0, di])
    wa_pad = wa_pad.astype(BF16)
    ba = gla_ba[0].reshape(2, 1, GLA_K_DIM)
    s_zero = jnp.zeros((N_BATCH, 2, GLA_HEADS, GLA_DV, GLA_DK), F32)
    s_ctx = _gla(p_ctx, lr_ctx.reshape(N_BATCH, CTX_LEN, LR_PAD), wa_pad, ba, None, s_zero, latent=False)
    gla = _gla(p_lat, lr_lat.reshape(N_BATCH, SEQ_LEN, LR_PAD), wa_pad, ba, gla_norm[0][None], s_ctx, latent=True)
    x1, h2, lg = _attn_out(attn.reshape(N_TOK, A_Q_DIM), gla.reshape(N_TOK, GLA_V_DIM), attn_w_out[0].astype(BF16),
                           x2d, mods[0][2], mods[0][3], mods[0][4], rwt)
    x2 = _moe(h2, lg, x1, mods[0][5], router_b, exp_w1, exp_w3, exp_w2, 0)

    g = _projection(x2, mods[1][0], mods[1][1], conv_w_in[0].astype(BF16), None, lat_row)
    x3, h2, lg = _conv_out(g, conv_w[0], conv_w_out[0].astype(BF16), x2, mods[1][2], mods[1][3], mods[1][4], rwt)
    x4 = _moe(h2, lg, x3, mods[1][5], router_b, exp_w1, exp_w3, exp_w2, 1)
    return x4.reshape(N_BATCH, SEQ_LEN, d)
```

```python
import functools

import jax
import jax.numpy as jnp
from jax import lax
from jax.experimental import pallas as pl
from jax.experimental.pallas import tpu as pltpu

F32 = jnp.float32
BF16 = jnp.bfloat16
I32 = jnp.int32

D_MODEL = 2048
N_BATCH = 4
SEQ_LEN = 2048
CTX_LEN = 256
N_TOK = N_BATCH * SEQ_LEN
GRID_W = 64
HEAD_DIM = 128
A_Q_HEADS = 8
A_KV_HEADS = 2
A_GROUP = A_Q_HEADS // A_KV_HEADS
WINDOW = 128
A_BLOCK = 128
ROPE_THETA = 10000.0
GLA_HEADS = 4
GLA_DK = 128
GLA_DV = 256
GLA_RANK = 16
GLA_TAU = 16.0
GLA_CHUNK = 64
N_EXPERTS = 32
N_GROUPS = 4
GROUP_SIZE = N_EXPERTS // N_GROUPS
D_EXPERT = 1024
EPS = 1e-6
NEG_INF = -1e30
A_Q_DIM = A_Q_HEADS * HEAD_DIM
A_KV_DIM = A_KV_HEADS * HEAD_DIM
GLA_K_DIM = GLA_HEADS * GLA_DK
GLA_V_DIM = GLA_HEADS * GLA_DV
PROJ_DIM = A_Q_DIM + 2 * A_KV_DIM + 2 * GLA_K_DIM + 2 * GLA_V_DIM
LR_PAD = 128

COL_AQ = 0
COL_GV = 1
COL_GG = 2
COL_GQ = 6
COL_GK = 7
COL_AK = 32
COL_AV = 34

MOD_TN = 1024
PROJ_TM = 512
PROJ_TN = 512
OUT_TM = 256
CONV_HALO = 16
GLA_ROWS = 512
ROUTE_TT = 512
ROUTE_LANES = 128
MOE_ALIGN = 8
MOE_BLOCK = 768
MOE_ROWS = 128
MOE_FC = 256
MOE_NVIS = N_EXPERTS + (2 * N_TOK) // MOE_BLOCK
MOE_TAIL_PIECES = (256, 128, 64, 32, 16, 8)
MOE_SLOTS = 2 * N_TOK + sum(MOE_TAIL_PIECES)
assert MOE_SLOTS >= 2 * N_TOK + N_EXPERTS * (MOE_ALIGN - 1) + MOE_ROWS
TOK_TM = 256
ROW_GROUP = 8
META_ROWS = 5
META_LANES = 128
VMEM_LIMIT = 56 * 1024 * 1024


def _cparams(sem):
    return pltpu.CompilerParams(dimension_semantics=sem, vmem_limit_bytes=VMEM_LIMIT)


def _silu(x):
    return x * jax.nn.sigmoid(x)


def _dot(a, b):
    return jnp.dot(a, b, preferred_element_type=F32)


def _dot_nt(a, b):
    return lax.dot_general(a, b, (((1,), (1,)), ((), ())), preferred_element_type=F32)


def _dot_tn(a, b):
    return lax.dot_general(a, b, (((0,), (0,)), ((), ())), preferred_element_type=F32)


def _rms_modulate(x, shift, scale):
    r = lax.rsqrt(jnp.mean(x * x, axis=-1, keepdims=True) + EPS)
    return (x * r) * (1.0 + scale) + shift


def _mod_kernel(cc_ref, w_ref, b_ref, o_ref):
    a = _silu(cc_ref[...])
    o_ref[0] = _dot(a.astype(BF16), w_ref[0].astype(BF16)) + b_ref[0]


def _modulation(cc, mod_w, mod_b):
    depth, d, n = mod_w.shape
    return pl.pallas_call(
        _mod_kernel,
        out_shape=jax.ShapeDtypeStruct((depth, 8, n), F32),
        grid=(depth, n // MOD_TN),
        in_specs=[
            pl.BlockSpec((8, d), lambda l, j: (0, 0)),
            pl.BlockSpec((1, d, MOD_TN), lambda l, j: (l, 0, j)),
            pl.BlockSpec((1, 1, MOD_TN), lambda l, j: (l, 0, j)),
        ],
        out_specs=pl.BlockSpec((1, 8, MOD_TN), lambda l, j: (l, 0, j)),
        compiler_params=_cparams(("arbitrary", "arbitrary")),
        name="adaln_modulation",
    )(cc, mod_w, mod_b.reshape(depth, 1, n))


def _proj_kernel(with_lr, x_ref, sh_ref, sc_ref, w_ref, *rest):
    if with_lr:
        wlr_ref, o_ref, olr_ref = rest
    else:
        (o_ref,) = rest
    hb = _rms_modulate(x_ref[...], sh_ref[0], sc_ref[0]).astype(BF16)
    for c in range(w_ref.shape[1] // PROJ_TN):
        cols = slice(c * PROJ_TN, (c + 1) * PROJ_TN)
        o_ref[:, cols] = _dot(hb, w_ref[:, cols]).astype(BF16)
    if with_lr:
        olr_ref[...] = _dot(hb, wlr_ref[...].astype(BF16)).astype(BF16)


def _projection(x2d, shift, scale, w, w_lr, mod_row):
    rows, d = x2d.shape
    n = w.shape[1]
    with_lr = w_lr is not None
    resident = lambda shape: pl.BlockSpec(shape, lambda i: (0, 0), pipeline_mode=pl.Buffered(1))
    in_specs = [
        pl.BlockSpec((PROJ_TM, d), lambda i: (i, 0)),
        pl.BlockSpec((1, 1, d), lambda i: (mod_row(i), 0, 0)),
        pl.BlockSpec((1, 1, d), lambda i: (mod_row(i), 0, 0)),
        resident((d, n)),
    ]
    out_shape = [jax.ShapeDtypeStruct((rows, n), BF16)]
    out_specs = [pl.BlockSpec((PROJ_TM, n), lambda i: (i, 0))]
    args = [x2d, shift, scale, w]
    if with_lr:
        in_specs.append(resident((d, LR_PAD)))
        out_shape.append(jax.ShapeDtypeStruct((rows, LR_PAD), BF16))
        out_specs.append(pl.BlockSpec((PROJ_TM, LR_PAD), lambda i: (i, 0)))
        args.append(w_lr)
    res = pl.pallas_call(
        functools.partial(_proj_kernel, with_lr),
        out_shape=out_shape,
        grid=(rows // PROJ_TM,),
        in_specs=in_specs,
        out_specs=out_specs,
        compiler_params=_cparams(("arbitrary",)),
        name="norm_mod_projection",
    )(*args)
    return res if with_lr else res[0]


def _swap_halves32(x):
    lane = lax.broadcasted_iota(I32, x.shape, 1)
    return jnp.where((lane & 63) < 32, pltpu.roll(x, 96, 1), pltpu.roll(x, 32, 1))


def _qk_norm(x, gain):
    return x * lax.rsqrt(jnp.mean(x * x, axis=-1, keepdims=True) + EPS) * gain


def _rope(x, cos, sin_signed):
    return x * cos + _swap_halves32(x) * sin_signed


ATT_WIN0 = CTX_LEN
ATT_LAT0 = CTX_LEN + A_BLOCK
ATT_ROWS = CTX_LEN + SEQ_LEN + 2 * A_BLOCK
ATT_WIN = 3 * A_BLOCK


def _attn_kernel(sink_ref, q_ref, k_ref, v_ref, kx_ref, vx_ref, cos_ref, sin_ref, qg_ref, kg_ref, o_ref,
                 kn_ref, vn_ref, band_ref):
    kvh = pl.program_id(1)
    nb = SEQ_LEN // A_BLOCK
    k_gain = kg_ref[...]
    q_gain = qg_ref[...]

    pad = jnp.zeros((A_BLOCK, HEAD_DIM), BF16)
    kn_ref[0:CTX_LEN, :] = _qk_norm(kx_ref[0].astype(F32), k_gain).astype(BF16)
    vn_ref[0:CTX_LEN, :] = vx_ref[0]
    for ref in (kn_ref, vn_ref):
        ref[ATT_WIN0:ATT_LAT0, :] = pad
        ref[ATT_LAT0 + SEQ_LEN:, :] = pad
    vn_ref[ATT_LAT0:ATT_LAT0 + SEQ_LEN, :] = v_ref[0]
    prep_rows = 4 * A_BLOCK
    for c in range(SEQ_LEN // prep_rows):
        r = slice(c * prep_rows, (c + 1) * prep_rows)
        kc = _rope(_qk_norm(k_ref[0, r, :].astype(F32), k_gain), cos_ref[r, :], sin_ref[r, :])
        kn_ref[ATT_LAT0 + c * prep_rows:ATT_LAT0 + (c + 1) * prep_rows, :] = kc.astype(BF16)

    qi = lax.broadcasted_iota(I32, (A_BLOCK, ATT_WIN), 0)
    wj = lax.broadcasted_iota(I32, (A_BLOCK, ATT_WIN), 1)
    band = jnp.where(jnp.abs(wj - A_BLOCK - qi) <= WINDOW, 1.0, 0.0)
    band_ref[0] = jnp.where(wj >= A_BLOCK, band, 0.0)
    band_ref[1] = band
    band_ref[2] = jnp.where(wj < 2 * A_BLOCK, band, 0.0)

    rows = A_GROUP * A_BLOCK
    rcol = lax.broadcasted_iota(I32, (rows, 1), 0)
    sink = jnp.zeros((rows, 1), F32)
    for g in range(A_GROUP):
        sink = jnp.where((rcol >= g * A_BLOCK) & (rcol < (g + 1) * A_BLOCK), sink_ref[kvh * A_GROUP + g], sink)

    def query_block(n):
        qrows = pl.ds(pl.multiple_of(n * A_BLOCK, A_BLOCK), A_BLOCK)
        wrows = pl.ds(pl.multiple_of(ATT_WIN0 + n * A_BLOCK, A_BLOCK), ATT_WIN)
        cos_q, sin_q = cos_ref[qrows, :], sin_ref[qrows, :]
        q = q_ref[0, qrows, :].astype(F32)
        qs = []
        for g in range(A_GROUP):
            qh = _rope(_qk_norm(q[:, g * HEAD_DIM:(g + 1) * HEAD_DIM], q_gain), cos_q, sin_q)
            qs.append((qh * HEAD_DIM ** -0.5).astype(BF16))
        qs = jnp.concatenate(qs, axis=0)
        mask = band_ref[jnp.where(n == 0, 0, jnp.where(n == nb - 1, 2, 1))]
        mask = jnp.concatenate([mask] * A_GROUP, axis=0)
        s_ctx = _dot_nt(qs, kn_ref[0:CTX_LEN, :])
        s_win = jnp.where(mask > 0.5, _dot_nt(qs, kn_ref[wrows, :]), NEG_INF)
        m = jnp.maximum(jnp.max(s_ctx, axis=-1, keepdims=True), jnp.max(s_win, axis=-1, keepdims=True))
        m = jnp.maximum(m, sink)
        p_ctx = jnp.exp(s_ctx - m)
        p_win = jnp.exp(s_win - m)
        denom = (jnp.sum(p_ctx, axis=-1, keepdims=True) + jnp.sum(p_win, axis=-1, keepdims=True)
                 + jnp.exp(sink - m))
        o = (_dot(p_ctx.astype(BF16), vn_ref[0:CTX_LEN, :]) + _dot(p_win.astype(BF16), vn_ref[wrows, :])) / denom
        for g in range(A_GROUP):
            o_ref[0, qrows, g * HEAD_DIM:(g + 1) * HEAD_DIM] = o[g * A_BLOCK:(g + 1) * A_BLOCK].astype(BF16)

    def query_block_pair(i, carry):
        query_block(2 * i)
        query_block(2 * i + 1)
        return carry

    lax.fori_loop(0, nb // 2, query_block_pair, 0)


def _attention(p_lat, p_ctx, cos, sin_signed, q_gain, k_gain, sink):
    gw = A_GROUP * HEAD_DIM
    lat = lambda width, col: pl.BlockSpec((1, SEQ_LEN, width), lambda b, k, s: (b, 0, col + k))
    ctx_blk = lambda col: pl.BlockSpec((1, CTX_LEN, HEAD_DIM), lambda b, k, s: (b, 0, col + k))
    full = lambda shape: pl.BlockSpec(shape, lambda b, k, s: (0,) * len(shape))
    return pl.pallas_call(
        _attn_kernel,
        out_shape=jax.ShapeDtypeStruct((N_BATCH, SEQ_LEN, A_Q_DIM), BF16),
        grid_spec=pltpu.PrefetchScalarGridSpec(
            num_scalar_prefetch=1,
            grid=(N_BATCH, A_KV_HEADS),
            in_specs=[
                lat(gw, COL_AQ), lat(HEAD_DIM, COL_AK), lat(HEAD_DIM, COL_AV),
                ctx_blk(COL_AK), ctx_blk(COL_AV),
                full((SEQ_LEN, HEAD_DIM)), full((SEQ_LEN, HEAD_DIM)),
                full((1, HEAD_DIM)), full((1, HEAD_DIM)),
            ],
            out_specs=pl.BlockSpec((1, SEQ_LEN, gw), lambda b, k, s: (b, 0, k)),
            scratch_shapes=[pltpu.VMEM((ATT_ROWS, HEAD_DIM), BF16), pltpu.VMEM((ATT_ROWS, HEAD_DIM), BF16),
                            pltpu.VMEM((3, A_BLOCK, ATT_WIN), F32)],
        ),
        compiler_params=_cparams(("arbitrary", "arbitrary")),
        name="windowed_sink_attention",
    )(sink, p_lat, p_lat, p_lat, p_ctx, p_ctx, cos, sin_signed, q_gain, k_gain)


def _split3_bf16(x):
    hi = x.astype(BF16)
    r1 = x - hi.astype(F32)
    mid = r1.astype(BF16)
    lo = (r1 - mid.astype(F32)).astype(BF16)
    return hi, mid, lo


def _gla_kernel(latent, nblk, q_ref, k_ref, v_ref, *rest):
    if latent:
        g_ref, lr_ref, wa_ref, ba_ref, gain_ref, s0_ref, o_ref, st_ref, ofwd_ref = rest
    else:
        lr_ref, wa_ref, ba_ref, s0_ref, o_ref, st_ref = rest
    d = pl.program_id(1)
    j = pl.program_id(2)
    blk_rows = k_ref.shape[1]
    nc = blk_rows // GLA_CHUNK
    blk = jnp.where(d == 0, j, nblk - 1 - j)

    @pl.when(j == 0)
    def _():
        st_ref[...] = s0_ref[0, 0]

    ii = lax.broadcasted_iota(I32, (GLA_CHUNK, GLA_CHUNK), 0)
    jj = lax.broadcasted_iota(I32, (GLA_CHUNK, GLA_CHUNK), 1)

    def scan_block(backward):
        incl = (jj >= ii) if backward else (jj <= ii)
        incl_b = jnp.where(incl, 1.0, 0.0).astype(BF16)
        for i in range(nc):
            r0 = (nc - 1 - i if backward else i) * GLA_CHUNK
            rows = slice(r0, r0 + GLA_CHUNK)
            z = _dot(lr_ref[0, rows, :], wa_ref[0]) + ba_ref[0]
            la = (jnp.minimum(z, 0.0) - jnp.log(1.0 + jnp.exp(-jnp.abs(z)))) / GLA_TAU
            hi, mid, lo = _split3_bf16(la)
            bcum = _dot(incl_b, hi) + _dot(incl_b, mid) + _dot(incl_b, lo)
            blast = jnp.sum(la, axis=0, keepdims=True)
            k = k_ref[0, rows, :].astype(F32)
            kl = (k * jnp.exp(blast - bcum)).astype(BF16)
            decay = jnp.exp(blast)
            v = v_ref[0, rows, :]
            if latent:
                qf = (q_ref[0, rows, :].astype(F32) * GLA_DK ** -0.5 * jnp.exp(bcum)).astype(BF16)
                kf = (k * jnp.exp(-bcum)).astype(BF16)
            for h in range(GLA_HEADS):
                ks = slice(h * GLA_DK, (h + 1) * GLA_DK)
                vs = slice(h * GLA_DV, (h + 1) * GLA_DV)
                st = st_ref[h]
                if latent:
                    att = jnp.where(incl, _dot_nt(qf[:, ks], kf[:, ks]), 0.0)
                    o = _dot(att.astype(BF16), v[:, vs]) + _dot_nt(qf[:, ks], st.astype(BF16))
                    orow = pl.ds(pl.multiple_of(blk * blk_rows + r0, GLA_CHUNK), GLA_CHUNK)
                    if backward:
                        ot = o + ofwd_ref[orow, vs]
                        on = ot * lax.rsqrt(jnp.mean(ot * ot, axis=-1, keepdims=True) + EPS) * gain_ref[...]
                        o_ref[0, rows, vs] = (on * _silu(g_ref[0, rows, vs].astype(F32))).astype(BF16)
                    else:
                        ofwd_ref[orow, vs] = o
                st_ref[h] = st * decay[:, ks] + _dot_tn(v[:, vs], kl[:, ks])

    for backward in (False, True):
        @pl.when(d == int(backward))
        def _():
            scan_block(backward)

    if not latent:
        @pl.when(j == nblk - 1)
        def _():
            o_ref[0, 0] = st_ref[...]


def _gla(p3, lr3, wa_pad, ba, o_gain, s0, latent):
    n = p3.shape[1]
    blk_rows = min(GLA_ROWS, n)
    nblk = n // blk_rows
    seq_blk = lambda d, j: jnp.where(d == 0, j, nblk - 1 - j)
    in_specs = [
        pl.BlockSpec((1, blk_rows, GLA_K_DIM), lambda b, d, j: (b, seq_blk(d, j), COL_GQ)),
        pl.BlockSpec((1, blk_rows, GLA_K_DIM), lambda b, d, j: (b, seq_blk(d, j), COL_GK)),
        pl.BlockSpec((1, blk_rows, GLA_V_DIM), lambda b, d, j: (b, seq_blk(d, j), COL_GV)),
    ]
    args = [p3, p3, p3]
    if latent:
        in_specs.append(pl.BlockSpec((1, blk_rows, GLA_V_DIM), lambda b, d, j: (b, seq_blk(d, j), COL_GG)))
        args.append(p3)
    in_specs += [
        pl.BlockSpec((1, blk_rows, LR_PAD), lambda b, d, j: (b, seq_blk(d, j), 0)),
        pl.BlockSpec((1, LR_PAD, GLA_K_DIM), lambda b, d, j: (d, 0, 0)),
        pl.BlockSpec((1, 1, GLA_K_DIM), lambda b, d, j: (d, 0, 0)),
    ]
    args += [lr3, wa_pad, ba]
    if latent:
        in_specs.append(pl.BlockSpec((1, GLA_DV), lambda b, d, j: (0, 0)))
        args.append(o_gain)
    state_spec = pl.BlockSpec((1, 1, GLA_HEADS, GLA_DV, GLA_DK), lambda b, d, j: (b, d, 0, 0, 0))
    in_specs.append(state_spec)
    args.append(s0)
    scratch = [pltpu.VMEM((GLA_HEADS, GLA_DV, GLA_DK), F32)]
    if latent:
        out_shape = jax.ShapeDtypeStruct((N_BATCH, n, GLA_V_DIM), BF16)
        out_spec = pl.BlockSpec((1, blk_rows, GLA_V_DIM),
                                lambda b, d, j: (b, jnp.where(d == 0, nblk - 1, nblk - 1 - j), 0))
        scratch.append(pltpu.VMEM((n, GLA_V_DIM), F32))
    else:
        out_shape = jax.ShapeDtypeStruct(s0.shape, F32)
        out_spec = state_spec
    return pl.pallas_call(
        functools.partial(_gla_kernel, latent, nblk),
        out_shape=out_shape,
        grid=(N_BATCH, 2, nblk),
        in_specs=in_specs,
        out_specs=out_spec,
        scratch_shapes=scratch,
        compiler_params=_cparams(("arbitrary", "arbitrary", "arbitrary")),
        name="gla_latent" if latent else "gla_context_state",
    )(*args)


def _router_logits(h, rw_ref):
    hh = h.astype(BF16)
    hl = (h - hh.astype(F32)).astype(BF16)
    return _dot(hh, rw_ref[0]) + _dot(hl, rw_ref[0]) + _dot(hh, rw_ref[1])


def _residual_and_moe_input(mix, x_ref, gate_ref, sh_ref, sc_ref, rwt_ref, x_out_ref, h_out_ref, lg_ref):
    x1 = x_ref[...] + gate_ref[0] * mix
    x_out_ref[...] = x1
    h = _rms_modulate(x1, sh_ref[0], sc_ref[0])
    h_out_ref[...] = h
    lg_ref[...] = _router_logits(h, rwt_ref)


def _attn_out_kernel(a1_ref, a2_ref, w_ref, x_ref, gate_ref, sh_ref, sc_ref, rwt_ref, x_out_ref, h_out_ref, lg_ref):
    mix = _dot(a1_ref[...], w_ref[0:A_Q_DIM, :]) + _dot(a2_ref[...], w_ref[A_Q_DIM:, :])
    _residual_and_moe_input(mix, x_ref, gate_ref, sh_ref, sc_ref, rwt_ref, x_out_ref, h_out_ref, lg_ref)


def _conv_out_kernel(gb_ref, gc_ref, u_ref, gcp_ref, up_ref, gcn_ref, un_ref, cw_ref, w_ref,
                     x_ref, gate_ref, sh_ref, sc_ref, rwt_ref, x_out_ref, h_out_ref, lg_ref):
    i = pl.program_id(0)
    tiles_per_seq = SEQ_LEN // OUT_TM
    t = gc_ref[...].astype(F32) * u_ref[...].astype(F32)
    first = (i % tiles_per_seq) == 0
    last = (i % tiles_per_seq) == tiles_per_seq - 1
    halo_last = slice(CONV_HALO - 1, CONV_HALO)
    t_before = jnp.where(first, 0.0, gcp_ref[halo_last, :].astype(F32) * up_ref[halo_last, :].astype(F32))
    t_after = jnp.where(last, 0.0, gcn_ref[0:1, :].astype(F32) * un_ref[0:1, :].astype(F32))
    row = lax.broadcasted_iota(I32, t.shape, 0)
    t_up = jnp.where(row == 0, t_before, pltpu.roll(t, 1, 0))
    t_dn = jnp.where(row == OUT_TM - 1, t_after, pltpu.roll(t, OUT_TM - 1, 0))
    y = cw_ref[0:1, :] * t_up + cw_ref[1:2, :] * t + cw_ref[2:3, :] * t_dn
    mix = _dot((gb_ref[...].astype(F32) * y).astype(BF16), w_ref[...])
    _residual_and_moe_input(mix, x_ref, gate_ref, sh_ref, sc_ref, rwt_ref, x_out_ref, h_out_ref, lg_ref)


def _mixer_out(kernel_fn, mixer_specs, mixer_args, x2d, gate, shift, scale, rwt, name):
    d = D_MODEL
    mod_spec = pl.BlockSpec((1, 1, d), lambda i: (i * OUT_TM // SEQ_LEN, 0, 0))
    row_spec = pl.BlockSpec((OUT_TM, d), lambda i: (i, 0))
    return pl.pallas_call(
        kernel_fn,
        out_shape=[jax.ShapeDtypeStruct((N_TOK, d), F32), jax.ShapeDtypeStruct((N_TOK, d), F32),
                   jax.ShapeDtypeStruct((N_TOK, ROUTE_LANES), F32)],
        grid=(N_TOK // OUT_TM,),
        in_specs=mixer_specs + [row_spec, mod_spec, mod_spec, mod_spec,
                                pl.BlockSpec((2, d, ROUTE_LANES), lambda i: (0, 0, 0))],
        out_specs=[row_spec, row_spec, pl.BlockSpec((OUT_TM, ROUTE_LANES), lambda i: (i, 0))],
        compiler_params=_cparams(("arbitrary",)),
        name=name,
    )(*mixer_args, x2d, gate, shift, scale, rwt)


def _attn_out(attn2d, gla2d, w_out, x2d, gate, shift, scale, rwt):
    half = pl.BlockSpec((OUT_TM, A_Q_DIM), lambda i: (i, 0))
    specs = [half, half, pl.BlockSpec((D_MODEL, D_MODEL), lambda i: (0, 0))]
    return _mixer_out(_attn_out_kernel, specs, [attn2d, gla2d, w_out], x2d, gate, shift, scale, rwt,
                      "attn_out_projection")


def _conv_out(g2d, conv_w, w_out, x2d, gate, shift, scale, rwt):
    d = D_MODEL
    sub = OUT_TM // CONV_HALO
    last_halo = N_TOK // CONV_HALO - 1
    main = lambda col: pl.BlockSpec((OUT_TM, d), lambda i: (i, col))
    before = lambda col: pl.BlockSpec((CONV_HALO, d), lambda i: (jnp.maximum(i * sub - 1, 0), col))
    after = lambda col: pl.BlockSpec((CONV_HALO, d), lambda i: (jnp.minimum((i + 1) * sub, last_halo), col))
    specs = [main(0), main(1), main(2), before(1), before(2), after(1), after(2),
             pl.BlockSpec((3, d), lambda i: (0, 0)), pl.BlockSpec((d, d), lambda i: (0, 0))]
    return _mixer_out(_conv_out_kernel, specs, [g2d] * 7 + [conv_w, w_out], x2d, gate, shift, scale, rwt,
                      "conv_out_projection")


def _first_argmax8(x, idx8):
    m = jnp.max(x, axis=0, keepdims=True)
    a = jnp.min(jnp.where(x == m, idx8, float(GROUP_SIZE)), axis=0, keepdims=True)
    return m, a


def _route_kernel(lg_ref, rb_ref, e_ref, w_ref, dest_ref, meta_ref, cnt_ref, carry_ref, pstart_ref):
    phase = pl.program_id(0)
    j = pl.program_id(1)
    tt = ROUTE_TT
    sc = jax.nn.sigmoid(lg_ref[...].T[:N_EXPERTS])
    grp = sc + rb_ref[...]
    idx8 = lax.broadcasted_iota(I32, (GROUP_SIZE, tt), 0).astype(F32)
    groups = [grp[g * GROUP_SIZE:(g + 1) * GROUP_SIZE] for g in range(N_GROUPS)]
    gscore = []
    for x in groups:
        m1, a1 = _first_argmax8(x, idx8)
        m2, _ = _first_argmax8(jnp.where(idx8 == a1, -jnp.inf, x), idx8)
        gscore.append(m1 + m2)
    gmax = functools.reduce(jnp.maximum, gscore)
    gsel = jnp.full((1, tt), float(N_GROUPS), F32)
    for g in reversed(range(N_GROUPS)):
        gsel = jnp.where(gscore[g] == gmax, float(g), gsel)
    in_grp = groups[0]
    for g in range(1, N_GROUPS):
        in_grp = jnp.where(gsel == float(g), groups[g], in_grp)
    _, a1 = _first_argmax8(in_grp, idx8)
    _, a2 = _first_argmax8(jnp.where(idx8 == a1, -jnp.inf, in_grp), idx8)
    e0 = gsel * GROUP_SIZE + a1
    e1 = gsel * GROUP_SIZE + a2
    idx32 = lax.broadcasted_iota(I32, (N_EXPERTS, tt), 0).astype(F32)
    oh0 = idx32 == e0
    oh1 = idx32 == e1
    w0 = jnp.sum(jnp.where(oh0, sc, 0.0), axis=0, keepdims=True)
    w1 = jnp.sum(jnp.where(oh1, sc, 0.0), axis=0, keepdims=True)
    wsum = w0 + w1
    ohs = jnp.where(oh0 | oh1, 1.0, 0.0)
    tile_cnt = jnp.sum(ohs, axis=1, keepdims=True)

    @pl.when(phase == 0)
    def _():
        @pl.when(j == 0)
        def _():
            cnt_ref[...] = jnp.zeros_like(cnt_ref)
        cnt_ref[...] += jnp.broadcast_to(tile_cnt, cnt_ref.shape)

    @pl.when(phase == 1)
    def _():
        @pl.when(j == 0)
        def _():
            cnt = cnt_ref[...]
            expert = lax.broadcasted_iota(I32, cnt.shape, 0)
            lane = lax.broadcasted_iota(I32, cnt.shape, 1).astype(F32)

            def exclusive_cumsum(v):
                inc = v
                for s in (1, 2, 4, 8, 16):
                    inc = inc + jnp.where(expert >= s, pltpu.roll(inc, s, 0), 0.0)
                return inc - v

            cnt_pad = jnp.floor((cnt + (MOE_ALIGN - 1)) * (1.0 / MOE_ALIGN)) * MOE_ALIGN
            pstart = exclusive_cumsum(cnt_pad)
            pstart_ref[...] = pstart
            carry_ref[...] = jnp.zeros_like(carry_ref)
            nvis_e = jnp.zeros_like(cnt)
            for kb in range(-(-N_TOK // MOE_BLOCK)):
                nvis_e += jnp.where(cnt > float(kb * MOE_BLOCK), 1.0, 0.0)
            vstart = exclusive_cumsum(nvis_e)
            vis_exp = jnp.sum(jnp.where(lane >= vstart + nvis_e, 1.0, 0.0), axis=0, keepdims=True)
            vis_exp = jnp.minimum(vis_exp, float(N_EXPERTS - 1))
            own = expert.astype(F32) == vis_exp
            of_visit = lambda col: jnp.sum(jnp.where(own, col, 0.0), axis=0, keepdims=True)
            kth = lane[0:1] - of_visit(vstart)
            vis_rows = jnp.clip(of_visit(cnt_pad) - kth * MOE_BLOCK, 0.0, float(MOE_BLOCK))
            vis_row0 = of_visit(pstart) + kth * MOE_BLOCK
            last_lane = lane[0:1] == float(META_LANES - 1)
            expert_lane = expert.astype(F32) == lane
            to_lane = lambda col: jnp.sum(jnp.where(expert_lane, col, 0.0), axis=0, keepdims=True)
            meta_ref[...] = jnp.concatenate([
                jnp.where(last_lane, jnp.sum(nvis_e, axis=0, keepdims=True), vis_exp),
                vis_rows,
                vis_row0,
                to_lane(pstart + cnt),
                jnp.where(last_lane, jnp.sum(cnt_pad, axis=0, keepdims=True), to_lane(cnt_pad - cnt)),
            ], axis=0).astype(I32)

        rr = lax.broadcasted_iota(I32, (tt, tt), 0)
        cc = lax.broadcasted_iota(I32, (tt, tt), 1)
        strict_upper = jnp.where(rr < cc, 1.0, 0.0).astype(BF16)
        before = _dot(ohs.astype(BF16), strict_upper)
        pos = before + carry_ref[:, 0:1] + pstart_ref[:, 0:1]
        d0 = jnp.sum(jnp.where(oh0, pos, 0.0), axis=0, keepdims=True)
        d1 = jnp.sum(jnp.where(oh1, pos, 0.0), axis=0, keepdims=True)
        carry_ref[...] += jnp.broadcast_to(tile_cnt, carry_ref.shape)
        e_ref[...] = jnp.concatenate([e0, e1], axis=0).astype(I32)
        w_ref[...] = jnp.concatenate([w0 / wsum, w1 / wsum], axis=0)
        dest_ref[...] = jnp.concatenate([d0, d1], axis=0).astype(I32)


def _route(logits_t, router_b):
    nt = N_TOK // ROUTE_TT
    tok = lambda dt: jax.ShapeDtypeStruct((2, N_TOK), dt)
    tok_spec = pl.BlockSpec((2, ROUTE_TT), lambda p, j: (0, j * p))
    return pl.pallas_call(
        _route_kernel,
        out_shape=[tok(I32), tok(F32), tok(I32), jax.ShapeDtypeStruct((META_ROWS, META_LANES), I32)],
        grid=(2, nt),
        in_specs=[pl.BlockSpec((ROUTE_TT, ROUTE_LANES), lambda p, j: (j, 0)),
                  pl.BlockSpec((N_EXPERTS, 1), lambda p, j: (0, 0))],
        out_specs=[tok_spec, tok_spec, tok_spec, pl.BlockSpec((META_ROWS, META_LANES), lambda p, j: (0, 0))],
        scratch_shapes=[pltpu.VMEM((N_EXPERTS, 128), F32), pltpu.VMEM((N_EXPERTS, 128), F32),
                        pltpu.VMEM((N_EXPERTS, 128), F32)],
        compiler_params=_cparams(("arbitrary", "arbitrary")),
        name="router_slot_assignment",
    )(logits_t, router_b.reshape(N_EXPERTS, 1))


def _dispatch_copy(dest_ref, h_ref, xs_ref, sem, i, g, u, k):
    dst = dest_ref[k * N_TOK + i * TOK_TM + g * ROW_GROUP + u]
    return pltpu.make_async_copy(h_ref.at[g, pl.ds(u, 1)], xs_ref.at[pl.ds(dst, 1)], sem)


def _for_each_tail_piece(meta_ref, zero_ref, slots_ref, sem, fn):
    used = meta_ref[4 * META_LANES + META_LANES - 1]
    tail = MOE_SLOTS - used
    row = used
    for size in MOE_TAIL_PIECES:
        take = tail & size

        @pl.when(take != 0)
        def _():
            fn(pltpu.make_async_copy(zero_ref.at[pl.ds(0, size)],
                                     slots_ref.at[pl.ds(pl.multiple_of(row, MOE_ALIGN), size)], sem))
        row = row + take


def _for_each_padding_row(meta_ref, zero_ref, xs_ref, sem, fn):
    def expert_padding(e, carry):
        row = meta_ref[3 * META_LANES + e]
        npad = meta_ref[4 * META_LANES + e]
        for r in range(MOE_ALIGN - 1):
            @pl.when(r < npad)
            def _():
                fn(pltpu.make_async_copy(zero_ref.at[pl.ds(0, 1)], xs_ref.at[pl.ds(row + r, 1)], sem))
        return carry

    lax.fori_loop(0, N_EXPERTS, expert_padding, 0)


def _zero_fill_padding(meta_ref, zero_ref, xs_ref, sem):
    zero_ref[...] = jnp.zeros_like(zero_ref)
    for fn in (lambda copy: copy.start(), lambda copy: copy.wait()):
        _for_each_padding_row(meta_ref, zero_ref, xs_ref, sem, fn)
        _for_each_tail_piece(meta_ref, zero_ref, xs_ref, sem, fn)


def _dispatch_kernel(dest_ref, meta_ref, h_ref, xs_ref, zero_ref, sem):
    i = pl.program_id(0)

    @pl.when(i == 0)
    def _():
        _zero_fill_padding(meta_ref, zero_ref, xs_ref, sem)

    def start(g, carry):
        for u in range(ROW_GROUP):
            for k in range(2):
                _dispatch_copy(dest_ref, h_ref, xs_ref, sem, i, g, u, k).start()
        return carry

    def wait(g, carry):
        for u in range(ROW_GROUP):
            for k in range(2):
                _dispatch_copy(dest_ref, h_ref, xs_ref, sem, i, g, u, k).wait()
        return carry

    lax.fori_loop(0, TOK_TM // ROW_GROUP, start, 0)
    lax.fori_loop(0, TOK_TM // ROW_GROUP, wait, 0)


def _dispatch(dest_flat, meta_flat, h2d):
    d = D_MODEL
    return pl.pallas_call(
        _dispatch_kernel,
        out_shape=jax.ShapeDtypeStruct((MOE_SLOTS, d), F32),
        grid_spec=pltpu.PrefetchScalarGridSpec(
            num_scalar_prefetch=2,
            grid=(N_TOK // TOK_TM,),
            in_specs=[pl.BlockSpec((TOK_TM // ROW_GROUP, ROW_GROUP, d), lambda i, dest, meta: (i, 0, 0))],
            out_specs=pl.BlockSpec(memory_space=pl.ANY),
            scratch_shapes=[pltpu.VMEM((MOE_TAIL_PIECES[0], d), F32), pltpu.SemaphoreType.DMA],
        ),
        compiler_params=_cparams(("arbitrary",)),
        name="moe_dispatch",
    )(dest_flat, meta_flat, h2d.reshape(N_TOK // ROW_GROUP, ROW_GROUP, d))


def _for_each_x_piece(meta_ref, xs_ref, xbuf_ref, sem, v, fn):
    rows = meta_ref[META_LANES + v]
    row0 = meta_ref[2 * META_LANES + v]
    slot = v % 2
    for p in range(MOE_BLOCK // MOE_ROWS):
        @pl.when(p * MOE_ROWS < rows)
        def _():
            src = xs_ref.at[pl.ds(pl.multiple_of(row0 + p * MOE_ROWS, MOE_ALIGN), MOE_ROWS)]
            fn(pltpu.make_async_copy(src, xbuf_ref.at[slot, pl.ds(p * MOE_ROWS, MOE_ROWS)], sem.at[slot]))


def _for_each_y_piece(meta_ref, ybuf_ref, yb_ref, sem, v, fn):
    rows = meta_ref[META_LANES + v]
    row0 = meta_ref[2 * META_LANES + v]

    def piece(off, size):
        dst = yb_ref.at[pl.ds(pl.multiple_of(row0 + off, MOE_ALIGN), size)]
        fn(pltpu.make_async_copy(ybuf_ref.at[pl.ds(pl.multiple_of(off, MOE_ALIGN), size)], dst, sem))

    for p in range(MOE_BLOCK // MOE_ROWS):
        @pl.when((p + 1) * MOE_ROWS <= rows)
        def _():
            piece(p * MOE_ROWS, MOE_ROWS)
    off = (rows // MOE_ROWS) * MOE_ROWS
    size = MOE_ROWS // 2
    while size >= MOE_ALIGN:
        take = (rows - off) & size

        @pl.when(take != 0)
        def _():
            piece(off, size)
        off = off + take
        size //= 2


def _expert_kernel(meta_ref, xs_ref, w1_ref, w3_ref, w2_ref, yb_ref, xbuf_ref, xb_ref, ybuf_ref, zero_ref,
                   xsem, ysem, zsem):
    v = pl.program_id(0)
    f = pl.program_id(1)
    nf = pl.num_programs(1)
    valid = meta_ref[META_LANES + v]
    start = lambda copy: copy.start()
    wait = lambda copy: copy.wait()

    @pl.when(f == 0)
    def _():
        @pl.when(v == 0)
        def _():
            zero_ref[...] = jnp.zeros_like(zero_ref)
            _for_each_tail_piece(meta_ref, zero_ref, yb_ref, zsem, start)
            _for_each_tail_piece(meta_ref, zero_ref, yb_ref, zsem, wait)
            _for_each_x_piece(meta_ref, xs_ref, xbuf_ref, xsem, v, start)

        _for_each_x_piece(meta_ref, xs_ref, xbuf_ref, xsem, v, wait)

        @pl.when(v + 1 < MOE_NVIS)
        def _():
            _for_each_x_piece(meta_ref, xs_ref, xbuf_ref, xsem, v + 1, start)

        for p in range(MOE_BLOCK // MOE_ROWS):
            @pl.when(p * MOE_ROWS < valid)
            def _():
                piece = slice(p * MOE_ROWS, (p + 1) * MOE_ROWS)
                xb_ref[piece, :] = xbuf_ref[v % 2, piece, :].astype(BF16)

    def previous_writeback_done():
        @pl.when(v > 0)
        def _():
            _for_each_y_piece(meta_ref, ybuf_ref, yb_ref, ysem, v - 1, wait)

    @pl.when((f == 0) & (valid == 0))
    def _():
        previous_writeback_done()

    def compute(nrows):
        x = xb_ref[:nrows, :]
        a = _silu(_dot(x, w1_ref[0, 0].astype(BF16))) * _dot(x, w3_ref[0, 0].astype(BF16))
        y = _dot(a.astype(BF16), w2_ref[0, 0].astype(BF16))

        @pl.when(f == 0)
        def _():
            previous_writeback_done()
            ybuf_ref[:nrows, :] = y

        @pl.when(f > 0)
        def _():
            ybuf_ref[:nrows, :] = y + ybuf_ref[:nrows, :]

    for nrows in range(MOE_ROWS, MOE_BLOCK + 1, MOE_ROWS):
        @pl.when((valid > nrows - MOE_ROWS) & (valid <= nrows))
        def _():
            compute(nrows)

    @pl.when(f == nf - 1)
    def _():
        _for_each_y_piece(meta_ref, ybuf_ref, yb_ref, ysem, v, start)

        @pl.when(v == MOE_NVIS - 1)
        def _():
            _for_each_y_piece(meta_ref, ybuf_ref, yb_ref, ysem, v, wait)


def _experts(meta, xs, w1, w3, w2, layer):
    d = D_MODEL
    nf = D_EXPERT // MOE_FC
    n_vis = lambda meta: meta[META_LANES - 1]
    used = lambda v, meta: jnp.minimum(v, n_vis(meta) - 1)
    chunk = lambda v, f, meta: jnp.where(v < n_vis(meta), f, nf - 1)
    w_in_spec = pl.BlockSpec((1, 1, d, MOE_FC), lambda v, f, meta: (layer, meta[used(v, meta)], 0, chunk(v, f, meta)))
    w_out_spec = pl.BlockSpec((1, 1, MOE_FC, d), lambda v, f, meta: (layer, meta[used(v, meta)], chunk(v, f, meta), 0))
    return pl.pallas_call(
        _expert_kernel,
        out_shape=jax.ShapeDtypeStruct((MOE_SLOTS, d), F32),
        grid_spec=pltpu.PrefetchScalarGridSpec(
            num_scalar_prefetch=1,
            grid=(MOE_NVIS, nf),
            in_specs=[pl.BlockSpec(memory_space=pl.ANY), w_in_spec, w_in_spec, w_out_spec],
            out_specs=pl.BlockSpec(memory_space=pl.ANY),
            scratch_shapes=[pltpu.VMEM((2, MOE_BLOCK, d), F32), pltpu.VMEM((MOE_BLOCK, d), BF16),
                            pltpu.VMEM((MOE_BLOCK, d), F32), pltpu.VMEM((MOE_TAIL_PIECES[0], d), F32),
                            pltpu.SemaphoreType.DMA((2,)), pltpu.SemaphoreType.DMA, pltpu.SemaphoreType.DMA],
        ),
        compiler_params=_cparams(("arbitrary", "arbitrary")),
        name="moe_experts",
    )(meta, xs, w1, w3, w2)


def _combine_copy(dest_ref, yb_ref, buf_ref, sem, i, g, u, k):
    src = dest_ref[k * N_TOK + i * TOK_TM + g * ROW_GROUP + u]
    return pltpu.make_async_copy(yb_ref.at[pl.ds(src, 1)], buf_ref.at[k, g, pl.ds(u, 1)], sem)


def _combine_kernel(dest_ref, yb_ref, x_ref, gate_ref, w_ref, o_ref, buf_ref, sem):
    i = pl.program_id(0)

    def start(g, carry):
        for u in range(ROW_GROUP):
            for k in range(2):
                _combine_copy(dest_ref, yb_ref, buf_ref, sem, i, g, u, k).start()
        return carry

    def wait(g, carry):
        for u in range(ROW_GROUP):
            for k in range(2):
                _combine_copy(dest_ref, yb_ref, buf_ref, sem, i, g, u, k).wait()
        return carry

    lax.fori_loop(0, TOK_TM // ROW_GROUP, start, 0)
    lax.fori_loop(0, TOK_TM // ROW_GROUP, wait, 0)
    w = w_ref[...]
    rows = lambda k: buf_ref[k].reshape(TOK_TM, buf_ref.shape[-1])
    y = w[:, 0:1] * rows(0) + w[:, 1:2] * rows(1)
    o_ref[...] = x_ref[...] + gate_ref[0] * y


def _combine(dest_flat, yb, x2d, gate, w_tok):
    d = D_MODEL
    return pl.pallas_call(
        _combine_kernel,
        out_shape=jax.ShapeDtypeStruct((N_TOK, d), F32),
        grid_spec=pltpu.PrefetchScalarGridSpec(
            num_scalar_prefetch=1,
            grid=(N_TOK // TOK_TM,),
            in_specs=[pl.BlockSpec(memory_space=pl.ANY),
                      pl.BlockSpec((TOK_TM, d), lambda i, dest: (i, 0)),
                      pl.BlockSpec((1, 1, d), lambda i, dest: (i * TOK_TM // SEQ_LEN, 0, 0)),
                      pl.BlockSpec((TOK_TM, 2), lambda i, dest: (i, 0))],
            out_specs=pl.BlockSpec((TOK_TM, d), lambda i, dest: (i, 0)),
            scratch_shapes=[pltpu.VMEM((2, TOK_TM // ROW_GROUP, ROW_GROUP, d), F32), pltpu.SemaphoreType.DMA],
        ),
        compiler_params=_cparams(("arbitrary",)),
        name="moe_combine",
    )(dest_flat, yb, x2d, gate, w_tok)


def _moe(h2d, logits_t, x2d, gate, router_b, w1, w3, w2, layer):
    _, w_sel, dest, meta = _route(logits_t, router_b)
    dest_flat = dest.reshape(2 * N_TOK)
    meta_flat = meta.reshape(META_ROWS * META_LANES)
    xs = _dispatch(dest_flat, meta_flat, h2d)
    yb = _experts(meta_flat, xs, w1, w3, w2, layer)
    return _combine(dest_flat, yb, x2d, gate, w_sel.T)


def _rope_tables():
    f = HEAD_DIM // 4
    inv_freq = ROPE_THETA ** (-jnp.arange(f, dtype=F32) / f)
    pos = jnp.arange(SEQ_LEN)
    row = (pos // GRID_W).astype(F32)[:, None] * inv_freq
    col = (pos % GRID_W).astype(F32)[:, None] * inv_freq
    cos = jnp.concatenate([jnp.cos(row), jnp.cos(row), jnp.cos(col), jnp.cos(col)], axis=-1)
    sin = jnp.concatenate([-jnp.sin(row), jnp.sin(row), -jnp.sin(col), jnp.sin(col)], axis=-1)
    return cos, sin


def _cast_block_kernel(src_ref, x_ref, o_ref):
    del src_ref
    o_ref[...] = x_ref[...].astype(BF16)


def _permute_attn_w_in(w):
    widths = dict(aq=A_Q_DIM, ak=A_KV_DIM, av=A_KV_DIM, gq=GLA_K_DIM, gk=GLA_K_DIM, gv=GLA_V_DIM, gg=GLA_V_DIM)
    starts, col = {}, 0
    for name in ("aq", "ak", "av", "gq", "gk", "gv", "gg"):
        starts[name], col = col, col + widths[name]
    blk = A_KV_DIM
    src = [c // blk for name in ("aq", "gv", "gg", "gq", "gk", "ak", "av")
           for c in range(starts[name], starts[name] + widths[name], blk)]
    d = w.shape[0]
    main = pl.pallas_call(
        _cast_block_kernel,
        out_shape=jax.ShapeDtypeStruct((d, PROJ_DIM), BF16),
        grid_spec=pltpu.PrefetchScalarGridSpec(
            num_scalar_prefetch=1,
            grid=(len(src),),
            in_specs=[pl.BlockSpec((d, blk), lambda j, src: (0, src[j]))],
            out_specs=pl.BlockSpec((d, blk), lambda j, src: (0, j)),
        ),
        compiler_params=_cparams(("arbitrary",)),
        name="permute_cast_w_in",
    )(jnp.asarray(src, I32), w)
    lr = jnp.pad(w[:, PROJ_DIM:], ((0, 0), (0, LR_PAD - 2 * GLA_RANK)))
    return main, lr


def kernel(x, c, ctx, c_ctx, mod_w, mod_b, attn_w_in, attn_q_norm, attn_k_norm, attn_sink, gla_wa2, gla_ba,
           gla_norm, attn_w_out, conv_w_in, conv_w, conv_w_out, router_w, router_b, exp_w1, exp_w3, exp_w2):
    d = D_MODEL
    cc = jnp.concatenate([c, c_ctx[None], jnp.zeros((8 - N_BATCH - 1, d), F32)], axis=0)
    m = _modulation(cc, mod_w, mod_b)
    mods = [[m[l, :, i * d:(i + 1) * d].reshape(8, 1, d) for i in range(6)] for l in range(mod_w.shape[0])]
    lat_row = lambda i: i * PROJ_TM // SEQ_LEN
    ctx_row = lambda i: N_BATCH
    rw_pad = jnp.pad(router_w, ((0, 0), (0, ROUTE_LANES - N_EXPERTS)))
    rw_hi = rw_pad.astype(BF16)
    rwt = jnp.stack([rw_hi, (rw_pad - rw_hi.astype(F32)).astype(BF16)])
    x2d = x.reshape(N_TOK, d)

    w_main, w_lr = _permute_attn_w_in(attn_w_in[0])
    p_lat, lr_lat = _projection(x2d, mods[0][0], mods[0][1], w_main, w_lr, lat_row)
    p_ctx, lr_ctx = _projection(ctx.reshape(N_BATCH * CTX_LEN, d), mods[0][0], mods[0][1], w_main, w_lr, ctx_row)
    p_lat = p_lat.reshape(N_BATCH, SEQ_LEN, PROJ_DIM)
    p_ctx = p_ctx.reshape(N_BATCH, CTX_LEN, PROJ_DIM)
    cos, sin_signed = _rope_tables()
    attn = _attention(p_lat, p_ctx, cos, sin_signed, attn_q_norm[0][None], attn_k_norm[0][None], attn_sink[0])
    wa_pad = jnp.zeros((2, LR_PAD, GLA_K_DIM), F32)
    for di in range(2):
        wa_pad = wa_pad.at[di, di * GLA_RANK:(di + 1) * GLA_RANK].set(gla_wa2[0, di])
    wa_pad = wa_pad.astype(BF16)
    ba = gla_ba[0].reshape(2, 1, GLA_K_DIM)
    s_zero = jnp.zeros((N_BATCH, 2, GLA_HEADS, GLA_DV, GLA_DK), F32)
    s_ctx = _gla(p_ctx, lr_ctx.reshape(N_BATCH, CTX_LEN, LR_PAD), wa_pad, ba, None, s_zero, latent=False)
    gla = _gla(p_lat, lr_lat.reshape(N_BATCH, SEQ_LEN, LR_PAD), wa_pad, ba, gla_norm[0][None], s_ctx, latent=True)
    x1, h2, lg = _attn_out(attn.reshape(N_TOK, A_Q_DIM), gla.reshape(N_TOK, GLA_V_DIM), attn_w_out[0].astype(BF16),
                           x2d, mods[0][2], mods[0][3], mods[0][4], rwt)
    x2 = _moe(h2, lg, x1, mods[0][5], router_b, exp_w1, exp_w3, exp_w2, 0)

    g = _projection(x2, mods[1][0], mods[1][1], conv_w_in[0].astype(BF16), None, lat_row)
    x3, h2, lg = _conv_out(g, conv_w[0], conv_w_out[0].astype(BF16), x2, mods[1][2], mods[1][3], mods[1][4], rwt)
    x4 = _moe(h2, lg, x3, mods[1][5], router_b, exp_w1, exp_w3, exp_w2, 1)
    return x4.reshape(N_BATCH, SEQ_LEN, d)
```

```python
import functools

import jax
import jax.numpy as jnp
from jax import lax
from jax.experimental import pallas as pl
from jax.experimental.pallas import tpu as pltpu

F32 = jnp.float32
BF16 = jnp.bfloat16
I32 = jnp.int32

D_MODEL = 2048
N_BATCH = 4
SEQ_LEN = 2048
CTX_LEN = 256
N_TOK = N_BATCH * SEQ_LEN
GRID_W = 64
HEAD_DIM = 128
A_Q_HEADS = 8
A_KV_HEADS = 2
A_GROUP = A_Q_HEADS // A_KV_HEADS
WINDOW = 128
A_BLOCK = 128
ROPE_THETA = 10000.0
GLA_HEADS = 4
GLA_DK = 128
GLA_DV = 256
GLA_RANK = 16
GLA_TAU = 16.0
GLA_CHUNK = 64
N_EXPERTS = 32
N_GROUPS = 4
GROUP_SIZE = N_EXPERTS // N_GROUPS
D_EXPERT = 1024
EPS = 1e-6
NEG_INF = -1e30
A_Q_DIM = A_Q_HEADS * HEAD_DIM
A_KV_DIM = A_KV_HEADS * HEAD_DIM
GLA_K_DIM = GLA_HEADS * GLA_DK
GLA_V_DIM = GLA_HEADS * GLA_DV
PROJ_DIM = A_Q_DIM + 2 * A_KV_DIM + 2 * GLA_K_DIM + 2 * GLA_V_DIM
LR_PAD = 128

COL_AQ = 0
COL_GV = 1
COL_GG = 2
COL_GQ = 6
COL_GK = 7
COL_AK = 32
COL_AV = 34

MOD_TN = 1024
PROJ_TM = 512
PROJ_TN = 512
OUT_TM = 256
CONV_HALO = 16
GLA_ROWS = 512
ROUTE_TT = 512
ROUTE_LANES = 128
MOE_ALIGN = 8
MOE_BLOCK = 768
MOE_ROWS = 128
MOE_FC = 256
MOE_NVIS = N_EXPERTS + (2 * N_TOK) // MOE_BLOCK
MOE_TAIL_PIECES = (256, 128, 64, 32, 16, 8)
MOE_SLOTS = 2 * N_TOK + sum(MOE_TAIL_PIECES)
assert MOE_SLOTS >= 2 * N_TOK + N_EXPERTS * (MOE_ALIGN - 1) + MOE_ROWS
TOK_TM = 256
ROW_GROUP = 8
META_ROWS = 5
META_LANES = 128
VMEM_LIMIT = 56 * 1024 * 1024


def _cparams(sem):
    return pltpu.CompilerParams(dimension_semantics=sem, vmem_limit_bytes=VMEM_LIMIT)


def _silu(x):
    return x * jax.nn.sigmoid(x)


def _dot(a, b):
    return jnp.dot(a, b, preferred_element_type=F32)


def _dot_nt(a, b):
    return lax.dot_general(a, b, (((1,), (1,)), ((), ())), preferred_element_type=F32)


def _dot_tn(a, b):
    return lax.dot_general(a, b, (((0,), (0,)), ((), ())), preferred_element_type=F32)


def _rms_modulate(x, shift, scale):
    r = lax.rsqrt(jnp.mean(x * x, axis=-1, keepdims=True) + EPS)
    return (x * r) * (1.0 + scale) + shift


def _mod_kernel(cc_ref, w_ref, b_ref, o_ref):
    a = _silu(cc_ref[...])
    o_ref[0] = _dot(a.astype(BF16), w_ref[0].astype(BF16)) + b_ref[0]


def _modulation(cc, mod_w, mod_b):
    depth, d, n = mod_w.shape
    return pl.pallas_call(
        _mod_kernel,
        out_shape=jax.ShapeDtypeStruct((depth, 8, n), F32),
        grid=(depth, n // MOD_TN),
        in_specs=[
            pl.BlockSpec((8, d), lambda l, j: (0, 0)),
            pl.BlockSpec((1, d, MOD_TN), lambda l, j: (l, 0, j)),
            pl.BlockSpec((1, 1, MOD_TN), lambda l, j: (l, 0, j)),
        ],
        out_specs=pl.BlockSpec((1, 8, MOD_TN), lambda l, j: (l, 0, j)),
        compiler_params=_cparams(("arbitrary", "arbitrary")),
        name="adaln_modulation",
    )(cc, mod_w, mod_b.reshape(depth, 1, n))


def _proj_kernel(with_lr, x_ref, sh_ref, sc_ref, w_ref, *rest):
    if with_lr:
        wlr_ref, o_ref, olr_ref = rest
    else:
        (o_ref,) = rest
    hb = _rms_modulate(x_ref[...], sh_ref[0], sc_ref[0]).astype(BF16)
    for c in range(w_ref.shape[1] // PROJ_TN):
        cols = slice(c * PROJ_TN, (c + 1) * PROJ_TN)
        o_ref[:, cols] = _dot(hb, w_ref[:, cols]).astype(BF16)
    if with_lr:
        olr_ref[...] = _dot(hb, wlr_ref[...].astype(BF16)).astype(BF16)


def _projection(x2d, shift, scale, w, w_lr, mod_row):
    rows, d = x2d.shape
    n = w.shape[1]
    with_lr = w_lr is not None
    resident = lambda shape: pl.BlockSpec(shape, lambda i: (0, 0), pipeline_mode=pl.Buffered(1))
    in_specs = [
        pl.BlockSpec((PROJ_TM, d), lambda i: (i, 0)),
        pl.BlockSpec((1, 1, d), lambda i: (mod_row(i), 0, 0)),
        pl.BlockSpec((1, 1, d), lambda i: (mod_row(i), 0, 0)),
        resident((d, n)),
    ]
    out_shape = [jax.ShapeDtypeStruct((rows, n), BF16)]
    out_specs = [pl.BlockSpec((PROJ_TM, n), lambda i: (i, 0))]
    args = [x2d, shift, scale, w]
    if with_lr:
        in_specs.append(resident((d, LR_PAD)))
        out_shape.append(jax.ShapeDtypeStruct((rows, LR_PAD), BF16))
        out_specs.append(pl.BlockSpec((PROJ_TM, LR_PAD), lambda i: (i, 0)))
        args.append(w_lr)
    res = pl.pallas_call(
        functools.partial(_proj_kernel, with_lr),
        out_shape=out_shape,
        grid=(rows // PROJ_TM,),
        in_specs=in_specs,
        out_specs=out_specs,
        compiler_params=_cparams(("arbitrary",)),
        name="norm_mod_projection",
    )(*args)
    return res if with_lr else res[0]


def _swap_halves32(x):
    lane = lax.broadcasted_iota(I32, x.shape, 1)
    return jnp.where((lane & 63) < 32, pltpu.roll(x, 96, 1), pltpu.roll(x, 32, 1))


def _qk_norm(x, gain):
    return x * lax.rsqrt(jnp.mean(x * x, axis=-1, keepdims=True) + EPS) * gain


def _rope(x, cos, sin_signed):
    return x * cos + _swap_halves32(x) * sin_signed


ATT_WIN0 = CTX_LEN
ATT_LAT0 = CTX_LEN + A_BLOCK
ATT_ROWS = CTX_LEN + SEQ_LEN + 2 * A_BLOCK
ATT_WIN = 3 * A_BLOCK


def _attn_kernel(sink_ref, q_ref, k_ref, v_ref, kx_ref, vx_ref, cos_ref, sin_ref, qg_ref, kg_ref, o_ref,
                 kn_ref, vn_ref, band_ref):
    kvh = pl.program_id(1)
    nb = SEQ_LEN // A_BLOCK
    k_gain = kg_ref[...]
    q_gain = qg_ref[...]

    pad = jnp.zeros((A_BLOCK, HEAD_DIM), BF16)
    kn_ref[0:CTX_LEN, :] = _qk_norm(kx_ref[0].astype(F32), k_gain).astype(BF16)
    vn_ref[0:CTX_LEN, :] = vx_ref[0]
    for ref in (kn_ref, vn_ref):
        ref[ATT_WIN0:ATT_LAT0, :] = pad
        ref[ATT_LAT0 + SEQ_LEN:, :] = pad
    vn_ref[ATT_LAT0:ATT_LAT0 + SEQ_LEN, :] = v_ref[0]
    prep_rows = 4 * A_BLOCK
    for c in range(SEQ_LEN // prep_rows):
        r = slice(c * prep_rows, (c + 1) * prep_rows)
        kc = _rope(_qk_norm(k_ref[0, r, :].astype(F32), k_gain), cos_ref[r, :], sin_ref[r, :])
        kn_ref[ATT_LAT0 + c * prep_rows:ATT_LAT0 + (c + 1) * prep_rows, :] = kc.astype(BF16)

    qi = lax.broadcasted_iota(I32, (A_BLOCK, ATT_WIN), 0)
    wj = lax.broadcasted_iota(I32, (A_BLOCK, ATT_WIN), 1)
    band = jnp.where(jnp.abs(wj - A_BLOCK - qi) <= WINDOW, 1.0, 0.0)
    band_ref[0] = jnp.where(wj >= A_BLOCK, band, 0.0)
    band_ref[1] = band
    band_ref[2] = jnp.where(wj < 2 * A_BLOCK, band, 0.0)

    rows = A_GROUP * A_BLOCK
    rcol = lax.broadcasted_iota(I32, (rows, 1), 0)
    sink = jnp.zeros((rows, 1), F32)
    for g in range(A_GROUP):
        sink = jnp.where((rcol >= g * A_BLOCK) & (rcol < (g + 1) * A_BLOCK), sink_ref[kvh * A_GROUP + g], sink)

    def query_block(n):
        qrows = pl.ds(pl.multiple_of(n * A_BLOCK, A_BLOCK), A_BLOCK)
        wrows = pl.ds(pl.multiple_of(ATT_WIN0 + n * A_BLOCK, A_BLOCK), ATT_WIN)
        cos_q, sin_q = cos_ref[qrows, :], sin_ref[qrows, :]
        q = q_ref[0, qrows, :].astype(F32)
        qs = []
        for g in range(A_GROUP):
            qh = _rope(_qk_norm(q[:, g * HEAD_DIM:(g + 1) * HEAD_DIM], q_gain), cos_q, sin_q)
            qs.append((qh * HEAD_DIM ** -0.5).astype(BF16))
        qs = jnp.concatenate(qs, axis=0)
        mask = band_ref[jnp.where(n == 0, 0, jnp.where(n == nb - 1, 2, 1))]
        mask = jnp.concatenate([mask] * A_GROUP, axis=0)
        s_ctx = _dot_nt(qs, kn_ref[0:CTX_LEN, :])
        s_win = jnp.where(mask > 0.5, _dot_nt(qs, kn_ref[wrows, :]), NEG_INF)
        m = jnp.maximum(jnp.max(s_ctx, axis=-1, keepdims=True), jnp.max(s_win, axis=-1, keepdims=True))
        m = jnp.maximum(m, sink)
        p_ctx = jnp.exp(s_ctx - m)
        p_win = jnp.exp(s_win - m)
        denom = (jnp.sum(p_ctx, axis=-1, keepdims=True) + jnp.sum(p_win, axis=-1, keepdims=True)
                 + jnp.exp(sink - m))
        o = (_dot(p_ctx.astype(BF16), vn_ref[0:CTX_LEN, :]) + _dot(p_win.astype(BF16), vn_ref[wrows, :])) / denom
        for g in range(A_GROUP):
            o_ref[0, qrows, g * HEAD_DIM:(g + 1) * HEAD_DIM] = o[g * A_BLOCK:(g + 1) * A_BLOCK].astype(BF16)

    def query_block_pair(i, carry):
        query_block(2 * i)
        query_block(2 * i + 1)
        return carry

    lax.fori_loop(0, nb // 2, query_block_pair, 0)


def _attention(p_lat, p_ctx, cos, sin_signed, q_gain, k_gain, sink):
    gw = A_GROUP * HEAD_DIM
    lat = lambda width, col: pl.BlockSpec((1, SEQ_LEN, width), lambda b, k, s: (b, 0, col + k))
    ctx_blk = lambda col: pl.BlockSpec((1, CTX_LEN, HEAD_DIM), lambda b, k, s: (b, 0, col + k))
    full = lambda shape: pl.BlockSpec(shape, lambda b, k, s: (0,) * len(shape))
    return pl.pallas_call(
        _attn_kernel,
        out_shape=jax.ShapeDtypeStruct((N_BATCH, SEQ_LEN, A_Q_DIM), BF16),
        grid_spec=pltpu.PrefetchScalarGridSpec(
            num_scalar_prefetch=1,
            grid=(N_BATCH, A_KV_HEADS),
            in_specs=[
                lat(gw, COL_AQ), lat(HEAD_DIM, COL_AK), lat(HEAD_DIM, COL_AV),
                ctx_blk(COL_AK), ctx_blk(COL_AV),
                full((SEQ_LEN, HEAD_DIM)), full((SEQ_LEN, HEAD_DIM)),
                full((1, HEAD_DIM)), full((1, HEAD_DIM)),
            ],
            out_specs=pl.BlockSpec((1, SEQ_LEN, gw), lambda b, k, s: (b, 0, k)),
            scratch_shapes=[pltpu.VMEM((ATT_ROWS, HEAD_DIM), BF16), pltpu.VMEM((ATT_ROWS, HEAD_DIM), BF16),
                            pltpu.VMEM((3, A_BLOCK, ATT_WIN), F32)],
        ),
        compiler_params=_cparams(("arbitrary", "arbitrary")),
        name="windowed_sink_attention",
    )(sink, p_lat, p_lat, p_lat, p_ctx, p_ctx, cos, sin_signed, q_gain, k_gain)


def _split3_bf16(x):
    hi = x.astype(BF16)
    r1 = x - hi.astype(F32)
    mid = r1.astype(BF16)
    lo = (r1 - mid.astype(F32)).astype(BF16)
    return hi, mid, lo


def _gla_kernel(latent, nblk, q_ref, k_ref, v_ref, *rest):
    if latent:
        g_ref, lr_ref, wa_ref, ba_ref, gain_ref, s0_ref, o_ref, st_ref, ofwd_ref = rest
    else:
        lr_ref, wa_ref, ba_ref, s0_ref, o_ref, st_ref = rest
    d = pl.program_id(1)
    j = pl.program_id(2)
    blk_rows = k_ref.shape[1]
    nc = blk_rows // GLA_CHUNK
    blk = jnp.where(d == 0, j, nblk - 1 - j)

    @pl.when(j == 0)
    def _():
        st_ref[...] = s0_ref[0, 0]

    ii = lax.broadcasted_iota(I32, (GLA_CHUNK, GLA_CHUNK), 0)
    jj = lax.broadcasted_iota(I32, (GLA_CHUNK, GLA_CHUNK), 1)

    def scan_block(backward):
        incl = (jj >= ii) if backward else (jj <= ii)
        incl_b = jnp.where(incl, 1.0, 0.0).astype(BF16)
        for i in range(nc):
            r0 = (nc - 1 - i if backward else i) * GLA_CHUNK
            rows = slice(r0, r0 + GLA_CHUNK)
            z = _dot(lr_ref[0, rows, :], wa_ref[0]) + ba_ref[0]
            la = (jnp.minimum(z, 0.0) - jnp.log(1.0 + jnp.exp(-jnp.abs(z)))) / GLA_TAU
            hi, mid, lo = _split3_bf16(la)
            bcum = _dot(incl_b, hi) + _dot(incl_b, mid) + _dot(incl_b, lo)
            blast = jnp.sum(la, axis=0, keepdims=True)
            k = k_ref[0, rows, :].astype(F32)
            kl = (k * jnp.exp(blast - bcum)).astype(BF16)
            decay = jnp.exp(blast)
            v = v_ref[0, rows, :]
            if latent:
                qf = (q_ref[0, rows, :].astype(F32) * GLA_DK ** -0.5 * jnp.exp(bcum)).astype(BF16)
                kf = (k * jnp.exp(-bcum)).astype(BF16)
            for h in range(GLA_HEADS):
                ks = slice(h * GLA_DK, (h + 1) * GLA_DK)
                vs = slice(h * GLA_DV, (h + 1) * GLA_DV)
                st = st_ref[h]
                if latent:
                    att = jnp.where(incl, _dot_nt(qf[:, ks], kf[:, ks]), 0.0)
                    o = _dot(att.astype(BF16), v[:, vs]) + _dot_nt(qf[:, ks], st.astype(BF16))
                    orow = pl.ds(pl.multiple_of(blk * blk_rows + r0, GLA_CHUNK), GLA_CHUNK)
                    if backward:
                        ot = o + ofwd_ref[orow, vs]
                        on = ot * lax.rsqrt(jnp.mean(ot * ot, axis=-1, keepdims=True) + EPS) * gain_ref[...]
                        o_ref[0, rows, vs] = (on * _silu(g_ref[0, rows, vs].astype(F32))).astype(BF16)
                    else:
                        ofwd_ref[orow, vs] = o
                st_ref[h] = st * decay[:, ks] + _dot_tn(v[:, vs], kl[:, ks])

    for backward in (False, True):
        @pl.when(d == int(backward))
        def _():
            scan_block(backward)

    if not latent:
        @pl.when(j == nblk - 1)
        def _():
            o_ref[0, 0] = st_ref[...]


def _gla(p3, lr3, wa_pad, ba, o_gain, s0, latent):
    n = p3.shape[1]
    blk_rows = min(GLA_ROWS, n)
    nblk = n // blk_rows
    seq_blk = lambda d, j: jnp.where(d == 0, j, nblk - 1 - j)
    in_specs = [
        pl.BlockSpec((1, blk_rows, GLA_K_DIM), lambda b, d, j: (b, seq_blk(d, j), COL_GQ)),
        pl.BlockSpec((1, blk_rows, GLA_K_DIM), lambda b, d, j: (b, seq_blk(d, j), COL_GK)),
        pl.BlockSpec((1, blk_rows, GLA_V_DIM), lambda b, d, j: (b, seq_blk(d, j), COL_GV)),
    ]
    args = [p3, p3, p3]
    if latent:
        in_specs.append(pl.BlockSpec((1, blk_rows, GLA_V_DIM), lambda b, d, j: (b, seq_blk(d, j), COL_GG)))
        args.append(p3)
    in_specs += [
        pl.BlockSpec((1, blk_rows, LR_PAD), lambda b, d, j: (b, seq_blk(d, j), 0)),
        pl.BlockSpec((1, LR_PAD, GLA_K_DIM), lambda b, d, j: (d, 0, 0)),
        pl.BlockSpec((1, 1, GLA_K_DIM), lambda b, d, j: (d, 0, 0)),
    ]
    args += [lr3, wa_pad, ba]
    if latent:
        in_specs.append(pl.BlockSpec((1, GLA_DV), lambda b, d, j: (0, 0)))
        args.append(o_gain)
    state_spec = pl.BlockSpec((1, 1, GLA_HEADS, GLA_DV, GLA_DK), lambda b, d, j: (b, d, 0, 0, 0))
    in_specs.append(state_spec)
    args.append(s0)
    scratch = [pltpu.VMEM((GLA_HEADS, GLA_DV, GLA_DK), F32)]
    if latent:
        out_shape = jax.ShapeDtypeStruct((N_BATCH, n, GLA_V_DIM), BF16)
        out_spec = pl.BlockSpec((1, blk_rows, GLA_V_DIM),
                                lambda b, d, j: (b, jnp.where(d == 0, nblk - 1, nblk - 1 - j), 0))
        scratch.append(pltpu.VMEM((n, GLA_V_DIM), F32))
    else:
        out_shape = jax.ShapeDtypeStruct(s0.shape, F32)
        out_spec = state_spec
    return pl.pallas_call(
        functools.partial(_gla_kernel, latent, nblk),
        out_shape=out_shape,
        grid=(N_BATCH, 2, nblk),
        in_specs=in_specs,
        out_specs=out_spec,
        scratch_shapes=scratch,
        compiler_params=_cparams(("arbitrary", "arbitrary", "arbitrary")),
        name="gla_latent" if latent else "gla_context_state",
    )(*args)


def _router_logits(h, rw_ref):
    hh = h.astype(BF16)
    hl = (h - hh.astype(F32)).astype(BF16)
    return _dot(hh, rw_ref[0]) + _dot(hl, rw_ref[0]) + _dot(hh, rw_ref[1])


def _residual_and_moe_input(mix, x_ref, gate_ref, sh_ref, sc_ref, rwt_ref, x_out_ref, h_out_ref, lg_ref):
    x1 = x_ref[...] + gate_ref[0] * mix
    x_out_ref[...] = x1
    h = _rms_modulate(x1, sh_ref[0], sc_ref[0])
    h_out_ref[...] = h
    lg_ref[...] = _router_logits(h, rwt_ref)


def _attn_out_kernel(a1_ref, a2_ref, w_ref, x_ref, gate_ref, sh_ref, sc_ref, rwt_ref, x_out_ref, h_out_ref, lg_ref):
    mix = _dot(a1_ref[...], w_ref[0:A_Q_DIM, :]) + _dot(a2_ref[...], w_ref[A_Q_DIM:, :])
    _residual_and_moe_input(mix, x_ref, gate_ref, sh_ref, sc_ref, rwt_ref, x_out_ref, h_out_ref, lg_ref)


def _conv_out_kernel(gb_ref, gc_ref, u_ref, gcp_ref, up_ref, gcn_ref, un_ref, cw_ref, w_ref,
                     x_ref, gate_ref, sh_ref, sc_ref, rwt_ref, x_out_ref, h_out_ref, lg_ref):
    i = pl.program_id(0)
    tiles_per_seq = SEQ_LEN // OUT_TM
    t = gc_ref[...].astype(F32) * u_ref[...].astype(F32)
    first = (i % tiles_per_seq) == 0
    last = (i % tiles_per_seq) == tiles_per_seq - 1
    halo_last = slice(CONV_HALO - 1, CONV_HALO)
    t_before = jnp.where(first, 0.0, gcp_ref[halo_last, :].astype(F32) * up_ref[halo_last, :].astype(F32))
    t_after = jnp.where(last, 0.0, gcn_ref[0:1, :].astype(F32) * un_ref[0:1, :].astype(F32))
    row = lax.broadcasted_iota(I32, t.shape, 0)
    t_up = jnp.where(row == 0, t_before, pltpu.roll(t, 1, 0))
    t_dn = jnp.where(row == OUT_TM - 1, t_after, pltpu.roll(t, OUT_TM - 1, 0))
    y = cw_ref[0:1, :] * t_up + cw_ref[1:2, :] * t + cw_ref[2:3, :] * t_dn
    mix = _dot((gb_ref[...].astype(F32) * y).astype(BF16), w_ref[...])
    _residual_and_moe_input(mix, x_ref, gate_ref, sh_ref, sc_ref, rwt_ref, x_out_ref, h_out_ref, lg_ref)


def _mixer_out(kernel_fn, mixer_specs, mixer_args, x2d, gate, shift, scale, rwt, name):
    d = D_MODEL
    mod_spec = pl.BlockSpec((1, 1, d), lambda i: (i * OUT_TM // SEQ_LEN, 0, 0))
    row_spec = pl.BlockSpec((OUT_TM, d), lambda i: (i, 0))
    return pl.pallas_call(
        kernel_fn,
        out_shape=[jax.ShapeDtypeStruct((N_TOK, d), F32), jax.ShapeDtypeStruct((N_TOK, d), F32),
                   jax.ShapeDtypeStruct((N_TOK, ROUTE_LANES), F32)],
        grid=(N_TOK // OUT_TM,),
        in_specs=mixer_specs + [row_spec, mod_spec, mod_spec, mod_spec,
                                pl.BlockSpec((2, d, ROUTE_LANES), lambda i: (0, 0, 0))],
        out_specs=[row_spec, row_spec, pl.BlockSpec((OUT_TM, ROUTE_LANES), lambda i: (i, 0))],
        compiler_params=_cparams(("arbitrary",)),
        name=name,
    )(*mixer_args, x2d, gate, shift, scale, rwt)


def _attn_out(attn2d, gla2d, w_out, x2d, gate, shift, scale, rwt):
    half = pl.BlockSpec((OUT_TM, A_Q_DIM), lambda i: (i, 0))
    specs = [half, half, pl.BlockSpec((D_MODEL, D_MODEL), lambda i: (0, 0))]
    return _mixer_out(_attn_out_kernel, specs, [attn2d, gla2d, w_out], x2d, gate, shift, scale, rwt,
                      "attn_out_projection")


def _conv_out(g2d, conv_w, w_out, x2d, gate, shift, scale, rwt):
    d = D_MODEL
    sub = OUT_TM // CONV_HALO
    last_halo = N_TOK // CONV_HALO - 1
    main = lambda col: pl.BlockSpec((OUT_TM, d), lambda i: (i, col))
    before = lambda col: pl.BlockSpec((CONV_HALO, d), lambda i: (jnp.maximum(i * sub - 1, 0), col))
    after = lambda col: pl.BlockSpec((CONV_HALO, d), lambda i: (jnp.minimum((i + 1) * sub, last_halo), col))
    specs = [main(0), main(1), main(2), before(1), before(2), after(1), after(2),
             pl.BlockSpec((3, d), lambda i: (0, 0)), pl.BlockSpec((d, d), lambda i: (0, 0))]
    return _mixer_out(_conv_out_kernel, specs, [g2d] * 7 + [conv_w, w_out], x2d, gate, shift, scale, rwt,
                      "conv_out_projection")


def _first_argmax8(x, idx8):
    m = jnp.max(x, axis=0, keepdims=True)
    a = jnp.min(jnp.where(x == m, idx8, float(GROUP_SIZE)), axis=0, keepdims=True)
    return m, a


def _route_kernel(lg_ref, rb_ref, e_ref, w_ref, dest_ref, meta_ref, cnt_ref, carry_ref, pstart_ref):
    phase = pl.program_id(0)
    j = pl.program_id(1)
    tt = ROUTE_TT
    sc = jax.nn.sigmoid(lg_ref[...].T[:N_EXPERTS])
    grp = sc + rb_ref[...]
    idx8 = lax.broadcasted_iota(I32, (GROUP_SIZE, tt), 0).astype(F32)
    groups = [grp[g * GROUP_SIZE:(g + 1) * GROUP_SIZE] for g in range(N_GROUPS)]
    gscore = []
    for x in groups:
        m1, a1 = _first_argmax8(x, idx8)
        m2, _ = _first_argmax8(jnp.where(idx8 == a1, -jnp.inf, x), idx8)
        gscore.append(m1 + m2)
    gmax = functools.reduce(jnp.maximum, gscore)
    gsel = jnp.full((1, tt), float(N_GROUPS), F32)
    for g in reversed(range(N_GROUPS)):
        gsel = jnp.where(gscore[g] == gmax, float(g), gsel)
    in_grp = groups[0]
    for g in range(1, N_GROUPS):
        in_grp = jnp.where(gsel == float(g), groups[g], in_grp)
    _, a1 = _first_argmax8(in_grp, idx8)
    _, a2 = _first_argmax8(jnp.where(idx8 == a1, -jnp.inf, in_grp), idx8)
    e0 = gsel * GROUP_SIZE + a1
    e1 = gsel * GROUP_SIZE + a2
    idx32 = lax.broadcasted_iota(I32, (N_EXPERTS, tt), 0).astype(F32)
    oh0 = idx32 == e0
    oh1 = idx32 == e1
    w0 = jnp.sum(jnp.where(oh0, sc, 0.0), axis=0, keepdims=True)
    w1 = jnp.sum(jnp.where(oh1, sc, 0.0), axis=0, keepdims=True)
    wsum = w0 + w1
    ohs = jnp.where(oh0 | oh1, 1.0, 0.0)
    tile_cnt = jnp.sum(ohs, axis=1, keepdims=True)

    @pl.when(phase == 0)
    def _():
        @pl.when(j == 0)
        def _():
            cnt_ref[...] = jnp.zeros_like(cnt_ref)
        cnt_ref[...] += jnp.broadcast_to(tile_cnt, cnt_ref.shape)

    @pl.when(phase == 1)
    def _():
        @pl.when(j == 0)
        def _():
            cnt = cnt_ref[...]
            expert = lax.broadcasted_iota(I32, cnt.shape, 0)
            lane = lax.broadcasted_iota(I32, cnt.shape, 1).astype(F32)

            def exclusive_cumsum(v):
                inc = v
                for s in (1, 2, 4, 8, 16):
                    inc = inc + jnp.where(expert >= s, pltpu.roll(inc, s, 0), 0.0)
                return inc - v

            cnt_pad = jnp.floor((cnt + (MOE_ALIGN - 1)) * (1.0 / MOE_ALIGN)) * MOE_ALIGN
            pstart = exclusive_cumsum(cnt_pad)
            pstart_ref[...] = pstart
            carry_ref[...] = jnp.zeros_like(carry_ref)
            nvis_e = jnp.zeros_like(cnt)
            for kb in range(-(-N_TOK // MOE_BLOCK)):
                nvis_e += jnp.where(cnt > float(kb * MOE_BLOCK), 1.0, 0.0)
            vstart = exclusive_cumsum(nvis_e)
            vis_exp = jnp.sum(jnp.where(lane >= vstart + nvis_e, 1.0, 0.0), axis=0, keepdims=True)
            vis_exp = jnp.minimum(vis_exp, float(N_EXPERTS - 1))
            own = expert.astype(F32) == vis_exp
            of_visit = lambda col: jnp.sum(jnp.where(own, col, 0.0), axis=0, keepdims=True)
            kth = lane[0:1] - of_visit(vstart)
            vis_rows = jnp.clip(of_visit(cnt_pad) - kth * MOE_BLOCK, 0.0, float(MOE_BLOCK))
            vis_row0 = of_visit(pstart) + kth * MOE_BLOCK
            last_lane = lane[0:1] == float(META_LANES - 1)
            expert_lane = expert.astype(F32) == lane
            to_lane = lambda col: jnp.sum(jnp.where(expert_lane, col, 0.0), axis=0, keepdims=True)
            meta_ref[...] = jnp.concatenate([
                jnp.where(last_lane, jnp.sum(nvis_e, axis=0, keepdims=True), vis_exp),
                vis_rows,
                vis_row0,
                to_lane(pstart + cnt),
                jnp.where(last_lane, jnp.sum(cnt_pad, axis=0, keepdims=True), to_lane(cnt_pad - cnt)),
            ], axis=0).astype(I32)

        rr = lax.broadcasted_iota(I32, (tt, tt), 0)
        cc = lax.broadcasted_iota(I32, (tt, tt), 1)
        strict_upper = jnp.where(rr < cc, 1.0, 0.0).astype(BF16)
        before = _dot(ohs.astype(BF16), strict_upper)
        pos = before + carry_ref[:, 0:1] + pstart_ref[:, 0:1]
        d0 = jnp.sum(jnp.where(oh0, pos, 0.0), axis=0, keepdims=True)
        d1 = jnp.sum(jnp.where(oh1, pos, 0.0), axis=0, keepdims=True)
        carry_ref[...] += jnp.broadcast_to(tile_cnt, carry_ref.shape)
        e_ref[...] = jnp.concatenate([e0, e1], axis=0).astype(I32)
        w_ref[...] = jnp.concatenate([w0 / wsum, w1 / wsum], axis=0)
        dest_ref[...] = jnp.concatenate([d0, d1], axis=0).astype(I32)


def _route(logits_t, router_b):
    nt = N_TOK // ROUTE_TT
    tok = lambda dt: jax.ShapeDtypeStruct((2, N_TOK), dt)
    tok_spec = pl.BlockSpec((2, ROUTE_TT), lambda p, j: (0, j * p))
    return pl.pallas_call(
        _route_kernel,
        out_shape=[tok(I32), tok(F32), tok(I32), jax.ShapeDtypeStruct((META_ROWS, META_LANES), I32)],
        grid=(2, nt),
        in_specs=[pl.BlockSpec((ROUTE_TT, ROUTE_LANES), lambda p, j: (j, 0)),
                  pl.BlockSpec((N_EXPERTS, 1), lambda p, j: (0, 0))],
        out_specs=[tok_spec, tok_spec, tok_spec, pl.BlockSpec((META_ROWS, META_LANES), lambda p, j: (0, 0))],
        scratch_shapes=[pltpu.VMEM((N_EXPERTS, 128), F32), pltpu.VMEM((N_EXPERTS, 128), F32),
                        pltpu.VMEM((N_EXPERTS, 128), F32)],
        compiler_params=_cparams(("arbitrary", "arbitrary")),
        name="router_slot_assignment",
    )(logits_t, router_b.reshape(N_EXPERTS, 1))


def _dispatch_copy(dest_ref, h_ref, xs_ref, sem, i, g, u, k):
    dst = dest_ref[k * N_TOK + i * TOK_TM + g * ROW_GROUP + u]
    return pltpu.make_async_copy(h_ref.at[g, pl.ds(u, 1)], xs_ref.at[pl.ds(dst, 1)], sem)


def _for_each_tail_piece(meta_ref, zero_ref, slots_ref, sem, fn):
    used = meta_ref[4 * META_LANES + META_LANES - 1]
    tail = MOE_SLOTS - used
    row = used
    for size in MOE_TAIL_PIECES:
        take = tail & size

        @pl.when(take != 0)
        def _():
            fn(pltpu.make_async_copy(zero_ref.at[pl.ds(0, size)],
                                     slots_ref.at[pl.ds(pl.multiple_of(row, MOE_ALIGN), size)], sem))
        row = row + take


def _for_each_padding_row(meta_ref, zero_ref, xs_ref, sem, fn):
    def expert_padding(e, carry):
        row = meta_ref[3 * META_LANES + e]
        npad = meta_ref[4 * META_LANES + e]
        for r in range(MOE_ALIGN - 1):
            @pl.when(r < npad)
            def _():
                fn(pltpu.make_async_copy(zero_ref.at[pl.ds(0, 1)], xs_ref.at[pl.ds(row + r, 1)], sem))
        return carry

    lax.fori_loop(0, N_EXPERTS, expert_padding, 0)


def _zero_fill_padding(meta_ref, zero_ref, xs_ref, sem):
    zero_ref[...] = jnp.zeros_like(zero_ref)
    for fn in (lambda copy: copy.start(), lambda copy: copy.wait()):
        _for_each_padding_row(meta_ref, zero_ref, xs_ref, sem, fn)
        _for_each_tail_piece(meta_ref, zero_ref, xs_ref, sem, fn)


def _dispatch_kernel(dest_ref, meta_ref, h_ref, xs_ref, zero_ref, sem):
    i = pl.program_id(0)

    @pl.when(i == 0)
    def _():
        _zero_fill_padding(meta_ref, zero_ref, xs_ref, sem)

    def start(g, carry):
        for u in range(ROW_GROUP):
            for k in range(2):
                _dispatch_copy(dest_ref, h_ref, xs_ref, sem, i, g, u, k).start()
        return carry

    def wait(g, carry):
        for u in range(ROW_GROUP):
            for k in range(2):
                _dispatch_copy(dest_ref, h_ref, xs_ref, sem, i, g, u, k).wait()
        return carry

    lax.fori_loop(0, TOK_TM // ROW_GROUP, start, 0)
    lax.fori_loop(0, TOK_TM // ROW_GROUP, wait, 0)


def _dispatch(dest_flat, meta_flat, h2d):
    d = D_MODEL
    return pl.pallas_call(
        _dispatch_kernel,
        out_shape=jax.ShapeDtypeStruct((MOE_SLOTS, d), F32),
        grid_spec=pltpu.PrefetchScalarGridSpec(
            num_scalar_prefetch=2,
            grid=(N_TOK // TOK_TM,),
            in_specs=[pl.BlockSpec((TOK_TM // ROW_GROUP, ROW_GROUP, d), lambda i, dest, meta: (i, 0, 0))],
            out_specs=pl.BlockSpec(memory_space=pl.ANY),
            scratch_shapes=[pltpu.VMEM((MOE_TAIL_PIECES[0], d), F32), pltpu.SemaphoreType.DMA],
        ),
        compiler_params=_cparams(("arbitrary",)),
        name="moe_dispatch",
    )(dest_flat, meta_flat, h2d.reshape(N_TOK // ROW_GROUP, ROW_GROUP, d))


def _for_each_x_piece(meta_ref, xs_ref, xbuf_ref, sem, v, fn):
    rows = meta_ref[META_LANES + v]
    row0 = meta_ref[2 * META_LANES + v]
    slot = v % 2
    for p in range(MOE_BLOCK // MOE_ROWS):
        @pl.when(p * MOE_ROWS < rows)
        def _():
            src = xs_ref.at[pl.ds(pl.multiple_of(row0 + p * MOE_ROWS, MOE_ALIGN), MOE_ROWS)]
            fn(pltpu.make_async_copy(src, xbuf_ref.at[slot, pl.ds(p * MOE_ROWS, MOE_ROWS)], sem.at[slot]))


def _for_each_y_piece(meta_ref, ybuf_ref, yb_ref, sem, v, fn):
    rows = meta_ref[META_LANES + v]
    row0 = meta_ref[2 * META_LANES + v]

    def piece(off, size):
        dst = yb_ref.at[pl.ds(pl.multiple_of(row0 + off, MOE_ALIGN), size)]
        fn(pltpu.make_async_copy(ybuf_ref.at[pl.ds(pl.multiple_of(off, MOE_ALIGN), size)], dst, sem))

    for p in range(MOE_BLOCK // MOE_ROWS):
        @pl.when((p + 1) * MOE_ROWS <= rows)
        def _():
            piece(p * MOE_ROWS, MOE_ROWS)
    off = (rows // MOE_ROWS) * MOE_ROWS
    size = MOE_ROWS // 2
    while size >= MOE_ALIGN:
        take = (rows - off) & size

        @pl.when(take != 0)
        def _():
            piece(off, size)
        off = off + take
        size //= 2


def _expert_kernel(meta_ref, xs_ref, w1_ref, w3_ref, w2_ref, yb_ref, xbuf_ref, xb_ref, ybuf_ref, zero_ref,
                   xsem, ysem, zsem):
    v = pl.program_id(0)
    f = pl.program_id(1)
    nf = pl.num_programs(1)
    valid = meta_ref[META_LANES + v]
    start = lambda copy: copy.start()
    wait = lambda copy: copy.wait()

    @pl.when(f == 0)
    def _():
        @pl.when(v == 0)
        def _():
            zero_ref[...] = jnp.zeros_like(zero_ref)
            _for_each_tail_piece(meta_ref, zero_ref, yb_ref, zsem, start)
            _for_each_tail_piece(meta_ref, zero_ref, yb_ref, zsem, wait)
            _for_each_x_piece(meta_ref, xs_ref, xbuf_ref, xsem, v, start)

        _for_each_x_piece(meta_ref, xs_ref, xbuf_ref, xsem, v, wait)

        @pl.when(v + 1 < pl.num_programs(0))
        def _():
            _for_each_x_piece(meta_ref, xs_ref, xbuf_ref, xsem, v + 1, start)

        for p in range(MOE_BLOCK // MOE_ROWS):
            @pl.when(p * MOE_ROWS < valid)
            def _():
                piece = slice(p * MOE_ROWS, (p + 1) * MOE_ROWS)
                xb_ref[piece, :] = xbuf_ref[v % 2, piece, :].astype(BF16)

    def previous_writeback_done():
        @pl.when(v > 0)
        def _():
            _for_each_y_piece(meta_ref, ybuf_ref, yb_ref, ysem, v - 1, wait)

    @pl.when((f == 0) & (valid == 0))
    def _():
        previous_writeback_done()

    def compute(nrows):
        x = xb_ref[:nrows, :]
        a = _silu(_dot(x, w1_ref[0, 0].astype(BF16))) * _dot(x, w3_ref[0, 0].astype(BF16))
        y = _dot(a.astype(BF16), w2_ref[0, 0].astype(BF16))

        @pl.when(f == 0)
        def _():
            previous_writeback_done()
            ybuf_ref[:nrows, :] = y

        @pl.when(f > 0)
        def _():
            ybuf_ref[:nrows, :] = y + ybuf_ref[:nrows, :]

    for nrows in range(MOE_ROWS, MOE_BLOCK + 1, MOE_ROWS):
        @pl.when((valid > nrows - MOE_ROWS) & (valid <= nrows))
        def _():
            compute(nrows)

    @pl.when(f == nf - 1)
    def _():
        _for_each_y_piece(meta_ref, ybuf_ref, yb_ref, ysem, v, start)

        @pl.when(v == pl.num_programs(0) - 1)
        def _():
            _for_each_y_piece(meta_ref, ybuf_ref, yb_ref, ysem, v, wait)


def _experts(meta, xs, w1, w3, w2, layer):
    d = D_MODEL
    nf = D_EXPERT // MOE_FC
    n_vis = lambda meta: meta[META_LANES - 1]
    used = lambda v, meta: jnp.minimum(v, n_vis(meta) - 1)
    chunk = lambda v, f, meta: jnp.where(v < n_vis(meta), f, nf - 1)
    w_in_spec = pl.BlockSpec((1, 1, d, MOE_FC), lambda v, f, meta: (layer, meta[used(v, meta)], 0, chunk(v, f, meta)))
    w_out_spec = pl.BlockSpec((1, 1, MOE_FC, d), lambda v, f, meta: (layer, meta[used(v, meta)], chunk(v, f, meta), 0))
    return pl.pallas_call(
        _expert_kernel,
        out_shape=jax.ShapeDtypeStruct((MOE_SLOTS, d), F32),
        grid_spec=pltpu.PrefetchScalarGridSpec(
            num_scalar_prefetch=1,
            grid=(meta[META_LANES - 1], nf),
            in_specs=[pl.BlockSpec(memory_space=pl.ANY), w_in_spec, w_in_spec, w_out_spec],
            out_specs=pl.BlockSpec(memory_space=pl.ANY),
            scratch_shapes=[pltpu.VMEM((2, MOE_BLOCK, d), F32), pltpu.VMEM((MOE_BLOCK, d), BF16),
                            pltpu.VMEM((MOE_BLOCK, d), F32), pltpu.VMEM((MOE_TAIL_PIECES[0], d), F32),
                            pltpu.SemaphoreType.DMA((2,)), pltpu.SemaphoreType.DMA, pltpu.SemaphoreType.DMA],
        ),
        compiler_params=_cparams(("arbitrary", "arbitrary")),
        name="moe_experts",
    )(meta, xs, w1, w3, w2)


def _combine_copy(dest_ref, yb_ref, buf_ref, sem, i, g, u, k):
    src = dest_ref[k * N_TOK + i * TOK_TM + g * ROW_GROUP + u]
    return pltpu.make_async_copy(yb_ref.at[pl.ds(src, 1)], buf_ref.at[k, g, pl.ds(u, 1)], sem)


def _combine_kernel(dest_ref, yb_ref, x_ref, gate_ref, w_ref, o_ref, buf_ref, sem):
    i = pl.program_id(0)

    def start(g, carry):
        for u in range(ROW_GROUP):
            for k in range(2):
                _combine_copy(dest_ref, yb_ref, buf_ref, sem, i, g, u, k).start()
        return carry

    def wait(g, carry):
        for u in range(ROW_GROUP):
            for k in range(2):
                _combine_copy(dest_ref, yb_ref, buf_ref, sem, i, g, u, k).wait()
        return carry

    lax.fori_loop(0, TOK_TM // ROW_GROUP, start, 0)
    lax.fori_loop(0, TOK_TM // ROW_GROUP, wait, 0)
    w = w_ref[...]
    rows = lambda k: buf_ref[k].reshape(TOK_TM, buf_ref.shape[-1])
    y = w[:, 0:1] * rows(0) + w[:, 1:2] * rows(1)
    o_ref[...] = x_ref[...] + gate_ref[0] * y


def _combine(dest_flat, yb, x2d, gate, w_tok):
    d = D_MODEL
    return pl.pallas_call(
        _combine_kernel,
        out_shape=jax.ShapeDtypeStruct((N_TOK, d), F32),
        grid_spec=pltpu.PrefetchScalarGridSpec(
            num_scalar_prefetch=1,
            grid=(N_TOK // TOK_TM,),
            in_specs=[pl.BlockSpec(memory_space=pl.ANY),
                      pl.BlockSpec((TOK_TM, d), lambda i, dest: (i, 0)),
                      pl.BlockSpec((1, 1, d), lambda i, dest: (i * TOK_TM // SEQ_LEN, 0, 0)),
                      pl.BlockSpec((TOK_TM, 2), lambda i, dest: (i, 0))],
            out_specs=pl.BlockSpec((TOK_TM, d), lambda i, dest: (i, 0)),
            scratch_shapes=[pltpu.VMEM((2, TOK_TM // ROW_GROUP, ROW_GROUP, d), F32), pltpu.SemaphoreType.DMA],
        ),
        compiler_params=_cparams(("arbitrary",)),
        name="moe_combine",
    )(dest_flat, yb, x2d, gate, w_tok)


def _moe(h2d, logits_t, x2d, gate, router_b, w1, w3, w2, layer):
    _, w_sel, dest, meta = _route(logits_t, router_b)
    dest_flat = dest.reshape(2 * N_TOK)
    meta_flat = meta.reshape(META_ROWS * META_LANES)
    xs = _dispatch(dest_flat, meta_flat, h2d)
    yb = _experts(meta_flat, xs, w1, w3, w2, layer)
    return _combine(dest_flat, yb, x2d, gate, w_sel.T)


def _rope_tables():
    f = HEAD_DIM // 4
    inv_freq = ROPE_THETA ** (-jnp.arange(f, dtype=F32) / f)
    pos = jnp.arange(SEQ_LEN)
    row = (pos // GRID_W).astype(F32)[:, None] * inv_freq
    col = (pos % GRID_W).astype(F32)[:, None] * inv_freq
    cos = jnp.concatenate([jnp.cos(row), jnp.cos(row), jnp.cos(col), jnp.cos(col)], axis=-1)
    sin = jnp.concatenate([-jnp.sin(row), jnp.sin(row), -jnp.sin(col), jnp.sin(col)], axis=-1)
    return cos, sin


def _cast_block_kernel(src_ref, x_ref, o_ref):
    del src_ref
    o_ref[...] = x_ref[...].astype(BF16)


def _permute_attn_w_in(w):
    widths = dict(aq=A_Q_DIM, ak=A_KV_DIM, av=A_KV_DIM, gq=GLA_K_DIM, gk=GLA_K_DIM, gv=GLA_V_DIM, gg=GLA_V_DIM)
    starts, col = {}, 0
    for name in ("aq", "ak", "av", "gq", "gk", "gv", "gg"):
        starts[name], col = col, col + widths[name]
    blk = A_KV_DIM
    src = [c // blk for name in ("aq", "gv", "gg", "gq", "gk", "ak", "av")
           for c in range(starts[name], starts[name] + widths[name], blk)]
    d = w.shape[0]
    main = pl.pallas_call(
        _cast_block_kernel,
        out_shape=jax.ShapeDtypeStruct((d, PROJ_DIM), BF16),
        grid_spec=pltpu.PrefetchScalarGridSpec(
            num_scalar_prefetch=1,
            grid=(len(src),),
            in_specs=[pl.BlockSpec((d, blk), lambda j, src: (0, src[j]))],
            out_specs=pl.BlockSpec((d, blk), lambda j, src: (0, j)),
        ),
        compiler_params=_cparams(("arbitrary",)),
        name="permute_cast_w_in",
    )(jnp.asarray(src, I32), w)
    lr = jnp.pad(w[:, PROJ_DIM:], ((0, 0), (0, LR_PAD - 2 * GLA_RANK)))
    return main, lr


def kernel(x, c, ctx, c_ctx, mod_w, mod_b, attn_w_in, attn_q_norm, attn_k_norm, attn_sink, gla_wa2, gla_ba,
           gla_norm, attn_w_out, conv_w_in, conv_w, conv_w_out, router_w, router_b, exp_w1, exp_w3, exp_w2):
    d = D_MODEL
    cc = jnp.concatenate([c, c_ctx[None], jnp.zeros((8 - N_BATCH - 1, d), F32)], axis=0)
    m = _modulation(cc, mod_w, mod_b)
    mods = [[m[l, :, i * d:(i + 1) * d].reshape(8, 1, d) for i in range(6)] for l in range(mod_w.shape[0])]
    lat_row = lambda i: i * PROJ_TM // SEQ_LEN
    ctx_row = lambda i: N_BATCH
    rw_pad = jnp.pad(router_w, ((0, 0), (0, ROUTE_LANES - N_EXPERTS)))
    rw_hi = rw_pad.astype(BF16)
    rwt = jnp.stack([rw_hi, (rw_pad - rw_hi.astype(F32)).astype(BF16)])
    x2d = x.reshape(N_TOK, d)

    w_main, w_lr = _permute_attn_w_in(attn_w_in[0])
    p_lat, lr_lat = _projection(x2d, mods[0][0], mods[0][1], w_main, w_lr, lat_row)
    p_ctx, lr_ctx = _projection(ctx.reshape(N_BATCH * CTX_LEN, d), mods[0][0], mods[0][1], w_main, w_lr, ctx_row)
    p_lat = p_lat.reshape(N_BATCH, SEQ_LEN, PROJ_DIM)
    p_ctx = p_ctx.reshape(N_BATCH, CTX_LEN, PROJ_DIM)
    cos, sin_signed = _rope_tables()
    attn = _attention(p_lat, p_ctx, cos, sin_signed, attn_q_norm[0][None], attn_k_norm[0][None], attn_sink[0])
    wa_pad = jnp.zeros((2, LR_PAD, GLA_K_DIM), F32)
    for di in range(2):
        wa_pad = wa_pad.at[di, di * GLA_RANK:(di + 1) * GLA_RANK].set(gla_wa2[0, di])
    wa_pad = wa_pad.astype(BF16)
    ba = gla_ba[0].reshape(2, 1, GLA_K_DIM)
    s_zero = jnp.zeros((N_BATCH, 2, GLA_HEADS, GLA_DV, GLA_DK), F32)
    s_ctx = _gla(p_ctx, lr_ctx.reshape(N_BATCH, CTX_LEN, LR_PAD), wa_pad, ba, None, s_zero, latent=False)
    gla = _gla(p_lat, lr_lat.reshape(N_BATCH, SEQ_LEN, LR_PAD), wa_pad, ba, gla_norm[0][None], s_ctx, latent=True)
    x1, h2, lg = _attn_out(attn.reshape(N_TOK, A_Q_DIM), gla.reshape(N_TOK, GLA_V_DIM), attn_w_out[0].astype(BF16),
                           x2d, mods[0][2], mods[0][3], mods[0][4], rwt)
    x2 = _moe(h2, lg, x1, mods[0][5], router_b, exp_w1, exp_w3, exp_w2, 0)

    g = _projection(x2, mods[1][0], mods[1][1], conv_w_in[0].astype(BF16), None, lat_row)
    x3, h2, lg = _conv_out(g, conv_w[0], conv_w_out[0].astype(BF16), x2, mods[1][2], mods[1][3], mods[1][4], rwt)
    x4 = _moe(h2, lg, x3, mods[1][5], router_b, exp_w1, exp_w3, exp_w2, 1)
    return x4.reshape(N_BATCH, SEQ_LEN, d)
```

```python
import functools

import jax
import jax.numpy as jnp
from jax import lax
from jax.experimental import pallas as pl
from jax.experimental.pallas import tpu as pltpu

F32 = jnp.float32
BF16 = jnp.bfloat16
I32 = jnp.int32

D_MODEL = 2048
N_BATCH = 4
SEQ_LEN = 2048
CTX_LEN = 256
N_TOK = N_BATCH * SEQ_LEN
GRID_W = 64
HEAD_DIM = 128
A_Q_HEADS = 8
A_KV_HEADS = 2
A_GROUP = A_Q_HEADS // A_KV_HEADS
WINDOW = 128
A_BLOCK = 128
ROPE_THETA = 10000.0
GLA_HEADS = 4
GLA_DK = 128
GLA_DV = 256
GLA_RANK = 16
GLA_TAU = 16.0
GLA_CHUNK = 64
N_EXPERTS = 32
N_GROUPS = 4
GROUP_SIZE = N_EXPERTS // N_GROUPS
D_EXPERT = 1024
EPS = 1e-6
NEG_INF = -1e30
A_Q_DIM = A_Q_HEADS * HEAD_DIM
A_KV_DIM = A_KV_HEADS * HEAD_DIM
GLA_K_DIM = GLA_HEADS * GLA_DK
GLA_V_DIM = GLA_HEADS * GLA_DV
PROJ_DIM = A_Q_DIM + 2 * A_KV_DIM + 2 * GLA_K_DIM + 2 * GLA_V_DIM
LR_PAD = 128

COL_AQ = 0
COL_GV = 1
COL_GG = 2
COL_GQ = 6
COL_GK = 7
COL_AK = 32
COL_AV = 34

MOD_TN = 2048
PROJ_TM = 512
PROJ_TN = 512
OUT_TM = 256
CONV_HALO = 16
GLA_ROWS = 512
ROUTE_TT = 512
ROUTE_LANES = 128
MOE_ALIGN = 8
MOE_BLOCK = 768
MOE_ROWS = 128
MOE_FC = 256
MOE_NVIS = N_EXPERTS + (2 * N_TOK) // MOE_BLOCK
MOE_TAIL_PIECES = (256, 128, 64, 32, 16, 8)
MOE_SLOTS = 2 * N_TOK + sum(MOE_TAIL_PIECES)
assert MOE_SLOTS >= 2 * N_TOK + N_EXPERTS * (MOE_ALIGN - 1) + MOE_ROWS
TOK_TM = 512
ROW_GROUP = 8
META_ROWS = 5
META_LANES = 128
VMEM_LIMIT = 56 * 1024 * 1024


def _cparams(sem):
    return pltpu.CompilerParams(dimension_semantics=sem, vmem_limit_bytes=VMEM_LIMIT)


def _silu(x):
    return x * jax.nn.sigmoid(x)


def _dot(a, b):
    return jnp.dot(a, b, preferred_element_type=F32)


def _dot_nt(a, b):
    return lax.dot_general(a, b, (((1,), (1,)), ((), ())), preferred_element_type=F32)


def _dot_tn(a, b):
    return lax.dot_general(a, b, (((0,), (0,)), ((), ())), preferred_element_type=F32)


def _rms_modulate(x, shift, scale):
    r = lax.rsqrt(jnp.mean(x * x, axis=-1, keepdims=True) + EPS)
    return (x * r) * (1.0 + scale) + shift


def _mod_kernel(cc_ref, w_ref, b_ref, o_ref):
    a = _silu(cc_ref[...])
    o_ref[0] = _dot(a.astype(BF16), w_ref[0].astype(BF16)) + b_ref[0]


def _modulation(cc, mod_w, mod_b):
    depth, d, n = mod_w.shape
    return pl.pallas_call(
        _mod_kernel,
        out_shape=jax.ShapeDtypeStruct((depth, 8, n), F32),
        grid=(depth, n // MOD_TN),
        in_specs=[
            pl.BlockSpec((8, d), lambda l, j: (0, 0)),
            pl.BlockSpec((1, d, MOD_TN), lambda l, j: (l, 0, j)),
            pl.BlockSpec((1, 1, MOD_TN), lambda l, j: (l, 0, j)),
        ],
        out_specs=pl.BlockSpec((1, 8, MOD_TN), lambda l, j: (l, 0, j)),
        compiler_params=_cparams(("arbitrary", "arbitrary")),
        name="adaln_modulation",
    )(cc, mod_w, mod_b.reshape(depth, 1, n))


def _proj_kernel(with_lr, x_ref, sh_ref, sc_ref, w_ref, *rest):
    if with_lr:
        wlr_ref, o_ref, olr_ref = rest
    else:
        (o_ref,) = rest
    hb = _rms_modulate(x_ref[...], sh_ref[0], sc_ref[0]).astype(BF16)
    for c in range(w_ref.shape[1] // PROJ_TN):
        cols = slice(c * PROJ_TN, (c + 1) * PROJ_TN)
        o_ref[:, cols] = _dot(hb, w_ref[:, cols]).astype(BF16)
    if with_lr:
        olr_ref[...] = _dot(hb, wlr_ref[...].astype(BF16)).astype(BF16)


def _projection(x2d, shift, scale, w, w_lr, mod_row):
    rows, d = x2d.shape
    n = w.shape[1]
    with_lr = w_lr is not None
    resident = lambda shape: pl.BlockSpec(shape, lambda i: (0, 0), pipeline_mode=pl.Buffered(1))
    in_specs = [
        pl.BlockSpec((PROJ_TM, d), lambda i: (i, 0)),
        pl.BlockSpec((1, 1, d), lambda i: (mod_row(i), 0, 0)),
        pl.BlockSpec((1, 1, d), lambda i: (mod_row(i), 0, 0)),
        resident((d, n)),
    ]
    out_shape = [jax.ShapeDtypeStruct((rows, n), BF16)]
    out_specs = [pl.BlockSpec((PROJ_TM, n), lambda i: (i, 0))]
    args = [x2d, shift, scale, w]
    if with_lr:
        in_specs.append(resident((d, LR_PAD)))
        out_shape.append(jax.ShapeDtypeStruct((rows, LR_PAD), BF16))
        out_specs.append(pl.BlockSpec((PROJ_TM, LR_PAD), lambda i: (i, 0)))
        args.append(w_lr)
    res = pl.pallas_call(
        functools.partial(_proj_kernel, with_lr),
        out_shape=out_shape,
        grid=(rows // PROJ_TM,),
        in_specs=in_specs,
        out_specs=out_specs,
        compiler_params=_cparams(("arbitrary",)),
        name="norm_mod_projection",
    )(*args)
    return res if with_lr else res[0]


def _swap_halves32(x):
    lane = lax.broadcasted_iota(I32, x.shape, 1)
    return jnp.where((lane & 63) < 32, pltpu.roll(x, 96, 1), pltpu.roll(x, 32, 1))


def _qk_norm(x, gain):
    return x * lax.rsqrt(jnp.mean(x * x, axis=-1, keepdims=True) + EPS) * gain


def _rope(x, cos, sin_signed):
    return x * cos + _swap_halves32(x) * sin_signed


ATT_WIN0 = CTX_LEN
ATT_LAT0 = CTX_LEN + A_BLOCK
ATT_ROWS = CTX_LEN + SEQ_LEN + 2 * A_BLOCK
ATT_WIN = 3 * A_BLOCK


def _attn_kernel(sink_ref, q_ref, k_ref, v_ref, kx_ref, vx_ref, cos_ref, sin_ref, qg_ref, kg_ref, o_ref,
                 kn_ref, vn_ref, band_ref):
    kvh = pl.program_id(1)
    nb = SEQ_LEN // A_BLOCK
    k_gain = kg_ref[...]
    q_gain = qg_ref[...]

    pad = jnp.zeros((A_BLOCK, HEAD_DIM), BF16)
    kn_ref[0:CTX_LEN, :] = _qk_norm(kx_ref[0].astype(F32), k_gain).astype(BF16)
    vn_ref[0:CTX_LEN, :] = vx_ref[0]
    for ref in (kn_ref, vn_ref):
        ref[ATT_WIN0:ATT_LAT0, :] = pad
        ref[ATT_LAT0 + SEQ_LEN:, :] = pad
    vn_ref[ATT_LAT0:ATT_LAT0 + SEQ_LEN, :] = v_ref[0]
    prep_rows = 4 * A_BLOCK
    for c in range(SEQ_LEN // prep_rows):
        r = slice(c * prep_rows, (c + 1) * prep_rows)
        kc = _rope(_qk_norm(k_ref[0, r, :].astype(F32), k_gain), cos_ref[r, :], sin_ref[r, :])
        kn_ref[ATT_LAT0 + c * prep_rows:ATT_LAT0 + (c + 1) * prep_rows, :] = kc.astype(BF16)

    qi = lax.broadcasted_iota(I32, (A_BLOCK, ATT_WIN), 0)
    wj = lax.broadcasted_iota(I32, (A_BLOCK, ATT_WIN), 1)
    band = jnp.where(jnp.abs(wj - A_BLOCK - qi) <= WINDOW, 1.0, 0.0)
    band_ref[0] = jnp.where(wj >= A_BLOCK, band, 0.0)
    band_ref[1] = band
    band_ref[2] = jnp.where(wj < 2 * A_BLOCK, band, 0.0)

    rows = A_GROUP * A_BLOCK
    rcol = lax.broadcasted_iota(I32, (rows, 1), 0)
    sink = jnp.zeros((rows, 1), F32)
    for g in range(A_GROUP):
        sink = jnp.where((rcol >= g * A_BLOCK) & (rcol < (g + 1) * A_BLOCK), sink_ref[kvh * A_GROUP + g], sink)

    def query_block(n):
        qrows = pl.ds(pl.multiple_of(n * A_BLOCK, A_BLOCK), A_BLOCK)
        wrows = pl.ds(pl.multiple_of(ATT_WIN0 + n * A_BLOCK, A_BLOCK), ATT_WIN)
        cos_q, sin_q = cos_ref[qrows, :], sin_ref[qrows, :]
        q = q_ref[0, qrows, :].astype(F32)
        qs = []
        for g in range(A_GROUP):
            qh = _rope(_qk_norm(q[:, g * HEAD_DIM:(g + 1) * HEAD_DIM], q_gain), cos_q, sin_q)
            qs.append((qh * HEAD_DIM ** -0.5).astype(BF16))
        qs = jnp.concatenate(qs, axis=0)
        mask = band_ref[jnp.where(n == 0, 0, jnp.where(n == nb - 1, 2, 1))]
        mask = jnp.concatenate([mask] * A_GROUP, axis=0)
        s_ctx = _dot_nt(qs, kn_ref[0:CTX_LEN, :])
        s_win = jnp.where(mask > 0.5, _dot_nt(qs, kn_ref[wrows, :]), NEG_INF)
        m = jnp.maximum(jnp.max(s_ctx, axis=-1, keepdims=True), jnp.max(s_win, axis=-1, keepdims=True))
        m = jnp.maximum(m, sink)
        p_ctx = jnp.exp(s_ctx - m)
        p_win = jnp.exp(s_win - m)
        denom = (jnp.sum(p_ctx, axis=-1, keepdims=True) + jnp.sum(p_win, axis=-1, keepdims=True)
                 + jnp.exp(sink - m))
        o = (_dot(p_ctx.astype(BF16), vn_ref[0:CTX_LEN, :]) + _dot(p_win.astype(BF16), vn_ref[wrows, :])) / denom
        for g in range(A_GROUP):
            o_ref[0, qrows, g * HEAD_DIM:(g + 1) * HEAD_DIM] = o[g * A_BLOCK:(g + 1) * A_BLOCK].astype(BF16)

    def query_block_pair(i, carry):
        query_block(2 * i)
        query_block(2 * i + 1)
        return carry

    lax.fori_loop(0, nb // 2, query_block_pair, 0)


def _attention(p_lat, p_ctx, cos, sin_signed, q_gain, k_gain, sink):
    gw = A_GROUP * HEAD_DIM
    lat = lambda width, col: pl.BlockSpec((1, SEQ_LEN, width), lambda b, k, s: (b, 0, col + k))
    ctx_blk = lambda col: pl.BlockSpec((1, CTX_LEN, HEAD_DIM), lambda b, k, s: (b, 0, col + k))
    full = lambda shape: pl.BlockSpec(shape, lambda b, k, s: (0,) * len(shape))
    return pl.pallas_call(
        _attn_kernel,
        out_shape=jax.ShapeDtypeStruct((N_BATCH, SEQ_LEN, A_Q_DIM), BF16),
        grid_spec=pltpu.PrefetchScalarGridSpec(
            num_scalar_prefetch=1,
            grid=(N_BATCH, A_KV_HEADS),
            in_specs=[
                lat(gw, COL_AQ), lat(HEAD_DIM, COL_AK), lat(HEAD_DIM, COL_AV),
                ctx_blk(COL_AK), ctx_blk(COL_AV),
                full((SEQ_LEN, HEAD_DIM)), full((SEQ_LEN, HEAD_DIM)),
                full((1, HEAD_DIM)), full((1, HEAD_DIM)),
            ],
            out_specs=pl.BlockSpec((1, SEQ_LEN, gw), lambda b, k, s: (b, 0, k)),
            scratch_shapes=[pltpu.VMEM((ATT_ROWS, HEAD_DIM), BF16), pltpu.VMEM((ATT_ROWS, HEAD_DIM), BF16),
                            pltpu.VMEM((3, A_BLOCK, ATT_WIN), F32)],
        ),
        compiler_params=_cparams(("arbitrary", "arbitrary")),
        name="windowed_sink_attention",
    )(sink, p_lat, p_lat, p_lat, p_ctx, p_ctx, cos, sin_signed, q_gain, k_gain)


def _split3_bf16(x):
    hi = x.astype(BF16)
    r1 = x - hi.astype(F32)
    mid = r1.astype(BF16)
    lo = (r1 - mid.astype(F32)).astype(BF16)
    return hi, mid, lo


def _gla_kernel(latent, nblk, q_ref, k_ref, v_ref, *rest):
    if latent:
        g_ref, lr_ref, wa_ref, ba_ref, gain_ref, s0_ref, o_ref, st_ref, ofwd_ref = rest
    else:
        lr_ref, wa_ref, ba_ref, s0_ref, o_ref, st_ref = rest
    d = pl.program_id(1)
    j = pl.program_id(2)
    blk_rows = k_ref.shape[1]
    nc = blk_rows // GLA_CHUNK
    blk = jnp.where(d == 0, j, nblk - 1 - j)

    @pl.when(j == 0)
    def _():
        st_ref[...] = s0_ref[0, 0]

    ii = lax.broadcasted_iota(I32, (GLA_CHUNK, GLA_CHUNK), 0)
    jj = lax.broadcasted_iota(I32, (GLA_CHUNK, GLA_CHUNK), 1)

    def scan_block(backward):
        incl = (jj >= ii) if backward else (jj <= ii)
        incl_b = jnp.where(incl, 1.0, 0.0).astype(BF16)
        for i in range(nc):
            r0 = (nc - 1 - i if backward else i) * GLA_CHUNK
            rows = slice(r0, r0 + GLA_CHUNK)
            z = _dot(lr_ref[0, rows, :], wa_ref[0]) + ba_ref[0]
            la = (jnp.minimum(z, 0.0) - jnp.log(1.0 + jnp.exp(-jnp.abs(z)))) / GLA_TAU
            hi, mid, lo = _split3_bf16(la)
            bcum = _dot(incl_b, hi) + _dot(incl_b, mid) + _dot(incl_b, lo)
            blast = jnp.sum(la, axis=0, keepdims=True)
            k = k_ref[0, rows, :].astype(F32)
            kl = (k * jnp.exp(blast - bcum)).astype(BF16)
            decay = jnp.exp(blast)
            v = v_ref[0, rows, :]
            if latent:
                qf = (q_ref[0, rows, :].astype(F32) * GLA_DK ** -0.5 * jnp.exp(bcum)).astype(BF16)
                kf = (k * jnp.exp(-bcum)).astype(BF16)
            for h in range(GLA_HEADS):
                ks = slice(h * GLA_DK, (h + 1) * GLA_DK)
                vs = slice(h * GLA_DV, (h + 1) * GLA_DV)
                st = st_ref[h]
                if latent:
                    att = jnp.where(incl, _dot_nt(qf[:, ks], kf[:, ks]), 0.0)
                    o = _dot(att.astype(BF16), v[:, vs]) + _dot_nt(qf[:, ks], st.astype(BF16))
                    orow = pl.ds(pl.multiple_of(blk * blk_rows + r0, GLA_CHUNK), GLA_CHUNK)
                    if backward:
                        ot = o + ofwd_ref[orow, vs]
                        on = ot * lax.rsqrt(jnp.mean(ot * ot, axis=-1, keepdims=True) + EPS) * gain_ref[...]
                        o_ref[0, rows, vs] = (on * _silu(g_ref[0, rows, vs].astype(F32))).astype(BF16)
                    else:
                        ofwd_ref[orow, vs] = o
                st_ref[h] = st * decay[:, ks] + _dot_tn(v[:, vs], kl[:, ks])

    for backward in (False, True):
        @pl.when(d == int(backward))
        def _():
            scan_block(backward)

    if not latent:
        @pl.when(j == nblk - 1)
        def _():
            o_ref[0, 0] = st_ref[...]


def _gla(p3, lr3, wa_pad, ba, o_gain, s0, latent):
    n = p3.shape[1]
    blk_rows = min(GLA_ROWS, n)
    nblk = n // blk_rows
    seq_blk = lambda d, j: jnp.where(d == 0, j, nblk - 1 - j)
    in_specs = [
        pl.BlockSpec((1, blk_rows, GLA_K_DIM), lambda b, d, j: (b, seq_blk(d, j), COL_GQ)),
        pl.BlockSpec((1, blk_rows, GLA_K_DIM), lambda b, d, j: (b, seq_blk(d, j), COL_GK)),
        pl.BlockSpec((1, blk_rows, GLA_V_DIM), lambda b, d, j: (b, seq_blk(d, j), COL_GV)),
    ]
    args = [p3, p3, p3]
    if latent:
        in_specs.append(pl.BlockSpec((1, blk_rows, GLA_V_DIM), lambda b, d, j: (b, seq_blk(d, j), COL_GG)))
        args.append(p3)
    in_specs += [
        pl.BlockSpec((1, blk_rows, LR_PAD), lambda b, d, j: (b, seq_blk(d, j), 0)),
        pl.BlockSpec((1, LR_PAD, GLA_K_DIM), lambda b, d, j: (d, 0, 0)),
        pl.BlockSpec((1, 1, GLA_K_DIM), lambda b, d, j: (d, 0, 0)),
    ]
    args += [lr3, wa_pad, ba]
    if latent:
        in_specs.append(pl.BlockSpec((1, GLA_DV), lambda b, d, j: (0, 0)))
        args.append(o_gain)
    state_spec = pl.BlockSpec((1, 1, GLA_HEADS, GLA_DV, GLA_DK), lambda b, d, j: (b, d, 0, 0, 0))
    in_specs.append(state_spec)
    args.append(s0)
    scratch = [pltpu.VMEM((GLA_HEADS, GLA_DV, GLA_DK), F32)]
    if latent:
        out_shape = jax.ShapeDtypeStruct((N_BATCH, n, GLA_V_DIM), BF16)
        out_spec = pl.BlockSpec((1, blk_rows, GLA_V_DIM),
                                lambda b, d, j: (b, jnp.where(d == 0, nblk - 1, nblk - 1 - j), 0))
        scratch.append(pltpu.VMEM((n, GLA_V_DIM), F32))
    else:
        out_shape = jax.ShapeDtypeStruct(s0.shape, F32)
        out_spec = state_spec
    return pl.pallas_call(
        functools.partial(_gla_kernel, latent, nblk),
        out_shape=out_shape,
        grid=(N_BATCH, 2, nblk),
        in_specs=in_specs,
        out_specs=out_spec,
        scratch_shapes=scratch,
        compiler_params=_cparams(("arbitrary", "arbitrary", "arbitrary")),
        name="gla_latent" if latent else "gla_context_state",
    )(*args)


def _router_logits(h, rw_ref):
    hh = h.astype(BF16)
    hl = (h - hh.astype(F32)).astype(BF16)
    return _dot(hh, rw_ref[0]) + _dot(hl, rw_ref[0]) + _dot(hh, rw_ref[1])


def _residual_and_moe_input(mix, x_ref, gate_ref, sh_ref, sc_ref, rwt_ref, x_out_ref, h_out_ref, lg_ref):
    x1 = x_ref[...] + gate_ref[0] * mix
    x_out_ref[...] = x1
    h = _rms_modulate(x1, sh_ref[0], sc_ref[0])
    h_out_ref[...] = h
    lg_ref[...] = _router_logits(h, rwt_ref)


def _attn_out_kernel(a1_ref, a2_ref, w_ref, x_ref, gate_ref, sh_ref, sc_ref, rwt_ref, x_out_ref, h_out_ref, lg_ref):
    mix = _dot(a1_ref[...], w_ref[0:A_Q_DIM, :]) + _dot(a2_ref[...], w_ref[A_Q_DIM:, :])
    _residual_and_moe_input(mix, x_ref, gate_ref, sh_ref, sc_ref, rwt_ref, x_out_ref, h_out_ref, lg_ref)


def _conv_out_kernel(gb_ref, gc_ref, u_ref, gcp_ref, up_ref, gcn_ref, un_ref, cw_ref, w_ref,
                     x_ref, gate_ref, sh_ref, sc_ref, rwt_ref, x_out_ref, h_out_ref, lg_ref):
    i = pl.program_id(0)
    tiles_per_seq = SEQ_LEN // OUT_TM
    t = gc_ref[...].astype(F32) * u_ref[...].astype(F32)
    first = (i % tiles_per_seq) == 0
    last = (i % tiles_per_seq) == tiles_per_seq - 1
    halo_last = slice(CONV_HALO - 1, CONV_HALO)
    t_before = jnp.where(first, 0.0, gcp_ref[halo_last, :].astype(F32) * up_ref[halo_last, :].astype(F32))
    t_after = jnp.where(last, 0.0, gcn_ref[0:1, :].astype(F32) * un_ref[0:1, :].astype(F32))
    row = lax.broadcasted_iota(I32, t.shape, 0)
    t_up = jnp.where(row == 0, t_before, pltpu.roll(t, 1, 0))
    t_dn = jnp.where(row == OUT_TM - 1, t_after, pltpu.roll(t, OUT_TM - 1, 0))
    y = cw_ref[0:1, :] * t_up + cw_ref[1:2, :] * t + cw_ref[2:3, :] * t_dn
    mix = _dot((gb_ref[...].astype(F32) * y).astype(BF16), w_ref[...])
    _residual_and_moe_input(mix, x_ref, gate_ref, sh_ref, sc_ref, rwt_ref, x_out_ref, h_out_ref, lg_ref)


def _mixer_out(kernel_fn, mixer_specs, mixer_args, x2d, gate, shift, scale, rwt, name):
    d = D_MODEL
    mod_spec = pl.BlockSpec((1, 1, d), lambda i: (i * OUT_TM // SEQ_LEN, 0, 0))
    row_spec = pl.BlockSpec((OUT_TM, d), lambda i: (i, 0))
    return pl.pallas_call(
        kernel_fn,
        out_shape=[jax.ShapeDtypeStruct((N_TOK, d), F32), jax.ShapeDtypeStruct((N_TOK, d), F32),
                   jax.ShapeDtypeStruct((N_TOK, ROUTE_LANES), F32)],
        grid=(N_TOK // OUT_TM,),
        in_specs=mixer_specs + [row_spec, mod_spec, mod_spec, mod_spec,
                                pl.BlockSpec((2, d, ROUTE_LANES), lambda i: (0, 0, 0))],
        out_specs=[row_spec, row_spec, pl.BlockSpec((OUT_TM, ROUTE_LANES), lambda i: (i, 0))],
        compiler_params=_cparams(("arbitrary",)),
        name=name,
    )(*mixer_args, x2d, gate, shift, scale, rwt)


def _attn_out(attn2d, gla2d, w_out, x2d, gate, shift, scale, rwt):
    half = pl.BlockSpec((OUT_TM, A_Q_DIM), lambda i: (i, 0))
    specs = [half, half, pl.BlockSpec((D_MODEL, D_MODEL), lambda i: (0, 0))]
    return _mixer_out(_attn_out_kernel, specs, [attn2d, gla2d, w_out], x2d, gate, shift, scale, rwt,
                      "attn_out_projection")


def _conv_out(g2d, conv_w, w_out, x2d, gate, shift, scale, rwt):
    d = D_MODEL
    sub = OUT_TM // CONV_HALO
    last_halo = N_TOK // CONV_HALO - 1
    main = lambda col: pl.BlockSpec((OUT_TM, d), lambda i: (i, col))
    before = lambda col: pl.BlockSpec((CONV_HALO, d), lambda i: (jnp.maximum(i * sub - 1, 0), col))
    after = lambda col: pl.BlockSpec((CONV_HALO, d), lambda i: (jnp.minimum((i + 1) * sub, last_halo), col))
    specs = [main(0), main(1), main(2), before(1), before(2), after(1), after(2),
             pl.BlockSpec((3, d), lambda i: (0, 0)), pl.BlockSpec((d, d), lambda i: (0, 0))]
    return _mixer_out(_conv_out_kernel, specs, [g2d] * 7 + [conv_w, w_out], x2d, gate, shift, scale, rwt,
                      "conv_out_projection")


def _first_argmax8(x, idx8):
    m = jnp.max(x, axis=0, keepdims=True)
    a = jnp.min(jnp.where(x == m, idx8, float(GROUP_SIZE)), axis=0, keepdims=True)
    return m, a


def _route_kernel(lg_ref, rb_ref, e_ref, w_ref, dest_ref, meta_ref, cnt_ref, carry_ref, pstart_ref):
    phase = pl.program_id(0)
    j = pl.program_id(1)
    tt = ROUTE_TT
    sc = jax.nn.sigmoid(lg_ref[...].T[:N_EXPERTS])
    grp = sc + rb_ref[...]
    idx8 = lax.broadcasted_iota(I32, (GROUP_SIZE, tt), 0).astype(F32)
    groups = [grp[g * GROUP_SIZE:(g + 1) * GROUP_SIZE] for g in range(N_GROUPS)]
    gscore = []
    for x in groups:
        m1, a1 = _first_argmax8(x, idx8)
        m2, _ = _first_argmax8(jnp.where(idx8 == a1, -jnp.inf, x), idx8)
        gscore.append(m1 + m2)
    gmax = functools.reduce(jnp.maximum, gscore)
    gsel = jnp.full((1, tt), float(N_GROUPS), F32)
    for g in reversed(range(N_GROUPS)):
        gsel = jnp.where(gscore[g] == gmax, float(g), gsel)
    in_grp = groups[0]
    for g in range(1, N_GROUPS):
        in_grp = jnp.where(gsel == float(g), groups[g], in_grp)
    _, a1 = _first_argmax8(in_grp, idx8)
    _, a2 = _first_argmax8(jnp.where(idx8 == a1, -jnp.inf, in_grp), idx8)
    e0 = gsel * GROUP_SIZE + a1
    e1 = gsel * GROUP_SIZE + a2
    idx32 = lax.broadcasted_iota(I32, (N_EXPERTS, tt), 0).astype(F32)
    oh0 = idx32 == e0
    oh1 = idx32 == e1
    w0 = jnp.sum(jnp.where(oh0, sc, 0.0), axis=0, keepdims=True)
    w1 = jnp.sum(jnp.where(oh1, sc, 0.0), axis=0, keepdims=True)
    wsum = w0 + w1
    ohs = jnp.where(oh0 | oh1, 1.0, 0.0)
    tile_cnt = jnp.sum(ohs, axis=1, keepdims=True)

    @pl.when(phase == 0)
    def _():
        @pl.when(j == 0)
        def _():
            cnt_ref[...] = jnp.zeros_like(cnt_ref)
        cnt_ref[...] += jnp.broadcast_to(tile_cnt, cnt_ref.shape)

    @pl.when(phase == 1)
    def _():
        @pl.when(j == 0)
        def _():
            cnt = cnt_ref[...]
            expert = lax.broadcasted_iota(I32, cnt.shape, 0)
            lane = lax.broadcasted_iota(I32, cnt.shape, 1).astype(F32)

            def exclusive_cumsum(v):
                inc = v
                for s in (1, 2, 4, 8, 16):
                    inc = inc + jnp.where(expert >= s, pltpu.roll(inc, s, 0), 0.0)
                return inc - v

            cnt_pad = jnp.floor((cnt + (MOE_ALIGN - 1)) * (1.0 / MOE_ALIGN)) * MOE_ALIGN
            pstart = exclusive_cumsum(cnt_pad)
            pstart_ref[...] = pstart
            carry_ref[...] = jnp.zeros_like(carry_ref)
            nvis_e = jnp.zeros_like(cnt)
            for kb in range(-(-N_TOK // MOE_BLOCK)):
                nvis_e += jnp.where(cnt > float(kb * MOE_BLOCK), 1.0, 0.0)
            vstart = exclusive_cumsum(nvis_e)
            vis_exp = jnp.sum(jnp.where(lane >= vstart + nvis_e, 1.0, 0.0), axis=0, keepdims=True)
            vis_exp = jnp.minimum(vis_exp, float(N_EXPERTS - 1))
            own = expert.astype(F32) == vis_exp
            of_visit = lambda col: jnp.sum(jnp.where(own, col, 0.0), axis=0, keepdims=True)
            kth = lane[0:1] - of_visit(vstart)
            vis_rows = jnp.clip(of_visit(cnt_pad) - kth * MOE_BLOCK, 0.0, float(MOE_BLOCK))
            vis_row0 = of_visit(pstart) + kth * MOE_BLOCK
            last_lane = lane[0:1] == float(META_LANES - 1)
            expert_lane = expert.astype(F32) == lane
            to_lane = lambda col: jnp.sum(jnp.where(expert_lane, col, 0.0), axis=0, keepdims=True)
            meta_ref[...] = jnp.concatenate([
                jnp.where(last_lane, jnp.sum(nvis_e, axis=0, keepdims=True), vis_exp),
                vis_rows,
                vis_row0,
                to_lane(pstart + cnt),
                jnp.where(last_lane, jnp.sum(cnt_pad, axis=0, keepdims=True), to_lane(cnt_pad - cnt)),
            ], axis=0).astype(I32)

        rr = lax.broadcasted_iota(I32, (tt, tt), 0)
        cc = lax.broadcasted_iota(I32, (tt, tt), 1)
        strict_upper = jnp.where(rr < cc, 1.0, 0.0).astype(BF16)
        before = _dot(ohs.astype(BF16), strict_upper)
        pos = before + carry_ref[:, 0:1] + pstart_ref[:, 0:1]
        d0 = jnp.sum(jnp.where(oh0, pos, 0.0), axis=0, keepdims=True)
        d1 = jnp.sum(jnp.where(oh1, pos, 0.0), axis=0, keepdims=True)
        carry_ref[...] += jnp.broadcast_to(tile_cnt, carry_ref.shape)
        e_ref[...] = jnp.concatenate([e0, e1], axis=0).astype(I32)
        w_ref[...] = jnp.concatenate([w0 / wsum, w1 / wsum], axis=0)
        dest_ref[...] = jnp.concatenate([d0, d1], axis=0).astype(I32)


def _route(logits_t, router_b):
    nt = N_TOK // ROUTE_TT
    tok = lambda dt: jax.ShapeDtypeStruct((2, N_TOK), dt)
    tok_spec = pl.BlockSpec((2, ROUTE_TT), lambda p, j: (0, j * p))
    return pl.pallas_call(
        _route_kernel,
        out_shape=[tok(I32), tok(F32), tok(I32), jax.ShapeDtypeStruct((META_ROWS, META_LANES), I32)],
        grid=(2, nt),
        in_specs=[pl.BlockSpec((ROUTE_TT, ROUTE_LANES), lambda p, j: (j, 0)),
                  pl.BlockSpec((N_EXPERTS, 1), lambda p, j: (0, 0))],
        out_specs=[tok_spec, tok_spec, tok_spec, pl.BlockSpec((META_ROWS, META_LANES), lambda p, j: (0, 0))],
        scratch_shapes=[pltpu.VMEM((N_EXPERTS, 128), F32), pltpu.VMEM((N_EXPERTS, 128), F32),
                        pltpu.VMEM((N_EXPERTS, 128), F32)],
        compiler_params=_cparams(("arbitrary", "arbitrary")),
        name="router_slot_assignment",
    )(logits_t, router_b.reshape(N_EXPERTS, 1))


def _dispatch_copy(dest_ref, h_ref, xs_ref, sem, i, g, u, k):
    dst = dest_ref[k * N_TOK + i * TOK_TM + g * ROW_GROUP + u]
    return pltpu.make_async_copy(h_ref.at[g, pl.ds(u, 1)], xs_ref.at[pl.ds(dst, 1)], sem)


def _for_each_tail_piece(meta_ref, zero_ref, slots_ref, sem, fn):
    used = meta_ref[4 * META_LANES + META_LANES - 1]
    tail = MOE_SLOTS - used
    row = used
    for size in MOE_TAIL_PIECES:
        take = tail & size

        @pl.when(take != 0)
        def _():
            fn(pltpu.make_async_copy(zero_ref.at[pl.ds(0, size)],
                                     slots_ref.at[pl.ds(pl.multiple_of(row, MOE_ALIGN), size)], sem))
        row = row + take


def _for_each_padding_row(meta_ref, zero_ref, xs_ref, sem, fn):
    def expert_padding(e, carry):
        row = meta_ref[3 * META_LANES + e]
        npad = meta_ref[4 * META_LANES + e]
        for r in range(MOE_ALIGN - 1):
            @pl.when(r < npad)
            def _():
                fn(pltpu.make_async_copy(zero_ref.at[pl.ds(0, 1)], xs_ref.at[pl.ds(row + r, 1)], sem))
        return carry

    lax.fori_loop(0, N_EXPERTS, expert_padding, 0)


def _zero_fill_padding(meta_ref, zero_ref, xs_ref, sem):
    zero_ref[...] = jnp.zeros_like(zero_ref)
    for fn in (lambda copy: copy.start(), lambda copy: copy.wait()):
        _for_each_padding_row(meta_ref, zero_ref, xs_ref, sem, fn)
        _for_each_tail_piece(meta_ref, zero_ref, xs_ref, sem, fn)


def _dispatch_kernel(dest_ref, meta_ref, h_ref, xs_ref, zero_ref, sem):
    i = pl.program_id(0)

    @pl.when(i == 0)
    def _():
        _zero_fill_padding(meta_ref, zero_ref, xs_ref, sem)

    def start(g, carry):
        for u in range(ROW_GROUP):
            for k in range(2):
                _dispatch_copy(dest_ref, h_ref, xs_ref, sem, i, g, u, k).start()
        return carry

    def wait(g, carry):
        for u in range(ROW_GROUP):
            for k in range(2):
                _dispatch_copy(dest_ref, h_ref, xs_ref, sem, i, g, u, k).wait()
        return carry

    lax.fori_loop(0, TOK_TM // ROW_GROUP, start, 0)
    lax.fori_loop(0, TOK_TM // ROW_GROUP, wait, 0)


def _dispatch(dest_flat, meta_flat, h2d):
    d = D_MODEL
    return pl.pallas_call(
        _dispatch_kernel,
        out_shape=jax.ShapeDtypeStruct((MOE_SLOTS, d), F32),
        grid_spec=pltpu.PrefetchScalarGridSpec(
            num_scalar_prefetch=2,
            grid=(N_TOK // TOK_TM,),
            in_specs=[pl.BlockSpec((TOK_TM // ROW_GROUP, ROW_GROUP, d), lambda i, dest, meta: (i, 0, 0))],
            out_specs=pl.BlockSpec(memory_space=pl.ANY),
            scratch_shapes=[pltpu.VMEM((MOE_TAIL_PIECES[0], d), F32), pltpu.SemaphoreType.DMA],
        ),
        compiler_params=_cparams(("arbitrary",)),
        name="moe_dispatch",
    )(dest_flat, meta_flat, h2d.reshape(N_TOK // ROW_GROUP, ROW_GROUP, d))


def _for_each_x_piece(meta_ref, xs_ref, xbuf_ref, sem, v, fn):
    rows = meta_ref[META_LANES + v]
    row0 = meta_ref[2 * META_LANES + v]
    slot = v % 2
    for p in range(MOE_BLOCK // MOE_ROWS):
        @pl.when(p * MOE_ROWS < rows)
        def _():
            src = xs_ref.at[pl.ds(pl.multiple_of(row0 + p * MOE_ROWS, MOE_ALIGN), MOE_ROWS)]
            fn(pltpu.make_async_copy(src, xbuf_ref.at[slot, pl.ds(p * MOE_ROWS, MOE_ROWS)], sem.at[slot]))


def _for_each_y_piece(meta_ref, ybuf_ref, yb_ref, sem, v, fn):
    rows = meta_ref[META_LANES + v]
    row0 = meta_ref[2 * META_LANES + v]

    def piece(off, size):
        dst = yb_ref.at[pl.ds(pl.multiple_of(row0 + off, MOE_ALIGN), size)]
        fn(pltpu.make_async_copy(ybuf_ref.at[pl.ds(pl.multiple_of(off, MOE_ALIGN), size)], dst, sem))

    for p in range(MOE_BLOCK // MOE_ROWS):
        @pl.when((p + 1) * MOE_ROWS <= rows)
        def _():
            piece(p * MOE_ROWS, MOE_ROWS)
    off = (rows // MOE_ROWS) * MOE_ROWS
    size = MOE_ROWS // 2
    while size >= MOE_ALIGN:
        take = (rows - off) & size

        @pl.when(take != 0)
        def _():
            piece(off, size)
        off = off + take
        size //= 2


def _expert_kernel(meta_ref, xs_ref, w1_ref, w3_ref, w2_ref, yb_ref, xbuf_ref, xb_ref, ybuf_ref, zero_ref,
                   xsem, ysem, zsem):
    v = pl.program_id(0)
    f = pl.program_id(1)
    nf = pl.num_programs(1)
    valid = meta_ref[META_LANES + v]
    start = lambda copy: copy.start()
    wait = lambda copy: copy.wait()

    @pl.when(f == 0)
    def _():
        @pl.when(v == 0)
        def _():
            zero_ref[...] = jnp.zeros_like(zero_ref)
            _for_each_tail_piece(meta_ref, zero_ref, yb_ref, zsem, start)
            _for_each_tail_piece(meta_ref, zero_ref, yb_ref, zsem, wait)
            _for_each_x_piece(meta_ref, xs_ref, xbuf_ref, xsem, v, start)

        _for_each_x_piece(meta_ref, xs_ref, xbuf_ref, xsem, v, wait)

        @pl.when(v + 1 < pl.num_programs(0))
        def _():
            _for_each_x_piece(meta_ref, xs_ref, xbuf_ref, xsem, v + 1, start)

        for p in range(MOE_BLOCK // MOE_ROWS):
            @pl.when(p * MOE_ROWS < valid)
            def _():
                piece = slice(p * MOE_ROWS, (p + 1) * MOE_ROWS)
                xb_ref[piece, :] = xbuf_ref[v % 2, piece, :].astype(BF16)

    def previous_writeback_done():
        @pl.when(v > 0)
        def _():
            _for_each_y_piece(meta_ref, ybuf_ref, yb_ref, ysem, v - 1, wait)

    @pl.when((f == 0) & (valid == 0))
    def _():
        previous_writeback_done()

    def compute(nrows):
        x = xb_ref[:nrows, :]
        a = _silu(_dot(x, w1_ref[0, 0].astype(BF16))) * _dot(x, w3_ref[0, 0].astype(BF16))
        y = _dot(a.astype(BF16), w2_ref[0, 0].astype(BF16))

        @pl.when(f == 0)
        def _():
            previous_writeback_done()
            ybuf_ref[:nrows, :] = y

        @pl.when(f > 0)
        def _():
            ybuf_ref[:nrows, :] = y + ybuf_ref[:nrows, :]

    for nrows in range(MOE_ROWS, MOE_BLOCK + 1, MOE_ROWS):
        @pl.when((valid > nrows - MOE_ROWS) & (valid <= nrows))
        def _():
            compute(nrows)

    @pl.when(f == nf - 1)
    def _():
        _for_each_y_piece(meta_ref, ybuf_ref, yb_ref, ysem, v, start)

        @pl.when(v == pl.num_programs(0) - 1)
        def _():
            _for_each_y_piece(meta_ref, ybuf_ref, yb_ref, ysem, v, wait)


def _experts(meta, xs, w1, w3, w2, layer):
    d = D_MODEL
    nf = D_EXPERT // MOE_FC
    n_vis = lambda meta: meta[META_LANES - 1]
    used = lambda v, meta: jnp.minimum(v, n_vis(meta) - 1)
    chunk = lambda v, f, meta: jnp.where(v < n_vis(meta), f, nf - 1)
    w_in_spec = pl.BlockSpec((1, 1, d, MOE_FC), lambda v, f, meta: (layer, meta[used(v, meta)], 0, chunk(v, f, meta)))
    w_out_spec = pl.BlockSpec((1, 1, MOE_FC, d), lambda v, f, meta: (layer, meta[used(v, meta)], chunk(v, f, meta), 0))
    return pl.pallas_call(
        _expert_kernel,
        out_shape=jax.ShapeDtypeStruct((MOE_SLOTS, d), F32),
        grid_spec=pltpu.PrefetchScalarGridSpec(
            num_scalar_prefetch=1,
            grid=(meta[META_LANES - 1], nf),
            in_specs=[pl.BlockSpec(memory_space=pl.ANY), w_in_spec, w_in_spec, w_out_spec],
            out_specs=pl.BlockSpec(memory_space=pl.ANY),
            scratch_shapes=[pltpu.VMEM((2, MOE_BLOCK, d), F32), pltpu.VMEM((MOE_BLOCK, d), BF16),
                            pltpu.VMEM((MOE_BLOCK, d), F32), pltpu.VMEM((MOE_TAIL_PIECES[0], d), F32),
                            pltpu.SemaphoreType.DMA((2,)), pltpu.SemaphoreType.DMA, pltpu.SemaphoreType.DMA],
        ),
        compiler_params=_cparams(("arbitrary", "arbitrary")),
        name="moe_experts",
    )(meta, xs, w1, w3, w2)


def _combine_copy(dest_ref, yb_ref, buf_ref, sem, i, g, u, k):
    src = dest_ref[k * N_TOK + i * TOK_TM + g * ROW_GROUP + u]
    return pltpu.make_async_copy(yb_ref.at[pl.ds(src, 1)], buf_ref.at[k, g, pl.ds(u, 1)], sem)


def _combine_kernel(dest_ref, yb_ref, x_ref, gate_ref, w_ref, o_ref, buf_ref, sem):
    i = pl.program_id(0)

    def start(g, carry):
        for u in range(ROW_GROUP):
            for k in range(2):
                _combine_copy(dest_ref, yb_ref, buf_ref, sem, i, g, u, k).start()
        return carry

    def wait(g, carry):
        for u in range(ROW_GROUP):
            for k in range(2):
                _combine_copy(dest_ref, yb_ref, buf_ref, sem, i, g, u, k).wait()
        return carry

    lax.fori_loop(0, TOK_TM // ROW_GROUP, start, 0)
    lax.fori_loop(0, TOK_TM // ROW_GROUP, wait, 0)
    w = w_ref[...]
    rows = lambda k: buf_ref[k].reshape(TOK_TM, buf_ref.shape[-1])
    y = w[:, 0:1] * rows(0) + w[:, 1:2] * rows(1)
    o_ref[...] = x_ref[...] + gate_ref[0] * y


def _combine(dest_flat, yb, x2d, gate, w_tok):
    d = D_MODEL
    return pl.pallas_call(
        _combine_kernel,
        out_shape=jax.ShapeDtypeStruct((N_TOK, d), F32),
        grid_spec=pltpu.PrefetchScalarGridSpec(
            num_scalar_prefetch=1,
            grid=(N_TOK // TOK_TM,),
            in_specs=[pl.BlockSpec(memory_space=pl.ANY),
                      pl.BlockSpec((TOK_TM, d), lambda i, dest: (i, 0)),
                      pl.BlockSpec((1, 1, d), lambda i, dest: (i * TOK_TM // SEQ_LEN, 0, 0)),
                      pl.BlockSpec((TOK_TM, 2), lambda i, dest: (i, 0))],
            out_specs=pl.BlockSpec((TOK_TM, d), lambda i, dest: (i, 0)),
            scratch_shapes=[pltpu.VMEM((2, TOK_TM // ROW_GROUP, ROW_GROUP, d), F32), pltpu.SemaphoreType.DMA],
        ),
        compiler_params=_cparams(("arbitrary",)),
        name="moe_combine",
    )(dest_flat, yb, x2d, gate, w_tok)


def _moe(h2d, logits_t, x2d, gate, router_b, w1, w3, w2, layer):
    _, w_sel, dest, meta = _route(logits_t, router_b)
    dest_flat = dest.reshape(2 * N_TOK)
    meta_flat = meta.reshape(META_ROWS * META_LANES)
    xs = _dispatch(dest_flat, meta_flat, h2d)
    yb = _experts(meta_flat, xs, w1, w3, w2, layer)
    return _combine(dest_flat, yb, x2d, gate, w_sel.T)


def _rope_tables():
    f = HEAD_DIM // 4
    inv_freq = ROPE_THETA ** (-jnp.arange(f, dtype=F32) / f)
    pos = jnp.arange(SEQ_LEN)
    row = (pos // GRID_W).astype(F32)[:, None] * inv_freq
    col = (pos % GRID_W).astype(F32)[:, None] * inv_freq
    cos = jnp.concatenate([jnp.cos(row), jnp.cos(row), jnp.cos(col), jnp.cos(col)], axis=-1)
    sin = jnp.concatenate([-jnp.sin(row), jnp.sin(row), -jnp.sin(col), jnp.sin(col)], axis=-1)
    return cos, sin


def _cast_block_kernel(src_ref, x_ref, o_ref):
    del src_ref
    o_ref[...] = x_ref[...].astype(BF16)


def _permute_attn_w_in(w):
    widths = dict(aq=A_Q_DIM, ak=A_KV_DIM, av=A_KV_DIM, gq=GLA_K_DIM, gk=GLA_K_DIM, gv=GLA_V_DIM, gg=GLA_V_DIM)
    starts, col = {}, 0
    for name in ("aq", "ak", "av", "gq", "gk", "gv", "gg"):
        starts[name], col = col, col + widths[name]
    blk = A_KV_DIM
    src = [c // blk for name in ("aq", "gv", "gg", "gq", "gk", "ak", "av")
           for c in range(starts[name], starts[name] + widths[name], blk)]
    d = w.shape[0]
    main = pl.pallas_call(
        _cast_block_kernel,
        out_shape=jax.ShapeDtypeStruct((d, PROJ_DIM), BF16),
        grid_spec=pltpu.PrefetchScalarGridSpec(
            num_scalar_prefetch=1,
            grid=(len(src),),
            in_specs=[pl.BlockSpec((d, blk), lambda j, src: (0, src[j]))],
            out_specs=pl.BlockSpec((d, blk), lambda j, src: (0, j)),
        ),
        compiler_params=_cparams(("arbitrary",)),
        name="permute_cast_w_in",
    )(jnp.asarray(src, I32), w)
    lr = jnp.pad(w[:, PROJ_DIM:], ((0, 0), (0, LR_PAD - 2 * GLA_RANK)))
    return main, lr


def kernel(x, c, ctx, c_ctx, mod_w, mod_b, attn_w_in, attn_q_norm, attn_k_norm, attn_sink, gla_wa2, gla_ba,
           gla_norm, attn_w_out, conv_w_in, conv_w, conv_w_out, router_w, router_b, exp_w1, exp_w3, exp_w2):
    d = D_MODEL
    cc = jnp.concatenate([c, c_ctx[None], jnp.zeros((8 - N_BATCH - 1, d), F32)], axis=0)
    m = _modulation(cc, mod_w, mod_b)
    mods = [[m[l, :, i * d:(i + 1) * d].reshape(8, 1, d) for i in range(6)] for l in range(mod_w.shape[0])]
    lat_row = lambda i: i * PROJ_TM // SEQ_LEN
    ctx_row = lambda i: N_BATCH
    rw_pad = jnp.pad(router_w, ((0, 0), (0, ROUTE_LANES - N_EXPERTS)))
    rw_hi = rw_pad.astype(BF16)
    rwt = jnp.stack([rw_hi, (rw_pad - rw_hi.astype(F32)).astype(BF16)])
    x2d = x.reshape(N_TOK, d)

    w_main, w_lr = _permute_attn_w_in(attn_w_in[0])
    p_lat, lr_lat = _projection(x2d, mods[0][0], mods[0][1], w_main, w_lr, lat_row)
    p_ctx, lr_ctx = _projection(ctx.reshape(N_BATCH * CTX_LEN, d), mods[0][0], mods[0][1], w_main, w_lr, ctx_row)
    p_lat = p_lat.reshape(N_BATCH, SEQ_LEN, PROJ_DIM)
    p_ctx = p_ctx.reshape(N_BATCH, CTX_LEN, PROJ_DIM)
    cos, sin_signed = _rope_tables()
    attn = _attention(p_lat, p_ctx, cos, sin_signed, attn_q_norm[0][None], attn_k_norm[0][None], attn_sink[0])
    wa_pad = jnp.zeros((2, LR_PAD, GLA_K_DIM), F32)
    for di in range(2):
        wa_pad = wa_pad.at[di, di * GLA_RANK:(di + 1) * GLA_RANK].set(gla_wa2[0, di])
    wa_pad = wa_pad.astype(BF16)
    ba = gla_ba[0].reshape(2, 1, GLA_K_DIM)
    s_zero = jnp.zeros((N_BATCH, 2, GLA_HEADS, GLA_DV, GLA_DK), F32)
    s_ctx = _gla(p_ctx, lr_ctx.reshape(N_BATCH, CTX_LEN, LR_PAD), wa_pad, ba, None, s_zero, latent=False)
    gla = _gla(p_lat, lr_lat.reshape(N_BATCH, SEQ_LEN, LR_PAD), wa_pad, ba, gla_norm[0][None], s_ctx, latent=True)
    x1, h2, lg = _attn_out(attn.reshape(N_TOK, A_Q_DIM), gla.reshape(N_TOK, GLA_V_DIM), attn_w_out[0].astype(BF16),
                           x2d, mods[0][2], mods[0][3], mods[0][4], rwt)
    x2 = _moe(h2, lg, x1, mods[0][5], router_b, exp_w1, exp_w3, exp_w2, 0)

    g = _projection(x2, mods[1][0], mods[1][1], conv_w_in[0].astype(BF16), None, lat_row)
    x3, h2, lg = _conv_out(g, conv_w[0], conv_w_out[0].astype(BF16), x2, mods[1][2], mods[1][3], mods[1][4], rwt)
    x4 = _moe(h2, lg, x3, mods[1][5], router_b, exp_w1, exp_w3, exp_w2, 1)
    return x4.reshape(N_BATCH, SEQ_LEN, d)
```

```python
import functools

import jax
import jax.numpy as jnp
from jax import lax
from jax.experimental import pallas as pl
from jax.experimental.pallas import tpu as pltpu

F32 = jnp.float32
BF16 = jnp.bfloat16
I32 = jnp.int32

D_MODEL = 2048
N_BATCH = 4
SEQ_LEN = 2048
CTX_LEN = 256
N_TOK = N_BATCH * SEQ_LEN
GRID_W = 64
HEAD_DIM = 128
A_Q_HEADS = 8
A_KV_HEADS = 2
A_GROUP = A_Q_HEADS // A_KV_HEADS
WINDOW = 128
A_BLOCK = 128
ROPE_THETA = 10000.0
GLA_HEADS = 4
GLA_DK = 128
GLA_DV = 256
GLA_RANK = 16
GLA_TAU = 16.0
GLA_CHUNK = 64
N_EXPERTS = 32
N_GROUPS = 4
GROUP_SIZE = N_EXPERTS // N_GROUPS
D_EXPERT = 1024
EPS = 1e-6
NEG_INF = -1e30
A_Q_DIM = A_Q_HEADS * HEAD_DIM
A_KV_DIM = A_KV_HEADS * HEAD_DIM
GLA_K_DIM = GLA_HEADS * GLA_DK
GLA_V_DIM = GLA_HEADS * GLA_DV
PROJ_DIM = A_Q_DIM + 2 * A_KV_DIM + 2 * GLA_K_DIM + 2 * GLA_V_DIM
LR_PAD = 128

COL_AQ = 0
COL_GV = 1
COL_GG = 2
COL_GQ = 6
COL_GK = 7
COL_AK = 32
COL_AV = 34

MOD_TN = 2048
PROJ_TM = 512
PROJ_TN = 512
OUT_TM = 256
CONV_HALO = 16
GLA_ROWS = 512
ROUTE_TT = 512
ROUTE_LANES = 128
MOE_ALIGN = 8
MOE_BLOCK = 768
MOE_ROWS = 128
MOE_FC = 256
MOE_NVIS = N_EXPERTS + (2 * N_TOK) // MOE_BLOCK
MOE_TAIL_PIECES = (256, 128, 64, 32, 16, 8)
MOE_SLOTS = 2 * N_TOK + sum(MOE_TAIL_PIECES)
assert MOE_SLOTS >= 2 * N_TOK + N_EXPERTS * (MOE_ALIGN - 1) + MOE_ROWS
TOK_TM = 1024
ROW_GROUP = 8
META_ROWS = 5
META_LANES = 128
VMEM_LIMIT = 56 * 1024 * 1024


def _cparams(sem):
    return pltpu.CompilerParams(dimension_semantics=sem, vmem_limit_bytes=VMEM_LIMIT)


def _silu(x):
    return x * jax.nn.sigmoid(x)


def _dot(a, b):
    return jnp.dot(a, b, preferred_element_type=F32)


def _dot_nt(a, b):
    return lax.dot_general(a, b, (((1,), (1,)), ((), ())), preferred_element_type=F32)


def _dot_tn(a, b):
    return lax.dot_general(a, b, (((0,), (0,)), ((), ())), preferred_element_type=F32)


def _rms_modulate(x, shift, scale):
    r = lax.rsqrt(jnp.mean(x * x, axis=-1, keepdims=True) + EPS)
    return (x * r) * (1.0 + scale) + shift


def _mod_kernel(cc_ref, w_ref, b_ref, o_ref):
    a = _silu(cc_ref[...])
    o_ref[0] = _dot(a.astype(BF16), w_ref[0].astype(BF16)) + b_ref[0]


def _modulation(cc, mod_w, mod_b):
    depth, d, n = mod_w.shape
    return pl.pallas_call(
        _mod_kernel,
        out_shape=jax.ShapeDtypeStruct((depth, 8, n), F32),
        grid=(depth, n // MOD_TN),
        in_specs=[
            pl.BlockSpec((8, d), lambda l, j: (0, 0)),
            pl.BlockSpec((1, d, MOD_TN), lambda l, j: (l, 0, j)),
            pl.BlockSpec((1, 1, MOD_TN), lambda l, j: (l, 0, j)),
        ],
        out_specs=pl.BlockSpec((1, 8, MOD_TN), lambda l, j: (l, 0, j)),
        compiler_params=_cparams(("arbitrary", "arbitrary")),
        name="adaln_modulation",
    )(cc, mod_w, mod_b.reshape(depth, 1, n))


def _proj_kernel(with_lr, x_ref, sh_ref, sc_ref, w_ref, *rest):
    if with_lr:
        wlr_ref, o_ref, olr_ref = rest
    else:
        (o_ref,) = rest
    hb = _rms_modulate(x_ref[...], sh_ref[0], sc_ref[0]).astype(BF16)
    for c in range(w_ref.shape[1] // PROJ_TN):
        cols = slice(c * PROJ_TN, (c + 1) * PROJ_TN)
        o_ref[:, cols] = _dot(hb, w_ref[:, cols]).astype(BF16)
    if with_lr:
        olr_ref[...] = _dot(hb, wlr_ref[...].astype(BF16)).astype(BF16)


def _projection(x2d, shift, scale, w, w_lr, mod_row):
    rows, d = x2d.shape
    n = w.shape[1]
    with_lr = w_lr is not None
    resident = lambda shape: pl.BlockSpec(shape, lambda i: (0, 0), pipeline_mode=pl.Buffered(1))
    in_specs = [
        pl.BlockSpec((PROJ_TM, d), lambda i: (i, 0)),
        pl.BlockSpec((1, 1, d), lambda i: (mod_row(i), 0, 0)),
        pl.BlockSpec((1, 1, d), lambda i: (mod_row(i), 0, 0)),
        resident((d, n)),
    ]
    out_shape = [jax.ShapeDtypeStruct((rows, n), BF16)]
    out_specs = [pl.BlockSpec((PROJ_TM, n), lambda i: (i, 0))]
    args = [x2d, shift, scale, w]
    if with_lr:
        in_specs.append(resident((d, LR_PAD)))
        out_shape.append(jax.ShapeDtypeStruct((rows, LR_PAD), BF16))
        out_specs.append(pl.BlockSpec((PROJ_TM, LR_PAD), lambda i: (i, 0)))
        args.append(w_lr)
    res = pl.pallas_call(
        functools.partial(_proj_kernel, with_lr),
        out_shape=out_shape,
        grid=(rows // PROJ_TM,),
        in_specs=in_specs,
        out_specs=out_specs,
        compiler_params=_cparams(("arbitrary",)),
        name="norm_mod_projection",
    )(*args)
    return res if with_lr else res[0]


def _swap_halves32(x):
    lane = lax.broadcasted_iota(I32, x.shape, 1)
    return jnp.where((lane & 63) < 32, pltpu.roll(x, 96, 1), pltpu.roll(x, 32, 1))


def _qk_norm(x, gain):
    return x * lax.rsqrt(jnp.mean(x * x, axis=-1, keepdims=True) + EPS) * gain


def _rope(x, cos, sin_signed):
    return x * cos + _swap_halves32(x) * sin_signed


ATT_WIN0 = CTX_LEN
ATT_LAT0 = CTX_LEN + A_BLOCK
ATT_ROWS = CTX_LEN + SEQ_LEN + 2 * A_BLOCK
ATT_WIN = 3 * A_BLOCK


def _attn_kernel(sink_ref, q_ref, k_ref, v_ref, kx_ref, vx_ref, cos_ref, sin_ref, qg_ref, kg_ref, o_ref,
                 kn_ref, vn_ref, band_ref):
    kvh = pl.program_id(1)
    nb = SEQ_LEN // A_BLOCK
    k_gain = kg_ref[...]
    q_gain = qg_ref[...]

    pad = jnp.zeros((A_BLOCK, HEAD_DIM), BF16)
    kn_ref[0:CTX_LEN, :] = _qk_norm(kx_ref[0].astype(F32), k_gain).astype(BF16)
    vn_ref[0:CTX_LEN, :] = vx_ref[0]
    for ref in (kn_ref, vn_ref):
        ref[ATT_WIN0:ATT_LAT0, :] = pad
        ref[ATT_LAT0 + SEQ_LEN:, :] = pad
    vn_ref[ATT_LAT0:ATT_LAT0 + SEQ_LEN, :] = v_ref[0]
    prep_rows = 4 * A_BLOCK
    for c in range(SEQ_LEN // prep_rows):
        r = slice(c * prep_rows, (c + 1) * prep_rows)
        kc = _rope(_qk_norm(k_ref[0, r, :].astype(F32), k_gain), cos_ref[r, :], sin_ref[r, :])
        kn_ref[ATT_LAT0 + c * prep_rows:ATT_LAT0 + (c + 1) * prep_rows, :] = kc.astype(BF16)

    qi = lax.broadcasted_iota(I32, (A_BLOCK, ATT_WIN), 0)
    wj = lax.broadcasted_iota(I32, (A_BLOCK, ATT_WIN), 1)
    band = jnp.where(jnp.abs(wj - A_BLOCK - qi) <= WINDOW, 1.0, 0.0)
    band_ref[0] = jnp.where(wj >= A_BLOCK, band, 0.0)
    band_ref[1] = band
    band_ref[2] = jnp.where(wj < 2 * A_BLOCK, band, 0.0)

    rows = A_GROUP * A_BLOCK
    rcol = lax.broadcasted_iota(I32, (rows, 1), 0)
    sink = jnp.zeros((rows, 1), F32)
    for g in range(A_GROUP):
        sink = jnp.where((rcol >= g * A_BLOCK) & (rcol < (g + 1) * A_BLOCK), sink_ref[kvh * A_GROUP + g], sink)

    def query_block(n):
        qrows = pl.ds(pl.multiple_of(n * A_BLOCK, A_BLOCK), A_BLOCK)
        wrows = pl.ds(pl.multiple_of(ATT_WIN0 + n * A_BLOCK, A_BLOCK), ATT_WIN)
        cos_q, sin_q = cos_ref[qrows, :], sin_ref[qrows, :]
        q = q_ref[0, qrows, :].astype(F32)
        qs = []
        for g in range(A_GROUP):
            qh = _rope(_qk_norm(q[:, g * HEAD_DIM:(g + 1) * HEAD_DIM], q_gain), cos_q, sin_q)
            qs.append((qh * HEAD_DIM ** -0.5).astype(BF16))
        qs = jnp.concatenate(qs, axis=0)
        mask = band_ref[jnp.where(n == 0, 0, jnp.where(n == nb - 1, 2, 1))]
        mask = jnp.concatenate([mask] * A_GROUP, axis=0)
        s_ctx = _dot_nt(qs, kn_ref[0:CTX_LEN, :])
        s_win = jnp.where(mask > 0.5, _dot_nt(qs, kn_ref[wrows, :]), NEG_INF)
        m = jnp.maximum(jnp.max(s_ctx, axis=-1, keepdims=True), jnp.max(s_win, axis=-1, keepdims=True))
        m = jnp.maximum(m, sink)
        p_ctx = jnp.exp(s_ctx - m)
        p_win = jnp.exp(s_win - m)
        denom = (jnp.sum(p_ctx, axis=-1, keepdims=True) + jnp.sum(p_win, axis=-1, keepdims=True)
                 + jnp.exp(sink - m))
        o = (_dot(p_ctx.astype(BF16), vn_ref[0:CTX_LEN, :]) + _dot(p_win.astype(BF16), vn_ref[wrows, :])) / denom
        for g in range(A_GROUP):
            o_ref[0, qrows, g * HEAD_DIM:(g + 1) * HEAD_DIM] = o[g * A_BLOCK:(g + 1) * A_BLOCK].astype(BF16)

    def query_block_pair(i, carry):
        query_block(2 * i)
        query_block(2 * i + 1)
        return carry

    lax.fori_loop(0, nb // 2, query_block_pair, 0)


def _attention(p_lat, p_ctx, cos, sin_signed, q_gain, k_gain, sink):
    gw = A_GROUP * HEAD_DIM
    lat = lambda width, col: pl.BlockSpec((1, SEQ_LEN, width), lambda b, k, s: (b, 0, col + k))
    ctx_blk = lambda col: pl.BlockSpec((1, CTX_LEN, HEAD_DIM), lambda b, k, s: (b, 0, col + k))
    full = lambda shape: pl.BlockSpec(shape, lambda b, k, s: (0,) * len(shape))
    return pl.pallas_call(
        _attn_kernel,
        out_shape=jax.ShapeDtypeStruct((N_BATCH, SEQ_LEN, A_Q_DIM), BF16),
        grid_spec=pltpu.PrefetchScalarGridSpec(
            num_scalar_prefetch=1,
            grid=(N_BATCH, A_KV_HEADS),
            in_specs=[
                lat(gw, COL_AQ), lat(HEAD_DIM, COL_AK), lat(HEAD_DIM, COL_AV),
                ctx_blk(COL_AK), ctx_blk(COL_AV),
                full((SEQ_LEN, HEAD_DIM)), full((SEQ_LEN, HEAD_DIM)),
                full((1, HEAD_DIM)), full((1, HEAD_DIM)),
            ],
            out_specs=pl.BlockSpec((1, SEQ_LEN, gw), lambda b, k, s: (b, 0, k)),
            scratch_shapes=[pltpu.VMEM((ATT_ROWS, HEAD_DIM), BF16), pltpu.VMEM((ATT_ROWS, HEAD_DIM), BF16),
                            pltpu.VMEM((3, A_BLOCK, ATT_WIN), F32)],
        ),
        compiler_params=_cparams(("arbitrary", "arbitrary")),
        name="windowed_sink_attention",
    )(sink, p_lat, p_lat, p_lat, p_ctx, p_ctx, cos, sin_signed, q_gain, k_gain)


def _split3_bf16(x):
    hi = x.astype(BF16)
    r1 = x - hi.astype(F32)
    mid = r1.astype(BF16)
    lo = (r1 - mid.astype(F32)).astype(BF16)
    return hi, mid, lo


def _gla_kernel(latent, nblk, q_ref, k_ref, v_ref, *rest):
    if latent:
        g_ref, lr_ref, wa_ref, ba_ref, gain_ref, s0_ref, o_ref, st_ref, ofwd_ref = rest
    else:
        lr_ref, wa_ref, ba_ref, s0_ref, o_ref, st_ref = rest
    d = pl.program_id(1)
    j = pl.program_id(2)
    blk_rows = k_ref.shape[1]
    nc = blk_rows // GLA_CHUNK
    blk = jnp.where(d == 0, j, nblk - 1 - j)

    @pl.when(j == 0)
    def _():
        st_ref[...] = s0_ref[0, 0]

    ii = lax.broadcasted_iota(I32, (GLA_CHUNK, GLA_CHUNK), 0)
    jj = lax.broadcasted_iota(I32, (GLA_CHUNK, GLA_CHUNK), 1)

    def scan_block(backward):
        incl = (jj >= ii) if backward else (jj <= ii)
        incl_b = jnp.where(incl, 1.0, 0.0).astype(BF16)
        for i in range(nc):
            r0 = (nc - 1 - i if backward else i) * GLA_CHUNK
            rows = slice(r0, r0 + GLA_CHUNK)
            z = _dot(lr_ref[0, rows, :], wa_ref[0]) + ba_ref[0]
            la = (jnp.minimum(z, 0.0) - jnp.log(1.0 + jnp.exp(-jnp.abs(z)))) / GLA_TAU
            hi, mid, lo = _split3_bf16(la)
            bcum = _dot(incl_b, hi) + _dot(incl_b, mid) + _dot(incl_b, lo)
            blast = jnp.sum(la, axis=0, keepdims=True)
            k = k_ref[0, rows, :].astype(F32)
            kl = (k * jnp.exp(blast - bcum)).astype(BF16)
            decay = jnp.exp(blast)
            v = v_ref[0, rows, :]
            if latent:
                qf = (q_ref[0, rows, :].astype(F32) * GLA_DK ** -0.5 * jnp.exp(bcum)).astype(BF16)
                kf = (k * jnp.exp(-bcum)).astype(BF16)
            for h in range(GLA_HEADS):
                ks = slice(h * GLA_DK, (h + 1) * GLA_DK)
                vs = slice(h * GLA_DV, (h + 1) * GLA_DV)
                st = st_ref[h]
                if latent:
                    att = jnp.where(incl, _dot_nt(qf[:, ks], kf[:, ks]), 0.0)
                    o = _dot(att.astype(BF16), v[:, vs]) + _dot_nt(qf[:, ks], st.astype(BF16))
                    orow = pl.ds(pl.multiple_of(blk * blk_rows + r0, GLA_CHUNK), GLA_CHUNK)
                    if backward:
                        ot = o + ofwd_ref[orow, vs]
                        on = ot * lax.rsqrt(jnp.mean(ot * ot, axis=-1, keepdims=True) + EPS) * gain_ref[...]
                        o_ref[0, rows, vs] = (on * _silu(g_ref[0, rows, vs].astype(F32))).astype(BF16)
                    else:
                        ofwd_ref[orow, vs] = o
                st_ref[h] = st * decay[:, ks] + _dot_tn(v[:, vs], kl[:, ks])

    for backward in (False, True):
        @pl.when(d == int(backward))
        def _():
            scan_block(backward)

    if not latent:
        @pl.when(j == nblk - 1)
        def _():
            o_ref[0, 0] = st_ref[...]


def _gla(p3, lr3, wa_pad, ba, o_gain, s0, latent):
    n = p3.shape[1]
    blk_rows = min(GLA_ROWS, n)
    nblk = n // blk_rows
    seq_blk = lambda d, j: jnp.where(d == 0, j, nblk - 1 - j)
    in_specs = [
        pl.BlockSpec((1, blk_rows, GLA_K_DIM), lambda b, d, j: (b, seq_blk(d, j), COL_GQ)),
        pl.BlockSpec((1, blk_rows, GLA_K_DIM), lambda b, d, j: (b, seq_blk(d, j), COL_GK)),
        pl.BlockSpec((1, blk_rows, GLA_V_DIM), lambda b, d, j: (b, seq_blk(d, j), COL_GV)),
    ]
    args = [p3, p3, p3]
    if latent:
        in_specs.append(pl.BlockSpec((1, blk_rows, GLA_V_DIM), lambda b, d, j: (b, seq_blk(d, j), COL_GG)))
        args.append(p3)
    in_specs += [
        pl.BlockSpec((1, blk_rows, LR_PAD), lambda b, d, j: (b, seq_blk(d, j), 0)),
        pl.BlockSpec((1, LR_PAD, GLA_K_DIM), lambda b, d, j: (d, 0, 0)),
        pl.BlockSpec((1, 1, GLA_K_DIM), lambda b, d, j: (d, 0, 0)),
    ]
    args += [lr3, wa_pad, ba]
    if latent:
        in_specs.append(pl.BlockSpec((1, GLA_DV), lambda b, d, j: (0, 0)))
        args.append(o_gain)
    state_spec = pl.BlockSpec((1, 1, GLA_HEADS, GLA_DV, GLA_DK), lambda b, d, j: (b, d, 0, 0, 0))
    in_specs.append(state_spec)
    args.append(s0)
    scratch = [pltpu.VMEM((GLA_HEADS, GLA_DV, GLA_DK), F32)]
    if latent:
        out_shape = jax.ShapeDtypeStruct((N_BATCH, n, GLA_V_DIM), BF16)
        out_spec = pl.BlockSpec((1, blk_rows, GLA_V_DIM),
                                lambda b, d, j: (b, jnp.where(d == 0, nblk - 1, nblk - 1 - j), 0))
        scratch.append(pltpu.VMEM((n, GLA_V_DIM), F32))
    else:
        out_shape = jax.ShapeDtypeStruct(s0.shape, F32)
        out_spec = state_spec
    return pl.pallas_call(
        functools.partial(_gla_kernel, latent, nblk),
        out_shape=out_shape,
        grid=(N_BATCH, 2, nblk),
        in_specs=in_specs,
        out_specs=out_spec,
        scratch_shapes=scratch,
        compiler_params=_cparams(("arbitrary", "arbitrary", "arbitrary")),
        name="gla_latent" if latent else "gla_context_state",
    )(*args)


def _router_logits(h, rw_ref):
    hh = h.astype(BF16)
    hl = (h - hh.astype(F32)).astype(BF16)
    return _dot(hh, rw_ref[0]) + _dot(hl, rw_ref[0]) + _dot(hh, rw_ref[1])


def _residual_and_moe_input(mix, x_ref, gate_ref, sh_ref, sc_ref, rwt_ref, x_out_ref, h_out_ref, lg_ref):
    x1 = x_ref[...] + gate_ref[0] * mix
    x_out_ref[...] = x1
    h = _rms_modulate(x1, sh_ref[0], sc_ref[0])
    h_out_ref[...] = h
    lg_ref[...] = _router_logits(h, rwt_ref)


def _attn_out_kernel(a1_ref, a2_ref, w_ref, x_ref, gate_ref, sh_ref, sc_ref, rwt_ref, x_out_ref, h_out_ref, lg_ref):
    mix = _dot(a1_ref[...], w_ref[0:A_Q_DIM, :]) + _dot(a2_ref[...], w_ref[A_Q_DIM:, :])
    _residual_and_moe_input(mix, x_ref, gate_ref, sh_ref, sc_ref, rwt_ref, x_out_ref, h_out_ref, lg_ref)


def _conv_out_kernel(gb_ref, gc_ref, u_ref, gcp_ref, up_ref, gcn_ref, un_ref, cw_ref, w_ref,
                     x_ref, gate_ref, sh_ref, sc_ref, rwt_ref, x_out_ref, h_out_ref, lg_ref):
    i = pl.program_id(0)
    tiles_per_seq = SEQ_LEN // OUT_TM
    t = gc_ref[...].astype(F32) * u_ref[...].astype(F32)
    first = (i % tiles_per_seq) == 0
    last = (i % tiles_per_seq) == tiles_per_seq - 1
    halo_last = slice(CONV_HALO - 1, CONV_HALO)
    t_before = jnp.where(first, 0.0, gcp_ref[halo_last, :].astype(F32) * up_ref[halo_last, :].astype(F32))
    t_after = jnp.where(last, 0.0, gcn_ref[0:1, :].astype(F32) * un_ref[0:1, :].astype(F32))
    row = lax.broadcasted_iota(I32, t.shape, 0)
    t_up = jnp.where(row == 0, t_before, pltpu.roll(t, 1, 0))
    t_dn = jnp.where(row == OUT_TM - 1, t_after, pltpu.roll(t, OUT_TM - 1, 0))
    y = cw_ref[0:1, :] * t_up + cw_ref[1:2, :] * t + cw_ref[2:3, :] * t_dn
    mix = _dot((gb_ref[...].astype(F32) * y).astype(BF16), w_ref[...])
    _residual_and_moe_input(mix, x_ref, gate_ref, sh_ref, sc_ref, rwt_ref, x_out_ref, h_out_ref, lg_ref)


def _mixer_out(kernel_fn, mixer_specs, mixer_args, x2d, gate, shift, scale, rwt, name):
    d = D_MODEL
    mod_spec = pl.BlockSpec((1, 1, d), lambda i: (i * OUT_TM // SEQ_LEN, 0, 0))
    row_spec = pl.BlockSpec((OUT_TM, d), lambda i: (i, 0))
    return pl.pallas_call(
        kernel_fn,
        out_shape=[jax.ShapeDtypeStruct((N_TOK, d), F32), jax.ShapeDtypeStruct((N_TOK, d), F32),
                   jax.ShapeDtypeStruct((N_TOK, ROUTE_LANES), F32)],
        grid=(N_TOK // OUT_TM,),
        in_specs=mixer_specs + [row_spec, mod_spec, mod_spec, mod_spec,
                                pl.BlockSpec((2, d, ROUTE_LANES), lambda i: (0, 0, 0))],
        out_specs=[row_spec, row_spec, pl.BlockSpec((OUT_TM, ROUTE_LANES), lambda i: (i, 0))],
        compiler_params=_cparams(("arbitrary",)),
        name=name,
    )(*mixer_args, x2d, gate, shift, scale, rwt)


def _attn_out(attn2d, gla2d, w_out, x2d, gate, shift, scale, rwt):
    half = pl.BlockSpec((OUT_TM, A_Q_DIM), lambda i: (i, 0))
    specs = [half, half, pl.BlockSpec((D_MODEL, D_MODEL), lambda i: (0, 0))]
    return _mixer_out(_attn_out_kernel, specs, [attn2d, gla2d, w_out], x2d, gate, shift, scale, rwt,
                      "attn_out_projection")


def _conv_out(g2d, conv_w, w_out, x2d, gate, shift, scale, rwt):
    d = D_MODEL
    sub = OUT_TM // CONV_HALO
    last_halo = N_TOK // CONV_HALO - 1
    main = lambda col: pl.BlockSpec((OUT_TM, d), lambda i: (i, col))
    before = lambda col: pl.BlockSpec((CONV_HALO, d), lambda i: (jnp.maximum(i * sub - 1, 0), col))
    after = lambda col: pl.BlockSpec((CONV_HALO, d), lambda i: (jnp.minimum((i + 1) * sub, last_halo), col))
    specs = [main(0), main(1), main(2), before(1), before(2), after(1), after(2),
             pl.BlockSpec((3, d), lambda i: (0, 0)), pl.BlockSpec((d, d), lambda i: (0, 0))]
    return _mixer_out(_conv_out_kernel, specs, [g2d] * 7 + [conv_w, w_out], x2d, gate, shift, scale, rwt,
                      "conv_out_projection")


def _first_argmax8(x, idx8):
    m = jnp.max(x, axis=0, keepdims=True)
    a = jnp.min(jnp.where(x == m, idx8, float(GROUP_SIZE)), axis=0, keepdims=True)
    return m, a


def _route_kernel(lg_ref, rb_ref, e_ref, w_ref, dest_ref, meta_ref, cnt_ref, carry_ref, pstart_ref):
    phase = pl.program_id(0)
    j = pl.program_id(1)
    tt = ROUTE_TT
    sc = jax.nn.sigmoid(lg_ref[...].T[:N_EXPERTS])
    grp = sc + rb_ref[...]
    idx8 = lax.broadcasted_iota(I32, (GROUP_SIZE, tt), 0).astype(F32)
    groups = [grp[g * GROUP_SIZE:(g + 1) * GROUP_SIZE] for g in range(N_GROUPS)]
    gscore = []
    for x in groups:
        m1, a1 = _first_argmax8(x, idx8)
        m2, _ = _first_argmax8(jnp.where(idx8 == a1, -jnp.inf, x), idx8)
        gscore.append(m1 + m2)
    gmax = functools.reduce(jnp.maximum, gscore)
    gsel = jnp.full((1, tt), float(N_GROUPS), F32)
    for g in reversed(range(N_GROUPS)):
        gsel = jnp.where(gscore[g] == gmax, float(g), gsel)
    in_grp = groups[0]
    for g in range(1, N_GROUPS):
        in_grp = jnp.where(gsel == float(g), groups[g], in_grp)
    _, a1 = _first_argmax8(in_grp, idx8)
    _, a2 = _first_argmax8(jnp.where(idx8 == a1, -jnp.inf, in_grp), idx8)
    e0 = gsel * GROUP_SIZE + a1
    e1 = gsel * GROUP_SIZE + a2
    idx32 = lax.broadcasted_iota(I32, (N_EXPERTS, tt), 0).astype(F32)
    oh0 = idx32 == e0
    oh1 = idx32 == e1
    w0 = jnp.sum(jnp.where(oh0, sc, 0.0), axis=0, keepdims=True)
    w1 = jnp.sum(jnp.where(oh1, sc, 0.0), axis=0, keepdims=True)
    wsum = w0 + w1
    ohs = jnp.where(oh0 | oh1, 1.0, 0.0)
    tile_cnt = jnp.sum(ohs, axis=1, keepdims=True)

    @pl.when(phase == 0)
    def _():
        @pl.when(j == 0)
        def _():
            cnt_ref[...] = jnp.zeros_like(cnt_ref)
        cnt_ref[...] += jnp.broadcast_to(tile_cnt, cnt_ref.shape)

    @pl.when(phase == 1)
    def _():
        @pl.when(j == 0)
        def _():
            cnt = cnt_ref[...]
            expert = lax.broadcasted_iota(I32, cnt.shape, 0)
            lane = lax.broadcasted_iota(I32, cnt.shape, 1).astype(F32)

            def exclusive_cumsum(v):
                inc = v
                for s in (1, 2, 4, 8, 16):
                    inc = inc + jnp.where(expert >= s, pltpu.roll(inc, s, 0), 0.0)
                return inc - v

            cnt_pad = jnp.floor((cnt + (MOE_ALIGN - 1)) * (1.0 / MOE_ALIGN)) * MOE_ALIGN
            pstart = exclusive_cumsum(cnt_pad)
            pstart_ref[...] = pstart
            carry_ref[...] = jnp.zeros_like(carry_ref)
            nvis_e = jnp.zeros_like(cnt)
            for kb in range(-(-N_TOK // MOE_BLOCK)):
                nvis_e += jnp.where(cnt > float(kb * MOE_BLOCK), 1.0, 0.0)
            vstart = exclusive_cumsum(nvis_e)
            vis_exp = jnp.sum(jnp.where(lane >= vstart + nvis_e, 1.0, 0.0), axis=0, keepdims=True)
            vis_exp = jnp.minimum(vis_exp, float(N_EXPERTS - 1))
            own = expert.astype(F32) == vis_exp
            of_visit = lambda col: jnp.sum(jnp.where(own, col, 0.0), axis=0, keepdims=True)
            kth = lane[0:1] - of_visit(vstart)
            vis_rows = jnp.clip(of_visit(cnt_pad) - kth * MOE_BLOCK, 0.0, float(MOE_BLOCK))
            vis_row0 = of_visit(pstart) + kth * MOE_BLOCK
            last_lane = lane[0:1] == float(META_LANES - 1)
            expert_lane = expert.astype(F32) == lane
            to_lane = lambda col: jnp.sum(jnp.where(expert_lane, col, 0.0), axis=0, keepdims=True)
            meta_ref[...] = jnp.concatenate([
                jnp.where(last_lane, jnp.sum(nvis_e, axis=0, keepdims=True), vis_exp),
                vis_rows,
                vis_row0,
                to_lane(pstart + cnt),
                jnp.where(last_lane, jnp.sum(cnt_pad, axis=0, keepdims=True), to_lane(cnt_pad - cnt)),
            ], axis=0).astype(I32)

        rr = lax.broadcasted_iota(I32, (tt, tt), 0)
        cc = lax.broadcasted_iota(I32, (tt, tt), 1)
        strict_upper = jnp.where(rr < cc, 1.0, 0.0).astype(BF16)
        before = _dot(ohs.astype(BF16), strict_upper)
        pos = before + carry_ref[:, 0:1] + pstart_ref[:, 0:1]
        d0 = jnp.sum(jnp.where(oh0, pos, 0.0), axis=0, keepdims=True)
        d1 = jnp.sum(jnp.where(oh1, pos, 0.0), axis=0, keepdims=True)
        carry_ref[...] += jnp.broadcast_to(tile_cnt, carry_ref.shape)
        e_ref[...] = jnp.concatenate([e0, e1], axis=0).astype(I32)
        w_ref[...] = jnp.concatenate([w0 / wsum, w1 / wsum], axis=0)
        dest_ref[...] = jnp.concatenate([d0, d1], axis=0).astype(I32)


def _route(logits_t, router_b):
    nt = N_TOK // ROUTE_TT
    tok = lambda dt: jax.ShapeDtypeStruct((2, N_TOK), dt)
    tok_spec = pl.BlockSpec((2, ROUTE_TT), lambda p, j: (0, j * p))
    return pl.pallas_call(
        _route_kernel,
        out_shape=[tok(I32), tok(F32), tok(I32), jax.ShapeDtypeStruct((META_ROWS, META_LANES), I32)],
        grid=(2, nt),
        in_specs=[pl.BlockSpec((ROUTE_TT, ROUTE_LANES), lambda p, j: (j, 0)),
                  pl.BlockSpec((N_EXPERTS, 1), lambda p, j: (0, 0))],
        out_specs=[tok_spec, tok_spec, tok_spec, pl.BlockSpec((META_ROWS, META_LANES), lambda p, j: (0, 0))],
        scratch_shapes=[pltpu.VMEM((N_EXPERTS, 128), F32), pltpu.VMEM((N_EXPERTS, 128), F32),
                        pltpu.VMEM((N_EXPERTS, 128), F32)],
        compiler_params=_cparams(("arbitrary", "arbitrary")),
        name="router_slot_assignment",
    )(logits_t, router_b.reshape(N_EXPERTS, 1))


def _dispatch_copy(dest_ref, h_ref, xs_ref, sem, i, g, u, k):
    dst = dest_ref[k * N_TOK + i * TOK_TM + g * ROW_GROUP + u]
    return pltpu.make_async_copy(h_ref.at[g, pl.ds(u, 1)], xs_ref.at[pl.ds(dst, 1)], sem)


def _for_each_tail_piece(meta_ref, zero_ref, slots_ref, sem, fn):
    used = meta_ref[4 * META_LANES + META_LANES - 1]
    tail = MOE_SLOTS - used
    row = used
    for size in MOE_TAIL_PIECES:
        take = tail & size

        @pl.when(take != 0)
        def _():
            fn(pltpu.make_async_copy(zero_ref.at[pl.ds(0, size)],
                                     slots_ref.at[pl.ds(pl.multiple_of(row, MOE_ALIGN), size)], sem))
        row = row + take


def _for_each_padding_row(meta_ref, zero_ref, xs_ref, sem, fn):
    def expert_padding(e, carry):
        row = meta_ref[3 * META_LANES + e]
        npad = meta_ref[4 * META_LANES + e]
        for r in range(MOE_ALIGN - 1):
            @pl.when(r < npad)
            def _():
                fn(pltpu.make_async_copy(zero_ref.at[pl.ds(0, 1)], xs_ref.at[pl.ds(row + r, 1)], sem))
        return carry

    lax.fori_loop(0, N_EXPERTS, expert_padding, 0)


def _zero_fill_padding(meta_ref, zero_ref, xs_ref, sem):
    zero_ref[...] = jnp.zeros_like(zero_ref)
    for fn in (lambda copy: copy.start(), lambda copy: copy.wait()):
        _for_each_padding_row(meta_ref, zero_ref, xs_ref, sem, fn)
        _for_each_tail_piece(meta_ref, zero_ref, xs_ref, sem, fn)


def _dispatch_kernel(dest_ref, meta_ref, h_ref, xs_ref, zero_ref, sem):
    i = pl.program_id(0)

    @pl.when(i == 0)
    def _():
        _zero_fill_padding(meta_ref, zero_ref, xs_ref, sem)

    def start(g, carry):
        for u in range(ROW_GROUP):
            for k in range(2):
                _dispatch_copy(dest_ref, h_ref, xs_ref, sem, i, g, u, k).start()
        return carry

    def wait(g, carry):
        for u in range(ROW_GROUP):
            for k in range(2):
                _dispatch_copy(dest_ref, h_ref, xs_ref, sem, i, g, u, k).wait()
        return carry

    lax.fori_loop(0, TOK_TM // ROW_GROUP, start, 0)
    lax.fori_loop(0, TOK_TM // ROW_GROUP, wait, 0)


def _dispatch(dest_flat, meta_flat, h2d):
    d = D_MODEL
    return pl.pallas_call(
        _dispatch_kernel,
        out_shape=jax.ShapeDtypeStruct((MOE_SLOTS, d), F32),
        grid_spec=pltpu.PrefetchScalarGridSpec(
            num_scalar_prefetch=2,
            grid=(N_TOK // TOK_TM,),
            in_specs=[pl.BlockSpec((TOK_TM // ROW_GROUP, ROW_GROUP, d), lambda i, dest, meta: (i, 0, 0))],
            out_specs=pl.BlockSpec(memory_space=pl.ANY),
            scratch_shapes=[pltpu.VMEM((MOE_TAIL_PIECES[0], d), F32), pltpu.SemaphoreType.DMA],
        ),
        compiler_params=_cparams(("arbitrary",)),
        name="moe_dispatch",
    )(dest_flat, meta_flat, h2d.reshape(N_TOK // ROW_GROUP, ROW_GROUP, d))


def _for_each_x_piece(meta_ref, xs_ref, xbuf_ref, sem, v, fn):
    rows = meta_ref[META_LANES + v]
    row0 = meta_ref[2 * META_LANES + v]
    slot = v % 2
    for p in range(MOE_BLOCK // MOE_ROWS):
        @pl.when(p * MOE_ROWS < rows)
        def _():
            src = xs_ref.at[pl.ds(pl.multiple_of(row0 + p * MOE_ROWS, MOE_ALIGN), MOE_ROWS)]
            fn(pltpu.make_async_copy(src, xbuf_ref.at[slot, pl.ds(p * MOE_ROWS, MOE_ROWS)], sem.at[slot]))


def _for_each_y_piece(meta_ref, ybuf_ref, yb_ref, sem, v, fn):
    rows = meta_ref[META_LANES + v]
    row0 = meta_ref[2 * META_LANES + v]

    def piece(off, size):
        dst = yb_ref.at[pl.ds(pl.multiple_of(row0 + off, MOE_ALIGN), size)]
        fn(pltpu.make_async_copy(ybuf_ref.at[pl.ds(pl.multiple_of(off, MOE_ALIGN), size)], dst, sem))

    for p in range(MOE_BLOCK // MOE_ROWS):
        @pl.when((p + 1) * MOE_ROWS <= rows)
        def _():
            piece(p * MOE_ROWS, MOE_ROWS)
    off = (rows // MOE_ROWS) * MOE_ROWS
    size = MOE_ROWS // 2
    while size >= MOE_ALIGN:
        take = (rows - off) & size

        @pl.when(take != 0)
        def _():
            piece(off, size)
        off = off + take
        size //= 2


def _expert_kernel(meta_ref, xs_ref, w1_ref, w3_ref, w2_ref, yb_ref, xbuf_ref, xb_ref, ybuf_ref, zero_ref,
                   xsem, ysem, zsem):
    v = pl.program_id(0)
    f = pl.program_id(1)
    nf = pl.num_programs(1)
    valid = meta_ref[META_LANES + v]
    start = lambda copy: copy.start()
    wait = lambda copy: copy.wait()

    @pl.when(f == 0)
    def _():
        @pl.when(v == 0)
        def _():
            zero_ref[...] = jnp.zeros_like(zero_ref)
            _for_each_tail_piece(meta_ref, zero_ref, yb_ref, zsem, start)
            _for_each_tail_piece(meta_ref, zero_ref, yb_ref, zsem, wait)
            _for_each_x_piece(meta_ref, xs_ref, xbuf_ref, xsem, v, start)

        _for_each_x_piece(meta_ref, xs_ref, xbuf_ref, xsem, v, wait)

        @pl.when(v + 1 < pl.num_programs(0))
        def _():
            _for_each_x_piece(meta_ref, xs_ref, xbuf_ref, xsem, v + 1, start)

        for p in range(MOE_BLOCK // MOE_ROWS):
            @pl.when(p * MOE_ROWS < valid)
            def _():
                piece = slice(p * MOE_ROWS, (p + 1) * MOE_ROWS)
                xb_ref[piece, :] = xbuf_ref[v % 2, piece, :].astype(BF16)

    def previous_writeback_done():
        @pl.when(v > 0)
        def _():
            _for_each_y_piece(meta_ref, ybuf_ref, yb_ref, ysem, v - 1, wait)

    @pl.when((f == 0) & (valid == 0))
    def _():
        previous_writeback_done()

    def compute(nrows):
        x = xb_ref[:nrows, :]
        a = _silu(_dot(x, w1_ref[0, 0].astype(BF16))) * _dot(x, w3_ref[0, 0].astype(BF16))
        y = _dot(a.astype(BF16), w2_ref[0, 0].astype(BF16))

        @pl.when(f == 0)
        def _():
            previous_writeback_done()
            ybuf_ref[:nrows, :] = y

        @pl.when(f > 0)
        def _():
            ybuf_ref[:nrows, :] = y + ybuf_ref[:nrows, :]

    for nrows in range(MOE_ROWS, MOE_BLOCK + 1, MOE_ROWS):
        @pl.when((valid > nrows - MOE_ROWS) & (valid <= nrows))
        def _():
            compute(nrows)

    @pl.when(f == nf - 1)
    def _():
        _for_each_y_piece(meta_ref, ybuf_ref, yb_ref, ysem, v, start)

        @pl.when(v == pl.num_programs(0) - 1)
        def _():
            _for_each_y_piece(meta_ref, ybuf_ref, yb_ref, ysem, v, wait)


def _experts(meta, xs, w1, w3, w2, layer):
    d = D_MODEL
    nf = D_EXPERT // MOE_FC
    n_vis = lambda meta: meta[META_LANES - 1]
    used = lambda v, meta: jnp.minimum(v, n_vis(meta) - 1)
    chunk = lambda v, f, meta: jnp.where(v < n_vis(meta), f, nf - 1)
    w_in_spec = pl.BlockSpec((1, 1, d, MOE_FC), lambda v, f, meta: (layer, meta[used(v, meta)], 0, chunk(v, f, meta)))
    w_out_spec = pl.BlockSpec((1, 1, MOE_FC, d), lambda v, f, meta: (layer, meta[used(v, meta)], chunk(v, f, meta), 0))
    return pl.pallas_call(
        _expert_kernel,
        out_shape=jax.ShapeDtypeStruct((MOE_SLOTS, d), F32),
        grid_spec=pltpu.PrefetchScalarGridSpec(
            num_scalar_prefetch=1,
            grid=(meta[META_LANES - 1], nf),
            in_specs=[pl.BlockSpec(memory_space=pl.ANY), w_in_spec, w_in_spec, w_out_spec],
            out_specs=pl.BlockSpec(memory_space=pl.ANY),
            scratch_shapes=[pltpu.VMEM((2, MOE_BLOCK, d), F32), pltpu.VMEM((MOE_BLOCK, d), BF16),
                            pltpu.VMEM((MOE_BLOCK, d), F32), pltpu.VMEM((MOE_TAIL_PIECES[0], d), F32),
                            pltpu.SemaphoreType.DMA((2,)), pltpu.SemaphoreType.DMA, pltpu.SemaphoreType.DMA],
        ),
        compiler_params=_cparams(("arbitrary", "arbitrary")),
        name="moe_experts",
    )(meta, xs, w1, w3, w2)


def _combine_copy(dest_ref, yb_ref, buf_ref, sem, i, g, u, k):
    src = dest_ref[k * N_TOK + i * TOK_TM + g * ROW_GROUP + u]
    return pltpu.make_async_copy(yb_ref.at[pl.ds(src, 1)], buf_ref.at[k, g, pl.ds(u, 1)], sem)


def _combine_kernel(dest_ref, yb_ref, x_ref, gate_ref, w_ref, o_ref, buf_ref, sem):
    i = pl.program_id(0)

    def start(g, carry):
        for u in range(ROW_GROUP):
            for k in range(2):
                _combine_copy(dest_ref, yb_ref, buf_ref, sem, i, g, u, k).start()
        return carry

    def wait(g, carry):
        for u in range(ROW_GROUP):
            for k in range(2):
                _combine_copy(dest_ref, yb_ref, buf_ref, sem, i, g, u, k).wait()
        return carry

    lax.fori_loop(0, TOK_TM // ROW_GROUP, start, 0)
    lax.fori_loop(0, TOK_TM // ROW_GROUP, wait, 0)
    w = w_ref[...]
    rows = lambda k: buf_ref[k].reshape(TOK_TM, buf_ref.shape[-1])
    y = w[:, 0:1] * rows(0) + w[:, 1:2] * rows(1)
    o_ref[...] = x_ref[...] + gate_ref[0] * y


def _combine(dest_flat, yb, x2d, gate, w_tok):
    d = D_MODEL
    return pl.pallas_call(
        _combine_kernel,
        out_shape=jax.ShapeDtypeStruct((N_TOK, d), F32),
        grid_spec=pltpu.PrefetchScalarGridSpec(
            num_scalar_prefetch=1,
            grid=(N_TOK // TOK_TM,),
            in_specs=[pl.BlockSpec(memory_space=pl.ANY),
                      pl.BlockSpec((TOK_TM, d), lambda i, dest: (i, 0)),
                      pl.BlockSpec((1, 1, d), lambda i, dest: (i * TOK_TM // SEQ_LEN, 0, 0)),
                      pl.BlockSpec((TOK_TM, 2), lambda i, dest: (i, 0))],
            out_specs=pl.BlockSpec((TOK_TM, d), lambda i, dest: (i, 0)),
            scratch_shapes=[pltpu.VMEM((2, TOK_TM // ROW_GROUP, ROW_GROUP, d), F32), pltpu.SemaphoreType.DMA],
        ),
        compiler_params=_cparams(("arbitrary",)),
        name="moe_combine",
    )(dest_flat, yb, x2d, gate, w_tok)


def _moe(h2d, logits_t, x2d, gate, router_b, w1, w3, w2, layer):
    _, w_sel, dest, meta = _route(logits_t, router_b)
    dest_flat = dest.reshape(2 * N_TOK)
    meta_flat = meta.reshape(META_ROWS * META_LANES)
    xs = _dispatch(dest_flat, meta_flat, h2d)
    yb = _experts(meta_flat, xs, w1, w3, w2, layer)
    return _combine(dest_flat, yb, x2d, gate, w_sel.T)


def _rope_tables():
    f = HEAD_DIM // 4
    inv_freq = ROPE_THETA ** (-jnp.arange(f, dtype=F32) / f)
    pos = jnp.arange(SEQ_LEN)
    row = (pos // GRID_W).astype(F32)[:, None] * inv_freq
    col = (pos % GRID_W).astype(F32)[:, None] * inv_freq
    cos = jnp.concatenate([jnp.cos(row), jnp.cos(row), jnp.cos(col), jnp.cos(col)], axis=-1)
    sin = jnp.concatenate([-jnp.sin(row), jnp.sin(row), -jnp.sin(col), jnp.sin(col)], axis=-1)
    return cos, sin


def _cast_block_kernel(src_ref, x_ref, o_ref):
    del src_ref
    o_ref[...] = x_ref[...].astype(BF16)


def _permute_attn_w_in(w):
    widths = dict(aq=A_Q_DIM, ak=A_KV_DIM, av=A_KV_DIM, gq=GLA_K_DIM, gk=GLA_K_DIM, gv=GLA_V_DIM, gg=GLA_V_DIM)
    starts, col = {}, 0
    for name in ("aq", "ak", "av", "gq", "gk", "gv", "gg"):
        starts[name], col = col, col + widths[name]
    blk = A_KV_DIM
    src = [c // blk for name in ("aq", "gv", "gg", "gq", "gk", "ak", "av")
           for c in range(starts[name], starts[name] + widths[name], blk)]
    d = w.shape[0]
    main = pl.pallas_call(
        _cast_block_kernel,
        out_shape=jax.ShapeDtypeStruct((d, PROJ_DIM), BF16),
        grid_spec=pltpu.PrefetchScalarGridSpec(
            num_scalar_prefetch=1,
            grid=(len(src),),
            in_specs=[pl.BlockSpec((d, blk), lambda j, src: (0, src[j]))],
            out_specs=pl.BlockSpec((d, blk), lambda j, src: (0, j)),
        ),
        compiler_params=_cparams(("arbitrary",)),
        name="permute_cast_w_in",
    )(jnp.asarray(src, I32), w)
    lr = jnp.pad(w[:, PROJ_DIM:], ((0, 0), (0, LR_PAD - 2 * GLA_RANK)))
    return main, lr


def kernel(x, c, ctx, c_ctx, mod_w, mod_b, attn_w_in, attn_q_norm, attn_k_norm, attn_sink, gla_wa2, gla_ba,
           gla_norm, attn_w_out, conv_w_in, conv_w, conv_w_out, router_w, router_b, exp_w1, exp_w3, exp_w2):
    d = D_MODEL
    cc = jnp.concatenate([c, c_ctx[None], jnp.zeros((8 - N_BATCH - 1, d), F32)], axis=0)
    m = _modulation(cc, mod_w, mod_b)
    mods = [[m[l, :, i * d:(i + 1) * d].reshape(8, 1, d) for i in range(6)] for l in range(mod_w.shape[0])]
    lat_row = lambda i: i * PROJ_TM // SEQ_LEN
    ctx_row = lambda i: N_BATCH
    rw_pad = jnp.pad(router_w, ((0, 0), (0, ROUTE_LANES - N_EXPERTS)))
    rw_hi = rw_pad.astype(BF16)
    rwt = jnp.stack([rw_hi, (rw_pad - rw_hi.astype(F32)).astype(BF16)])
    x2d = x.reshape(N_TOK, d)

    w_main, w_lr = _permute_attn_w_in(attn_w_in[0])
    p_lat, lr_lat = _projection(x2d, mods[0][0], mods[0][1], w_main, w_lr, lat_row)
    p_ctx, lr_ctx = _projection(ctx.reshape(N_BATCH * CTX_LEN, d), mods[0][0], mods[0][1], w_main, w_lr, ctx_row)
    p_lat = p_lat.reshape(N_BATCH, SEQ_LEN, PROJ_DIM)
    p_ctx = p_ctx.reshape(N_BATCH, CTX_LEN, PROJ_DIM)
    cos, sin_signed = _rope_tables()
    attn = _attention(p_lat, p_ctx, cos, sin_signed, attn_q_norm[0][None], attn_k_norm[0][None], attn_sink[0])
    wa_pad = jnp.zeros((2, LR_PAD, GLA_K_DIM), F32)
    for di in range(2):
        wa_pad = wa_pad.at[di, di * GLA_RANK:(di + 1) * GLA_RANK].set(gla_wa2[0, di])
    wa_pad = wa_pad.astype(BF16)
    ba = gla_ba[0].reshape(2, 1, GLA_K_DIM)
    s_zero = jnp.zeros((N_BATCH, 2, GLA_HEADS, GLA_DV, GLA_DK), F32)
    s_ctx = _gla(p_ctx, lr_ctx.reshape(N_BATCH, CTX_LEN, LR_PAD), wa_pad, ba, None, s_zero, latent=False)
    gla = _gla(p_lat, lr_lat.reshape(N_BATCH, SEQ_LEN, LR_PAD), wa_pad, ba, gla_norm[0][None], s_ctx, latent=True)
    x1, h2, lg = _attn_out(attn.reshape(N_TOK, A_Q_DIM), gla.reshape(N_TOK, GLA_V_DIM), attn_w_out[0].astype(BF16),
                           x2d, mods[0][2], mods[0][3], mods[0][4], rwt)
    x2 = _moe(h2, lg, x1, mods[0][5], router_b, exp_w1, exp_w3, exp_w2, 0)

    g = _projection(x2, mods[1][0], mods[1][1], conv_w_in[0].astype(BF16), None, lat_row)
    x3, h2, lg = _conv_out(g, conv_w[0], conv_w_out[0].astype(BF16), x2, mods[1][2], mods[1][3], mods[1][4], rwt)
    x4 = _moe(h2, lg, x3, mods[1][5], router_b, exp_w1, exp_w3, exp_w2, 1)
    return x4.reshape(N_BATCH, SEQ_LEN, d)
```
